```python
import functools
import jax, jax.numpy as jnp
from jax import lax
import numpy as np

D_MODEL = 1024
BATCH = 8
SEQ = 4096
DEPTH = 1

CHUNK = 64
N_META = 16
META_PAD = CHUNK - N_META
DN_HEADS = 4
DN_DK = 128
DN_DV = 128
DN_QK = DN_HEADS * DN_DK
DN_VW = DN_HEADS * DN_DV
SHORT_CONV = 4
CONV_CH = 512
CONV_WIDTH = 31
IN_SIZES = (DN_QK, DN_QK, DN_VW, DN_HEADS, DN_HEADS, DN_VW, 2 * CONV_CH)
IN_COLS = int(sum(IN_SIZES))
IN_SPLITS = [int(i) for i in np.cumsum(IN_SIZES)[:-1]]
N_EXPERTS = 32
TOP_K = 4
D_FF = 1024
SWIGLU_LIMIT = 7.0
SWIGLU_ALPHA = 1.702
MOE_BLOCK = 256
EPS = 1e-6

kernel_name = "hybrid_deltanet_conformer_moe_block"


def rms_norm(x, g):
    xf = x.astype(jnp.float32)
    y = xf * lax.rsqrt(jnp.mean(xf * xf, axis=-1, keepdims=True) + EPS)
    return (y * g.astype(jnp.float32)).astype(x.dtype)


def layer_norm(x, g, b):
    xf = x.astype(jnp.float32)
    mu = jnp.mean(xf, axis=-1, keepdims=True)
    xc = xf - mu
    y = xc * lax.rsqrt(jnp.mean(xc * xc, axis=-1, keepdims=True) + 1e-5)
    return (y * g.astype(jnp.float32) + b.astype(jnp.float32)).astype(x.dtype)


def l2_norm(x):
    xf = x.astype(jnp.float32)
    return xf * lax.rsqrt(jnp.sum(xf * xf, axis=-1, keepdims=True) + EPS)


def causal_depthwise_conv(x, w):
    k, c = w.shape
    return lax.conv_general_dilated(
        x, w[:, None, :].astype(x.dtype), window_strides=(1,), padding=[(k - 1, 0)],
        dimension_numbers=('NWC', 'WIO', 'NWC'), feature_group_count=c)


def chunked_gated_delta_rule(q, k, v, g, beta):
    b, lp, h, dk = q.shape
    dv = v.shape[-1]
    n = lp // CHUNK

    def blocks(t):
        return jnp.moveaxis(t.reshape((b, n, CHUNK, h) + t.shape[3:]), 3, 2)

    q, k, v, g, beta = (blocks(t) for t in (q, k, v, g, beta))
    g_cum = jnp.cumsum(g, axis=-1)
    incl = jnp.tril(jnp.ones((CHUNK, CHUNK), dtype=bool))
    strict = jnp.tril(jnp.ones((CHUNK, CHUNK), dtype=bool), -1)
    diff = g_cum[..., :, None] - g_cum[..., None, :]
    gamma = jnp.where(incl, jnp.exp(jnp.where(incl, diff, 0.0)), 0.0)
    k_beta = k * beta[..., None]
    a_mat = jnp.where(strict, jnp.einsum('bnhik,bnhjk->bnhij', k_beta, k) * gamma, 0.0)
    t_mat = a_mat + jnp.eye(CHUNK, dtype=a_mat.dtype)
    solve = functools.partial(jax.lax.linalg.triangular_solve, left_side=True, lower=True,
                              unit_diagonal=True)
    u = solve(t_mat, v * beta[..., None])
    w = solve(t_mat, k_beta * jnp.exp(g_cum)[..., None])
    qk = jnp.einsum('bnhik,bnhjk->bnhij', q, k) * gamma
    q_dec = q * jnp.exp(g_cum)[..., None]
    k_dec = k * jnp.exp(g_cum[..., -1:] - g_cum)[..., None]
    chunk_decay = jnp.exp(g_cum[..., -1])

    def step(state, inp):
        u_c, w_c, qk_c, qd_c, kd_c, dec_c = inp
        v_new = u_c - jnp.einsum('bhck,bhkv->bhcv', w_c, state)
        o_c = (jnp.einsum('bhck,bhkv->bhcv', qd_c, state)
               + jnp.einsum('bhij,bhjv->bhiv', qk_c, v_new))
        state = state * dec_c[..., None, None] + jnp.einsum('bhck,bhcv->bhkv', kd_c, v_new)
        return state, o_c

    xs = tuple(jnp.moveaxis(t, 1, 0) for t in (u, w, qk, q_dec, k_dec, chunk_decay))
    s0 = jnp.zeros((b, h, dk, dv), jnp.float32)
    _, o = lax.scan(step, s0, xs)
    o = jnp.moveaxis(o, 0, 1)
    return jnp.moveaxis(o, 2, 3).reshape(b, lp, h, dv)


def mixer_block(hn, w_in, conv_qkv_w, a_log, dt_bias, dn_norm_g, w_dn_out, b_glu,
                conv_dw_w, conv_dw_b, conv_ln_g, conv_ln_b, w_conv_out, b_conv_out,
                w_merge, b_merge, w_out):
    bsz, length, _ = hn.shape
    dt = hn.dtype
    proj = hn @ w_in
    q_p, k_p, v_p, a_p, b_p, go_p, glu_p = jnp.split(proj, IN_SPLITS, axis=-1)

    qkv = jnp.concatenate([q_p, k_p, v_p], axis=-1)
    qkv = jnp.pad(qkv, ((0, 0), (META_PAD, 0), (0, 0)))
    qkv = jax.nn.silu(causal_depthwise_conv(qkv, conv_qkv_w))
    lp = length + META_PAD
    q, k, v = jnp.split(qkv, [DN_QK, 2 * DN_QK], axis=-1)
    q = l2_norm(q.reshape(bsz, lp, DN_HEADS, DN_DK)) * (DN_DK ** -0.5)
    k = l2_norm(k.reshape(bsz, lp, DN_HEADS, DN_DK))
    v = v.reshape(bsz, lp, DN_HEADS, DN_DV).astype(jnp.float32)
    beta = jax.nn.sigmoid(b_p.astype(jnp.float32))
    g = -jnp.exp(a_log.astype(jnp.float32)) * jax.nn.softplus(
        a_p.astype(jnp.float32) + dt_bias.astype(jnp.float32))
    pad3 = ((0, 0), (META_PAD, 0), (0, 0))
    beta = jnp.pad(beta, pad3)
    g = jnp.pad(g, pad3)
    o = chunked_gated_delta_rule(q, k, v, g, beta)[:, META_PAD:]
    o = rms_norm(o, dn_norm_g).astype(dt)
    o = o * jax.nn.silu(go_p.reshape(bsz, length, DN_HEADS, DN_DV))
    y_a = o.reshape(bsz, length, DN_VW) @ w_dn_out

    u = glu_p + b_glu
    u = u[..., :CONV_CH] * jax.nn.sigmoid(u[..., CONV_CH:])
    u = causal_depthwise_conv(u, conv_dw_w) + conv_dw_b
    u = jax.nn.silu(layer_norm(u, conv_ln_g, conv_ln_b))
    y_b = u @ w_conv_out + b_conv_out

    gates = jax.nn.sigmoid(hn @ w_merge + b_merge)
    y = gates[..., :D_MODEL] * y_a + gates[..., D_MODEL:] * y_b
    return y @ w_out


def moe_ffn(hn, w_router, b_router, w_gate_up, b_gate_up, w_down, b_down):
    bsz, length, d = hn.shape
    tokens = hn.reshape(-1, d)
    t = tokens.shape[0]
    logits = tokens @ w_router + b_router
    top_vals, top_idx = lax.top_k(logits, TOP_K)
    gates = jax.nn.softmax(top_vals.astype(jnp.float32), axis=-1).astype(hn.dtype)
    m = t * TOP_K
    flat_e = top_idx.reshape(-1).astype(jnp.int32)
    flat_tok = jnp.repeat(jnp.arange(t, dtype=jnp.int32), TOP_K)
    order = jnp.argsort(flat_e, stable=True)
    sorted_e = flat_e[order]
    counts = jnp.bincount(flat_e, length=N_EXPERTS)
    padded = (counts + MOE_BLOCK - 1) // MOE_BLOCK * MOE_BLOCK
    starts = jnp.cumsum(counts) - counts
    pstarts = jnp.cumsum(padded) - padded
    dest = pstarts[sorted_e] + (jnp.arange(m, dtype=jnp.int32) - starts[sorted_e])
    n_blocks = -(-m // MOE_BLOCK) + N_EXPERTS
    cap = n_blocks * MOE_BLOCK
    tok_buf = jnp.full((cap,), t, jnp.int32).at[dest].set(flat_tok[order])
    gate_buf = jnp.zeros((cap,), hn.dtype).at[dest].set(gates.reshape(-1)[order])
    block_e = jnp.minimum(
        jnp.searchsorted(jnp.cumsum(padded), jnp.arange(n_blocks, dtype=jnp.int32) * MOE_BLOCK,
                         side='right'), N_EXPERTS - 1)
    tokens_pad = jnp.concatenate([tokens, jnp.zeros((1, d), hn.dtype)], axis=0)
    x_blocks = tokens_pad[tok_buf].reshape(n_blocks, MOE_BLOCK, d)

    def expert_block(args):
        xb, e = args
        gu = xb @ w_gate_up[e] + b_gate_up[e]
        gate = jnp.minimum(gu[:, :D_FF], SWIGLU_LIMIT)
        up = jnp.clip(gu[:, D_FF:], -SWIGLU_LIMIT, SWIGLU_LIMIT)
        act = (up + 1.0) * gate * jax.nn.sigmoid(SWIGLU_ALPHA * gate)
        return act @ w_down[e] + b_down[e]

    y = lax.map(expert_block, (x_blocks, block_e)).reshape(cap, d)
    out = jnp.zeros((t + 1, d), hn.dtype).at[tok_buf].add(y * gate_buf[:, None])
    return out[:t].reshape(bsz, length, d)


def setup_inputs(seed: int = 0) -> dict:
    key = jax.random.key(seed)
    ks = jax.random.split(key, 32)
    f32 = jnp.float32
    nrm = lambda k, shape, scale: jax.random.normal(k, shape, f32) * scale
    gain = lambda k, shape: 1.0 + 0.02 * jax.random.normal(k, shape, f32)
    L = DEPTH
    dt0 = jnp.exp(jax.random.uniform(ks[5], (L, DN_HEADS), f32, np.log(1e-3), np.log(1e-1)))
    return {
        'x': nrm(ks[0], (BATCH, SEQ, D_MODEL), 1.0),
        'meta_tokens': nrm(ks[1], (N_META, D_MODEL), 1.0),
        'norm_mix_g': gain(ks[2], (L, D_MODEL)),
        'w_in': nrm(ks[3], (L, D_MODEL, IN_COLS), D_MODEL ** -0.5),
        'conv_qkv_w': nrm(ks[4], (L, SHORT_CONV, 2 * DN_QK + DN_VW), SHORT_CONV ** -0.5),
        'a_log': jnp.log(jax.random.uniform(ks[6], (L, DN_HEADS), f32, 1.0, 16.0)),
        'dt_bias': dt0 + jnp.log(-jnp.expm1(-dt0)),
        'dn_norm_g': gain(ks[7], (L, DN_DV)),
        'w_dn_out': nrm(ks[8], (L, DN_VW, D_MODEL), DN_VW ** -0.5),
        'b_glu': nrm(ks[9], (L, 2 * CONV_CH), 0.02),
        'conv_dw_w': nrm(ks[10], (L, CONV_WIDTH, CONV_CH), CONV_WIDTH ** -0.5),
        'conv_dw_b': nrm(ks[11], (L, CONV_CH), 0.02),
        'conv_ln_g': gain(ks[12], (L, CONV_CH)),
        'conv_ln_b': nrm(ks[13], (L, CONV_CH), 0.02),
        'w_conv_out': nrm(ks[14], (L, CONV_CH, D_MODEL), CONV_CH ** -0.5),
        'b_conv_out': nrm(ks[15], (L, D_MODEL), 0.02),
        'w_merge': nrm(ks[16], (L, D_MODEL, 2 * D_MODEL), D_MODEL ** -0.5),
        'b_merge': nrm(ks[17], (L, 2 * D_MODEL), 0.02),
        'w_out': nrm(ks[18], (L, D_MODEL, D_MODEL), D_MODEL ** -0.5),
        'norm_ffn_g': gain(ks[19], (L, D_MODEL)),
        'w_router': nrm(ks[20], (L, D_MODEL, N_EXPERTS), D_MODEL ** -0.5),
        'b_router': nrm(ks[21], (L, N_EXPERTS), 0.01),
        'w_gate_up': nrm(ks[22], (L, N_EXPERTS, D_MODEL, 2 * D_FF), D_MODEL ** -0.5),
        'b_gate_up': nrm(ks[23], (L, N_EXPERTS, 2 * D_FF), 0.02),
        'w_down': nrm(ks[24], (L, N_EXPERTS, D_FF, D_MODEL), D_FF ** -0.5),
        'b_down': nrm(ks[25], (L, N_EXPERTS, D_MODEL), 0.02),
        'final_norm_g': gain(ks[26], (D_MODEL,)),
    }


def reference(x, meta_tokens, norm_mix_g, w_in, conv_qkv_w, a_log, dt_bias, dn_norm_g,
              w_dn_out, b_glu, conv_dw_w, conv_dw_b, conv_ln_g, conv_ln_b, w_conv_out,
              b_conv_out, w_merge, b_merge, w_out, norm_ffn_g, w_router, b_router,
              w_gate_up, b_gate_up, w_down, b_down, final_norm_g):
    bsz = x.shape[0]
    meta = jnp.broadcast_to(meta_tokens[None].astype(x.dtype), (bsz, N_META, x.shape[-1]))
    h = jnp.concatenate([meta, x], axis=1)
    for l in range(DEPTH):
        hn = rms_norm(h, norm_mix_g[l])
        h = h + mixer_block(hn, w_in[l], conv_qkv_w[l], a_log[l], dt_bias[l], dn_norm_g[l],
                            w_dn_out[l], b_glu[l], conv_dw_w[l], conv_dw_b[l], conv_ln_g[l],
                            conv_ln_b[l], w_conv_out[l], b_conv_out[l], w_merge[l],
                            b_merge[l], w_out[l])
        hn = rms_norm(h, norm_ffn_g[l])
        h = h + moe_ffn(hn, w_router[l], b_router[l], w_gate_up[l], b_gate_up[l],
                        w_down[l], b_down[l])
    out = rms_norm(h, final_norm_g)
    return out[:, N_META:]
```

```python
import functools

import jax
import jax.numpy as jnp
from jax import lax
from jax.experimental import pallas as pl
from jax.experimental.pallas import tpu as pltpu

F32 = jnp.float32
BF16 = jnp.bfloat16
I32 = jnp.int32

CHUNK = 64
TOP_K = 4
MOE_BLOCK = 256
SWIGLU_LIMIT = 7.0
SWIGLU_ALPHA = 1.702
EPS = 1e-6
LN_EPS = 1e-5
LANES = 128
SUBLANES = 8
BF16_ROWS = 16
VMEM_LIMIT = 56 * 1024 * 1024


def _dot(a, b):
    return jnp.dot(a, b, preferred_element_type=F32)


def _dot_hi(a, b):
    return jnp.dot(a, b, precision=lax.Precision.HIGHEST, preferred_element_type=F32)


def _dot_nt(a, b):
    return lax.dot_general(a, b, (((1,), (1,)), ((), ())), preferred_element_type=F32)


def _dot_tn(a, b):
    return lax.dot_general(a, b, (((0,), (0,)), ((), ())), preferred_element_type=F32)


def _sigmoid(x):
    return 1.0 / (1.0 + jnp.exp(-x))


def _silu(x):
    return x * _sigmoid(x)


def _softplus(x):
    return jnp.maximum(x, 0.0) + jnp.log(1.0 + jnp.exp(-jnp.abs(x)))


def _rms(x, g):
    return x * lax.rsqrt(jnp.mean(x * x, axis=-1, keepdims=True) + EPS) * g


def _iota(shape, dim):
    return lax.broadcasted_iota(I32, shape, dim)


def _pick_tile(n, target, mult):
    best = None
    for t in range(mult, min(n, target) + 1, mult):
        if n % t == 0:
            best = t
    assert best is not None, (n, target, mult)
    return best


def _params(sem):
    return pltpu.CompilerParams(dimension_semantics=sem, vmem_limit_bytes=VMEM_LIMIT)


def _front_kernel(h_ref, g_ref, wqkv_ref, wab_ref, wgo_ref, wglu_ref, wmg_ref, bglu_ref, bmg_ref,
                  alog_ref, dtb_ref, qkv_ref, gb_ref, go_ref, u_ref, gates_ref,
                  *, tm, meta_pad, heads, conv_ch):
    j = pl.program_id(1)
    hn = _rms(h_ref[...], g_ref[...]).astype(BF16)
    qkv_ref[...] = _dot(hn, wqkv_ref[...]).astype(BF16)
    ab = _dot(hn, wab_ref[...])
    valid = (j * tm + _iota((tm, 1), 0)) >= meta_pad
    lane = _iota((1, LANES), 1)
    g = -jnp.exp(alog_ref[...]) * _softplus(ab + dtb_ref[...])
    gb = jnp.where(lane < heads, g, _sigmoid(ab))
    gb_ref[...] = jnp.where(valid, gb, 0.0)
    go_ref[...] = _silu(_dot(hn, wgo_ref[...])).astype(BF16)
    glu = _dot(hn, wglu_ref[...]) + bglu_ref[...]
    u = glu[:, :conv_ch] * _sigmoid(glu[:, conv_ch:])
    u_ref[...] = jnp.where(valid, u, 0.0)
    gates_ref[...] = _sigmoid(_dot(hn, wmg_ref[...]) + bmg_ref[...]).astype(BF16)


def _conv_kernel(u_ref, gate_ref, w_ref, b_ref, lng_ref, lnb_ref, wo_ref, bo_ref, y_ref,
                 xcat_ref, cbuf_ref, *, tm, kw, halo, rb):
    j = pl.program_id(1)

    @pl.when(j == 0)
    def _():
        xcat_ref[0:halo, :] = jnp.zeros((halo, xcat_ref.shape[1]), F32)

    @pl.when(j > 0)
    def _():
        xcat_ref[0:halo, :] = xcat_ref[tm:tm + halo, :]

    xcat_ref[halo:halo + tm, :] = u_ref[...]
    off = halo - (kw - 1)
    bias = b_ref[...]
    for r in range(tm // rb):
        acc = jnp.zeros((rb, xcat_ref.shape[1]), F32)
        for t in range(kw):
            acc = acc + xcat_ref[pl.ds(r * rb + off + t, rb), :] * w_ref[t:t + 1, :]
        cbuf_ref[r * rb:(r + 1) * rb, :] = acc + bias
    c = cbuf_ref[...]
    mu = jnp.mean(c, axis=-1, keepdims=True)
    xc = c - mu
    ln = xc * lax.rsqrt(jnp.mean(xc * xc, axis=-1, keepdims=True) + LN_EPS) * lng_ref[...] + lnb_ref[...]
    yb = _dot(_silu(ln).astype(BF16), wo_ref[...]) + bo_ref[...]
    y_ref[...] = (gate_ref[...].astype(F32) * yb).astype(BF16)


def _unit_lower_inverse(a, row, col):
    eye = (row == col).astype(F32)
    blk16 = (row >> 4) == (col >> 4)
    blk32 = (row >> 5) == (col >> 5)
    n1 = jnp.where(blk16, a, 0.0)
    n2 = _dot_hi(n1, n1)
    n4 = _dot_hi(n2, n2)
    n8 = _dot_hi(n4, n4)
    x = eye - n1
    x = x + _dot_hi(x, n2)
    x = x + _dot_hi(x, n4)
    x = x + _dot_hi(x, n8)
    l1 = jnp.where(blk32 & jnp.logical_not(blk16), a, 0.0)
    x = x - _dot_hi(x, _dot_hi(l1, x))
    l2 = jnp.where(blk32, 0.0, a)
    x = x - _dot_hi(x, _dot_hi(l2, x))
    return x


def _delta_kernel(qkv_ref, gb_ref, go_ref, cw_ref, ng_ref, o_ref, s_ref, xc_ref,
                  *, heads, dk, dv, ch, ksz, halo):
    c = pl.program_id(1)
    qk_w = heads * dk

    @pl.when(c == 0)
    def _():
        s_ref[...] = jnp.zeros(s_ref.shape, F32)
        xc_ref[0:halo, :] = jnp.zeros((halo, xc_ref.shape[1]), F32)

    @pl.when(c > 0)
    def _():
        xc_ref[0:halo, :] = xc_ref[ch:ch + halo, :]

    xc_ref[halo:halo + ch, :] = qkv_ref[...].astype(F32)
    off = halo - (ksz - 1)
    acc = xc_ref[pl.ds(off, ch), :] * cw_ref[0:1, :]
    for t in range(1, ksz):
        acc = acc + xc_ref[pl.ds(off + t, ch), :] * cw_ref[t:t + 1, :]
    qkv = _silu(acc)

    gb = gb_ref[...]
    row = _iota((ch, ch), 0)
    col = _iota((ch, ch), 1)
    incl = row >= col
    strict = row > col
    gc = _dot_hi(incl.astype(F32), gb)
    gct = lax.dot_general(gb, (row <= col).astype(F32), (((0,), (0,)), ((), ())),
                          precision=lax.Precision.HIGHEST, preferred_element_type=F32)
    glast = gc[ch - 1:ch, :]
    e_g = jnp.exp(gc)
    e_kd = jnp.exp(glast - gc)
    e_last = jnp.exp(glast)

    for h in range(heads):
        q = qkv[:, h * dk:(h + 1) * dk]
        k = qkv[:, qk_w + h * dk:qk_w + (h + 1) * dk]
        v = qkv[:, 2 * qk_w + h * dv:2 * qk_w + (h + 1) * dv]
        qn = q * lax.rsqrt(jnp.sum(q * q, axis=-1, keepdims=True) + EPS) * (dk ** -0.5)
        kn = k * lax.rsqrt(jnp.sum(k * k, axis=-1, keepdims=True) + EPS)
        beta = gb[:, heads + h:heads + h + 1]
        diff = gc[:, h:h + 1] - gct[h:h + 1, :]
        gamma = jnp.where(incl, jnp.exp(jnp.where(incl, diff, 0.0)), 0.0)
        kb = kn * beta
        a = jnp.where(strict, _dot_nt(kb, kn) * gamma, 0.0)
        tinv = _unit_lower_inverse(a, row, col)
        uw = _dot_hi(tinv, jnp.concatenate([v * beta, kb * e_g[:, h:h + 1]], axis=-1))
        u = uw[:, :dv]
        w = uw[:, dv:]
        qk = jnp.where(incl, _dot_nt(qn, kn) * gamma, 0.0)
        qd = qn * e_g[:, h:h + 1]
        kd = kn * e_kd[:, h:h + 1]
        s = s_ref[h]
        ws = _dot(jnp.concatenate([w, qd], axis=0), s)
        v_new = u - ws[:ch]
        o = ws[ch:] + _dot(qk, v_new)
        s_ref[h] = s * e_last[:, h:h + 1] + _dot_tn(kd, v_new)
        on = _rms(o, ng_ref[...])
        o_ref[:, h * dv:(h + 1) * dv] = (on * go_ref[:, h * dv:(h + 1) * dv].astype(F32)).astype(BF16)


def _mid_kernel(h_ref, o_ref, ga_ref, ybg_ref, wdn_ref, wout_ref, g2_ref, wr_ref, br_ref,
                h2_ref, hn3_ref, topi_ref, topg_ref, *, tm, n_exp):
    ya = _dot(o_ref[...], wdn_ref[...])
    y = ga_ref[...].astype(F32) * ya + ybg_ref[...].astype(F32)
    h2 = h_ref[...] + _dot(y.astype(BF16), wout_ref[...])
    h2_ref[...] = h2
    hn = _rms(h2, g2_ref[...])
    for s in range(hn.shape[1] // LANES):
        hn3_ref[:, s, :] = hn[:, s * LANES:(s + 1) * LANES]
    logits = _dot(hn.astype(BF16), wr_ref[...]) + br_ref[...]
    lane = _iota((tm, LANES), 1)
    lane_f = lane.astype(F32)
    l = jnp.where(lane < n_exp, logits, -jnp.inf)
    vals, idxs = [], []
    for _ in range(TOP_K):
        m = jnp.max(l, axis=-1, keepdims=True)
        idx = jnp.min(jnp.where(l == m, lane_f, float(LANES)), axis=-1, keepdims=True)
        vals.append(m)
        idxs.append(idx)
        l = jnp.where(lane_f == idx, -jnp.inf, l)
    es = [jnp.exp(v - vals[0]) for v in vals]
    den = es[0]
    for e in es[1:]:
        den = den + e
    topi = jnp.zeros((tm, LANES), F32)
    topg = jnp.zeros((tm, LANES), F32)
    for k in range(TOP_K):
        topi = jnp.where(lane == k, idxs[k], topi)
        topg = jnp.where(lane == k, es[k] / den, topg)
    topi_ref[...] = topi.astype(I32)
    topg_ref[...] = topg


def _choice_mask(topi, lane, n_exp):
    m = jnp.zeros(topi.shape, F32)
    for k in range(TOP_K):
        m = m + (lane == topi[:, k:k + 1]).astype(F32)
    return jnp.where(lane < n_exp, m, 0.0)


def _rank_kernel(topi_ref, rank_ref, cnt_ref, carry_ref, *, tm, n_exp):
    i = pl.program_id(0)

    @pl.when(i == 0)
    def _():
        carry_ref[...] = jnp.zeros(carry_ref.shape, F32)

    lane = _iota((tm, LANES), 1)
    m = _choice_mask(topi_ref[...], lane, n_exp)
    tri = (_iota((tm, tm), 0) >= _iota((tm, tm), 1)).astype(BF16)
    cum = _dot(tri, m.astype(BF16))
    carry = carry_ref[...]
    rank_ref[...] = (carry + cum - m).astype(I32)
    new_carry = carry + cum[tm - 1:tm, :]
    carry_ref[...] = new_carry
    cnt_ref[...] = new_carry


def _dest_kernel(topi_ref, rank_ref, pstart_ref, dest_ref, *, tm):
    lane = _iota((tm, LANES), 1)
    topi = topi_ref[...]
    pos = (rank_ref[...] + pstart_ref[...]).astype(F32)
    dest = jnp.zeros((tm, LANES), F32)
    for k in range(TOP_K):
        dk = jnp.sum(jnp.where(lane == topi[:, k:k + 1], pos, 0.0), axis=-1, keepdims=True)
        dest = jnp.where(lane == k, dk, dest)
    dest_ref[...] = dest.astype(I32)


def _row_copies_wait(src_ref, dst_ref, sem, n_rows, count):
    for _ in range(count):
        pltpu.make_async_copy(src_ref.at[pl.ds(0, n_rows)], dst_ref.at[pl.ds(0, n_rows)], sem).wait()


def _dispatch_kernel(dest_ref, hn3_ref, xs_in_ref, xs_ref, sem, *, tm):
    del xs_in_ref

    def body(t, carry):
        for k in range(TOP_K):
            pltpu.make_async_copy(hn3_ref.at[t], xs_ref.at[dest_ref[t * TOP_K + k]], sem).start()
        return carry

    lax.fori_loop(0, tm, body, 0)
    _row_copies_wait(hn3_ref, xs_ref, sem, tm, TOP_K)


def _expert_kernel(be_ref, nbu_ref, x_ref, wgu_ref, bgu_ref, wd_ref, bd_ref, y_ref, *, d_ff):
    del be_ref
    i = pl.program_id(0)
    n_s = x_ref.shape[1]

    @pl.when(i < nbu_ref[0])
    def _():
        x = jnp.concatenate([x_ref[:, s, :] for s in range(n_s)], axis=-1).astype(BF16)
        gu = _dot(x, wgu_ref[0]) + bgu_ref[0]
        gate = jnp.minimum(gu[:, :d_ff], SWIGLU_LIMIT)
        up = jnp.clip(gu[:, d_ff:], -SWIGLU_LIMIT, SWIGLU_LIMIT)
        act = (up + 1.0) * gate * _sigmoid(SWIGLU_ALPHA * gate)
        y = _dot(act.astype(BF16), wd_ref[0]) + bd_ref[0]
        for s in range(n_s):
            y_ref[:, s, :] = y[:, s * LANES:(s + 1) * LANES]

    @pl.when(i >= nbu_ref[0])
    def _():
        y_ref[...] = jnp.zeros(y_ref.shape, F32)


def _combine_kernel(dest_ref, h2_ref, topg_ref, fg_ref, ys_ref, out_ref, gbuf_ref, sem, *, tm):
    def body(t, carry):
        for k in range(TOP_K):
            pltpu.make_async_copy(ys_ref.at[dest_ref[t * TOP_K + k]], gbuf_ref.at[k, t], sem).start()
        return carry

    lax.fori_loop(0, tm, body, 0)
    for k in range(TOP_K):
        pltpu.make_async_copy(ys_ref.at[pl.ds(0, tm)], gbuf_ref.at[k], sem).wait()
    n_s = gbuf_ref.shape[2]
    topg = topg_ref[...]
    acc = h2_ref[...]
    for k in range(TOP_K):
        yk = jnp.concatenate([gbuf_ref[k, :, s, :] for s in range(n_s)], axis=-1)
        acc = acc + topg[:, k:k + 1] * yk
    out_ref[...] = _rms(acc, fg_ref[...])


def _pad_lanes(v, fill=0.0):
    v = v.astype(F32)
    return jnp.concatenate([v, jnp.full((LANES - v.shape[0],), fill, F32)])[None, :]


def kernel(x, meta_tokens, norm_mix_g, w_in, conv_qkv_w, a_log, dt_bias, dn_norm_g, w_dn_out, b_glu,
           conv_dw_w, conv_dw_b, conv_ln_g, conv_ln_b, w_conv_out, b_conv_out, w_merge, b_merge, w_out,
           norm_ffn_g, w_router, b_router, w_gate_up, b_gate_up, w_down, b_down, final_norm_g):
    bsz, seq, d = x.shape
    depth = w_in.shape[0]
    n_meta = meta_tokens.shape[0]
    heads = a_log.shape[1]
    vw = w_dn_out.shape[1]
    qk_w = (conv_qkv_w.shape[2] - vw) // 2
    dk, dv = qk_w // heads, vw // heads
    ksz = conv_qkv_w.shape[1]
    kw, conv_ch = conv_dw_w.shape[1], conv_dw_w.shape[2]
    n_exp, d_ff = w_gate_up.shape[1], w_down.shape[2]
    meta_pad = CHUNK - n_meta
    lp = seq + n_meta + meta_pad
    assert depth == 1 and lp % CHUNK == 0 and dk == LANES and dv == LANES and d % LANES == 0
    assert 2 * heads <= LANES and n_exp <= LANES and kw - 1 <= 32 and ksz - 1 <= SUBLANES
    rows = bsz * lp
    n_s = d // LANES

    meta = jnp.broadcast_to(meta_tokens[None].astype(x.dtype), (bsz, n_meta, d))
    hp = jnp.concatenate([jnp.zeros((bsz, meta_pad, d), x.dtype), meta, x], axis=1).reshape(rows, d)

    wi = w_in[0]
    o_a = 2 * qk_w + vw
    o_go = o_a + 2 * heads
    o_glu = o_go + vw
    w_qkv = wi[:, :o_a].astype(BF16)
    w_ab = jnp.pad(wi[:, o_a:o_go], ((0, 0), (0, LANES - 2 * heads))).astype(BF16)
    w_go = wi[:, o_go:o_glu].astype(BF16)
    w_glu = wi[:, o_glu:].astype(BF16)
    row2 = lambda v: v.astype(F32).reshape(1, -1)

    tm1 = _pick_tile(lp, 320, BF16_ROWS)
    nt1 = lp // tm1
    rspec = lambda cols, tm, nt: pl.BlockSpec((tm, cols), lambda b, j: (b * nt + j, 0))
    full = lambda a: pl.BlockSpec(a.shape, lambda *_: (0,) * a.ndim)
    front_in = [hp, row2(norm_mix_g[0]), w_qkv, w_ab, w_go, w_glu, w_merge[0].astype(BF16),
                row2(b_glu[0]), row2(b_merge[0]), _pad_lanes(a_log[0]), _pad_lanes(dt_bias[0])]
    qkv_pre, gbeta, go_act, u_glu, gates = pl.pallas_call(
        functools.partial(_front_kernel, tm=tm1, meta_pad=meta_pad, heads=heads, conv_ch=conv_ch),
        grid=(bsz, nt1),
        in_specs=[rspec(d, tm1, nt1)] + [full(a) for a in front_in[1:]],
        out_specs=[rspec(o_a, tm1, nt1), rspec(LANES, tm1, nt1), rspec(vw, tm1, nt1),
                   rspec(conv_ch, tm1, nt1), rspec(2 * d, tm1, nt1)],
        out_shape=[jax.ShapeDtypeStruct((rows, o_a), BF16), jax.ShapeDtypeStruct((rows, LANES), F32),
                   jax.ShapeDtypeStruct((rows, vw), BF16), jax.ShapeDtypeStruct((rows, conv_ch), F32),
                   jax.ShapeDtypeStruct((rows, 2 * d), BF16)],
        compiler_params=_params(("arbitrary", "arbitrary")),
        name="front",
    )(*front_in)

    tmc = _pick_tile(lp, 320, 32)
    ntc = lp // tmc
    halo = 32
    conv_in = [u_glu, gates, conv_dw_w[0].astype(F32), row2(conv_dw_b[0]), row2(conv_ln_g[0]),
               row2(conv_ln_b[0]), w_conv_out[0].astype(BF16), row2(b_conv_out[0])]
    ybg = pl.pallas_call(
        functools.partial(_conv_kernel, tm=tmc, kw=kw, halo=halo, rb=32),
        grid=(bsz, ntc),
        in_specs=[rspec(conv_ch, tmc, ntc),
                  pl.BlockSpec((tmc, d), lambda b, j: (b * ntc + j, 1))] + [full(a) for a in conv_in[2:]],
        out_specs=rspec(d, tmc, ntc),
        out_shape=jax.ShapeDtypeStruct((rows, d), BF16),
        scratch_shapes=[pltpu.VMEM((tmc + halo, conv_ch), F32), pltpu.VMEM((tmc, conv_ch), F32)],
        compiler_params=_params(("arbitrary", "arbitrary")),
        name="convmod",
    )(*conv_in)

    nc = lp // CHUNK
    cspec = lambda cols: pl.BlockSpec((CHUNK, cols), lambda b, c: (b * nc + c, 0))
    delta_in = [qkv_pre, gbeta, go_act, conv_qkv_w[0].astype(F32), row2(dn_norm_g[0])]
    o_dn = pl.pallas_call(
        functools.partial(_delta_kernel, heads=heads, dk=dk, dv=dv, ch=CHUNK, ksz=ksz, halo=SUBLANES),
        grid=(bsz, nc),
        in_specs=[cspec(o_a), cspec(LANES), cspec(vw), full(delta_in[3]), full(delta_in[4])],
        out_specs=cspec(vw),
        out_shape=jax.ShapeDtypeStruct((rows, vw), BF16),
        scratch_shapes=[pltpu.VMEM((heads, dk, dv), F32), pltpu.VMEM((CHUNK + SUBLANES, o_a), F32)],
        compiler_params=_params(("arbitrary", "arbitrary")),
        name="delta",
    )(*delta_in)

    tm3 = _pick_tile(lp, 320, BF16_ROWS)
    nt3 = lp // tm3
    w_r = jnp.pad(w_router[0], ((0, 0), (0, LANES - n_exp))).astype(BF16)
    mid_in = [hp, o_dn, gates, ybg, w_dn_out[0].astype(BF16), w_out[0].astype(BF16),
              row2(norm_ffn_g[0]), w_r, _pad_lanes(b_router[0])]
    h2, hn3, topi, topg = pl.pallas_call(
        functools.partial(_mid_kernel, tm=tm3, n_exp=n_exp),
        grid=(bsz, nt3),
        in_specs=[rspec(d, tm3, nt3), rspec(vw, tm3, nt3), rspec(d, tm3, nt3), rspec(d, tm3, nt3)]
        + [full(a) for a in mid_in[4:]],
        out_specs=[rspec(d, tm3, nt3), pl.BlockSpec((tm3, n_s, LANES), lambda b, j: (b * nt3 + j, 0, 0)),
                   rspec(LANES, tm3, nt3), rspec(LANES, tm3, nt3)],
        out_shape=[jax.ShapeDtypeStruct((rows, d), F32), jax.ShapeDtypeStruct((rows, n_s, LANES), F32),
                   jax.ShapeDtypeStruct((rows, LANES), I32), jax.ShapeDtypeStruct((rows, LANES), F32)],
        compiler_params=_params(("arbitrary", "arbitrary")),
        name="mid",
    )(*mid_in)

    tmr = _pick_tile(rows, 256, SUBLANES)
    ntr = rows // tmr
    tspec = pl.BlockSpec((tmr, LANES), lambda i: (i, 0))
    one = pl.BlockSpec((1, LANES), lambda i: (0, 0))
    rank, cnt = pl.pallas_call(
        functools.partial(_rank_kernel, tm=tmr, n_exp=n_exp),
        grid=(ntr,),
        in_specs=[tspec],
        out_specs=[tspec, one],
        out_shape=[jax.ShapeDtypeStruct((rows, LANES), I32), jax.ShapeDtypeStruct((1, LANES), F32)],
        scratch_shapes=[pltpu.VMEM((1, LANES), F32)],
        compiler_params=_params(("arbitrary",)),
        name="rank",
    )(topi)

    n_rows_routed = rows * TOP_K
    n_blocks = -(-n_rows_routed // MOE_BLOCK) + n_exp
    cap = n_blocks * MOE_BLOCK
    counts = cnt[0, :n_exp].astype(I32)
    padded = (counts + MOE_BLOCK - 1) // MOE_BLOCK * MOE_BLOCK
    pend = jnp.cumsum(padded)
    pstart = pend - padded
    n_used = (pend[-1] // MOE_BLOCK).astype(I32).reshape(1)
    block_e = jnp.minimum(
        jnp.searchsorted(pend, jnp.arange(n_blocks, dtype=I32) * MOE_BLOCK, side='right'), n_exp - 1).astype(I32)
    pstart_row = jnp.pad(pstart, (0, LANES - n_exp)).astype(I32)[None, :]

    dest = pl.pallas_call(
        functools.partial(_dest_kernel, tm=tmr),
        grid=(ntr,),
        in_specs=[tspec, tspec, one],
        out_specs=tspec,
        out_shape=jax.ShapeDtypeStruct((rows, LANES), I32),
        compiler_params=_params(("arbitrary",)),
        name="dest",
    )(topi, rank, pstart_row)
    dest_flat = dest[:, :TOP_K].reshape(rows * TOP_K)

    smem_idx = pl.BlockSpec((tmr * TOP_K,), lambda i: (i,), memory_space=pltpu.SMEM)
    tok3 = pl.BlockSpec((tmr, n_s, LANES), lambda i: (i, 0, 0))
    anyspec = pl.BlockSpec(memory_space=pl.ANY)
    xs = pl.pallas_call(
        functools.partial(_dispatch_kernel, tm=tmr),
        grid=(ntr,),
        in_specs=[smem_idx, tok3, anyspec],
        out_specs=anyspec,
        out_shape=jax.ShapeDtypeStruct((cap, n_s, LANES), F32),
        scratch_shapes=[pltpu.SemaphoreType.DMA(())],
        input_output_aliases={2: 0},
        compiler_params=_params(("arbitrary",)),
        name="dispatch",
    )(dest_flat, hn3, jnp.zeros((cap, n_s, LANES), F32))

    blk3 = pl.BlockSpec((MOE_BLOCK, n_s, LANES), lambda i, be, nbu: (i, 0, 0))
    ew = lambda a: pl.BlockSpec((1,) + a.shape[1:], lambda i, be, nbu: (be[i],) + (0,) * (a.ndim - 1))
    e_in = [w_gate_up[0].astype(BF16), b_gate_up[0].astype(F32)[:, None, :],
            w_down[0].astype(BF16), b_down[0].astype(F32)[:, None, :]]
    ys = pl.pallas_call(
        functools.partial(_expert_kernel, d_ff=d_ff),
        grid_spec=pltpu.PrefetchScalarGridSpec(
            num_scalar_prefetch=2, grid=(n_blocks,),
            in_specs=[blk3] + [ew(a) for a in e_in],
            out_specs=blk3),
        out_shape=jax.ShapeDtypeStruct((cap, n_s, LANES), F32),
        compiler_params=_params(("arbitrary",)),
        name="experts",
    )(block_e, n_used, xs, *e_in)

    out = pl.pallas_call(
        functools.partial(_combine_kernel, tm=tmr),
        grid=(ntr,),
        in_specs=[smem_idx, pl.BlockSpec((tmr, d), lambda i: (i, 0)), tspec,
                  pl.BlockSpec((1, d), lambda i: (0, 0)), anyspec],
        out_specs=pl.BlockSpec((tmr, d), lambda i: (i, 0)),
        out_shape=jax.ShapeDtypeStruct((rows, d), x.dtype),
        scratch_shapes=[pltpu.VMEM((TOP_K, tmr, n_s, LANES), F32), pltpu.SemaphoreType.DMA(())],
        compiler_params=_params(("arbitrary",)),
        name="combine",
    )(dest_flat, h2, topg, row2(final_norm_g), ys)
    return out.reshape(bsz, lp, d)[:, meta_pad + n_meta:]
```

```python
import functools

import jax
import jax.numpy as jnp
from jax import lax
from jax.experimental import pallas as pl
from jax.experimental.pallas import tpu as pltpu

F32 = jnp.float32
BF16 = jnp.bfloat16
I32 = jnp.int32

CHUNK = 64
TOP_K = 4
MOE_BLOCK = 512
TOKEN_TILE = 256
SWIGLU_LIMIT = 7.0
SWIGLU_ALPHA = 1.702
EPS = 1e-6
LN_EPS = 1e-5
LANES = 128
SUBLANES = 8
BF16_ROWS = 16
VMEM_LIMIT = 56 * 1024 * 1024


def _dot(a, b):
    return jnp.dot(a, b, preferred_element_type=F32)


def _dot_hi(a, b):
    return jnp.dot(a, b, precision=lax.Precision.HIGHEST, preferred_element_type=F32)


def _dot_nt(a, b):
    return lax.dot_general(a, b, (((1,), (1,)), ((), ())), preferred_element_type=F32)


def _dot_tn(a, b):
    return lax.dot_general(a, b, (((0,), (0,)), ((), ())), preferred_element_type=F32)


def _sigmoid(x):
    return 1.0 / (1.0 + jnp.exp(-x))


def _silu(x):
    return x * _sigmoid(x)


def _softplus(x):
    return jnp.maximum(x, 0.0) + jnp.log(1.0 + jnp.exp(-jnp.abs(x)))


def _rms(x, g):
    return x * lax.rsqrt(jnp.mean(x * x, axis=-1, keepdims=True) + EPS) * g


def _iota(shape, dim):
    return lax.broadcasted_iota(I32, shape, dim)


def _pick_tile(n, target, mult):
    best = None
    for t in range(mult, min(n, target) + 1, mult):
        if n % t == 0:
            best = t
    assert best is not None, (n, target, mult)
    return best


def _params(sem, flags=None):
    return pltpu.CompilerParams(dimension_semantics=sem, vmem_limit_bytes=VMEM_LIMIT, flags=flags)


def _front_kernel(h_ref, g_ref, wqkv_ref, wab_ref, wgo_ref, wglu_ref, wmg_ref, bglu_ref, bmg_ref,
                  alog_ref, dtb_ref, qkv_ref, gb_ref, go_ref, u_ref, gates_ref,
                  *, tm, meta_pad, heads, conv_ch):
    j = pl.program_id(1)
    hn = _rms(h_ref[...], g_ref[...]).astype(BF16)
    qkv_ref[...] = _dot(hn, wqkv_ref[...]).astype(BF16)
    ab = _dot(hn, wab_ref[...])
    valid = (j * tm + _iota((tm, 1), 0)) >= meta_pad
    lane = _iota((1, LANES), 1)
    g = -jnp.exp(alog_ref[...]) * _softplus(ab + dtb_ref[...])
    gb = jnp.where(lane < heads, g, _sigmoid(ab))
    gb_ref[...] = jnp.where(valid, gb, 0.0)
    go_ref[...] = _silu(_dot(hn, wgo_ref[...])).astype(BF16)
    glu = _dot(hn, wglu_ref[...]) + bglu_ref[...]
    u = glu[:, :conv_ch] * _sigmoid(glu[:, conv_ch:])
    u_ref[...] = jnp.where(valid, u, 0.0)
    gates_ref[...] = _sigmoid(_dot(hn, wmg_ref[...]) + bmg_ref[...]).astype(BF16)


def _conv_kernel(u_ref, gate_ref, w_ref, b_ref, lng_ref, lnb_ref, wo_ref, bo_ref, y_ref,
                 xcat_ref, cbuf_ref, *, tm, kw, halo, rb):
    j = pl.program_id(1)

    @pl.when(j == 0)
    def _():
        xcat_ref[0:halo, :] = jnp.zeros((halo, xcat_ref.shape[1]), F32)

    @pl.when(j > 0)
    def _():
        xcat_ref[0:halo, :] = xcat_ref[tm:tm + halo, :]

    xcat_ref[halo:halo + tm, :] = u_ref[...]
    off = halo - (kw - 1)
    bias = b_ref[...]
    for r in range(tm // rb):
        acc = jnp.zeros((rb, xcat_ref.shape[1]), F32)
        for t in range(kw):
            acc = acc + xcat_ref[pl.ds(r * rb + off + t, rb), :] * w_ref[t:t + 1, :]
        cbuf_ref[r * rb:(r + 1) * rb, :] = acc + bias
    c = cbuf_ref[...]
    mu = jnp.mean(c, axis=-1, keepdims=True)
    xc = c - mu
    ln = xc * lax.rsqrt(jnp.mean(xc * xc, axis=-1, keepdims=True) + LN_EPS) * lng_ref[...] + lnb_ref[...]
    yb = _dot(_silu(ln).astype(BF16), wo_ref[...]) + bo_ref[...]
    y_ref[...] = (gate_ref[...].astype(F32) * yb).astype(BF16)


def _bdot(a, b):
    return jnp.dot(a.astype(BF16), b.astype(BF16), preferred_element_type=F32)


def _split3(x):
    x1 = x.astype(BF16)
    r1 = x - x1.astype(F32)
    x2 = r1.astype(BF16)
    x3 = (r1 - x2.astype(F32)).astype(BF16)
    return x1, x2, x3


def _unit_lower_inverses_minus_eye(mats, row, col):
    ch = mats[0].shape[0]
    blk16 = (row >> 4) == (col >> 4)
    blk32 = (row >> 5) == (col >> 5)
    n1 = [jnp.where(blk16, a, 0.0) for a in mats]
    n2 = [_bdot(n, n) for n in n1]
    r = [_bdot(jnp.concatenate([n, m], axis=0), m) for n, m in zip(n1, n2)]
    n4 = [x[ch:] for x in r]
    y = [m - n - x[:ch] for n, m, x in zip(n1, n2, r)]
    r = [_bdot(jnp.concatenate([yy, m], axis=0), m) for yy, m in zip(y, n4)]
    n8 = [x[ch:] for x in r]
    y = [yy + m + x[:ch] for yy, m, x in zip(y, n4, r)]
    y = [yy + m + _bdot(yy, m) for yy, m in zip(y, n8)]
    for mask in (blk32 & jnp.logical_not(blk16), jnp.logical_not(blk32)):
        ls = [jnp.where(mask, a, 0.0) for a in mats]
        ms = [l + _bdot(l, yy) for l, yy in zip(ls, y)]
        y = [yy - m - _bdot(yy, m) for yy, m in zip(y, ms)]
    return y


def _delta_kernel(qkv_ref, gb_ref, go_ref, cw_ref, ng_ref, o_ref, s_ref, xc_ref,
                  *, nb, heads, dk, dv, ch, ksz, halo):
    c = pl.program_id(1)
    qk_w = heads * dk

    @pl.when(c == 0)
    def _():
        s_ref[...] = jnp.zeros(s_ref.shape, F32)
        xc_ref[:, 0:halo, :] = jnp.zeros((nb, halo, xc_ref.shape[2]), BF16)

    @pl.when(c > 0)
    def _():
        xc_ref[:, 0:halo, :] = xc_ref[:, ch:ch + halo, :]

    xc_ref[:, halo:halo + ch, :] = qkv_ref[...]
    row = _iota((ch, ch), 0)
    col = _iota((ch, ch), 1)
    incl = row >= col
    strict = row > col
    tril = incl.astype(BF16)
    triu = (row <= col).astype(BF16)
    sr = _iota(((ksz - 1) * ch, halo + ch), 0)
    sc = _iota(((ksz - 1) * ch, halo + ch), 1)
    shift = (sc == (sr & (ch - 1)) + (sr >> (ch.bit_length() - 1)) + (halo - (ksz - 1))).astype(BF16)

    qn, kn, kb, vb, gamma, eg, ekd, elast = [], [], [], [], [], [], [], []
    shifted = [_dot(shift, xc_ref[b]) for b in range(nb)]
    for b in range(nb):
        acc = xc_ref[b, halo:halo + ch, :].astype(F32) * cw_ref[ksz - 1:ksz, :]
        for t in range(ksz - 1):
            acc = acc + shifted[b][t * ch:(t + 1) * ch, :] * cw_ref[t:t + 1, :]
        qkv = _silu(acc)
        gb = gb_ref[b]
        parts = _split3(gb)
        gc = sum(_dot(tril, p) for p in parts)
        gct = sum(_dot_tn(p, triu) for p in parts)
        glast = gc[ch - 1:ch, :]
        e_g = jnp.exp(gc)
        e_kd = jnp.exp(glast - gc)
        e_last = jnp.exp(glast)
        for h in range(heads):
            q = qkv[:, h * dk:(h + 1) * dk]
            k = qkv[:, qk_w + h * dk:qk_w + (h + 1) * dk]
            v = qkv[:, 2 * qk_w + h * dv:2 * qk_w + (h + 1) * dv]
            qn.append(q * lax.rsqrt(jnp.sum(q * q, axis=-1, keepdims=True) + EPS) * (dk ** -0.5))
            kn.append(k * lax.rsqrt(jnp.sum(k * k, axis=-1, keepdims=True) + EPS))
            beta = gb[:, heads + h:heads + h + 1]
            kb.append(kn[-1] * beta)
            vb.append(v * beta)
            diff = gc[:, h:h + 1] - gct[h:h + 1, :]
            gamma.append(jnp.where(incl, jnp.exp(jnp.where(incl, diff, 0.0)), 0.0))
            eg.append(e_g[:, h:h + 1])
            ekd.append(e_kd[:, h:h + 1])
            elast.append(e_last[:, h:h + 1])

    n = nb * heads
    kq = [_dot_nt(jnp.concatenate([kb[i], qn[i]], axis=0).astype(BF16), kn[i].astype(BF16))
          for i in range(n)]
    a = [jnp.where(strict, kq[i][:ch] * gamma[i], 0.0) for i in range(n)]
    qk = [kq[i][ch:] * gamma[i] for i in range(n)]
    y = _unit_lower_inverses_minus_eye(a, row, col)
    rhs = [jnp.concatenate([vb[i], kb[i] * eg[i]], axis=-1) for i in range(n)]
    uw = [rhs[i] + _bdot(y[i], rhs[i]) for i in range(n)]
    s = [s_ref[i // heads, i % heads] for i in range(n)]
    ws = [_bdot(jnp.concatenate([uw[i][:, dv:], qn[i] * eg[i]], axis=0), s[i]) for i in range(n)]
    v_new = [uw[i][:, :dv] - ws[i][:ch] for i in range(n)]
    o = [ws[i][ch:] + _bdot(qk[i], v_new[i]) for i in range(n)]
    s_new = [s[i] * elast[i] + _dot_tn((kn[i] * ekd[i]).astype(BF16), v_new[i].astype(BF16))
             for i in range(n)]
    for i in range(n):
        b, h = i // heads, i % heads
        s_ref[b, h] = s_new[i]
        on = _rms(o[i], ng_ref[...])
        o_ref[b, :, h * dv:(h + 1) * dv] = (
            on * go_ref[b, :, h * dv:(h + 1) * dv].astype(F32)).astype(BF16)


def _mid_kernel(h_ref, o_ref, ga_ref, ybg_ref, wdn_ref, wout_ref, g2_ref, wr_ref, br_ref,
                h2_ref, hn_ref, topi_ref, topg_ref, *, tm, n_exp, parts):
    rs = tm // parts
    sl = [pl.ds(p * rs, rs) for p in range(parts)]
    ya = [_dot(o_ref[s, :], wdn_ref[...]) for s in sl]
    y = [ga_ref[s, :].astype(F32) * a + ybg_ref[s, :].astype(F32) for s, a in zip(sl, ya)]
    h2 = [h_ref[s, :] + _dot(v.astype(BF16), wout_ref[...]) for s, v in zip(sl, y)]
    hn = [_rms(v, g2_ref[...]) for v in h2]
    for s, v, w in zip(sl, h2, hn):
        h2_ref[s, :] = v
        hn_ref[s, :] = w
    logits = [_dot(v.astype(BF16), wr_ref[...]) + br_ref[...] for v in hn]
    lane = _iota((rs, LANES), 1)
    lane_f = lane.astype(F32)
    l = [jnp.where(lane < n_exp, v, -jnp.inf) for v in logits]
    vals = [[] for _ in range(parts)]
    idxs = [[] for _ in range(parts)]
    for _ in range(TOP_K):
        m = [jnp.max(v, axis=-1, keepdims=True) for v in l]
        idx = [jnp.min(jnp.where(v == mm, lane_f, float(LANES)), axis=-1, keepdims=True)
               for v, mm in zip(l, m)]
        l = [jnp.where(lane_f == ii, -jnp.inf, v) for v, ii in zip(l, idx)]
        for p in range(parts):
            vals[p].append(m[p])
            idxs[p].append(idx[p])
    for p in range(parts):
        es = [jnp.exp(v - vals[p][0]) for v in vals[p]]
        den = es[0]
        for e in es[1:]:
            den = den + e
        topi = jnp.zeros((rs, LANES), F32)
        topg = jnp.zeros((rs, LANES), F32)
        for k in range(TOP_K):
            topi = jnp.where(lane == k, idxs[p][k], topi)
            topg = jnp.where(lane == k, es[k] / den, topg)
        topi_ref[sl[p], :] = topi.astype(I32)
        topg_ref[sl[p], :] = topg


def _choice_mask(topi, lane, n_exp):
    m = jnp.zeros(topi.shape, F32)
    for k in range(TOP_K):
        m = m + (lane == topi[:, k:k + 1]).astype(F32)
    return jnp.where(lane < n_exp, m, 0.0)


def _plan_kernel(topi_ref, pos_ref, n8_ref, off8_ref, carry8_ref, tot8_ref, carry_ref, *, tm, n_exp):
    i = pl.program_id(0)

    @pl.when(i == 0)
    def _():
        carry_ref[...] = jnp.zeros(carry_ref.shape, F32)

    lane = _iota((tm, LANES), 1)
    topi = topi_ref[...]
    m = _choice_mask(topi, lane, n_exp)
    tri = (_iota((tm, tm), 0) >= _iota((tm, tm), 1)).astype(BF16)
    cum = _dot(tri, m.astype(BF16))
    n8 = jnp.floor((cum[tm - 1:tm, :] + (SUBLANES - 1)) * (1.0 / SUBLANES)) * SUBLANES
    before = (_iota((LANES, LANES), 0) < _iota((LANES, LANES), 1)).astype(BF16)
    off8 = _dot(jnp.broadcast_to(n8, (SUBLANES, LANES)).astype(BF16), before)[0:1]
    posf = cum - m + off8
    pos = jnp.zeros((tm, LANES), F32)
    for k in range(TOP_K):
        pk = jnp.sum(jnp.where(lane == topi[:, k:k + 1], posf, 0.0), axis=-1, keepdims=True)
        pos = jnp.where(lane == k, pk, pos)
    pos_ref[...] = pos.astype(I32)
    carry = carry_ref[...]
    n8_ref[0] = n8.astype(I32)
    off8_ref[0] = off8.astype(I32)
    carry8_ref[0] = carry.astype(I32)
    carry = carry + n8
    carry_ref[...] = carry
    tot8_ref[...] = carry


def _pow2_floor(n):
    return 1 << (n.bit_length() - 1)


def _strip_pieces(n8, max_rows):
    pieces = []
    b = max_rows
    while b >= SUBLANES:
        pieces.append(((n8 & b) != 0, n8 & ~(2 * b - 1), b))
        b //= 2
    return pieces


def _wait_rows(total, make_copy, max_rows):
    b = max_rows
    while b >= SUBLANES:
        @pl.when((total & b) != 0)
        def _():
            make_copy(b).wait()
        b //= 2


def _aligned(start, rows):
    return pl.ds(pl.multiple_of(start, SUBLANES), rows)


def _dispatch_kernel(pend_ref, padded_ref, n8_ref, off8_ref, dbase_ref, hn_ref, pos_ref, xs_ref,
                     zbuf_ref, sbuf_ref, tot_ref, zsem, sems, *, tm, width, n_exp, n_steps, n_blocks):
    i = pl.program_id(0)
    slot = i % 2

    def zero_block(start):
        return pltpu.make_async_copy(zbuf_ref, xs_ref.at[_aligned(start, MOE_BLOCK), :], zsem)

    @pl.when(i == 0)
    def _():
        zbuf_ref[...] = jnp.zeros(zbuf_ref.shape, F32)
        for e in range(n_exp):
            @pl.when(padded_ref[e] > 0)
            def _():
                zero_block(pend_ref[e] - MOE_BLOCK).start()
        first_unused = pend_ref[n_exp - 1] // MOE_BLOCK

        def start_tail(j, carry):
            zero_block(j * MOE_BLOCK).start()
            return carry

        lax.fori_loop(first_unused, n_blocks, start_tail, 0)
        for e in range(n_exp):
            @pl.when(padded_ref[e] > 0)
            def _():
                zero_block(0).wait()

        def wait_tail(j, carry):
            zero_block(0).wait()
            return carry

        lax.fori_loop(first_unused, n_blocks, wait_tail, 0)

    def strip_wait(s):
        _wait_rows(tot_ref[s], lambda b: pltpu.make_async_copy(
            sbuf_ref.at[s, pl.ds(0, b), :], xs_ref.at[pl.ds(0, b), :], sems.at[s]), _pow2_floor(width))

    @pl.when(i >= 2)
    def _():
        strip_wait(slot)

    pos = pos_ref[...]
    lane = _iota((tm, width), 1)
    sel = lane == pos[:, 0:1]
    for k in range(1, TOP_K):
        sel = jnp.logical_or(sel, lane == pos[:, k:k + 1])
    sbuf_ref[slot] = _dot_tn(sel.astype(BF16), hn_ref[...].astype(BF16))

    total = 0
    for e in range(n_exp):
        n8, so, do = n8_ref[e], off8_ref[e], dbase_ref[e]
        total = total + n8
        for cond, ofs, rows in _strip_pieces(n8, _pow2_floor(tm)):
            @pl.when(cond)
            def _():
                pltpu.make_async_copy(sbuf_ref.at[slot, _aligned(so + ofs, rows), :],
                                      xs_ref.at[_aligned(do + ofs, rows), :], sems.at[slot]).start()
    tot_ref[slot] = total

    @pl.when(i == n_steps - 1)
    def _():
        if n_steps >= 2:
            strip_wait(1 - slot)
        strip_wait(slot)


def _expert_kernel(be_ref, nbu_ref, x_ref, wgu_ref, bgu_ref, wd_ref, bd_ref, y_ref, wgu16_ref, wd16_ref,
                   *, d_ff):
    i = pl.program_id(0)
    used = i < nbu_ref[0]
    new_expert = jnp.logical_or(i == 0, be_ref[i] != be_ref[jnp.maximum(i - 1, 0)])

    @pl.when(jnp.logical_and(used, new_expert))
    def _():
        wgu16_ref[...] = wgu_ref[0].astype(BF16)
        wd16_ref[...] = wd_ref[0].astype(BF16)

    @pl.when(used)
    def _():
        gu = _dot(x_ref[...].astype(BF16), wgu16_ref[...]) + bgu_ref[0]
        gate = jnp.minimum(gu[:, :d_ff], SWIGLU_LIMIT)
        up = jnp.clip(gu[:, d_ff:], -SWIGLU_LIMIT, SWIGLU_LIMIT)
        act = (up + 1.0) * gate * _sigmoid(SWIGLU_ALPHA * gate)
        y_ref[...] = _dot(act.astype(BF16), wd16_ref[...]) + bd_ref[0]

    @pl.when(jnp.logical_not(used))
    def _():
        y_ref[...] = jnp.zeros(y_ref.shape, F32)


def _combine_kernel(n8c_ref, off8c_ref, dbc_ref, n8n_ref, off8n_ref, dbn_ref, pos_ref, h2_ref, topg_ref, fg_ref,
                    ys_ref, out_ref, gbuf_ref, tot_ref, sems, *, tm, width, n_exp, n_steps):
    i = pl.program_id(0)
    slot = i % 2

    def fetch(n8_ref, off8_ref, db_ref, s):
        total = 0
        for e in range(n_exp):
            n8, so, do = n8_ref[e], off8_ref[e], db_ref[e]
            total = total + n8
            for cond, ofs, rows in _strip_pieces(n8, _pow2_floor(tm)):
                @pl.when(cond)
                def _():
                    pltpu.make_async_copy(ys_ref.at[_aligned(do + ofs, rows), :],
                                          gbuf_ref.at[s, _aligned(so + ofs, rows), :], sems.at[s]).start()
        tot_ref[s] = total

    @pl.when(i == 0)
    def _():
        gbuf_ref[...] = jnp.zeros(gbuf_ref.shape, F32)
        fetch(n8c_ref, off8c_ref, dbc_ref, 0)

    @pl.when(i + 1 < n_steps)
    def _():
        fetch(n8n_ref, off8n_ref, dbn_ref, 1 - slot)

    _wait_rows(tot_ref[slot], lambda b: pltpu.make_async_copy(
        ys_ref.at[pl.ds(0, b), :], gbuf_ref.at[slot, pl.ds(0, b), :], sems.at[slot]), _pow2_floor(width))
    pos = pos_ref[...]
    topg = topg_ref[...]
    lane = _iota((tm, width), 1)
    g = jnp.zeros((tm, width), F32)
    for k in range(TOP_K):
        g = g + jnp.where(lane == pos[:, k:k + 1], topg[:, k:k + 1], 0.0)
    moe = _dot(g.astype(BF16), gbuf_ref[slot].astype(BF16))
    out_ref[...] = _rms(h2_ref[...] + moe, fg_ref[...])


def _pad_lanes(v, fill=0.0):
    v = v.astype(F32)
    return jnp.concatenate([v, jnp.full((LANES - v.shape[0],), fill, F32)])[None, :]


def kernel(x, meta_tokens, norm_mix_g, w_in, conv_qkv_w, a_log, dt_bias, dn_norm_g, w_dn_out, b_glu,
           conv_dw_w, conv_dw_b, conv_ln_g, conv_ln_b, w_conv_out, b_conv_out, w_merge, b_merge, w_out,
           norm_ffn_g, w_router, b_router, w_gate_up, b_gate_up, w_down, b_down, final_norm_g):
    bsz, seq, d = x.shape
    depth = w_in.shape[0]
    n_meta = meta_tokens.shape[0]
    heads = a_log.shape[1]
    vw = w_dn_out.shape[1]
    qk_w = (conv_qkv_w.shape[2] - vw) // 2
    dk, dv = qk_w // heads, vw // heads
    ksz = conv_qkv_w.shape[1]
    kw, conv_ch = conv_dw_w.shape[1], conv_dw_w.shape[2]
    n_exp, d_ff = w_gate_up.shape[1], w_down.shape[2]
    meta_pad = CHUNK - n_meta
    lp = seq + n_meta + meta_pad
    assert depth == 1 and lp % CHUNK == 0 and dk == LANES and dv == LANES and d % LANES == 0
    assert 2 * heads <= LANES and n_exp <= LANES and kw - 1 <= 32 and ksz - 1 <= SUBLANES
    rows = bsz * lp
    n_s = d // LANES

    meta = jnp.broadcast_to(meta_tokens[None].astype(x.dtype), (bsz, n_meta, d))
    hp = jnp.concatenate([jnp.zeros((bsz, meta_pad, d), x.dtype), meta, x], axis=1).reshape(rows, d)

    wi = w_in[0]
    o_a = 2 * qk_w + vw
    o_go = o_a + 2 * heads
    o_glu = o_go + vw
    w_qkv = wi[:, :o_a].astype(BF16)
    w_ab = jnp.pad(wi[:, o_a:o_go], ((0, 0), (0, LANES - 2 * heads))).astype(BF16)
    w_go = wi[:, o_go:o_glu].astype(BF16)
    w_glu = wi[:, o_glu:].astype(BF16)
    row2 = lambda v: v.astype(F32).reshape(1, -1)

    tm1 = _pick_tile(lp, 320, BF16_ROWS)
    nt1 = lp // tm1
    rspec = lambda cols, tm, nt: pl.BlockSpec((tm, cols), lambda b, j: (b * nt + j, 0))
    full = lambda a: pl.BlockSpec(a.shape, lambda *_: (0,) * a.ndim)
    front_in = [hp, row2(norm_mix_g[0]), w_qkv, w_ab, w_go, w_glu, w_merge[0].astype(BF16),
                row2(b_glu[0]), row2(b_merge[0]), _pad_lanes(a_log[0]), _pad_lanes(dt_bias[0])]
    qkv_pre, gbeta, go_act, u_glu, gates = pl.pallas_call(
        functools.partial(_front_kernel, tm=tm1, meta_pad=meta_pad, heads=heads, conv_ch=conv_ch),
        grid=(bsz, nt1),
        in_specs=[rspec(d, tm1, nt1)] + [full(a) for a in front_in[1:]],
        out_specs=[rspec(o_a, tm1, nt1), rspec(LANES, tm1, nt1), rspec(vw, tm1, nt1),
                   rspec(conv_ch, tm1, nt1), rspec(2 * d, tm1, nt1)],
        out_shape=[jax.ShapeDtypeStruct((rows, o_a), BF16), jax.ShapeDtypeStruct((rows, LANES), F32),
                   jax.ShapeDtypeStruct((rows, vw), BF16), jax.ShapeDtypeStruct((rows, conv_ch), F32),
                   jax.ShapeDtypeStruct((rows, 2 * d), BF16)],
        compiler_params=_params(("arbitrary", "arbitrary")),
        name="front",
    )(*front_in)

    tmc = _pick_tile(lp, 320, 32)
    ntc = lp // tmc
    halo = 32
    conv_in = [u_glu, gates, conv_dw_w[0].astype(F32), row2(conv_dw_b[0]), row2(conv_ln_g[0]),
               row2(conv_ln_b[0]), w_conv_out[0].astype(BF16), row2(b_conv_out[0])]
    ybg = pl.pallas_call(
        functools.partial(_conv_kernel, tm=tmc, kw=kw, halo=halo, rb=32),
        grid=(bsz, ntc),
        in_specs=[rspec(conv_ch, tmc, ntc),
                  pl.BlockSpec((tmc, d), lambda b, j: (b * ntc + j, 1))] + [full(a) for a in conv_in[2:]],
        out_specs=rspec(d, tmc, ntc),
        out_shape=jax.ShapeDtypeStruct((rows, d), BF16),
        scratch_shapes=[pltpu.VMEM((tmc + halo, conv_ch), F32), pltpu.VMEM((tmc, conv_ch), F32)],
        compiler_params=_params(("arbitrary", "arbitrary")),
        name="convmod",
    )(*conv_in)

    nc = lp // CHUNK
    nbd = max(n for n in (1, 2, 4) if bsz % n == 0)
    cspec = lambda cols: pl.BlockSpec((nbd, CHUNK, cols), lambda b, c: (b, c, 0))
    delta_in = [qkv_pre.reshape(bsz, lp, o_a), gbeta.reshape(bsz, lp, LANES), go_act.reshape(bsz, lp, vw),
                conv_qkv_w[0].astype(F32), row2(dn_norm_g[0])]
    o_dn = pl.pallas_call(
        functools.partial(_delta_kernel, nb=nbd, heads=heads, dk=dk, dv=dv, ch=CHUNK, ksz=ksz, halo=BF16_ROWS),
        grid=(bsz // nbd, nc),
        in_specs=[cspec(o_a), cspec(LANES), cspec(vw), full(delta_in[3]), full(delta_in[4])],
        out_specs=cspec(vw),
        out_shape=jax.ShapeDtypeStruct((bsz, lp, vw), BF16),
        scratch_shapes=[pltpu.VMEM((nbd, heads, dk, dv), F32),
                        pltpu.VMEM((nbd, CHUNK + BF16_ROWS, o_a), BF16)],
        compiler_params=_params(("arbitrary", "arbitrary")),
        name="delta",
    )(*delta_in).reshape(rows, vw)

    tm3 = _pick_tile(lp, 320, BF16_ROWS)
    nt3 = lp // tm3
    w_r = jnp.pad(w_router[0], ((0, 0), (0, LANES - n_exp))).astype(BF16)
    mid_in = [hp, o_dn, gates, ybg, w_dn_out[0].astype(BF16), w_out[0].astype(BF16),
              row2(norm_ffn_g[0]), w_r, _pad_lanes(b_router[0])]
    h2, hn2, topi, topg = pl.pallas_call(
        functools.partial(_mid_kernel, tm=tm3, n_exp=n_exp, parts=2 if tm3 % (2 * BF16_ROWS) == 0 else 1),
        grid=(bsz, nt3),
        in_specs=[rspec(d, tm3, nt3), rspec(vw, tm3, nt3), rspec(d, tm3, nt3), rspec(d, tm3, nt3)]
        + [full(a) for a in mid_in[4:]],
        out_specs=[rspec(d, tm3, nt3), rspec(d, tm3, nt3), rspec(LANES, tm3, nt3), rspec(LANES, tm3, nt3)],
        out_shape=[jax.ShapeDtypeStruct((rows, d), F32), jax.ShapeDtypeStruct((rows, d), F32),
                   jax.ShapeDtypeStruct((rows, LANES), I32), jax.ShapeDtypeStruct((rows, LANES), F32)],
        compiler_params=_params(("arbitrary", "arbitrary")),
        name="mid",
    )(*mid_in)

    tmd = _pick_tile(rows, TOKEN_TILE, SUBLANES)
    ntd = rows // tmd
    width = tmd * TOP_K + n_exp * SUBLANES
    tspec = pl.BlockSpec((tmd, LANES), lambda i, *_: (i, 0))
    tab = pl.BlockSpec((1, 1, LANES), lambda i: (i, 0, 0))
    tab_shape = jax.ShapeDtypeStruct((ntd, 1, LANES), I32)
    pos, n8_t, off8_t, carry8_t, tot8 = pl.pallas_call(
        functools.partial(_plan_kernel, tm=tmd, n_exp=n_exp),
        grid=(ntd,),
        in_specs=[tspec],
        out_specs=[tspec, tab, tab, tab, pl.BlockSpec((1, LANES), lambda i: (0, 0))],
        out_shape=[jax.ShapeDtypeStruct((rows, LANES), I32), tab_shape, tab_shape, tab_shape,
                   jax.ShapeDtypeStruct((1, LANES), F32)],
        scratch_shapes=[pltpu.VMEM((1, LANES), F32)],
        compiler_params=_params(("arbitrary",)),
        name="plan",
    )(topi)

    n_blocks = -(-(rows * TOP_K + n_exp * (SUBLANES - 1) * ntd) // MOE_BLOCK) + n_exp
    cap = n_blocks * MOE_BLOCK
    counts = tot8[0, :n_exp].astype(I32)
    padded = (counts + MOE_BLOCK - 1) // MOE_BLOCK * MOE_BLOCK
    pend = jnp.cumsum(padded).astype(I32)
    pstart = pend - padded
    n_used = (pend[-1] // MOE_BLOCK).astype(I32).reshape(1)
    block_start = jnp.arange(n_blocks, dtype=I32) * MOE_BLOCK
    block_e = jnp.minimum(jnp.sum((pend[None, :] <= block_start[:, None]).astype(I32), axis=1), n_exp - 1)
    dbase = (carry8_t.reshape(ntd, LANES) + jnp.pad(pstart, (0, LANES - n_exp))[None, :]).reshape(ntd * LANES)
    n8_f = n8_t.reshape(ntd * LANES)
    off8_f = off8_t.reshape(ntd * LANES)

    anyspec = pl.BlockSpec(memory_space=pl.ANY)
    stab = lambda f: pl.BlockSpec((LANES,), lambda i, *_: (f(i),), memory_space=pltpu.SMEM)
    cur = lambda i: i
    xs = pl.pallas_call(
        functools.partial(_dispatch_kernel, tm=tmd, width=width, n_exp=n_exp, n_steps=ntd, n_blocks=n_blocks),
        grid_spec=pltpu.PrefetchScalarGridSpec(
            num_scalar_prefetch=2, grid=(ntd,),
            in_specs=[stab(cur), stab(cur), stab(cur), pl.BlockSpec((tmd, d), lambda i, *_: (i, 0)), tspec],
            out_specs=anyspec,
            scratch_shapes=[pltpu.VMEM((MOE_BLOCK, d), F32), pltpu.VMEM((2, width, d), F32),
                            pltpu.SMEM((2,), I32), pltpu.SemaphoreType.DMA(()), pltpu.SemaphoreType.DMA((2,))]),
        out_shape=jax.ShapeDtypeStruct((cap, d), F32),
        compiler_params=_params(("arbitrary",)),
        name="dispatch",
    )(pend, padded, n8_f, off8_f, dbase, hn2, pos)

    last_used = lambda i, nbu: jnp.minimum(i, nbu[0] - 1)
    ew = lambda a: pl.BlockSpec((1,) + a.shape[1:], lambda i, be, nbu: (be[i],) + (0,) * (a.ndim - 1))
    e_in = [w_gate_up[0], b_gate_up[0].astype(F32)[:, None, :], w_down[0], b_down[0].astype(F32)[:, None, :]]
    ys = pl.pallas_call(
        functools.partial(_expert_kernel, d_ff=d_ff),
        grid_spec=pltpu.PrefetchScalarGridSpec(
            num_scalar_prefetch=2, grid=(n_blocks,),
            in_specs=[pl.BlockSpec((MOE_BLOCK, d), lambda i, be, nbu: (last_used(i, nbu), 0))]
            + [ew(a) for a in e_in],
            out_specs=pl.BlockSpec((MOE_BLOCK, d), lambda i, be, nbu: (i, 0)),
            scratch_shapes=[pltpu.VMEM((d, 2 * d_ff), BF16), pltpu.VMEM((d_ff, d), BF16)]),
        out_shape=jax.ShapeDtypeStruct((cap, d), F32),
        compiler_params=_params(("arbitrary",)),
        name="experts",
    )(block_e, n_used, xs, *e_in)

    nxt = lambda i: jnp.minimum(i + 1, ntd - 1)
    out = pl.pallas_call(
        functools.partial(_combine_kernel, tm=tmd, width=width, n_exp=n_exp, n_steps=ntd),
        grid=(ntd,),
        in_specs=[stab(cur), stab(cur), stab(cur), stab(nxt), stab(nxt), stab(nxt), tspec,
                  pl.BlockSpec((tmd, d), lambda i: (i, 0)), tspec, pl.BlockSpec((1, d), lambda i: (0, 0)), anyspec],
        out_specs=pl.BlockSpec((tmd, d), lambda i: (i, 0)),
        out_shape=jax.ShapeDtypeStruct((rows, d), x.dtype),
        scratch_shapes=[pltpu.VMEM((2, width, d), F32), pltpu.SMEM((2,), I32), pltpu.SemaphoreType.DMA((2,))],
        compiler_params=_params(("arbitrary",)),
        name="combine",
    )(n8_f, off8_f, dbase, n8_f, off8_f, dbase, pos, h2, topg, row2(final_norm_g), ys)
    return out.reshape(bsz, lp, d)[:, meta_pad + n_meta:]
```

```python
import functools

import jax
import jax.numpy as jnp
from jax import lax
from jax.experimental import pallas as pl
from jax.experimental.pallas import tpu as pltpu

F32 = jnp.float32
BF16 = jnp.bfloat16
I32 = jnp.int32

CHUNK = 64
TOP_K = 4
MOE_BLOCK = 512
TOKEN_TILE = 256
SWIGLU_LIMIT = 7.0
SWIGLU_ALPHA = 1.702
EPS = 1e-6
LN_EPS = 1e-5
LANES = 128
SUBLANES = 8
BF16_ROWS = 16
VMEM_LIMIT = 56 * 1024 * 1024


def _dot(a, b):
    return jnp.dot(a, b, preferred_element_type=F32)


def _dot_hi(a, b):
    return jnp.dot(a, b, precision=lax.Precision.HIGHEST, preferred_element_type=F32)


def _dot_nt(a, b):
    return lax.dot_general(a, b, (((1,), (1,)), ((), ())), preferred_element_type=F32)


def _dot_tn(a, b):
    return lax.dot_general(a, b, (((0,), (0,)), ((), ())), preferred_element_type=F32)


def _sigmoid(x):
    return 1.0 / (1.0 + jnp.exp(-x))


def _silu(x):
    return x * _sigmoid(x)


def _softplus(x):
    return jnp.maximum(x, 0.0) + jnp.log(1.0 + jnp.exp(-jnp.abs(x)))


def _rms(x, g):
    return x * lax.rsqrt(jnp.mean(x * x, axis=-1, keepdims=True) + EPS) * g


def _iota(shape, dim):
    return lax.broadcasted_iota(I32, shape, dim)


def _pick_tile(n, target, mult):
    best = None
    for t in range(mult, min(n, target) + 1, mult):
        if n % t == 0:
            best = t
    assert best is not None, (n, target, mult)
    return best


def _params(sem, flags=None):
    return pltpu.CompilerParams(dimension_semantics=sem, vmem_limit_bytes=VMEM_LIMIT, flags=flags)


def _front_kernel(h_ref, g_ref, wqkv_ref, wab_ref, wgo_ref, wglu_ref, wmg_ref, bglu_ref, bmg_ref,
                  alog_ref, dtb_ref, qkv_ref, gb_ref, go_ref, u_ref, gates_ref,
                  *, tm, meta_pad, heads, conv_ch):
    j = pl.program_id(1)
    hn = _rms(h_ref[...], g_ref[...]).astype(BF16)
    qkv_ref[...] = _dot(hn, wqkv_ref[...]).astype(BF16)
    ab = _dot(hn, wab_ref[...])
    valid = (j * tm + _iota((tm, 1), 0)) >= meta_pad
    lane = _iota((1, LANES), 1)
    g = -jnp.exp(alog_ref[...]) * _softplus(ab + dtb_ref[...])
    gb = jnp.where(lane < heads, g, _sigmoid(ab))
    gb_ref[...] = jnp.where(valid, gb, 0.0)
    go_ref[...] = _silu(_dot(hn, wgo_ref[...])).astype(BF16)
    glu = _dot(hn, wglu_ref[...]) + bglu_ref[...]
    u = glu[:, :conv_ch] * _sigmoid(glu[:, conv_ch:])
    u_ref[...] = jnp.where(valid, u, 0.0).astype(BF16)
    gates_ref[...] = _sigmoid(_dot(hn, wmg_ref[...]) + bmg_ref[...]).astype(BF16)


def _conv_kernel(u_ref, gate_ref, w_ref, b_ref, lng_ref, lnb_ref, wo_ref, bo_ref, y_ref,
                 xcat_ref, xs_ref, cbuf_ref, *, tm, kw, halo, lead, rb):
    j = pl.program_id(1)
    base = lead + halo
    total = base + tm
    n_c = xcat_ref.shape[1]

    @pl.when(j == 0)
    def _():
        xcat_ref[0:base, :] = jnp.zeros((base, n_c), BF16)

    @pl.when(j > 0)
    def _():
        xcat_ref[lead:base, :] = xcat_ref[tm + lead:tm + base, :]

    xcat_ref[base:total, :] = u_ref[...]
    sr = _iota((SUBLANES * rb, lead + rb), 0)
    sc = _iota((SUBLANES * rb, lead + rb), 1)
    shift = (sc == (sr & (rb - 1)) + lead - (sr >> (rb.bit_length() - 1))).astype(BF16)
    bias = b_ref[...]
    for t0 in range(lead, total, rb):
        res = _dot(shift, xcat_ref[t0 - lead:t0 + rb, :])
        for r in range(SUBLANES):
            xs_ref[r, t0:t0 + rb, :] = res[r * rb:(r + 1) * rb, :]
        if t0 < base:
            continue
        i0 = t0
        blk = (i0 - base) // rb
        acc = jnp.zeros((rb, n_c), F32)
        for s in range(kw):
            a, r = divmod(s, SUBLANES)
            wt = w_ref[(kw - 1 - s) * SUBLANES:(kw - s) * SUBLANES, :]
            acc = acc + xs_ref[r, i0 - SUBLANES * a:i0 - SUBLANES * a + rb, :] * jnp.concatenate(
                [wt] * (rb // SUBLANES), axis=0)
        cbuf_ref[blk * rb:(blk + 1) * rb, :] = acc + bias
    c = cbuf_ref[...]
    mu = jnp.mean(c, axis=-1, keepdims=True)
    xc = c - mu
    ln = xc * lax.rsqrt(jnp.mean(xc * xc, axis=-1, keepdims=True) + LN_EPS) * lng_ref[...] + lnb_ref[...]
    yb = _dot(_silu(ln).astype(BF16), wo_ref[...]) + bo_ref[...]
    y_ref[...] = (gate_ref[...].astype(F32) * yb).astype(BF16)


def _bdot(a, b):
    return jnp.dot(a.astype(BF16), b.astype(BF16), preferred_element_type=F32)


def _split3(x):
    x1 = x.astype(BF16)
    r1 = x - x1.astype(F32)
    x2 = r1.astype(BF16)
    x3 = (r1 - x2.astype(F32)).astype(BF16)
    return x1, x2, x3


def _unit_lower_inverses_minus_eye(mats, row, col):
    ch = mats[0].shape[0]
    blk16 = (row >> 4) == (col >> 4)
    blk32 = (row >> 5) == (col >> 5)
    n1 = [jnp.where(blk16, a, 0.0) for a in mats]
    n2 = [_bdot(n, n) for n in n1]
    r = [_bdot(jnp.concatenate([n, m], axis=0), m) for n, m in zip(n1, n2)]
    n4 = [x[ch:] for x in r]
    y = [m - n - x[:ch] for n, m, x in zip(n1, n2, r)]
    r = [_bdot(jnp.concatenate([yy, m], axis=0), m) for yy, m in zip(y, n4)]
    n8 = [x[ch:] for x in r]
    y = [yy + m + x[:ch] for yy, m, x in zip(y, n4, r)]
    y = [yy + m + _bdot(yy, m) for yy, m in zip(y, n8)]
    for mask in (blk32 & jnp.logical_not(blk16), jnp.logical_not(blk32)):
        ls = [jnp.where(mask, a, 0.0) for a in mats]
        ms = [l + _bdot(l, yy) for l, yy in zip(ls, y)]
        y = [yy - m - _bdot(yy, m) for yy, m in zip(y, ms)]
    return y


def _delta_kernel(qkv_ref, gb_ref, go_ref, cw_ref, ng_ref, o_ref, s_ref, xc_ref,
                  *, nb, heads, dk, dv, ch, ksz, halo):
    c = pl.program_id(1)
    qk_w = heads * dk

    @pl.when(c == 0)
    def _():
        s_ref[...] = jnp.zeros(s_ref.shape, F32)
        xc_ref[:, 0:halo, :] = jnp.zeros((nb, halo, xc_ref.shape[2]), BF16)

    @pl.when(c > 0)
    def _():
        xc_ref[:, 0:halo, :] = xc_ref[:, ch:ch + halo, :]

    xc_ref[:, halo:halo + ch, :] = qkv_ref[...]
    row = _iota((ch, ch), 0)
    col = _iota((ch, ch), 1)
    incl = row >= col
    strict = row > col
    tril = incl.astype(BF16)
    triu = (row <= col).astype(BF16)
    sr = _iota(((ksz - 1) * ch, halo + ch), 0)
    sc = _iota(((ksz - 1) * ch, halo + ch), 1)
    shift = (sc == (sr & (ch - 1)) + (sr >> (ch.bit_length() - 1)) + (halo - (ksz - 1))).astype(BF16)

    qn, kn, kb, vb, gamma, eg, ekd, elast = [], [], [], [], [], [], [], []
    shifted = [_dot(shift, xc_ref[b]) for b in range(nb)]
    for b in range(nb):
        acc = xc_ref[b, halo:halo + ch, :].astype(F32) * cw_ref[ksz - 1:ksz, :]
        for t in range(ksz - 1):
            acc = acc + shifted[b][t * ch:(t + 1) * ch, :] * cw_ref[t:t + 1, :]
        qkv = _silu(acc)
        gb = gb_ref[b]
        parts = _split3(gb)
        gc = sum(_dot(tril, p) for p in parts)
        gct = sum(_dot_tn(p, triu) for p in parts)
        glast = gc[ch - 1:ch, :]
        e_g = jnp.exp(gc)
        e_kd = jnp.exp(glast - gc)
        e_last = jnp.exp(glast)
        for h in range(heads):
            q = qkv[:, h * dk:(h + 1) * dk]
            k = qkv[:, qk_w + h * dk:qk_w + (h + 1) * dk]
            v = qkv[:, 2 * qk_w + h * dv:2 * qk_w + (h + 1) * dv]
            qn.append(q * lax.rsqrt(jnp.sum(q * q, axis=-1, keepdims=True) + EPS) * (dk ** -0.5))
            kn.append(k * lax.rsqrt(jnp.sum(k * k, axis=-1, keepdims=True) + EPS))
            beta = gb[:, heads + h:heads + h + 1]
            kb.append(kn[-1] * beta)
            vb.append(v * beta)
            diff = gc[:, h:h + 1] - gct[h:h + 1, :]
            gamma.append(jnp.where(incl, jnp.exp(jnp.where(incl, diff, 0.0)), 0.0))
            eg.append(e_g[:, h:h + 1])
            ekd.append(e_kd[:, h:h + 1])
            elast.append(e_last[:, h:h + 1])

    n = nb * heads
    kq = [_dot_nt(jnp.concatenate([kb[i], qn[i]], axis=0).astype(BF16), kn[i].astype(BF16))
          for i in range(n)]
    a = [jnp.where(strict, kq[i][:ch] * gamma[i], 0.0) for i in range(n)]
    qk = [kq[i][ch:] * gamma[i] for i in range(n)]
    y = _unit_lower_inverses_minus_eye(a, row, col)
    rhs = [jnp.concatenate([vb[i], kb[i] * eg[i]], axis=-1) for i in range(n)]
    uw = [rhs[i] + _bdot(y[i], rhs[i]) for i in range(n)]
    s = [s_ref[i // heads, i % heads] for i in range(n)]
    ws = [_bdot(jnp.concatenate([uw[i][:, dv:], qn[i] * eg[i]], axis=0), s[i]) for i in range(n)]
    v_new = [uw[i][:, :dv] - ws[i][:ch] for i in range(n)]
    o = [ws[i][ch:] + _bdot(qk[i], v_new[i]) for i in range(n)]
    s_new = [s[i] * elast[i] + _dot_tn((kn[i] * ekd[i]).astype(BF16), v_new[i].astype(BF16))
             for i in range(n)]
    for i in range(n):
        b, h = i // heads, i % heads
        s_ref[b, h] = s_new[i]
        on = _rms(o[i], ng_ref[...])
        o_ref[b, :, h * dv:(h + 1) * dv] = (
            on * go_ref[b, :, h * dv:(h + 1) * dv].astype(F32)).astype(BF16)


def _mid_kernel(h_ref, o_ref, ga_ref, ybg_ref, wdn_ref, wout_ref, g2_ref, wr_ref, br_ref,
                h2_ref, hn_ref, topi_ref, topg_ref, *, tm, n_exp, parts):
    rs = tm // parts
    sl = [pl.ds(p * rs, rs) for p in range(parts)]
    ya = [_dot(o_ref[s, :], wdn_ref[...]) for s in sl]
    y = [ga_ref[s, :].astype(F32) * a + ybg_ref[s, :].astype(F32) for s, a in zip(sl, ya)]
    h2 = [h_ref[s, :] + _dot(v.astype(BF16), wout_ref[...]) for s, v in zip(sl, y)]
    hn = [_rms(v, g2_ref[...]) for v in h2]
    for s, v, w in zip(sl, h2, hn):
        h2_ref[s, :] = v
        hn_ref[s, :] = w
    logits = [_dot(v.astype(BF16), wr_ref[...]) + br_ref[...] for v in hn]
    lane = _iota((rs, LANES), 1)
    lane_f = lane.astype(F32)
    l = [jnp.where(lane < n_exp, v, -jnp.inf) for v in logits]
    vals = [[] for _ in range(parts)]
    idxs = [[] for _ in range(parts)]
    for _ in range(TOP_K):
        m = [jnp.max(v, axis=-1, keepdims=True) for v in l]
        idx = [jnp.min(jnp.where(v == mm, lane_f, float(LANES)), axis=-1, keepdims=True)
               for v, mm in zip(l, m)]
        l = [jnp.where(lane_f == ii, -jnp.inf, v) for v, ii in zip(l, idx)]
        for p in range(parts):
            vals[p].append(m[p])
            idxs[p].append(idx[p])
    for p in range(parts):
        es = [jnp.exp(v - vals[p][0]) for v in vals[p]]
        den = es[0]
        for e in es[1:]:
            den = den + e
        topi = jnp.zeros((rs, LANES), F32)
        topg = jnp.zeros((rs, LANES), F32)
        for k in range(TOP_K):
            topi = jnp.where(lane == k, idxs[p][k], topi)
            topg = jnp.where(lane == k, es[k] / den, topg)
        topi_ref[sl[p], :] = topi.astype(I32)
        topg_ref[sl[p], :] = topg


def _choice_mask(topi, lane, n_exp):
    m = jnp.zeros(topi.shape, F32)
    for k in range(TOP_K):
        m = m + (lane == topi[:, k:k + 1]).astype(F32)
    return jnp.where(lane < n_exp, m, 0.0)


def _plan_kernel(topi_ref, pos_ref, n8_ref, off8_ref, carry8_ref, tot8_ref, carry_ref, *, tm, tiles, n_exp):
    i = pl.program_id(0)

    @pl.when(i == 0)
    def _():
        carry_ref[...] = jnp.zeros(carry_ref.shape, F32)

    lane = _iota((tm, LANES), 1)
    tri = (_iota((tm, tm), 0) >= _iota((tm, tm), 1)).astype(BF16)
    before = (_iota((LANES, LANES), 0) < _iota((LANES, LANES), 1)).astype(BF16)
    carry = carry_ref[...]
    for j in range(tiles):
        rows = pl.ds(j * tm, tm)
        topi = topi_ref[rows, :]
        m = _choice_mask(topi, lane, n_exp)
        cum = _dot(tri, m.astype(BF16))
        n8 = jnp.floor((cum[tm - 1:tm, :] + (SUBLANES - 1)) * (1.0 / SUBLANES)) * SUBLANES
        off8 = _dot(jnp.broadcast_to(n8, (SUBLANES, LANES)).astype(BF16), before)[0:1]
        posf = cum - m + off8
        pos = jnp.zeros((tm, LANES), F32)
        for k in range(TOP_K):
            pk = jnp.sum(jnp.where(lane == topi[:, k:k + 1], posf, 0.0), axis=-1, keepdims=True)
            pos = jnp.where(lane == k, pk, pos)
        pos_ref[rows, :] = pos.astype(I32)
        n8_ref[j] = n8.astype(I32)
        off8_ref[j] = off8.astype(I32)
        carry8_ref[j] = carry.astype(I32)
        carry = carry + n8
    carry_ref[...] = carry
    tot8_ref[...] = carry


def _pow2_floor(n):
    return 1 << (n.bit_length() - 1)


def _strip_pieces(n8, max_rows):
    pieces = []
    b = max_rows
    while b >= SUBLANES:
        pieces.append(((n8 & b) != 0, n8 & ~(2 * b - 1), b))
        b //= 2
    return pieces


def _wait_rows(total, make_copy, max_rows):
    b = max_rows
    while b >= SUBLANES:
        @pl.when((total & b) != 0)
        def _():
            make_copy(b).wait()
        b //= 2


def _aligned(start, rows):
    return pl.ds(pl.multiple_of(start, SUBLANES), rows)


def _dispatch_kernel(pend_ref, padded_ref, n8_ref, off8_ref, dbase_ref, hn_ref, pos_ref, xs_ref,
                     zbuf_ref, sbuf_ref, tot_ref, zsem, sems, *, tm, width, n_exp, n_steps, n_blocks):
    i = pl.program_id(0)
    slot = i % 2

    def zero_block(start):
        return pltpu.make_async_copy(zbuf_ref, xs_ref.at[_aligned(start, MOE_BLOCK), :], zsem)

    @pl.when(i == 0)
    def _():
        zbuf_ref[...] = jnp.zeros(zbuf_ref.shape, F32)
        for e in range(n_exp):
            @pl.when(padded_ref[e] > 0)
            def _():
                zero_block(pend_ref[e] - MOE_BLOCK).start()
        first_unused = pend_ref[n_exp - 1] // MOE_BLOCK

        def start_tail(j, carry):
            zero_block(j * MOE_BLOCK).start()
            return carry

        lax.fori_loop(first_unused, n_blocks, start_tail, 0)
        for e in range(n_exp):
            @pl.when(padded_ref[e] > 0)
            def _():
                zero_block(0).wait()

        def wait_tail(j, carry):
            zero_block(0).wait()
            return carry

        lax.fori_loop(first_unused, n_blocks, wait_tail, 0)

    def strip_wait(s):
        _wait_rows(tot_ref[s], lambda b: pltpu.make_async_copy(
            sbuf_ref.at[s, pl.ds(0, b), :], xs_ref.at[pl.ds(0, b), :], sems.at[s]), _pow2_floor(width))

    @pl.when(i >= 2)
    def _():
        strip_wait(slot)

    pos = pos_ref[...]
    lane = _iota((tm, width), 1)
    sel = lane == pos[:, 0:1]
    for k in range(1, TOP_K):
        sel = jnp.logical_or(sel, lane == pos[:, k:k + 1])
    sbuf_ref[slot] = _dot_tn(sel.astype(BF16), hn_ref[...].astype(BF16))

    total = 0
    for e in range(n_exp):
        n8, so, do = n8_ref[e], off8_ref[e], dbase_ref[e]
        total = total + n8
        for cond, ofs, rows in _strip_pieces(n8, _pow2_floor(tm)):
            @pl.when(cond)
            def _():
                pltpu.make_async_copy(sbuf_ref.at[slot, _aligned(so + ofs, rows), :],
                                      xs_ref.at[_aligned(do + ofs, rows), :], sems.at[slot]).start()
    tot_ref[slot] = total

    @pl.when(i == n_steps - 1)
    def _():
        if n_steps >= 2:
            strip_wait(1 - slot)
        strip_wait(slot)


def _expert_kernel(be_ref, nbu_ref, x_ref, wgu_ref, bgu_ref, wd_ref, bd_ref, y_ref, wgu16_ref, wd16_ref,
                   *, d_ff):
    i = pl.program_id(0)
    used = i < nbu_ref[0]
    new_expert = jnp.logical_or(i == 0, be_ref[i] != be_ref[jnp.maximum(i - 1, 0)])

    @pl.when(jnp.logical_and(used, new_expert))
    def _():
        wgu16_ref[...] = wgu_ref[0].astype(BF16)
        wd16_ref[...] = wd_ref[0].astype(BF16)

    @pl.when(used)
    def _():
        gu = _dot(x_ref[...].astype(BF16), wgu16_ref[...]) + bgu_ref[0]
        gate = jnp.minimum(gu[:, :d_ff], SWIGLU_LIMIT)
        up = jnp.clip(gu[:, d_ff:], -SWIGLU_LIMIT, SWIGLU_LIMIT)
        act = (up + 1.0) * gate * _sigmoid(SWIGLU_ALPHA * gate)
        y_ref[...] = _dot(act.astype(BF16), wd16_ref[...]) + bd_ref[0]

    @pl.when(jnp.logical_not(used))
    def _():
        y_ref[...] = jnp.zeros(y_ref.shape, F32)


def _combine_kernel(n8c_ref, off8c_ref, dbc_ref, n8n_ref, off8n_ref, dbn_ref, pos_ref, h2_ref, topg_ref, fg_ref,
                    ys_ref, out_ref, gbuf_ref, tot_ref, sems, *, tm, width, n_exp, n_steps):
    i = pl.program_id(0)
    slot = i % 2

    def fetch(n8_ref, off8_ref, db_ref, s):
        total = 0
        for e in range(n_exp):
            n8, so, do = n8_ref[e], off8_ref[e], db_ref[e]
            total = total + n8
            for cond, ofs, rows in _strip_pieces(n8, _pow2_floor(tm)):
                @pl.when(cond)
                def _():
                    pltpu.make_async_copy(ys_ref.at[_aligned(do + ofs, rows), :],
                                          gbuf_ref.at[s, _aligned(so + ofs, rows), :], sems.at[s]).start()
        tot_ref[s] = total

    @pl.when(i == 0)
    def _():
        gbuf_ref[...] = jnp.zeros(gbuf_ref.shape, F32)
        fetch(n8c_ref, off8c_ref, dbc_ref, 0)

    @pl.when(i + 1 < n_steps)
    def _():
        fetch(n8n_ref, off8n_ref, dbn_ref, 1 - slot)

    _wait_rows(tot_ref[slot], lambda b: pltpu.make_async_copy(
        ys_ref.at[pl.ds(0, b), :], gbuf_ref.at[slot, pl.ds(0, b), :], sems.at[slot]), _pow2_floor(width))
    pos = pos_ref[...]
    topg = topg_ref[...]
    lane = _iota((tm, width), 1)
    g = jnp.zeros((tm, width), F32)
    for k in range(TOP_K):
        g = g + jnp.where(lane == pos[:, k:k + 1], topg[:, k:k + 1], 0.0)
    moe = _dot(g.astype(BF16), gbuf_ref[slot].astype(BF16))
    out_ref[...] = _rms(h2_ref[...] + moe, fg_ref[...])


def _pad_lanes(v, fill=0.0):
    v = v.astype(F32)
    return jnp.concatenate([v, jnp.full((LANES - v.shape[0],), fill, F32)])[None, :]


def kernel(x, meta_tokens, norm_mix_g, w_in, conv_qkv_w, a_log, dt_bias, dn_norm_g, w_dn_out, b_glu,
           conv_dw_w, conv_dw_b, conv_ln_g, conv_ln_b, w_conv_out, b_conv_out, w_merge, b_merge, w_out,
           norm_ffn_g, w_router, b_router, w_gate_up, b_gate_up, w_down, b_down, final_norm_g):
    bsz, seq, d = x.shape
    depth = w_in.shape[0]
    n_meta = meta_tokens.shape[0]
    heads = a_log.shape[1]
    vw = w_dn_out.shape[1]
    qk_w = (conv_qkv_w.shape[2] - vw) // 2
    dk, dv = qk_w // heads, vw // heads
    ksz = conv_qkv_w.shape[1]
    kw, conv_ch = conv_dw_w.shape[1], conv_dw_w.shape[2]
    n_exp, d_ff = w_gate_up.shape[1], w_down.shape[2]
    meta_pad = CHUNK - n_meta
    lp = seq + n_meta + meta_pad
    assert depth == 1 and lp % CHUNK == 0 and dk == LANES and dv == LANES and d % LANES == 0
    assert 2 * heads <= LANES and n_exp <= LANES and kw - 1 <= 32 and ksz - 1 <= SUBLANES
    rows = bsz * lp
    n_s = d // LANES

    meta = jnp.broadcast_to(meta_tokens[None].astype(x.dtype), (bsz, n_meta, d))
    hp = jnp.concatenate([jnp.zeros((bsz, meta_pad, d), x.dtype), meta, x], axis=1).reshape(rows, d)

    wi = w_in[0]
    o_a = 2 * qk_w + vw
    o_go = o_a + 2 * heads
    o_glu = o_go + vw
    w_qkv = wi[:, :o_a].astype(BF16)
    w_ab = jnp.pad(wi[:, o_a:o_go], ((0, 0), (0, LANES - 2 * heads))).astype(BF16)
    w_go = wi[:, o_go:o_glu].astype(BF16)
    w_glu = wi[:, o_glu:].astype(BF16)
    row2 = lambda v: v.astype(F32).reshape(1, -1)

    tm1 = _pick_tile(lp, 320, BF16_ROWS)
    nt1 = lp // tm1
    rspec = lambda cols, tm, nt: pl.BlockSpec((tm, cols), lambda b, j: (b * nt + j, 0))
    full = lambda a: pl.BlockSpec(a.shape, lambda *_: (0,) * a.ndim)
    front_in = [hp, row2(norm_mix_g[0]), w_qkv, w_ab, w_go, w_glu, w_merge[0].astype(BF16),
                row2(b_glu[0]), row2(b_merge[0]), _pad_lanes(a_log[0]), _pad_lanes(dt_bias[0])]
    qkv_pre, gbeta, go_act, u_glu, gates = pl.pallas_call(
        functools.partial(_front_kernel, tm=tm1, meta_pad=meta_pad, heads=heads, conv_ch=conv_ch),
        grid=(bsz, nt1),
        in_specs=[rspec(d, tm1, nt1)] + [full(a) for a in front_in[1:]],
        out_specs=[rspec(o_a, tm1, nt1), rspec(LANES, tm1, nt1), rspec(vw, tm1, nt1),
                   rspec(conv_ch, tm1, nt1), rspec(2 * d, tm1, nt1)],
        out_shape=[jax.ShapeDtypeStruct((rows, o_a), BF16), jax.ShapeDtypeStruct((rows, LANES), F32),
                   jax.ShapeDtypeStruct((rows, vw), BF16), jax.ShapeDtypeStruct((rows, conv_ch), BF16),
                   jax.ShapeDtypeStruct((rows, 2 * d), BF16)],
        compiler_params=_params(("arbitrary", "arbitrary")),
        name="front",
    )(*front_in)

    tmc = _pick_tile(lp, 320, 32)
    ntc = lp // tmc
    halo, lead = 32, BF16_ROWS
    conv_w8 = jnp.repeat(conv_dw_w[0].astype(F32), SUBLANES, axis=0)
    conv_in = [u_glu, gates, conv_w8, row2(conv_dw_b[0]), row2(conv_ln_g[0]),
               row2(conv_ln_b[0]), w_conv_out[0].astype(BF16), row2(b_conv_out[0])]
    ybg = pl.pallas_call(
        functools.partial(_conv_kernel, tm=tmc, kw=kw, halo=halo, lead=lead, rb=32),
        grid=(bsz, ntc),
        in_specs=[rspec(conv_ch, tmc, ntc),
                  pl.BlockSpec((tmc, d), lambda b, j: (b * ntc + j, 1))] + [full(a) for a in conv_in[2:]],
        out_specs=rspec(d, tmc, ntc),
        out_shape=jax.ShapeDtypeStruct((rows, d), BF16),
        scratch_shapes=[pltpu.VMEM((lead + halo + tmc, conv_ch), BF16),
                        pltpu.VMEM((SUBLANES, lead + halo + tmc, conv_ch), F32),
                        pltpu.VMEM((tmc, conv_ch), F32)],
        compiler_params=_params(("arbitrary", "arbitrary")),
        name="convmod",
    )(*conv_in)

    nc = lp // CHUNK
    nbd = max(n for n in (1, 2, 4) if bsz % n == 0)
    cspec = lambda cols: pl.BlockSpec((nbd, CHUNK, cols), lambda b, c: (b, c, 0))
    delta_in = [qkv_pre.reshape(bsz, lp, o_a), gbeta.reshape(bsz, lp, LANES), go_act.reshape(bsz, lp, vw),
                conv_qkv_w[0].astype(F32), row2(dn_norm_g[0])]
    o_dn = pl.pallas_call(
        functools.partial(_delta_kernel, nb=nbd, heads=heads, dk=dk, dv=dv, ch=CHUNK, ksz=ksz, halo=BF16_ROWS),
        grid=(bsz // nbd, nc),
        in_specs=[cspec(o_a), cspec(LANES), cspec(vw), full(delta_in[3]), full(delta_in[4])],
        out_specs=cspec(vw),
        out_shape=jax.ShapeDtypeStruct((bsz, lp, vw), BF16),
        scratch_shapes=[pltpu.VMEM((nbd, heads, dk, dv), F32),
                        pltpu.VMEM((nbd, CHUNK + BF16_ROWS, o_a), BF16)],
        compiler_params=_params(("arbitrary", "arbitrary")),
        name="delta",
    )(*delta_in).reshape(rows, vw)

    tm3 = _pick_tile(lp, 320, BF16_ROWS)
    nt3 = lp // tm3
    w_r = jnp.pad(w_router[0], ((0, 0), (0, LANES - n_exp))).astype(BF16)
    mid_in = [hp, o_dn, gates, ybg, w_dn_out[0].astype(BF16), w_out[0].astype(BF16),
              row2(norm_ffn_g[0]), w_r, _pad_lanes(b_router[0])]
    h2, hn2, topi, topg = pl.pallas_call(
        functools.partial(_mid_kernel, tm=tm3, n_exp=n_exp, parts=2 if tm3 % (2 * BF16_ROWS) == 0 else 1),
        grid=(bsz, nt3),
        in_specs=[rspec(d, tm3, nt3), rspec(vw, tm3, nt3), rspec(d, tm3, nt3), rspec(d, tm3, nt3)]
        + [full(a) for a in mid_in[4:]],
        out_specs=[rspec(d, tm3, nt3), rspec(d, tm3, nt3), rspec(LANES, tm3, nt3), rspec(LANES, tm3, nt3)],
        out_shape=[jax.ShapeDtypeStruct((rows, d), F32), jax.ShapeDtypeStruct((rows, d), F32),
                   jax.ShapeDtypeStruct((rows, LANES), I32), jax.ShapeDtypeStruct((rows, LANES), F32)],
        compiler_params=_params(("arbitrary", "arbitrary")),
        name="mid",
    )(*mid_in)

    tmd = _pick_tile(rows, TOKEN_TILE, SUBLANES)
    ntd = rows // tmd
    width = tmd * TOP_K + n_exp * SUBLANES
    tspec = pl.BlockSpec((tmd, LANES), lambda i, *_: (i, 0))
    ptiles = _pick_tile(ntd, 5, 1)
    pspec = pl.BlockSpec((ptiles * tmd, LANES), lambda i: (i, 0))
    tab = pl.BlockSpec((ptiles, 1, LANES), lambda i: (i, 0, 0))
    tab_shape = jax.ShapeDtypeStruct((ntd, 1, LANES), I32)
    pos, n8_t, off8_t, carry8_t, tot8 = pl.pallas_call(
        functools.partial(_plan_kernel, tm=tmd, tiles=ptiles, n_exp=n_exp),
        grid=(ntd // ptiles,),
        in_specs=[pspec],
        out_specs=[pspec, tab, tab, tab, pl.BlockSpec((1, LANES), lambda i: (0, 0))],
        out_shape=[jax.ShapeDtypeStruct((rows, LANES), I32), tab_shape, tab_shape, tab_shape,
                   jax.ShapeDtypeStruct((1, LANES), F32)],
        scratch_shapes=[pltpu.VMEM((1, LANES), F32)],
        compiler_params=_params(("arbitrary",)),
        name="plan",
    )(topi)

    n_blocks = -(-(rows * TOP_K + n_exp * (SUBLANES - 1) * ntd) // MOE_BLOCK) + n_exp
    cap = n_blocks * MOE_BLOCK
    counts = tot8[0, :n_exp].astype(I32)
    padded = (counts + MOE_BLOCK - 1) // MOE_BLOCK * MOE_BLOCK
    pend = jnp.cumsum(padded).astype(I32)
    pstart = pend - padded
    n_used = (pend[-1] // MOE_BLOCK).astype(I32).reshape(1)
    block_start = jnp.arange(n_blocks, dtype=I32) * MOE_BLOCK
    block_e = jnp.minimum(jnp.sum((pend[None, :] <= block_start[:, None]).astype(I32), axis=1), n_exp - 1)
    dbase = (carry8_t.reshape(ntd, LANES) + jnp.pad(pstart, (0, LANES - n_exp))[None, :]).reshape(ntd * LANES)
    n8_f = n8_t.reshape(ntd * LANES)
    off8_f = off8_t.reshape(ntd * LANES)

    anyspec = pl.BlockSpec(memory_space=pl.ANY)
    stab = lambda f: pl.BlockSpec((LANES,), lambda i, *_: (f(i),), memory_space=pltpu.SMEM)
    cur = lambda i: i
    xs = pl.pallas_call(
        functools.partial(_dispatch_kernel, tm=tmd, width=width, n_exp=n_exp, n_steps=ntd, n_blocks=n_blocks),
        grid_spec=pltpu.PrefetchScalarGridSpec(
            num_scalar_prefetch=2, grid=(ntd,),
            in_specs=[stab(cur), stab(cur), stab(cur), pl.BlockSpec((tmd, d), lambda i, *_: (i, 0)), tspec],
            out_specs=anyspec,
            scratch_shapes=[pltpu.VMEM((MOE_BLOCK, d), F32), pltpu.VMEM((2, width, d), F32),
                            pltpu.SMEM((2,), I32), pltpu.SemaphoreType.DMA(()), pltpu.SemaphoreType.DMA((2,))]),
        out_shape=jax.ShapeDtypeStruct((cap, d), F32),
        compiler_params=_params(("arbitrary",)),
        name="dispatch",
    )(pend, padded, n8_f, off8_f, dbase, hn2, pos)

    last_used = lambda i, nbu: jnp.minimum(i, nbu[0] - 1)
    ew = lambda a: pl.BlockSpec((1,) + a.shape[1:], lambda i, be, nbu: (be[i],) + (0,) * (a.ndim - 1))
    e_in = [w_gate_up[0], b_gate_up[0].astype(F32)[:, None, :], w_down[0], b_down[0].astype(F32)[:, None, :]]
    ys = pl.pallas_call(
        functools.partial(_expert_kernel, d_ff=d_ff),
        grid_spec=pltpu.PrefetchScalarGridSpec(
            num_scalar_prefetch=2, grid=(n_blocks,),
            in_specs=[pl.BlockSpec((MOE_BLOCK, d), lambda i, be, nbu: (last_used(i, nbu), 0))]
            + [ew(a) for a in e_in],
            out_specs=pl.BlockSpec((MOE_BLOCK, d), lambda i, be, nbu: (i, 0)),
            scratch_shapes=[pltpu.VMEM((d, 2 * d_ff), BF16), pltpu.VMEM((d_ff, d), BF16)]),
        out_shape=jax.ShapeDtypeStruct((cap, d), F32),
        compiler_params=_params(("arbitrary",)),
        name="experts",
    )(block_e, n_used, xs, *e_in)

    nxt = lambda i: jnp.minimum(i + 1, ntd - 1)
    out = pl.pallas_call(
        functools.partial(_combine_kernel, tm=tmd, width=width, n_exp=n_exp, n_steps=ntd),
        grid=(ntd,),
        in_specs=[stab(cur), stab(cur), stab(cur), stab(nxt), stab(nxt), stab(nxt), tspec,
                  pl.BlockSpec((tmd, d), lambda i: (i, 0)), tspec, pl.BlockSpec((1, d), lambda i: (0, 0)), anyspec],
        out_specs=pl.BlockSpec((tmd, d), lambda i: (i, 0)),
        out_shape=jax.ShapeDtypeStruct((rows, d), x.dtype),
        scratch_shapes=[pltpu.VMEM((2, width, d), F32), pltpu.SMEM((2,), I32), pltpu.SemaphoreType.DMA((2,))],
        compiler_params=_params(("arbitrary",)),
        name="combine",
    )(n8_f, off8_f, dbase, n8_f, off8_f, dbase, pos, h2, topg, row2(final_norm_g), ys)
    return out.reshape(bsz, lp, d)[:, meta_pad + n_meta:]
```

```python
import functools

import jax
import jax.numpy as jnp
from jax import lax
from jax.experimental import pallas as pl
from jax.experimental.pallas import tpu as pltpu

F32 = jnp.float32
BF16 = jnp.bfloat16
I32 = jnp.int32

CHUNK = 64
TOP_K = 4
MOE_BLOCK = 512
TOKEN_TILE = 256
SWIGLU_LIMIT = 7.0
SWIGLU_ALPHA = 1.702
EPS = 1e-6
LN_EPS = 1e-5
LANES = 128
SUBLANES = 8
BF16_ROWS = 16
VMEM_LIMIT = 56 * 1024 * 1024


def _dot(a, b):
    return jnp.dot(a, b, preferred_element_type=F32)


def _dot_hi(a, b):
    return jnp.dot(a, b, precision=lax.Precision.HIGHEST, preferred_element_type=F32)


def _dot_nt(a, b):
    return lax.dot_general(a, b, (((1,), (1,)), ((), ())), preferred_element_type=F32)


def _dot_tn(a, b):
    return lax.dot_general(a, b, (((0,), (0,)), ((), ())), preferred_element_type=F32)


def _sigmoid(x):
    return 1.0 / (1.0 + jnp.exp(-x))


def _silu(x):
    return x * _sigmoid(x)


def _softplus(x):
    return jnp.maximum(x, 0.0) + jnp.log(1.0 + jnp.exp(-jnp.abs(x)))


def _rms(x, g):
    return x * lax.rsqrt(jnp.mean(x * x, axis=-1, keepdims=True) + EPS) * g


def _iota(shape, dim):
    return lax.broadcasted_iota(I32, shape, dim)


def _pick_tile(n, target, mult):
    best = None
    for t in range(mult, min(n, target) + 1, mult):
        if n % t == 0:
            best = t
    assert best is not None, (n, target, mult)
    return best


def _params(sem, flags=None):
    return pltpu.CompilerParams(dimension_semantics=sem, vmem_limit_bytes=VMEM_LIMIT, flags=flags)


def _front_kernel(head_ref, x_ref, g_ref, wqkv_ref, wab_ref, wgo_ref, wglu_ref, wmg_ref, bglu_ref, bmg_ref,
                  alog_ref, dtb_ref, h_ref, qkv_ref, gb_ref, go_ref, u_ref, gates_ref,
                  *, tm, meta_pad, heads, conv_ch):
    j = pl.program_id(1)
    xb = x_ref[0]
    n_head = head_ref.shape[0]
    h = jnp.where(j == 0, jnp.concatenate([head_ref[...], xb[:tm - n_head]], axis=0), xb)
    h_ref[...] = h
    hn = _rms(h, g_ref[...]).astype(BF16)
    qkv_ref[...] = _dot(hn, wqkv_ref[...]).astype(BF16)
    ab = _dot(hn, wab_ref[...])
    valid = (j * tm + _iota((tm, 1), 0)) >= meta_pad
    lane = _iota((1, LANES), 1)
    g = -jnp.exp(alog_ref[...]) * _softplus(ab + dtb_ref[...])
    gb = jnp.where(lane < heads, g, _sigmoid(ab))
    gb_ref[...] = jnp.where(valid, gb, 0.0)
    go_ref[...] = _silu(_dot(hn, wgo_ref[...])).astype(BF16)
    glu = _dot(hn, wglu_ref[...]) + bglu_ref[...]
    u = glu[:, :conv_ch] * _sigmoid(glu[:, conv_ch:])
    u_ref[...] = jnp.where(valid, u, 0.0).astype(BF16)
    gates_ref[...] = _sigmoid(_dot(hn, wmg_ref[...]) + bmg_ref[...]).astype(BF16)


def _conv_kernel(u_ref, gate_ref, w_ref, b_ref, lng_ref, lnb_ref, wo_ref, bo_ref, y_ref,
                 xcat_ref, xs_ref, cbuf_ref, *, tm, kw, halo, lead, rb):
    j = pl.program_id(1)
    base = lead + halo
    total = base + tm
    n_c = xcat_ref.shape[1]

    @pl.when(j == 0)
    def _():
        xcat_ref[0:base, :] = jnp.zeros((base, n_c), BF16)

    @pl.when(j > 0)
    def _():
        xcat_ref[lead:base, :] = xcat_ref[tm + lead:tm + base, :]

    xcat_ref[base:total, :] = u_ref[...]
    sr = _iota((SUBLANES * rb, lead + rb), 0)
    sc = _iota((SUBLANES * rb, lead + rb), 1)
    shift = (sc == (sr & (rb - 1)) + lead - (sr >> (rb.bit_length() - 1))).astype(BF16)
    bias = b_ref[...]
    for t0 in range(lead, total, rb):
        res = _dot(shift, xcat_ref[t0 - lead:t0 + rb, :])
        for r in range(SUBLANES):
            xs_ref[r, t0:t0 + rb, :] = res[r * rb:(r + 1) * rb, :]
        if t0 < base:
            continue
        i0 = t0
        blk = (i0 - base) // rb
        acc = jnp.zeros((rb, n_c), F32)
        for s in range(kw):
            a, r = divmod(s, SUBLANES)
            wt = w_ref[(kw - 1 - s) * SUBLANES:(kw - s) * SUBLANES, :]
            acc = acc + xs_ref[r, i0 - SUBLANES * a:i0 - SUBLANES * a + rb, :] * jnp.concatenate(
                [wt] * (rb // SUBLANES), axis=0)
        cbuf_ref[blk * rb:(blk + 1) * rb, :] = acc + bias
    c = cbuf_ref[...]
    mu = jnp.mean(c, axis=-1, keepdims=True)
    xc = c - mu
    ln = xc * lax.rsqrt(jnp.mean(xc * xc, axis=-1, keepdims=True) + LN_EPS) * lng_ref[...] + lnb_ref[...]
    yb = _dot(_silu(ln).astype(BF16), wo_ref[...]) + bo_ref[...]
    y_ref[...] = (gate_ref[...].astype(F32) * yb).astype(BF16)


def _bdot(a, b):
    return jnp.dot(a.astype(BF16), b.astype(BF16), preferred_element_type=F32)


def _split3(x):
    x1 = x.astype(BF16)
    r1 = x - x1.astype(F32)
    x2 = r1.astype(BF16)
    x3 = (r1 - x2.astype(F32)).astype(BF16)
    return x1, x2, x3


def _unit_lower_inverses_minus_eye(mats, row, col):
    ch = mats[0].shape[0]
    blk16 = (row >> 4) == (col >> 4)
    blk32 = (row >> 5) == (col >> 5)
    n1 = [jnp.where(blk16, a, 0.0) for a in mats]
    n2 = [_bdot(n, n) for n in n1]
    r = [_bdot(jnp.concatenate([n, m], axis=0), m) for n, m in zip(n1, n2)]
    n4 = [x[ch:] for x in r]
    y = [m - n - x[:ch] for n, m, x in zip(n1, n2, r)]
    r = [_bdot(jnp.concatenate([yy, m], axis=0), m) for yy, m in zip(y, n4)]
    n8 = [x[ch:] for x in r]
    y = [yy + m + x[:ch] for yy, m, x in zip(y, n4, r)]
    y = [yy + m + _bdot(yy, m) for yy, m in zip(y, n8)]
    for mask in (blk32 & jnp.logical_not(blk16), jnp.logical_not(blk32)):
        ls = [jnp.where(mask, a, 0.0) for a in mats]
        ms = [l + _bdot(l, yy) for l, yy in zip(ls, y)]
        y = [yy - m - _bdot(yy, m) for yy, m in zip(y, ms)]
    return y


def _delta_kernel(qkv_ref, gb_ref, go_ref, cw_ref, ng_ref, o_ref, s_ref, xc_ref,
                  *, nb, heads, dk, dv, ch, ksz, halo):
    c = pl.program_id(1)
    qk_w = heads * dk

    @pl.when(c == 0)
    def _():
        s_ref[...] = jnp.zeros(s_ref.shape, F32)
        xc_ref[:, 0:halo, :] = jnp.zeros((nb, halo, xc_ref.shape[2]), BF16)

    @pl.when(c > 0)
    def _():
        xc_ref[:, 0:halo, :] = xc_ref[:, ch:ch + halo, :]

    xc_ref[:, halo:halo + ch, :] = qkv_ref[...]
    row = _iota((ch, ch), 0)
    col = _iota((ch, ch), 1)
    incl = row >= col
    strict = row > col
    tril = incl.astype(BF16)
    triu = (row <= col).astype(BF16)
    sr = _iota(((ksz - 1) * ch, halo + ch), 0)
    sc = _iota(((ksz - 1) * ch, halo + ch), 1)
    shift = (sc == (sr & (ch - 1)) + (sr >> (ch.bit_length() - 1)) + (halo - (ksz - 1))).astype(BF16)

    qn, kn, kb, vb, gamma, eg, ekd, elast = [], [], [], [], [], [], [], []
    shifted = [_dot(shift, xc_ref[b]) for b in range(nb)]
    for b in range(nb):
        acc = xc_ref[b, halo:halo + ch, :].astype(F32) * cw_ref[ksz - 1:ksz, :]
        for t in range(ksz - 1):
            acc = acc + shifted[b][t * ch:(t + 1) * ch, :] * cw_ref[t:t + 1, :]
        qkv = _silu(acc)
        gb = gb_ref[b]
        parts = _split3(gb)
        gc = sum(_dot(tril, p) for p in parts)
        gct = sum(_dot_tn(p, triu) for p in parts)
        glast = gc[ch - 1:ch, :]
        e_g = jnp.exp(gc)
        e_kd = jnp.exp(glast - gc)
        e_last = jnp.exp(glast)
        for h in range(heads):
            q = qkv[:, h * dk:(h + 1) * dk]
            k = qkv[:, qk_w + h * dk:qk_w + (h + 1) * dk]
            v = qkv[:, 2 * qk_w + h * dv:2 * qk_w + (h + 1) * dv]
            qn.append(q * lax.rsqrt(jnp.sum(q * q, axis=-1, keepdims=True) + EPS) * (dk ** -0.5))
            kn.append(k * lax.rsqrt(jnp.sum(k * k, axis=-1, keepdims=True) + EPS))
            beta = gb[:, heads + h:heads + h + 1]
            kb.append(kn[-1] * beta)
            vb.append(v * beta)
            diff = gc[:, h:h + 1] - gct[h:h + 1, :]
            gamma.append(jnp.where(incl, jnp.exp(jnp.where(incl, diff, 0.0)), 0.0))
            eg.append(e_g[:, h:h + 1])
            ekd.append(e_kd[:, h:h + 1])
            elast.append(e_last[:, h:h + 1])

    n = nb * heads
    kq = [_dot_nt(jnp.concatenate([kb[i], qn[i]], axis=0).astype(BF16), kn[i].astype(BF16))
          for i in range(n)]
    a = [jnp.where(strict, kq[i][:ch] * gamma[i], 0.0) for i in range(n)]
    qk = [kq[i][ch:] * gamma[i] for i in range(n)]
    y = _unit_lower_inverses_minus_eye(a, row, col)
    rhs = [jnp.concatenate([vb[i], kb[i] * eg[i]], axis=-1) for i in range(n)]
    uw = [rhs[i] + _bdot(y[i], rhs[i]) for i in range(n)]
    s = [s_ref[i // heads, i % heads] for i in range(n)]
    ws = [_bdot(jnp.concatenate([uw[i][:, dv:], qn[i] * eg[i]], axis=0), s[i]) for i in range(n)]
    v_new = [uw[i][:, :dv] - ws[i][:ch] for i in range(n)]
    o = [ws[i][ch:] + _bdot(qk[i], v_new[i]) for i in range(n)]
    s_new = [s[i] * elast[i] + _dot_tn((kn[i] * ekd[i]).astype(BF16), v_new[i].astype(BF16))
             for i in range(n)]
    for i in range(n):
        b, h = i // heads, i % heads
        s_ref[b, h] = s_new[i]
        on = _rms(o[i], ng_ref[...])
        o_ref[b, :, h * dv:(h + 1) * dv] = (
            on * go_ref[b, :, h * dv:(h + 1) * dv].astype(F32)).astype(BF16)


def _mid_kernel(h_ref, o_ref, ga_ref, ybg_ref, wdn_ref, wout_ref, g2_ref, wr_ref, br_ref,
                h2_ref, hn_ref, topi_ref, topg_ref, *, tm, n_exp, parts):
    rs = tm // parts
    sl = [pl.ds(p * rs, rs) for p in range(parts)]
    ya = [_dot(o_ref[s, :], wdn_ref[...]) for s in sl]
    y = [ga_ref[s, :].astype(F32) * a + ybg_ref[s, :].astype(F32) for s, a in zip(sl, ya)]
    h2 = [h_ref[s, :] + _dot(v.astype(BF16), wout_ref[...]) for s, v in zip(sl, y)]
    hn = [_rms(v, g2_ref[...]) for v in h2]
    for s, v, w in zip(sl, h2, hn):
        h2_ref[s, :] = v
        hn_ref[s, :] = w
    logits = [_dot(v.astype(BF16), wr_ref[...]) + br_ref[...] for v in hn]
    lane = _iota((rs, LANES), 1)
    lane_f = lane.astype(F32)
    l = [jnp.where(lane < n_exp, v, -jnp.inf) for v in logits]
    vals = [[] for _ in range(parts)]
    idxs = [[] for _ in range(parts)]
    for _ in range(TOP_K):
        m = [jnp.max(v, axis=-1, keepdims=True) for v in l]
        idx = [jnp.min(jnp.where(v == mm, lane_f, float(LANES)), axis=-1, keepdims=True)
               for v, mm in zip(l, m)]
        l = [jnp.where(lane_f == ii, -jnp.inf, v) for v, ii in zip(l, idx)]
        for p in range(parts):
            vals[p].append(m[p])
            idxs[p].append(idx[p])
    for p in range(parts):
        es = [jnp.exp(v - vals[p][0]) for v in vals[p]]
        den = es[0]
        for e in es[1:]:
            den = den + e
        topi = jnp.zeros((rs, LANES), F32)
        topg = jnp.zeros((rs, LANES), F32)
        for k in range(TOP_K):
            topi = jnp.where(lane == k, idxs[p][k], topi)
            topg = jnp.where(lane == k, es[k] / den, topg)
        topi_ref[sl[p], :] = topi.astype(I32)
        topg_ref[sl[p], :] = topg


def _choice_mask(topi, lane, n_exp):
    m = jnp.zeros(topi.shape, F32)
    for k in range(TOP_K):
        m = m + (lane == topi[:, k:k + 1]).astype(F32)
    return jnp.where(lane < n_exp, m, 0.0)


def _plan_kernel(topi_ref, pos_ref, n8_ref, off8_ref, carry8_ref, tot8_ref, carry_ref, *, tm, tiles, n_exp):
    i = pl.program_id(0)

    @pl.when(i == 0)
    def _():
        carry_ref[...] = jnp.zeros(carry_ref.shape, F32)

    lane = _iota((tm, LANES), 1)
    tri = (_iota((tm, tm), 0) >= _iota((tm, tm), 1)).astype(BF16)
    before = (_iota((LANES, LANES), 0) < _iota((LANES, LANES), 1)).astype(BF16)
    carry = carry_ref[...]
    for j in range(tiles):
        rows = pl.ds(j * tm, tm)
        topi = topi_ref[rows, :]
        m = _choice_mask(topi, lane, n_exp)
        cum = _dot(tri, m.astype(BF16))
        n8 = jnp.floor((cum[tm - 1:tm, :] + (SUBLANES - 1)) * (1.0 / SUBLANES)) * SUBLANES
        off8 = _dot(jnp.broadcast_to(n8, (SUBLANES, LANES)).astype(BF16), before)[0:1]
        posf = cum - m + off8
        pos = jnp.zeros((tm, LANES), F32)
        for k in range(TOP_K):
            pk = jnp.sum(jnp.where(lane == topi[:, k:k + 1], posf, 0.0), axis=-1, keepdims=True)
            pos = jnp.where(lane == k, pk, pos)
        pos_ref[rows, :] = pos.astype(I32)
        n8_ref[j] = n8.astype(I32)
        off8_ref[j] = off8.astype(I32)
        carry8_ref[j] = carry.astype(I32)
        carry = carry + n8
    carry_ref[...] = carry
    tot8_ref[...] = carry


def _pow2_floor(n):
    return 1 << (n.bit_length() - 1)


def _strip_pieces(n8, max_rows):
    pieces = []
    b = max_rows
    while b >= SUBLANES:
        pieces.append(((n8 & b) != 0, n8 & ~(2 * b - 1), b))
        b //= 2
    return pieces


def _wait_rows(total, make_copy, max_rows):
    b = max_rows
    while b >= SUBLANES:
        @pl.when((total & b) != 0)
        def _():
            make_copy(b).wait()
        b //= 2


def _aligned(start, rows):
    return pl.ds(pl.multiple_of(start, SUBLANES), rows)


def _dispatch_kernel(pend_ref, padded_ref, n8_ref, off8_ref, dbase_ref, hn_ref, pos_ref, xs_ref,
                     zbuf_ref, sbuf_ref, tot_ref, zsem, sems, *, tm, width, n_exp, n_steps, n_blocks):
    i = pl.program_id(0)
    slot = i % 2

    def zero_block(start):
        return pltpu.make_async_copy(zbuf_ref, xs_ref.at[_aligned(start, MOE_BLOCK), :], zsem)

    @pl.when(i == 0)
    def _():
        zbuf_ref[...] = jnp.zeros(zbuf_ref.shape, F32)
        for e in range(n_exp):
            @pl.when(padded_ref[e] > 0)
            def _():
                zero_block(pend_ref[e] - MOE_BLOCK).start()
        first_unused = pend_ref[n_exp - 1] // MOE_BLOCK

        def start_tail(j, carry):
            zero_block(j * MOE_BLOCK).start()
            return carry

        lax.fori_loop(first_unused, n_blocks, start_tail, 0)
        for e in range(n_exp):
            @pl.when(padded_ref[e] > 0)
            def _():
                zero_block(0).wait()

        def wait_tail(j, carry):
            zero_block(0).wait()
            return carry

        lax.fori_loop(first_unused, n_blocks, wait_tail, 0)

    def strip_wait(s):
        _wait_rows(tot_ref[s], lambda b: pltpu.make_async_copy(
            sbuf_ref.at[s, pl.ds(0, b), :], xs_ref.at[pl.ds(0, b), :], sems.at[s]), _pow2_floor(width))

    @pl.when(i >= 2)
    def _():
        strip_wait(slot)

    pos = pos_ref[...]
    lane = _iota((tm, width), 1)
    sel = lane == pos[:, 0:1]
    for k in range(1, TOP_K):
        sel = jnp.logical_or(sel, lane == pos[:, k:k + 1])
    sbuf_ref[slot] = _dot_tn(sel.astype(BF16), hn_ref[...].astype(BF16))

    total = 0
    for e in range(n_exp):
        n8, so, do = n8_ref[e], off8_ref[e], dbase_ref[e]
        total = total + n8
        for cond, ofs, rows in _strip_pieces(n8, _pow2_floor(tm)):
            @pl.when(cond)
            def _():
                pltpu.make_async_copy(sbuf_ref.at[slot, _aligned(so + ofs, rows), :],
                                      xs_ref.at[_aligned(do + ofs, rows), :], sems.at[slot]).start()
    tot_ref[slot] = total

    @pl.when(i == n_steps - 1)
    def _():
        if n_steps >= 2:
            strip_wait(1 - slot)
        strip_wait(slot)


def _expert_kernel(be_ref, nbu_ref, x_ref, wgu_ref, bgu_ref, wd_ref, bd_ref, y_ref, wgu16_ref, wd16_ref,
                   *, d_ff):
    i = pl.program_id(0)
    used = i < nbu_ref[0]
    new_expert = jnp.logical_or(i == 0, be_ref[i] != be_ref[jnp.maximum(i - 1, 0)])

    @pl.when(jnp.logical_and(used, new_expert))
    def _():
        wgu16_ref[...] = wgu_ref[0].astype(BF16)
        wd16_ref[...] = wd_ref[0].astype(BF16)

    @pl.when(used)
    def _():
        gu = _dot(x_ref[...].astype(BF16), wgu16_ref[...]) + bgu_ref[0]
        gate = jnp.minimum(gu[:, :d_ff], SWIGLU_LIMIT)
        up = jnp.clip(gu[:, d_ff:], -SWIGLU_LIMIT, SWIGLU_LIMIT)
        act = (up + 1.0) * gate * _sigmoid(SWIGLU_ALPHA * gate)
        y_ref[...] = _dot(act.astype(BF16), wd16_ref[...]) + bd_ref[0]

    @pl.when(jnp.logical_not(used))
    def _():
        y_ref[...] = jnp.zeros(y_ref.shape, F32)


def _combine_kernel(n8c_ref, off8c_ref, dbc_ref, n8n_ref, off8n_ref, dbn_ref, pos_ref, h2_ref, topg_ref, fg_ref,
                    ys_ref, out_ref, gbuf_ref, tot_ref, sems, *, tm, width, n_exp, n_steps):
    i = pl.program_id(0)
    slot = i % 2

    def fetch(n8_ref, off8_ref, db_ref, s):
        total = 0
        for e in range(n_exp):
            n8, so, do = n8_ref[e], off8_ref[e], db_ref[e]
            total = total + n8
            for cond, ofs, rows in _strip_pieces(n8, _pow2_floor(tm)):
                @pl.when(cond)
                def _():
                    pltpu.make_async_copy(ys_ref.at[_aligned(do + ofs, rows), :],
                                          gbuf_ref.at[s, _aligned(so + ofs, rows), :], sems.at[s]).start()
        tot_ref[s] = total

    @pl.when(i == 0)
    def _():
        gbuf_ref[...] = jnp.zeros(gbuf_ref.shape, F32)
        fetch(n8c_ref, off8c_ref, dbc_ref, 0)

    @pl.when(i + 1 < n_steps)
    def _():
        fetch(n8n_ref, off8n_ref, dbn_ref, 1 - slot)

    _wait_rows(tot_ref[slot], lambda b: pltpu.make_async_copy(
        ys_ref.at[pl.ds(0, b), :], gbuf_ref.at[slot, pl.ds(0, b), :], sems.at[slot]), _pow2_floor(width))
    pos = pos_ref[...]
    topg = topg_ref[...]
    lane = _iota((tm, width), 1)
    g = jnp.zeros((tm, width), F32)
    for k in range(TOP_K):
        g = g + jnp.where(lane == pos[:, k:k + 1], topg[:, k:k + 1], 0.0)
    moe = _dot(g.astype(BF16), gbuf_ref[slot].astype(BF16))
    out_ref[...] = _rms(h2_ref[...] + moe, fg_ref[...])


def _pad_lanes(v, fill=0.0):
    v = v.astype(F32)
    return jnp.concatenate([v, jnp.full((LANES - v.shape[0],), fill, F32)])[None, :]


def kernel(x, meta_tokens, norm_mix_g, w_in, conv_qkv_w, a_log, dt_bias, dn_norm_g, w_dn_out, b_glu,
           conv_dw_w, conv_dw_b, conv_ln_g, conv_ln_b, w_conv_out, b_conv_out, w_merge, b_merge, w_out,
           norm_ffn_g, w_router, b_router, w_gate_up, b_gate_up, w_down, b_down, final_norm_g):
    bsz, seq, d = x.shape
    depth = w_in.shape[0]
    n_meta = meta_tokens.shape[0]
    heads = a_log.shape[1]
    vw = w_dn_out.shape[1]
    qk_w = (conv_qkv_w.shape[2] - vw) // 2
    dk, dv = qk_w // heads, vw // heads
    ksz = conv_qkv_w.shape[1]
    kw, conv_ch = conv_dw_w.shape[1], conv_dw_w.shape[2]
    n_exp, d_ff = w_gate_up.shape[1], w_down.shape[2]
    meta_pad = CHUNK - n_meta
    lp = seq + n_meta + meta_pad
    assert depth == 1 and lp % CHUNK == 0 and dk == LANES and dv == LANES and d % LANES == 0
    assert 2 * heads <= LANES and n_exp <= LANES and kw - 1 <= 32 and ksz - 1 <= SUBLANES
    rows = bsz * lp
    n_head = meta_pad + n_meta
    head = jnp.concatenate([jnp.zeros((meta_pad, d), x.dtype), meta_tokens.astype(x.dtype)], axis=0)

    wi = w_in[0]
    o_a = 2 * qk_w + vw
    o_go = o_a + 2 * heads
    o_glu = o_go + vw
    w_qkv = wi[:, :o_a].astype(BF16)
    w_ab = jnp.pad(wi[:, o_a:o_go], ((0, 0), (0, LANES - 2 * heads))).astype(BF16)
    w_go = wi[:, o_go:o_glu].astype(BF16)
    w_glu = wi[:, o_glu:].astype(BF16)
    row2 = lambda v: v.astype(F32).reshape(1, -1)

    tm1 = _pick_tile(lp, min(320, seq), BF16_ROWS)
    nt1 = lp // tm1
    assert n_head % SUBLANES == 0 and n_head <= tm1 <= seq
    rspec = lambda cols, tm, nt: pl.BlockSpec((tm, cols), lambda b, j: (b * nt + j, 0))
    full = lambda a: pl.BlockSpec(a.shape, lambda *_: (0,) * a.ndim)
    x_rows = pl.BlockSpec((pl.Element(1), pl.Element(tm1), pl.Element(d)),
                          lambda b, j: (b, pl.multiple_of(jnp.maximum(j * tm1 - n_head, 0), SUBLANES), 0))
    front_in = [head, x, row2(norm_mix_g[0]), w_qkv, w_ab, w_go, w_glu, w_merge[0].astype(BF16),
                row2(b_glu[0]), row2(b_merge[0]), _pad_lanes(a_log[0]), _pad_lanes(dt_bias[0])]
    hp, qkv_pre, gbeta, go_act, u_glu, gates = pl.pallas_call(
        functools.partial(_front_kernel, tm=tm1, meta_pad=meta_pad, heads=heads, conv_ch=conv_ch),
        grid=(bsz, nt1),
        in_specs=[full(head), x_rows] + [full(a) for a in front_in[2:]],
        out_specs=[rspec(d, tm1, nt1), rspec(o_a, tm1, nt1), rspec(LANES, tm1, nt1), rspec(vw, tm1, nt1),
                   rspec(conv_ch, tm1, nt1), rspec(2 * d, tm1, nt1)],
        out_shape=[jax.ShapeDtypeStruct((rows, d), F32),
                   jax.ShapeDtypeStruct((rows, o_a), BF16), jax.ShapeDtypeStruct((rows, LANES), F32),
                   jax.ShapeDtypeStruct((rows, vw), BF16), jax.ShapeDtypeStruct((rows, conv_ch), BF16),
                   jax.ShapeDtypeStruct((rows, 2 * d), BF16)],
        compiler_params=_params(("arbitrary", "arbitrary")),
        name="front",
    )(*front_in)

    tmc = _pick_tile(lp, 320, 32)
    ntc = lp // tmc
    halo, lead = 32, BF16_ROWS
    conv_w8 = jnp.repeat(conv_dw_w[0].astype(F32), SUBLANES, axis=0)
    conv_in = [u_glu, gates, conv_w8, row2(conv_dw_b[0]), row2(conv_ln_g[0]),
               row2(conv_ln_b[0]), w_conv_out[0].astype(BF16), row2(b_conv_out[0])]
    ybg = pl.pallas_call(
        functools.partial(_conv_kernel, tm=tmc, kw=kw, halo=halo, lead=lead, rb=32),
        grid=(bsz, ntc),
        in_specs=[rspec(conv_ch, tmc, ntc),
                  pl.BlockSpec((tmc, d), lambda b, j: (b * ntc + j, 1))] + [full(a) for a in conv_in[2:]],
        out_specs=rspec(d, tmc, ntc),
        out_shape=jax.ShapeDtypeStruct((rows, d), BF16),
        scratch_shapes=[pltpu.VMEM((lead + halo + tmc, conv_ch), BF16),
                        pltpu.VMEM((SUBLANES, lead + halo + tmc, conv_ch), F32),
                        pltpu.VMEM((tmc, conv_ch), F32)],
        compiler_params=_params(("arbitrary", "arbitrary")),
        name="convmod",
    )(*conv_in)

    nc = lp // CHUNK
    nbd = max(n for n in (1, 2, 4) if bsz % n == 0)
    cspec = lambda cols: pl.BlockSpec((nbd, CHUNK, cols), lambda b, c: (b, c, 0))
    delta_in = [qkv_pre.reshape(bsz, lp, o_a), gbeta.reshape(bsz, lp, LANES), go_act.reshape(bsz, lp, vw),
                conv_qkv_w[0].astype(F32), row2(dn_norm_g[0])]
    o_dn = pl.pallas_call(
        functools.partial(_delta_kernel, nb=nbd, heads=heads, dk=dk, dv=dv, ch=CHUNK, ksz=ksz, halo=BF16_ROWS),
        grid=(bsz // nbd, nc),
        in_specs=[cspec(o_a), cspec(LANES), cspec(vw), full(delta_in[3]), full(delta_in[4])],
        out_specs=cspec(vw),
        out_shape=jax.ShapeDtypeStruct((bsz, lp, vw), BF16),
        scratch_shapes=[pltpu.VMEM((nbd, heads, dk, dv), F32),
                        pltpu.VMEM((nbd, CHUNK + BF16_ROWS, o_a), BF16)],
        compiler_params=_params(("arbitrary", "arbitrary")),
        name="delta",
    )(*delta_in).reshape(rows, vw)

    tm3 = _pick_tile(lp, 320, BF16_ROWS)
    nt3 = lp // tm3
    w_r = jnp.pad(w_router[0], ((0, 0), (0, LANES - n_exp))).astype(BF16)
    mid_in = [hp, o_dn, gates, ybg, w_dn_out[0].astype(BF16), w_out[0].astype(BF16),
              row2(norm_ffn_g[0]), w_r, _pad_lanes(b_router[0])]
    h2, hn2, topi, topg = pl.pallas_call(
        functools.partial(_mid_kernel, tm=tm3, n_exp=n_exp, parts=2 if tm3 % (2 * BF16_ROWS) == 0 else 1),
        grid=(bsz, nt3),
        in_specs=[rspec(d, tm3, nt3), rspec(vw, tm3, nt3), rspec(d, tm3, nt3), rspec(d, tm3, nt3)]
        + [full(a) for a in mid_in[4:]],
        out_specs=[rspec(d, tm3, nt3), rspec(d, tm3, nt3), rspec(LANES, tm3, nt3), rspec(LANES, tm3, nt3)],
        out_shape=[jax.ShapeDtypeStruct((rows, d), F32), jax.ShapeDtypeStruct((rows, d), F32),
                   jax.ShapeDtypeStruct((rows, LANES), I32), jax.ShapeDtypeStruct((rows, LANES), F32)],
        compiler_params=_params(("arbitrary", "arbitrary")),
        name="mid",
    )(*mid_in)

    tmd = _pick_tile(seq, TOKEN_TILE, SUBLANES)
    tpb = seq // tmd
    ntd = bsz * tpb
    tokens = bsz * seq
    width = tmd * TOP_K + n_exp * SUBLANES
    row0 = lambda i: pl.multiple_of((i // tpb) * lp + n_head + (i % tpb) * tmd, SUBLANES)
    prow = lambda r, cols, f=row0: pl.BlockSpec((pl.Element(r), pl.Element(cols)), lambda i, *_: (f(i), 0))
    tspec = pl.BlockSpec((tmd, LANES), lambda i, *_: (i, 0))
    ptiles = _pick_tile(tpb, 5, 1)
    pspec = pl.BlockSpec((ptiles * tmd, LANES), lambda i: (i, 0))
    tab = pl.BlockSpec((ptiles, 1, LANES), lambda i: (i, 0, 0))
    tab_shape = jax.ShapeDtypeStruct((ntd, 1, LANES), I32)
    pos, n8_t, off8_t, carry8_t, tot8 = pl.pallas_call(
        functools.partial(_plan_kernel, tm=tmd, tiles=ptiles, n_exp=n_exp),
        grid=(ntd // ptiles,),
        in_specs=[prow(ptiles * tmd, LANES, lambda i: row0(i * ptiles))],
        out_specs=[pspec, tab, tab, tab, pl.BlockSpec((1, LANES), lambda i: (0, 0))],
        out_shape=[jax.ShapeDtypeStruct((tokens, LANES), I32), tab_shape, tab_shape, tab_shape,
                   jax.ShapeDtypeStruct((1, LANES), F32)],
        scratch_shapes=[pltpu.VMEM((1, LANES), F32)],
        compiler_params=_params(("arbitrary",)),
        name="plan",
    )(topi)

    n_blocks = -(-(tokens * TOP_K + n_exp * (SUBLANES - 1) * ntd) // MOE_BLOCK) + n_exp
    cap = n_blocks * MOE_BLOCK
    counts = tot8[0, :n_exp].astype(I32)
    padded = (counts + MOE_BLOCK - 1) // MOE_BLOCK * MOE_BLOCK
    pend = jnp.cumsum(padded).astype(I32)
    pstart = pend - padded
    n_used = (pend[-1] // MOE_BLOCK).astype(I32).reshape(1)
    block_start = jnp.arange(n_blocks, dtype=I32) * MOE_BLOCK
    block_e = jnp.minimum(jnp.sum((pend[None, :] <= block_start[:, None]).astype(I32), axis=1), n_exp - 1)
    dbase = (carry8_t.reshape(ntd, LANES) + jnp.pad(pstart, (0, LANES - n_exp))[None, :]).reshape(ntd * LANES)
    n8_f = n8_t.reshape(ntd * LANES)
    off8_f = off8_t.reshape(ntd * LANES)

    anyspec = pl.BlockSpec(memory_space=pl.ANY)
    stab = lambda f: pl.BlockSpec((LANES,), lambda i, *_: (f(i),), memory_space=pltpu.SMEM)
    cur = lambda i: i
    xs = pl.pallas_call(
        functools.partial(_dispatch_kernel, tm=tmd, width=width, n_exp=n_exp, n_steps=ntd, n_blocks=n_blocks),
        grid_spec=pltpu.PrefetchScalarGridSpec(
            num_scalar_prefetch=2, grid=(ntd,),
            in_specs=[stab(cur), stab(cur), stab(cur), prow(tmd, d), tspec],
            out_specs=anyspec,
            scratch_shapes=[pltpu.VMEM((MOE_BLOCK, d), F32), pltpu.VMEM((2, width, d), F32),
                            pltpu.SMEM((2,), I32), pltpu.SemaphoreType.DMA(()), pltpu.SemaphoreType.DMA((2,))]),
        out_shape=jax.ShapeDtypeStruct((cap, d), F32),
        compiler_params=_params(("arbitrary",)),
        name="dispatch",
    )(pend, padded, n8_f, off8_f, dbase, hn2, pos)

    last_used = lambda i, nbu: jnp.minimum(i, nbu[0] - 1)
    ew = lambda a: pl.BlockSpec((1,) + a.shape[1:], lambda i, be, nbu: (be[i],) + (0,) * (a.ndim - 1))
    e_in = [w_gate_up[0], b_gate_up[0].astype(F32)[:, None, :], w_down[0], b_down[0].astype(F32)[:, None, :]]
    ys = pl.pallas_call(
        functools.partial(_expert_kernel, d_ff=d_ff),
        grid_spec=pltpu.PrefetchScalarGridSpec(
            num_scalar_prefetch=2, grid=(n_blocks,),
            in_specs=[pl.BlockSpec((MOE_BLOCK, d), lambda i, be, nbu: (last_used(i, nbu), 0))]
            + [ew(a) for a in e_in],
            out_specs=pl.BlockSpec((MOE_BLOCK, d), lambda i, be, nbu: (i, 0)),
            scratch_shapes=[pltpu.VMEM((d, 2 * d_ff), BF16), pltpu.VMEM((d_ff, d), BF16)]),
        out_shape=jax.ShapeDtypeStruct((cap, d), F32),
        compiler_params=_params(("arbitrary",)),
        name="experts",
    )(block_e, n_used, xs, *e_in)

    nxt = lambda i: jnp.minimum(i + 1, ntd - 1)
    out = pl.pallas_call(
        functools.partial(_combine_kernel, tm=tmd, width=width, n_exp=n_exp, n_steps=ntd),
        grid=(ntd,),
        in_specs=[stab(cur), stab(cur), stab(cur), stab(nxt), stab(nxt), stab(nxt), tspec,
                  prow(tmd, d), prow(tmd, LANES), pl.BlockSpec((1, d), lambda i: (0, 0)), anyspec],
        out_specs=pl.BlockSpec((tmd, d), lambda i: (i, 0)),
        out_shape=jax.ShapeDtypeStruct((tokens, d), x.dtype),
        scratch_shapes=[pltpu.VMEM((2, width, d), F32), pltpu.SMEM((2,), I32), pltpu.SemaphoreType.DMA((2,))],
        compiler_params=_params(("arbitrary",)),
        name="combine",
    )(n8_f, off8_f, dbase, n8_f, off8_f, dbase, pos, h2, topg, row2(final_norm_g), ys)
    return out.reshape(bsz, seq, d)
```

```python
import functools

import jax
import jax.numpy as jnp
from jax import lax
from jax.experimental import pallas as pl
from jax.experimental.pallas import tpu as pltpu

F32 = jnp.float32
BF16 = jnp.bfloat16
I32 = jnp.int32

CHUNK = 64
TOP_K = 4
MOE_BLOCK = 512
TOKEN_TILE = 256
SWIGLU_LIMIT = 7.0
SWIGLU_ALPHA = 1.702
EPS = 1e-6
LN_EPS = 1e-5
LANES = 128
SUBLANES = 8
BF16_ROWS = 16
VMEM_LIMIT = 56 * 1024 * 1024


def _dot(a, b):
    return jnp.dot(a, b, preferred_element_type=F32)


def _dot_hi(a, b):
    return jnp.dot(a, b, precision=lax.Precision.HIGHEST, preferred_element_type=F32)


def _dot_nt(a, b):
    return lax.dot_general(a, b, (((1,), (1,)), ((), ())), preferred_element_type=F32)


def _dot_tn(a, b):
    return lax.dot_general(a, b, (((0,), (0,)), ((), ())), preferred_element_type=F32)


def _sigmoid(x):
    return 1.0 / (1.0 + jnp.exp(-x))


def _silu(x):
    return x * _sigmoid(x)


def _softplus(x):
    return jnp.maximum(x, 0.0) + jnp.log(1.0 + jnp.exp(-jnp.abs(x)))


def _rms(x, g):
    return x * lax.rsqrt(jnp.mean(x * x, axis=-1, keepdims=True) + EPS) * g


def _iota(shape, dim):
    return lax.broadcasted_iota(I32, shape, dim)


def _pick_tile(n, target, mult):
    best = None
    for t in range(mult, min(n, target) + 1, mult):
        if n % t == 0:
            best = t
    assert best is not None, (n, target, mult)
    return best


def _params(sem, flags=None):
    return pltpu.CompilerParams(dimension_semantics=sem, vmem_limit_bytes=VMEM_LIMIT, flags=flags)


def _front_kernel(head_ref, x_ref, g_ref, wqkv_ref, wab_ref, wgo_ref, wglu_ref, wmg_ref, bglu_ref, bmg_ref,
                  alog_ref, dtb_ref, h_ref, qkv_ref, gb_ref, go_ref, u_ref, gates_ref,
                  *, tm, meta_pad, heads, conv_ch):
    j = pl.program_id(1)
    xb = x_ref[0]
    n_head = head_ref.shape[0]
    h = jnp.where(j == 0, jnp.concatenate([head_ref[...], xb[:tm - n_head]], axis=0), xb)
    h_ref[...] = h
    hn = _rms(h, g_ref[...]).astype(BF16)
    qkv_ref[...] = _dot(hn, wqkv_ref[...]).astype(BF16)
    ab = _dot(hn, wab_ref[...])
    valid = (j * tm + _iota((tm, 1), 0)) >= meta_pad
    lane = _iota((1, LANES), 1)
    g = -jnp.exp(alog_ref[...]) * _softplus(ab + dtb_ref[...])
    gb = jnp.where(lane < heads, g, _sigmoid(ab))
    gb_ref[...] = jnp.where(valid, gb, 0.0)
    go_ref[...] = _silu(_dot(hn, wgo_ref[...])).astype(BF16)
    glu = _dot(hn, wglu_ref[...]) + bglu_ref[...]
    u = glu[:, :conv_ch] * _sigmoid(glu[:, conv_ch:])
    u_ref[...] = jnp.where(valid, u, 0.0).astype(BF16)
    gates_ref[...] = _sigmoid(_dot(hn, wmg_ref[...]) + bmg_ref[...]).astype(BF16)


def _conv_kernel(u_ref, gate_ref, w_ref, b_ref, lng_ref, lnb_ref, wo_ref, bo_ref, y_ref,
                 xcat_ref, xs_ref, cbuf_ref, *, tm, kw, halo, lead, rb):
    j = pl.program_id(1)
    base = lead + halo
    total = base + tm
    n_c = xcat_ref.shape[1]

    @pl.when(j == 0)
    def _():
        xcat_ref[0:base, :] = jnp.zeros((base, n_c), BF16)

    @pl.when(j > 0)
    def _():
        xcat_ref[lead:base, :] = xcat_ref[tm + lead:tm + base, :]

    xcat_ref[base:total, :] = u_ref[...]
    sr = _iota((SUBLANES * rb, lead + rb), 0)
    sc = _iota((SUBLANES * rb, lead + rb), 1)
    shift = (sc == (sr & (rb - 1)) + lead - (sr >> (rb.bit_length() - 1))).astype(BF16)
    bias = b_ref[...]
    for t0 in range(lead, total, rb):
        res = _dot(shift, xcat_ref[t0 - lead:t0 + rb, :])
        for r in range(SUBLANES):
            xs_ref[r, t0:t0 + rb, :] = res[r * rb:(r + 1) * rb, :]
        if t0 < base:
            continue
        i0 = t0
        blk = (i0 - base) // rb
        acc = jnp.zeros((rb, n_c), F32)
        for s in range(kw):
            a, r = divmod(s, SUBLANES)
            wt = w_ref[(kw - 1 - s) * SUBLANES:(kw - s) * SUBLANES, :]
            acc = acc + xs_ref[r, i0 - SUBLANES * a:i0 - SUBLANES * a + rb, :] * jnp.concatenate(
                [wt] * (rb // SUBLANES), axis=0)
        cbuf_ref[blk * rb:(blk + 1) * rb, :] = acc + bias
    c = cbuf_ref[...]
    mu = jnp.mean(c, axis=-1, keepdims=True)
    xc = c - mu
    ln = xc * lax.rsqrt(jnp.mean(xc * xc, axis=-1, keepdims=True) + LN_EPS) * lng_ref[...] + lnb_ref[...]
    yb = _dot(_silu(ln).astype(BF16), wo_ref[...]) + bo_ref[...]
    y_ref[...] = (gate_ref[...].astype(F32) * yb).astype(BF16)


def _bdot(a, b):
    return jnp.dot(a.astype(BF16), b.astype(BF16), preferred_element_type=F32)


def _split3(x):
    x1 = x.astype(BF16)
    r1 = x - x1.astype(F32)
    x2 = r1.astype(BF16)
    x3 = (r1 - x2.astype(F32)).astype(BF16)
    return x1, x2, x3


def _unit_lower_inverses_minus_eye(mats, row, col):
    ch = mats[0].shape[0]
    blk16 = (row >> 4) == (col >> 4)
    blk32 = (row >> 5) == (col >> 5)
    n1 = [jnp.where(blk16, a, 0.0) for a in mats]
    n2 = [_bdot(n, n) for n in n1]
    r = [_bdot(jnp.concatenate([n, m], axis=0), m) for n, m in zip(n1, n2)]
    n4 = [x[ch:] for x in r]
    y = [m - n - x[:ch] for n, m, x in zip(n1, n2, r)]
    r = [_bdot(jnp.concatenate([yy, m], axis=0), m) for yy, m in zip(y, n4)]
    n8 = [x[ch:] for x in r]
    y = [yy + m + x[:ch] for yy, m, x in zip(y, n4, r)]
    y = [yy + m + _bdot(yy, m) for yy, m in zip(y, n8)]
    for mask in (blk32 & jnp.logical_not(blk16), jnp.logical_not(blk32)):
        ls = [jnp.where(mask, a, 0.0) for a in mats]
        ms = [l + _bdot(l, yy) for l, yy in zip(ls, y)]
        y = [yy - m - _bdot(yy, m) for yy, m in zip(y, ms)]
    return y


def _delta_kernel(qkv_ref, gb_ref, go_ref, cw_ref, ng_ref, o_ref, s_ref, xc_ref,
                  *, nb, heads, dk, dv, ch, ksz, halo):
    c = pl.program_id(1)
    qk_w = heads * dk

    @pl.when(c == 0)
    def _():
        s_ref[...] = jnp.zeros(s_ref.shape, F32)
        xc_ref[:, 0:halo, :] = jnp.zeros((nb, halo, xc_ref.shape[2]), BF16)

    @pl.when(c > 0)
    def _():
        xc_ref[:, 0:halo, :] = xc_ref[:, ch:ch + halo, :]

    xc_ref[:, halo:halo + ch, :] = qkv_ref[...]
    row = _iota((ch, ch), 0)
    col = _iota((ch, ch), 1)
    incl = row >= col
    strict = row > col
    tril = incl.astype(BF16)
    triu = (row <= col).astype(BF16)
    sr = _iota(((ksz - 1) * ch, halo + ch), 0)
    sc = _iota(((ksz - 1) * ch, halo + ch), 1)
    shift = (sc == (sr & (ch - 1)) + (sr >> (ch.bit_length() - 1)) + (halo - (ksz - 1))).astype(BF16)

    qn, kn, kb, vb, gamma, eg, ekd, elast = [], [], [], [], [], [], [], []
    shifted = [_dot(shift, xc_ref[b]) for b in range(nb)]
    for b in range(nb):
        acc = xc_ref[b, halo:halo + ch, :].astype(F32) * cw_ref[ksz - 1:ksz, :]
        for t in range(ksz - 1):
            acc = acc + shifted[b][t * ch:(t + 1) * ch, :] * cw_ref[t:t + 1, :]
        qkv = _silu(acc)
        gb = gb_ref[b]
        parts = _split3(gb)
        gc = sum(_dot(tril, p) for p in parts)
        gct = sum(_dot_tn(p, triu) for p in parts)
        glast = gc[ch - 1:ch, :]
        e_g = jnp.exp(gc)
        e_kd = jnp.exp(glast - gc)
        e_last = jnp.exp(glast)
        for h in range(heads):
            q = qkv[:, h * dk:(h + 1) * dk]
            k = qkv[:, qk_w + h * dk:qk_w + (h + 1) * dk]
            v = qkv[:, 2 * qk_w + h * dv:2 * qk_w + (h + 1) * dv]
            qn.append(q * lax.rsqrt(jnp.sum(q * q, axis=-1, keepdims=True) + EPS) * (dk ** -0.5))
            kn.append(k * lax.rsqrt(jnp.sum(k * k, axis=-1, keepdims=True) + EPS))
            beta = gb[:, heads + h:heads + h + 1]
            kb.append(kn[-1] * beta)
            vb.append(v * beta)
            diff = gc[:, h:h + 1] - gct[h:h + 1, :]
            gamma.append(jnp.where(incl, jnp.exp(jnp.where(incl, diff, 0.0)), 0.0))
            eg.append(e_g[:, h:h + 1])
            ekd.append(e_kd[:, h:h + 1])
            elast.append(e_last[:, h:h + 1])

    n = nb * heads
    kq = [_dot_nt(jnp.concatenate([kb[i], qn[i]], axis=0).astype(BF16), kn[i].astype(BF16))
          for i in range(n)]
    a = [jnp.where(strict, kq[i][:ch] * gamma[i], 0.0) for i in range(n)]
    qk = [kq[i][ch:] * gamma[i] for i in range(n)]
    y = _unit_lower_inverses_minus_eye(a, row, col)
    rhs = [jnp.concatenate([vb[i], kb[i] * eg[i]], axis=-1) for i in range(n)]
    uw = [rhs[i] + _bdot(y[i], rhs[i]) for i in range(n)]
    s = [s_ref[i // heads, i % heads] for i in range(n)]
    ws = [_bdot(jnp.concatenate([uw[i][:, dv:], qn[i] * eg[i]], axis=0), s[i]) for i in range(n)]
    v_new = [uw[i][:, :dv] - ws[i][:ch] for i in range(n)]
    o = [ws[i][ch:] + _bdot(qk[i], v_new[i]) for i in range(n)]
    s_new = [s[i] * elast[i] + _dot_tn((kn[i] * ekd[i]).astype(BF16), v_new[i].astype(BF16))
             for i in range(n)]
    for i in range(n):
        b, h = i // heads, i % heads
        s_ref[b, h] = s_new[i]
        on = _rms(o[i], ng_ref[...])
        o_ref[b, :, h * dv:(h + 1) * dv] = (
            on * go_ref[b, :, h * dv:(h + 1) * dv].astype(F32)).astype(BF16)


def _mid_kernel(h_ref, o_ref, ga_ref, ybg_ref, wdn_ref, wout_ref, g2_ref, wr_ref, br_ref,
                h2_ref, hn_ref, topi_ref, topg_ref, *, tm, n_exp, parts):
    rs = tm // parts
    sl = [pl.ds(p * rs, rs) for p in range(parts)]
    ya = [_dot(o_ref[s, :], wdn_ref[...]) for s in sl]
    y = [ga_ref[s, :].astype(F32) * a + ybg_ref[s, :].astype(F32) for s, a in zip(sl, ya)]
    h2 = [h_ref[s, :] + _dot(v.astype(BF16), wout_ref[...]) for s, v in zip(sl, y)]
    hn = [_rms(v, g2_ref[...]) for v in h2]
    for s, v, w in zip(sl, h2, hn):
        h2_ref[s, :] = v
        hn_ref[s, :] = w
    logits = [_dot(v.astype(BF16), wr_ref[...]) + br_ref[...] for v in hn]
    lane = _iota((rs, LANES), 1)
    lane_f = lane.astype(F32)
    l = [jnp.where(lane < n_exp, v, -jnp.inf) for v in logits]
    vals = [[] for _ in range(parts)]
    idxs = [[] for _ in range(parts)]
    for _ in range(TOP_K):
        m = [jnp.max(v, axis=-1, keepdims=True) for v in l]
        idx = [jnp.min(jnp.where(v == mm, lane_f, float(LANES)), axis=-1, keepdims=True)
               for v, mm in zip(l, m)]
        l = [jnp.where(lane_f == ii, -jnp.inf, v) for v, ii in zip(l, idx)]
        for p in range(parts):
            vals[p].append(m[p])
            idxs[p].append(idx[p])
    for p in range(parts):
        es = [jnp.exp(v - vals[p][0]) for v in vals[p]]
        den = es[0]
        for e in es[1:]:
            den = den + e
        topi = jnp.zeros((rs, LANES), F32)
        topg = jnp.zeros((rs, LANES), F32)
        for k in range(TOP_K):
            topi = jnp.where(lane == k, idxs[p][k], topi)
            topg = jnp.where(lane == k, es[k] / den, topg)
        topi_ref[sl[p], :] = topi.astype(I32)
        topg_ref[sl[p], :] = topg


def _choice_mask(topi, lane, n_exp):
    m = jnp.zeros(topi.shape, F32)
    for k in range(TOP_K):
        m = m + (lane == topi[:, k:k + 1]).astype(F32)
    return jnp.where(lane < n_exp, m, 0.0)


def _plan_kernel(topi_ref, pos_ref, post_ref, n8_ref, off8_ref, carry8_ref, tot8_ref, carry_ref,
                 *, tm, tiles, n_exp):
    i = pl.program_id(0)

    @pl.when(i == 0)
    def _():
        carry_ref[...] = jnp.zeros(carry_ref.shape, F32)

    lane = _iota((tm, LANES), 1)
    tri = (_iota((tm, tm), 0) >= _iota((tm, tm), 1)).astype(BF16)
    before = (_iota((LANES, LANES), 0) < _iota((LANES, LANES), 1)).astype(BF16)
    carry = carry_ref[...]
    for j in range(tiles):
        rows = pl.ds(j * tm, tm)
        topi = topi_ref[rows, :]
        m = _choice_mask(topi, lane, n_exp)
        cum = _dot(tri, m.astype(BF16))
        n8 = jnp.floor((cum[tm - 1:tm, :] + (SUBLANES - 1)) * (1.0 / SUBLANES)) * SUBLANES
        off8 = _dot(jnp.broadcast_to(n8, (SUBLANES, LANES)).astype(BF16), before)[0:1]
        posf = cum - m + off8
        pos = jnp.zeros((tm, LANES), F32)
        for k in range(TOP_K):
            pk = jnp.sum(jnp.where(lane == topi[:, k:k + 1], posf, 0.0), axis=-1, keepdims=True)
            pos = jnp.where(lane == k, pk, pos)
        pos_ref[rows, :] = pos.astype(I32)
        post_ref[j] = jnp.transpose(pos)[0:SUBLANES, :].astype(I32)
        n8_ref[j] = n8.astype(I32)
        off8_ref[j] = off8.astype(I32)
        carry8_ref[j] = carry.astype(I32)
        carry = carry + n8
    carry_ref[...] = carry
    tot8_ref[...] = carry


def _pow2_floor(n):
    return 1 << (n.bit_length() - 1)


def _strip_pieces(n8, max_rows):
    pieces = []
    b = max_rows
    while b >= SUBLANES:
        pieces.append(((n8 & b) != 0, n8 & ~(2 * b - 1), b))
        b //= 2
    return pieces


LONG_PIECE = 64


def _copy_strip(n8, start_piece):
    n_long = n8 >> (LONG_PIECE.bit_length() - 1)

    def long_piece(j, carry):
        start_piece(j * LONG_PIECE, LONG_PIECE)
        return carry

    lax.fori_loop(0, n_long, long_piece, 0)
    for cond, ofs, rows in _strip_pieces(n8 & (LONG_PIECE - 1), LONG_PIECE // 2):
        @pl.when(cond)
        def _():
            start_piece(n_long * LONG_PIECE + ofs, rows)


def _wait_rows(total, make_copy, max_rows):
    b = max_rows
    while b >= SUBLANES:
        @pl.when((total & b) != 0)
        def _():
            make_copy(b).wait()
        b //= 2


def _aligned(start, rows):
    return pl.ds(pl.multiple_of(start, SUBLANES), rows)


def _dispatch_kernel(pend_ref, padded_ref, n8_ref, off8_ref, dbase_ref, hn_ref, post_ref, xs_ref,
                     zbuf_ref, sbuf_ref, tot_ref, zsem, sems, *, tm, width, n_exp, n_steps, n_blocks):
    i = pl.program_id(0)
    slot = i % 2

    def zero_block(start):
        return pltpu.make_async_copy(zbuf_ref, xs_ref.at[_aligned(start, MOE_BLOCK), :], zsem)

    @pl.when(i == 0)
    def _():
        zbuf_ref[...] = jnp.zeros(zbuf_ref.shape, F32)
        for e in range(n_exp):
            @pl.when(padded_ref[e] > 0)
            def _():
                zero_block(pend_ref[e] - MOE_BLOCK).start()
        first_unused = pend_ref[n_exp - 1] // MOE_BLOCK

        def start_tail(j, carry):
            zero_block(j * MOE_BLOCK).start()
            return carry

        lax.fori_loop(first_unused, n_blocks, start_tail, 0)
        for e in range(n_exp):
            @pl.when(padded_ref[e] > 0)
            def _():
                zero_block(0).wait()

        def wait_tail(j, carry):
            zero_block(0).wait()
            return carry

        lax.fori_loop(first_unused, n_blocks, wait_tail, 0)

    def strip_wait(s):
        _wait_rows(tot_ref[s], lambda b: pltpu.make_async_copy(
            sbuf_ref.at[s, pl.ds(0, b), :], xs_ref.at[pl.ds(0, b), :], sems.at[s]), _pow2_floor(width))

    @pl.when(i >= 2)
    def _():
        strip_wait(slot)

    post = post_ref[0]
    orow = _iota((width, tm), 0)
    sel = orow == post[0:1, :]
    for k in range(1, TOP_K):
        sel = jnp.logical_or(sel, orow == post[k:k + 1, :])
    sbuf_ref[slot] = _dot(sel.astype(BF16), hn_ref[...].astype(BF16))

    total = 0
    for e in range(n_exp):
        n8, so, do = n8_ref[e], off8_ref[e], dbase_ref[e]
        total = total + n8
        _copy_strip(n8, lambda ofs, rows, so=so, do=do: pltpu.make_async_copy(
            sbuf_ref.at[slot, _aligned(so + ofs, rows), :], xs_ref.at[_aligned(do + ofs, rows), :],
            sems.at[slot]).start())
    tot_ref[slot] = total

    @pl.when(i == n_steps - 1)
    def _():
        if n_steps >= 2:
            strip_wait(1 - slot)
        strip_wait(slot)


def _expert_kernel(be_ref, nbu_ref, x_ref, wgu_ref, bgu_ref, wd_ref, bd_ref, y_ref, wgu16_ref, wd16_ref,
                   *, d_ff, parts):
    i = pl.program_id(0)
    used = i < nbu_ref[0]
    new_expert = jnp.logical_or(i == 0, be_ref[i] != be_ref[jnp.maximum(i - 1, 0)])

    @pl.when(jnp.logical_and(used, new_expert))
    def _():
        wgu16_ref[...] = wgu_ref[0].astype(BF16)
        wd16_ref[...] = wd_ref[0].astype(BF16)

    @pl.when(used)
    def _():
        rs = x_ref.shape[0] // parts
        sl = [pl.ds(p * rs, rs) for p in range(parts)]
        gu = [_dot(x_ref[s, :].astype(BF16), wgu16_ref[...]) + bgu_ref[0] for s in sl]
        act = []
        for v in gu:
            gate = jnp.minimum(v[:, :d_ff], SWIGLU_LIMIT)
            up = jnp.clip(v[:, d_ff:], -SWIGLU_LIMIT, SWIGLU_LIMIT)
            act.append(((up + 1.0) * gate * _sigmoid(SWIGLU_ALPHA * gate)).astype(BF16))
        for s, a in zip(sl, act):
            y_ref[s, :] = _dot(a, wd16_ref[...]) + bd_ref[0]

    @pl.when(jnp.logical_not(used))
    def _():
        y_ref[...] = jnp.zeros(y_ref.shape, F32)


def _combine_kernel(n8c_ref, off8c_ref, dbc_ref, n8n_ref, off8n_ref, dbn_ref, pos_ref, h2_ref, topg_ref, fg_ref,
                    ys_ref, out_ref, gbuf_ref, tot_ref, sems, *, tm, width, n_exp, n_steps):
    i = pl.program_id(0)
    slot = i % 2

    def fetch(n8_ref, off8_ref, db_ref, s):
        total = 0
        for e in range(n_exp):
            n8, so, do = n8_ref[e], off8_ref[e], db_ref[e]
            total = total + n8
            _copy_strip(n8, lambda ofs, rows, so=so, do=do: pltpu.make_async_copy(
                ys_ref.at[_aligned(do + ofs, rows), :], gbuf_ref.at[s, _aligned(so + ofs, rows), :],
                sems.at[s]).start())
        tot_ref[s] = total

    @pl.when(i == 0)
    def _():
        gbuf_ref[...] = jnp.zeros(gbuf_ref.shape, F32)
        fetch(n8c_ref, off8c_ref, dbc_ref, 0)

    @pl.when(i + 1 < n_steps)
    def _():
        fetch(n8n_ref, off8n_ref, dbn_ref, 1 - slot)

    _wait_rows(tot_ref[slot], lambda b: pltpu.make_async_copy(
        ys_ref.at[pl.ds(0, b), :], gbuf_ref.at[slot, pl.ds(0, b), :], sems.at[slot]), _pow2_floor(width))
    pos = pos_ref[...]
    topg = topg_ref[...]
    lane = _iota((tm, width), 1)
    g = jnp.zeros((tm, width), F32)
    for k in range(TOP_K):
        g = g + jnp.where(lane == pos[:, k:k + 1], topg[:, k:k + 1], 0.0)
    moe = _dot(g.astype(BF16), gbuf_ref[slot].astype(BF16))
    out_ref[...] = _rms(h2_ref[...] + moe, fg_ref[...])


def _pad_lanes(v, fill=0.0):
    v = v.astype(F32)
    return jnp.concatenate([v, jnp.full((LANES - v.shape[0],), fill, F32)])[None, :]


def kernel(x, meta_tokens, norm_mix_g, w_in, conv_qkv_w, a_log, dt_bias, dn_norm_g, w_dn_out, b_glu,
           conv_dw_w, conv_dw_b, conv_ln_g, conv_ln_b, w_conv_out, b_conv_out, w_merge, b_merge, w_out,
           norm_ffn_g, w_router, b_router, w_gate_up, b_gate_up, w_down, b_down, final_norm_g):
    bsz, seq, d = x.shape
    depth = w_in.shape[0]
    n_meta = meta_tokens.shape[0]
    heads = a_log.shape[1]
    vw = w_dn_out.shape[1]
    qk_w = (conv_qkv_w.shape[2] - vw) // 2
    dk, dv = qk_w // heads, vw // heads
    ksz = conv_qkv_w.shape[1]
    kw, conv_ch = conv_dw_w.shape[1], conv_dw_w.shape[2]
    n_exp, d_ff = w_gate_up.shape[1], w_down.shape[2]
    meta_pad = CHUNK - n_meta
    lp = seq + n_meta + meta_pad
    assert depth == 1 and lp % CHUNK == 0 and dk == LANES and dv == LANES and d % LANES == 0
    assert 2 * heads <= LANES and n_exp <= LANES and kw - 1 <= 32 and ksz - 1 <= SUBLANES
    rows = bsz * lp
    n_head = meta_pad + n_meta
    head = jnp.concatenate([jnp.zeros((meta_pad, d), x.dtype), meta_tokens.astype(x.dtype)], axis=0)

    wi = w_in[0]
    o_a = 2 * qk_w + vw
    o_go = o_a + 2 * heads
    o_glu = o_go + vw
    w_qkv = wi[:, :o_a].astype(BF16)
    w_ab = jnp.pad(wi[:, o_a:o_go], ((0, 0), (0, LANES - 2 * heads))).astype(BF16)
    w_go = wi[:, o_go:o_glu].astype(BF16)
    w_glu = wi[:, o_glu:].astype(BF16)
    row2 = lambda v: v.astype(F32).reshape(1, -1)

    tm1 = _pick_tile(lp, min(320, seq), BF16_ROWS)
    nt1 = lp // tm1
    assert n_head % SUBLANES == 0 and n_head <= tm1 <= seq
    rspec = lambda cols, tm, nt: pl.BlockSpec((tm, cols), lambda b, j: (b * nt + j, 0))
    full = lambda a: pl.BlockSpec(a.shape, lambda *_: (0,) * a.ndim)
    x_rows = pl.BlockSpec((pl.Element(1), pl.Element(tm1), pl.Element(d)),
                          lambda b, j: (b, pl.multiple_of(jnp.maximum(j * tm1 - n_head, 0), SUBLANES), 0))
    front_in = [head, x, row2(norm_mix_g[0]), w_qkv, w_ab, w_go, w_glu, w_merge[0].astype(BF16),
                row2(b_glu[0]), row2(b_merge[0]), _pad_lanes(a_log[0]), _pad_lanes(dt_bias[0])]
    hp, qkv_pre, gbeta, go_act, u_glu, gates = pl.pallas_call(
        functools.partial(_front_kernel, tm=tm1, meta_pad=meta_pad, heads=heads, conv_ch=conv_ch),
        grid=(bsz, nt1),
        in_specs=[full(head), x_rows] + [full(a) for a in front_in[2:]],
        out_specs=[rspec(d, tm1, nt1), rspec(o_a, tm1, nt1), rspec(LANES, tm1, nt1), rspec(vw, tm1, nt1),
                   rspec(conv_ch, tm1, nt1), rspec(2 * d, tm1, nt1)],
        out_shape=[jax.ShapeDtypeStruct((rows, d), F32),
                   jax.ShapeDtypeStruct((rows, o_a), BF16), jax.ShapeDtypeStruct((rows, LANES), F32),
                   jax.ShapeDtypeStruct((rows, vw), BF16), jax.ShapeDtypeStruct((rows, conv_ch), BF16),
                   jax.ShapeDtypeStruct((rows, 2 * d), BF16)],
        compiler_params=_params(("arbitrary", "arbitrary")),
        name="front",
    )(*front_in)

    tmc = _pick_tile(lp, 320, 32)
    ntc = lp // tmc
    halo, lead = 32, BF16_ROWS
    conv_w8 = jnp.repeat(conv_dw_w[0].astype(F32), SUBLANES, axis=0)
    conv_in = [u_glu, gates, conv_w8, row2(conv_dw_b[0]), row2(conv_ln_g[0]),
               row2(conv_ln_b[0]), w_conv_out[0].astype(BF16), row2(b_conv_out[0])]
    ybg = pl.pallas_call(
        functools.partial(_conv_kernel, tm=tmc, kw=kw, halo=halo, lead=lead, rb=32),
        grid=(bsz, ntc),
        in_specs=[rspec(conv_ch, tmc, ntc),
                  pl.BlockSpec((tmc, d), lambda b, j: (b * ntc + j, 1))] + [full(a) for a in conv_in[2:]],
        out_specs=rspec(d, tmc, ntc),
        out_shape=jax.ShapeDtypeStruct((rows, d), BF16),
        scratch_shapes=[pltpu.VMEM((lead + halo + tmc, conv_ch), BF16),
                        pltpu.VMEM((SUBLANES, lead + halo + tmc, conv_ch), F32),
                        pltpu.VMEM((tmc, conv_ch), F32)],
        compiler_params=_params(("arbitrary", "arbitrary")),
        name="convmod",
    )(*conv_in)

    nc = lp // CHUNK
    nbd = max(n for n in (1, 2, 4) if bsz % n == 0)
    cspec = lambda cols: pl.BlockSpec((nbd, CHUNK, cols), lambda b, c: (b, c, 0))
    delta_in = [qkv_pre.reshape(bsz, lp, o_a), gbeta.reshape(bsz, lp, LANES), go_act.reshape(bsz, lp, vw),
                conv_qkv_w[0].astype(F32), row2(dn_norm_g[0])]
    o_dn = pl.pallas_call(
        functools.partial(_delta_kernel, nb=nbd, heads=heads, dk=dk, dv=dv, ch=CHUNK, ksz=ksz, halo=BF16_ROWS),
        grid=(bsz // nbd, nc),
        in_specs=[cspec(o_a), cspec(LANES), cspec(vw), full(delta_in[3]), full(delta_in[4])],
        out_specs=cspec(vw),
        out_shape=jax.ShapeDtypeStruct((bsz, lp, vw), BF16),
        scratch_shapes=[pltpu.VMEM((nbd, heads, dk, dv), F32),
                        pltpu.VMEM((nbd, CHUNK + BF16_ROWS, o_a), BF16)],
        compiler_params=_params(("arbitrary", "arbitrary")),
        name="delta",
    )(*delta_in).reshape(rows, vw)

    tm3 = _pick_tile(lp, 320, BF16_ROWS)
    nt3 = lp // tm3
    w_r = jnp.pad(w_router[0], ((0, 0), (0, LANES - n_exp))).astype(BF16)
    mid_in = [hp, o_dn, gates, ybg, w_dn_out[0].astype(BF16), w_out[0].astype(BF16),
              row2(norm_ffn_g[0]), w_r, _pad_lanes(b_router[0])]
    h2, hn2, topi, topg = pl.pallas_call(
        functools.partial(_mid_kernel, tm=tm3, n_exp=n_exp, parts=2 if tm3 % (2 * BF16_ROWS) == 0 else 1),
        grid=(bsz, nt3),
        in_specs=[rspec(d, tm3, nt3), rspec(vw, tm3, nt3), rspec(d, tm3, nt3), rspec(d, tm3, nt3)]
        + [full(a) for a in mid_in[4:]],
        out_specs=[rspec(d, tm3, nt3), rspec(d, tm3, nt3), rspec(LANES, tm3, nt3), rspec(LANES, tm3, nt3)],
        out_shape=[jax.ShapeDtypeStruct((rows, d), F32), jax.ShapeDtypeStruct((rows, d), F32),
                   jax.ShapeDtypeStruct((rows, LANES), I32), jax.ShapeDtypeStruct((rows, LANES), F32)],
        compiler_params=_params(("arbitrary", "arbitrary")),
        name="mid",
    )(*mid_in)

    tmd = _pick_tile(seq, TOKEN_TILE, SUBLANES)
    tpb = seq // tmd
    ntd = bsz * tpb
    tokens = bsz * seq
    width = tmd * TOP_K + n_exp * SUBLANES
    row0 = lambda i: pl.multiple_of((i // tpb) * lp + n_head + (i % tpb) * tmd, SUBLANES)
    prow = lambda r, cols, f=row0: pl.BlockSpec((pl.Element(r), pl.Element(cols)), lambda i, *_: (f(i), 0))
    tspec = pl.BlockSpec((tmd, LANES), lambda i, *_: (i, 0))
    ptiles = _pick_tile(tpb, 5, 1)
    pspec = pl.BlockSpec((ptiles * tmd, LANES), lambda i: (i, 0))
    tab = pl.BlockSpec((ptiles, 1, LANES), lambda i: (i, 0, 0))
    tab_shape = jax.ShapeDtypeStruct((ntd, 1, LANES), I32)
    pos, pos_t, n8_t, off8_t, carry8_t, tot8 = pl.pallas_call(
        functools.partial(_plan_kernel, tm=tmd, tiles=ptiles, n_exp=n_exp),
        grid=(ntd // ptiles,),
        in_specs=[prow(ptiles * tmd, LANES, lambda i: row0(i * ptiles))],
        out_specs=[pspec, pl.BlockSpec((ptiles, SUBLANES, tmd), lambda i: (i, 0, 0)), tab, tab, tab,
                   pl.BlockSpec((1, LANES), lambda i: (0, 0))],
        out_shape=[jax.ShapeDtypeStruct((tokens, LANES), I32), jax.ShapeDtypeStruct((ntd, SUBLANES, tmd), I32),
                   tab_shape, tab_shape, tab_shape, jax.ShapeDtypeStruct((1, LANES), F32)],
        scratch_shapes=[pltpu.VMEM((1, LANES), F32)],
        compiler_params=_params(("arbitrary",)),
        name="plan",
    )(topi)

    n_blocks = -(-(tokens * TOP_K + n_exp * (SUBLANES - 1) * ntd) // MOE_BLOCK) + n_exp
    cap = n_blocks * MOE_BLOCK
    counts = tot8[0, :n_exp].astype(I32)
    padded = (counts + MOE_BLOCK - 1) // MOE_BLOCK * MOE_BLOCK
    pend = jnp.cumsum(padded).astype(I32)
    pstart = pend - padded
    n_used = (pend[-1] // MOE_BLOCK).astype(I32).reshape(1)
    block_start = jnp.arange(n_blocks, dtype=I32) * MOE_BLOCK
    block_e = jnp.minimum(jnp.sum((pend[None, :] <= block_start[:, None]).astype(I32), axis=1), n_exp - 1)
    dbase = (carry8_t.reshape(ntd, LANES) + jnp.pad(pstart, (0, LANES - n_exp))[None, :]).reshape(ntd * LANES)
    n8_f = n8_t.reshape(ntd * LANES)
    off8_f = off8_t.reshape(ntd * LANES)

    anyspec = pl.BlockSpec(memory_space=pl.ANY)
    stab = lambda f: pl.BlockSpec((LANES,), lambda i, *_: (f(i),), memory_space=pltpu.SMEM)
    cur = lambda i: i
    xs = pl.pallas_call(
        functools.partial(_dispatch_kernel, tm=tmd, width=width, n_exp=n_exp, n_steps=ntd, n_blocks=n_blocks),
        grid_spec=pltpu.PrefetchScalarGridSpec(
            num_scalar_prefetch=2, grid=(ntd,),
            in_specs=[stab(cur), stab(cur), stab(cur), prow(tmd, d),
                      pl.BlockSpec((1, SUBLANES, tmd), lambda i, *_: (i, 0, 0))],
            out_specs=anyspec,
            scratch_shapes=[pltpu.VMEM((MOE_BLOCK, d), F32), pltpu.VMEM((2, width, d), F32),
                            pltpu.SMEM((2,), I32), pltpu.SemaphoreType.DMA(()), pltpu.SemaphoreType.DMA((2,))]),
        out_shape=jax.ShapeDtypeStruct((cap, d), F32),
        compiler_params=_params(("arbitrary",)),
        name="dispatch",
    )(pend, padded, n8_f, off8_f, dbase, hn2, pos_t)

    last_used = lambda i, nbu: jnp.minimum(i, nbu[0] - 1)
    ew = lambda a: pl.BlockSpec((1,) + a.shape[1:], lambda i, be, nbu: (be[i],) + (0,) * (a.ndim - 1))
    e_in = [w_gate_up[0], b_gate_up[0].astype(F32)[:, None, :], w_down[0], b_down[0].astype(F32)[:, None, :]]
    ys = pl.pallas_call(
        functools.partial(_expert_kernel, d_ff=d_ff, parts=2),
        grid_spec=pltpu.PrefetchScalarGridSpec(
            num_scalar_prefetch=2, grid=(n_blocks,),
            in_specs=[pl.BlockSpec((MOE_BLOCK, d), lambda i, be, nbu: (last_used(i, nbu), 0))]
            + [ew(a) for a in e_in],
            out_specs=pl.BlockSpec((MOE_BLOCK, d), lambda i, be, nbu: (i, 0)),
            scratch_shapes=[pltpu.VMEM((d, 2 * d_ff), BF16), pltpu.VMEM((d_ff, d), BF16)]),
        out_shape=jax.ShapeDtypeStruct((cap, d), F32),
        compiler_params=_params(("arbitrary",)),
        name="experts",
    )(block_e, n_used, xs, *e_in)

    nxt = lambda i: jnp.minimum(i + 1, ntd - 1)
    out = pl.pallas_call(
        functools.partial(_combine_kernel, tm=tmd, width=width, n_exp=n_exp, n_steps=ntd),
        grid=(ntd,),
        in_specs=[stab(cur), stab(cur), stab(cur), stab(nxt), stab(nxt), stab(nxt), tspec,
                  prow(tmd, d), prow(tmd, LANES), pl.BlockSpec((1, d), lambda i: (0, 0)), anyspec],
        out_specs=pl.BlockSpec((tmd, d), lambda i: (i, 0)),
        out_shape=jax.ShapeDtypeStruct((tokens, d), x.dtype),
        scratch_shapes=[pltpu.VMEM((2, width, d), F32), pltpu.SMEM((2,), I32), pltpu.SemaphoreType.DMA((2,))],
        compiler_params=_params(("arbitrary",)),
        name="combine",
    )(n8_f, off8_f, dbase, n8_f, off8_f, dbase, pos, h2, topg, row2(final_norm_g), ys)
    return out.reshape(bsz, seq, d)
```

```python
import functools

import jax
import jax.numpy as jnp
from jax import lax
from jax.experimental import pallas as pl
from jax.experimental.pallas import tpu as pltpu

F32 = jnp.float32
BF16 = jnp.bfloat16
I32 = jnp.int32

CHUNK = 64
TOP_K = 4
MOE_BLOCK = 512
TOKEN_TILE = 256
MID_TILE = 832
MID_PARTS = 4
SWIGLU_LIMIT = 7.0
SWIGLU_ALPHA = 1.702
EPS = 1e-6
LN_EPS = 1e-5
LANES = 128
SUBLANES = 8
BF16_ROWS = 16
VMEM_LIMIT = 56 * 1024 * 1024


def _dot(a, b):
    return jnp.dot(a, b, preferred_element_type=F32)


def _dot_nt(a, b):
    return lax.dot_general(a, b, (((1,), (1,)), ((), ())), preferred_element_type=F32)


def _dot_tn(a, b):
    return lax.dot_general(a, b, (((0,), (0,)), ((), ())), preferred_element_type=F32)


def _sigmoid(x):
    return 1.0 / (1.0 + jnp.exp(-x))


def _silu(x):
    return x * _sigmoid(x)


def _softplus(x):
    return jnp.maximum(x, 0.0) + jnp.log(1.0 + jnp.exp(-jnp.abs(x)))


def _rms(x, g):
    return x * lax.rsqrt(jnp.mean(x * x, axis=-1, keepdims=True) + EPS) * g


def _iota(shape, dim):
    return lax.broadcasted_iota(I32, shape, dim)


def _pick_tile(n, target, mult):
    best = None
    for t in range(mult, min(n, target) + 1, mult):
        if n % t == 0:
            best = t
    assert best is not None, (n, target, mult)
    return best


def _params(sem, flags=None):
    return pltpu.CompilerParams(dimension_semantics=sem, vmem_limit_bytes=VMEM_LIMIT, flags=flags)


def _front_kernel(head_ref, x_ref, g_ref, wqkv_ref, wab_ref, wgo_ref, wglu_ref, wmg_ref, bglu_ref, bmg_ref,
                  alog_ref, dtb_ref, h_ref, qkv_ref, gb_ref, go_ref, u_ref, gates_ref,
                  *, tm, meta_pad, heads, conv_ch):
    j = pl.program_id(1)
    xb = x_ref[0]
    n_head = head_ref.shape[0]
    h = jnp.where(j == 0, jnp.concatenate([head_ref[...], xb[:tm - n_head]], axis=0), xb)
    h_ref[...] = h
    hn = _rms(h, g_ref[...]).astype(BF16)
    qkv_ref[...] = _dot(hn, wqkv_ref[...]).astype(BF16)
    ab = _dot(hn, wab_ref[...])
    valid = (j * tm + _iota((tm, 1), 0)) >= meta_pad
    lane = _iota((1, LANES), 1)
    g = -jnp.exp(alog_ref[...]) * _softplus(ab + dtb_ref[...])
    gb = jnp.where(lane < heads, g, _sigmoid(ab))
    gb_ref[...] = jnp.where(valid, gb, 0.0)
    go_ref[...] = _silu(_dot(hn, wgo_ref[...])).astype(BF16)
    glu = _dot(hn, wglu_ref[...]) + bglu_ref[...]
    u = glu[:, :conv_ch] * _sigmoid(glu[:, conv_ch:])
    u_ref[...] = jnp.where(valid, u, 0.0).astype(BF16)
    gates_ref[...] = _sigmoid(_dot(hn, wmg_ref[...]) + bmg_ref[...]).astype(BF16)


def _conv_kernel(u_ref, gate_ref, w_ref, b_ref, lng_ref, lnb_ref, wo_ref, bo_ref, y_ref,
                 xcat_ref, xs_ref, cbuf_ref, *, tm, kw, halo, lead, rb):
    j = pl.program_id(1)
    base = lead + halo
    total = base + tm
    n_c = xcat_ref.shape[1]

    @pl.when(j == 0)
    def _():
        xcat_ref[0:base, :] = jnp.zeros((base, n_c), BF16)

    @pl.when(j > 0)
    def _():
        xcat_ref[lead:base, :] = xcat_ref[tm + lead:tm + base, :]

    xcat_ref[base:total, :] = u_ref[...]
    sr = _iota((SUBLANES * rb, lead + rb), 0)
    sc = _iota((SUBLANES * rb, lead + rb), 1)
    shift = (sc == (sr & (rb - 1)) + lead - (sr >> (rb.bit_length() - 1))).astype(BF16)
    bias = b_ref[...]
    for t0 in range(lead, total, rb):
        res = _dot(shift, xcat_ref[t0 - lead:t0 + rb, :])
        for r in range(SUBLANES):
            xs_ref[r, t0:t0 + rb, :] = res[r * rb:(r + 1) * rb, :]
        if t0 < base:
            continue
        i0 = t0
        blk = (i0 - base) // rb
        acc = jnp.zeros((rb, n_c), F32)
        for s in range(kw):
            a, r = divmod(s, SUBLANES)
            wt = w_ref[(kw - 1 - s) * SUBLANES:(kw - s) * SUBLANES, :]
            acc = acc + xs_ref[r, i0 - SUBLANES * a:i0 - SUBLANES * a + rb, :] * jnp.concatenate(
                [wt] * (rb // SUBLANES), axis=0)
        cbuf_ref[blk * rb:(blk + 1) * rb, :] = acc + bias
    c = cbuf_ref[...]
    mu = jnp.mean(c, axis=-1, keepdims=True)
    xc = c - mu
    ln = xc * lax.rsqrt(jnp.mean(xc * xc, axis=-1, keepdims=True) + LN_EPS) * lng_ref[...] + lnb_ref[...]
    yb = _dot(_silu(ln).astype(BF16), wo_ref[...]) + bo_ref[...]
    y_ref[...] = (gate_ref[...].astype(F32) * yb).astype(BF16)


def _bdot(a, b):
    return jnp.dot(a.astype(BF16), b.astype(BF16), preferred_element_type=F32)


def _split3(x):
    x1 = x.astype(BF16)
    r1 = x - x1.astype(F32)
    x2 = r1.astype(BF16)
    x3 = (r1 - x2.astype(F32)).astype(BF16)
    return x1, x2, x3


def _unit_lower_inverses_minus_eye(mats, row, col):
    ch = mats[0].shape[0]
    blk16 = (row >> 4) == (col >> 4)
    blk32 = (row >> 5) == (col >> 5)
    n1 = [jnp.where(blk16, a, 0.0) for a in mats]
    n2 = [_bdot(n, n) for n in n1]
    r = [_bdot(jnp.concatenate([n, m], axis=0), m) for n, m in zip(n1, n2)]
    n4 = [x[ch:] for x in r]
    y = [m - n - x[:ch] for n, m, x in zip(n1, n2, r)]
    r = [_bdot(jnp.concatenate([yy, m], axis=0), m) for yy, m in zip(y, n4)]
    n8 = [x[ch:] for x in r]
    y = [yy + m + x[:ch] for yy, m, x in zip(y, n4, r)]
    y = [yy + m + _bdot(yy, m) for yy, m in zip(y, n8)]
    for mask in (blk32 & jnp.logical_not(blk16), jnp.logical_not(blk32)):
        ls = [jnp.where(mask, a, 0.0) for a in mats]
        ms = [l + _bdot(l, yy) for l, yy in zip(ls, y)]
        y = [yy - m - _bdot(yy, m) for yy, m in zip(y, ms)]
    return y


def _delta_kernel(qkv_ref, gb_ref, go_ref, cw_ref, ng_ref, o_ref, s_ref, xc_ref,
                  *, nb, heads, dk, dv, ch, ksz, halo):
    c = pl.program_id(1)
    qk_w = heads * dk

    @pl.when(c == 0)
    def _():
        s_ref[...] = jnp.zeros(s_ref.shape, F32)
        xc_ref[:, 0:halo, :] = jnp.zeros((nb, halo, xc_ref.shape[2]), BF16)

    @pl.when(c > 0)
    def _():
        xc_ref[:, 0:halo, :] = xc_ref[:, ch:ch + halo, :]

    xc_ref[:, halo:halo + ch, :] = qkv_ref[...]
    row = _iota((ch, ch), 0)
    col = _iota((ch, ch), 1)
    incl = row >= col
    strict = row > col
    tril = incl.astype(BF16)
    triu = (row <= col).astype(BF16)
    sr = _iota(((ksz - 1) * ch, halo + ch), 0)
    sc = _iota(((ksz - 1) * ch, halo + ch), 1)
    shift = (sc == (sr & (ch - 1)) + (sr >> (ch.bit_length() - 1)) + (halo - (ksz - 1))).astype(BF16)

    qn, kn, kb, vb, gamma, eg, ekd, elast = [], [], [], [], [], [], [], []
    shifted = [_dot(shift, xc_ref[b]) for b in range(nb)]
    for b in range(nb):
        acc = xc_ref[b, halo:halo + ch, :].astype(F32) * cw_ref[ksz - 1:ksz, :]
        for t in range(ksz - 1):
            acc = acc + shifted[b][t * ch:(t + 1) * ch, :] * cw_ref[t:t + 1, :]
        qkv = _silu(acc)
        gb = gb_ref[b]
        parts = _split3(gb)
        gc = sum(_dot(tril, p) for p in parts)
        gct = sum(_dot_tn(p, triu) for p in parts)
        glast = gc[ch - 1:ch, :]
        e_g = jnp.exp(gc)
        e_kd = jnp.exp(glast - gc)
        e_last = jnp.exp(glast)
        for h in range(heads):
            q = qkv[:, h * dk:(h + 1) * dk]
            k = qkv[:, qk_w + h * dk:qk_w + (h + 1) * dk]
            v = qkv[:, 2 * qk_w + h * dv:2 * qk_w + (h + 1) * dv]
            qn.append(q * lax.rsqrt(jnp.sum(q * q, axis=-1, keepdims=True) + EPS) * (dk ** -0.5))
            kn.append(k * lax.rsqrt(jnp.sum(k * k, axis=-1, keepdims=True) + EPS))
            beta = gb[:, heads + h:heads + h + 1]
            kb.append(kn[-1] * beta)
            vb.append(v * beta)
            diff = gc[:, h:h + 1] - gct[h:h + 1, :]
            gamma.append(jnp.where(incl, jnp.exp(jnp.where(incl, diff, 0.0)), 0.0))
            eg.append(e_g[:, h:h + 1])
            ekd.append(e_kd[:, h:h + 1])
            elast.append(e_last[:, h:h + 1])

    n = nb * heads
    kq = [_dot_nt(jnp.concatenate([kb[i], qn[i]], axis=0).astype(BF16), kn[i].astype(BF16))
          for i in range(n)]
    a = [jnp.where(strict, kq[i][:ch] * gamma[i], 0.0) for i in range(n)]
    qk = [kq[i][ch:] * gamma[i] for i in range(n)]
    y = _unit_lower_inverses_minus_eye(a, row, col)
    rhs = [jnp.concatenate([vb[i], kb[i] * eg[i]], axis=-1) for i in range(n)]
    uw = [rhs[i] + _bdot(y[i], rhs[i]) for i in range(n)]
    s = [s_ref[i // heads, i % heads] for i in range(n)]
    ws = [_bdot(jnp.concatenate([uw[i][:, dv:], qn[i] * eg[i]], axis=0), s[i]) for i in range(n)]
    v_new = [uw[i][:, :dv] - ws[i][:ch] for i in range(n)]
    o = [ws[i][ch:] + _bdot(qk[i], v_new[i]) for i in range(n)]
    s_new = [s[i] * elast[i] + _dot_tn((kn[i] * ekd[i]).astype(BF16), v_new[i].astype(BF16))
             for i in range(n)]
    for i in range(n):
        b, h = i // heads, i % heads
        s_ref[b, h] = s_new[i]
        on = _rms(o[i], ng_ref[...])
        o_ref[b, :, h * dv:(h + 1) * dv] = (
            on * go_ref[b, :, h * dv:(h + 1) * dv].astype(F32)).astype(BF16)


def _mid_kernel(h_ref, o_ref, ga_ref, ybg_ref, wdn_ref, wout_ref, g2_ref, wr_ref, br_ref,
                h2_ref, hn_ref, topi_ref, topg_ref, *, tm, n_exp, parts):
    rs = tm // parts
    sl = [pl.ds(p * rs, rs) for p in range(parts)]
    ya = [_dot(o_ref[s, :], wdn_ref[...]) for s in sl]
    y = [ga_ref[s, :].astype(F32) * a + ybg_ref[s, :].astype(F32) for s, a in zip(sl, ya)]
    h2 = [h_ref[s, :] + _dot(v.astype(BF16), wout_ref[...]) for s, v in zip(sl, y)]
    hn = [_rms(v, g2_ref[...]) for v in h2]
    for s, v, w in zip(sl, h2, hn):
        h2_ref[s, :] = v
        hn_ref[s, :] = w
    logits = [_dot(v.astype(BF16), wr_ref[...]) + br_ref[...] for v in hn]
    lane = _iota((rs, LANES), 1)
    lane_f = lane.astype(F32)
    l = [jnp.where(lane < n_exp, v, -jnp.inf) for v in logits]
    vals = [[] for _ in range(parts)]
    idxs = [[] for _ in range(parts)]
    for _ in range(TOP_K):
        m = [jnp.max(v, axis=-1, keepdims=True) for v in l]
        idx = [jnp.min(jnp.where(v == mm, lane_f, float(LANES)), axis=-1, keepdims=True)
               for v, mm in zip(l, m)]
        l = [jnp.where(lane_f == ii, -jnp.inf, v) for v, ii in zip(l, idx)]
        for p in range(parts):
            vals[p].append(m[p])
            idxs[p].append(idx[p])
    for p in range(parts):
        es = [jnp.exp(v - vals[p][0]) for v in vals[p]]
        den = es[0]
        for e in es[1:]:
            den = den + e
        topi = jnp.zeros((rs, LANES), F32)
        topg = jnp.zeros((rs, LANES), F32)
        for k in range(TOP_K):
            topi = jnp.where(lane == k, idxs[p][k], topi)
            topg = jnp.where(lane == k, es[k] / den, topg)
        topi_ref[sl[p], :] = topi.astype(I32)
        topg_ref[sl[p], :] = topg


def _choice_mask(topi, lane, n_exp):
    m = jnp.zeros(topi.shape, F32)
    for k in range(TOP_K):
        m = m + (lane == topi[:, k:k + 1]).astype(F32)
    return jnp.where(lane < n_exp, m, 0.0)


def _plan_kernel(topi_ref, pos_ref, post_ref, n8_ref, off8_ref, carry8_ref, tot8_ref, carry_ref,
                 *, tm, tiles, n_exp):
    i = pl.program_id(0)

    @pl.when(i == 0)
    def _():
        carry_ref[...] = jnp.zeros(carry_ref.shape, F32)

    lane = _iota((tm, LANES), 1)
    tri = (_iota((tm, tm), 0) >= _iota((tm, tm), 1)).astype(BF16)
    before = (_iota((LANES, LANES), 0) < _iota((LANES, LANES), 1)).astype(BF16)
    carry = carry_ref[...]
    for j in range(tiles):
        rows = pl.ds(j * tm, tm)
        topi = topi_ref[rows, :]
        m = _choice_mask(topi, lane, n_exp)
        cum = _dot(tri, m.astype(BF16))
        n8 = jnp.floor((cum[tm - 1:tm, :] + (SUBLANES - 1)) * (1.0 / SUBLANES)) * SUBLANES
        off8 = _dot(jnp.broadcast_to(n8, (SUBLANES, LANES)).astype(BF16), before)[0:1]
        posf = cum - m + off8
        pos = jnp.zeros((tm, LANES), F32)
        for k in range(TOP_K):
            pk = jnp.sum(jnp.where(lane == topi[:, k:k + 1], posf, 0.0), axis=-1, keepdims=True)
            pos = jnp.where(lane == k, pk, pos)
        pos_ref[rows, :] = pos.astype(I32)
        post_ref[j] = jnp.transpose(pos)[0:SUBLANES, :].astype(I32)
        n8_ref[j] = n8.astype(I32)
        off8_ref[j] = off8.astype(I32)
        carry8_ref[j] = carry.astype(I32)
        carry = carry + n8
    carry_ref[...] = carry
    tot8_ref[...] = carry


def _pow2_floor(n):
    return 1 << (n.bit_length() - 1)


def _strip_pieces(n8, max_rows):
    pieces = []
    b = max_rows
    while b >= SUBLANES:
        pieces.append(((n8 & b) != 0, n8 & ~(2 * b - 1), b))
        b //= 2
    return pieces


LONG_PIECE = 64


def _copy_strip(n8, start_piece):
    n_long = n8 >> (LONG_PIECE.bit_length() - 1)

    def long_piece(j, carry):
        start_piece(j * LONG_PIECE, LONG_PIECE)
        return carry

    lax.fori_loop(0, n_long, long_piece, 0)
    for cond, ofs, rows in _strip_pieces(n8 & (LONG_PIECE - 1), LONG_PIECE // 2):
        @pl.when(cond)
        def _():
            start_piece(n_long * LONG_PIECE + ofs, rows)


def _wait_rows(total, make_copy, max_rows):
    b = max_rows
    while b >= SUBLANES:
        @pl.when((total & b) != 0)
        def _():
            make_copy(b).wait()
        b //= 2


def _aligned(start, rows):
    return pl.ds(pl.multiple_of(start, SUBLANES), rows)


def _dispatch_kernel(pend_ref, padded_ref, n8_ref, off8_ref, dbase_ref, hn_ref, post_ref, xs_ref,
                     zbuf_ref, sbuf_ref, tot_ref, zsem, sems, *, tm, width, n_exp, n_steps, n_blocks):
    i = pl.program_id(0)
    slot = i % 2

    def zero_block(start):
        return pltpu.make_async_copy(zbuf_ref, xs_ref.at[_aligned(start, MOE_BLOCK), :], zsem)

    @pl.when(i == 0)
    def _():
        zbuf_ref[...] = jnp.zeros(zbuf_ref.shape, F32)
        for e in range(n_exp):
            @pl.when(padded_ref[e] > 0)
            def _():
                zero_block(pend_ref[e] - MOE_BLOCK).start()
        first_unused = pend_ref[n_exp - 1] // MOE_BLOCK

        def start_tail(j, carry):
            zero_block(j * MOE_BLOCK).start()
            return carry

        lax.fori_loop(first_unused, n_blocks, start_tail, 0)
        for e in range(n_exp):
            @pl.when(padded_ref[e] > 0)
            def _():
                zero_block(0).wait()

        def wait_tail(j, carry):
            zero_block(0).wait()
            return carry

        lax.fori_loop(first_unused, n_blocks, wait_tail, 0)

    def strip_wait(s):
        _wait_rows(tot_ref[s], lambda b: pltpu.make_async_copy(
            sbuf_ref.at[s, pl.ds(0, b), :], xs_ref.at[pl.ds(0, b), :], sems.at[s]), _pow2_floor(width))

    @pl.when(i >= 2)
    def _():
        strip_wait(slot)

    post = post_ref[0]
    orow = _iota((width, tm), 0)
    sel = orow == post[0:1, :]
    for k in range(1, TOP_K):
        sel = jnp.logical_or(sel, orow == post[k:k + 1, :])
    sbuf_ref[slot] = _dot(sel.astype(BF16), hn_ref[...].astype(BF16))

    total = 0
    for e in range(n_exp):
        n8, so, do = n8_ref[e], off8_ref[e], dbase_ref[e]
        total = total + n8
        _copy_strip(n8, lambda ofs, rows, so=so, do=do: pltpu.make_async_copy(
            sbuf_ref.at[slot, _aligned(so + ofs, rows), :], xs_ref.at[_aligned(do + ofs, rows), :],
            sems.at[slot]).start())
    tot_ref[slot] = total

    @pl.when(i == n_steps - 1)
    def _():
        if n_steps >= 2:
            strip_wait(1 - slot)
        strip_wait(slot)


def _expert_kernel(be_ref, nbu_ref, x_ref, wgu_ref, bgu_ref, wd_ref, bd_ref, y_ref, wgu16_ref, wd16_ref,
                   *, d_ff, parts):
    i = pl.program_id(0)
    used = i < nbu_ref[0]
    new_expert = jnp.logical_or(i == 0, be_ref[i] != be_ref[jnp.maximum(i - 1, 0)])

    @pl.when(jnp.logical_and(used, new_expert))
    def _():
        wgu16_ref[...] = wgu_ref[0].astype(BF16)
        wd16_ref[...] = wd_ref[0].astype(BF16)

    @pl.when(used)
    def _():
        rs = x_ref.shape[0] // parts
        sl = [pl.ds(p * rs, rs) for p in range(parts)]
        gu = [_dot(x_ref[s, :].astype(BF16), wgu16_ref[...]) + bgu_ref[0] for s in sl]
        act = []
        for v in gu:
            gate = jnp.minimum(v[:, :d_ff], SWIGLU_LIMIT)
            up = jnp.clip(v[:, d_ff:], -SWIGLU_LIMIT, SWIGLU_LIMIT)
            act.append(((up + 1.0) * gate * _sigmoid(SWIGLU_ALPHA * gate)).astype(BF16))
        for s, a in zip(sl, act):
            y_ref[s, :] = _dot(a, wd16_ref[...]) + bd_ref[0]

    @pl.when(jnp.logical_not(used))
    def _():
        y_ref[...] = jnp.zeros(y_ref.shape, F32)


def _combine_kernel(n8c_ref, off8c_ref, dbc_ref, n8n_ref, off8n_ref, dbn_ref, pos_ref, h2_ref, topg_ref, fg_ref,
                    ys_ref, out_ref, gbuf_ref, tot_ref, sems, *, tm, width, n_exp, n_steps):
    i = pl.program_id(0)
    slot = i % 2

    def fetch(n8_ref, off8_ref, db_ref, s):
        total = 0
        for e in range(n_exp):
            n8, so, do = n8_ref[e], off8_ref[e], db_ref[e]
            total = total + n8
            _copy_strip(n8, lambda ofs, rows, so=so, do=do: pltpu.make_async_copy(
                ys_ref.at[_aligned(do + ofs, rows), :], gbuf_ref.at[s, _aligned(so + ofs, rows), :],
                sems.at[s]).start())
        tot_ref[s] = total

    @pl.when(i == 0)
    def _():
        gbuf_ref[...] = jnp.zeros(gbuf_ref.shape, F32)
        fetch(n8c_ref, off8c_ref, dbc_ref, 0)

    @pl.when(i + 1 < n_steps)
    def _():
        fetch(n8n_ref, off8n_ref, dbn_ref, 1 - slot)

    _wait_rows(tot_ref[slot], lambda b: pltpu.make_async_copy(
        ys_ref.at[pl.ds(0, b), :], gbuf_ref.at[slot, pl.ds(0, b), :], sems.at[slot]), _pow2_floor(width))
    pos = pos_ref[...]
    topg = topg_ref[...]
    lane = _iota((tm, width), 1)
    g = jnp.zeros((tm, width), F32)
    for k in range(TOP_K):
        g = g + jnp.where(lane == pos[:, k:k + 1], topg[:, k:k + 1], 0.0)
    moe = _dot(g.astype(BF16), gbuf_ref[slot].astype(BF16))
    out_ref[...] = _rms(h2_ref[...] + moe, fg_ref[...])


def _pad_lanes(v, fill=0.0):
    v = v.astype(F32)
    return jnp.concatenate([v, jnp.full((LANES - v.shape[0],), fill, F32)])[None, :]


def kernel(x, meta_tokens, norm_mix_g, w_in, conv_qkv_w, a_log, dt_bias, dn_norm_g, w_dn_out, b_glu,
           conv_dw_w, conv_dw_b, conv_ln_g, conv_ln_b, w_conv_out, b_conv_out, w_merge, b_merge, w_out,
           norm_ffn_g, w_router, b_router, w_gate_up, b_gate_up, w_down, b_down, final_norm_g):
    bsz, seq, d = x.shape
    depth = w_in.shape[0]
    n_meta = meta_tokens.shape[0]
    heads = a_log.shape[1]
    vw = w_dn_out.shape[1]
    qk_w = (conv_qkv_w.shape[2] - vw) // 2
    dk, dv = qk_w // heads, vw // heads
    ksz = conv_qkv_w.shape[1]
    kw, conv_ch = conv_dw_w.shape[1], conv_dw_w.shape[2]
    n_exp, d_ff = w_gate_up.shape[1], w_down.shape[2]
    meta_pad = CHUNK - n_meta
    lp = seq + n_meta + meta_pad
    assert depth == 1 and lp % CHUNK == 0 and dk == LANES and dv == LANES and d % LANES == 0
    assert 2 * heads <= LANES and n_exp <= LANES and kw - 1 <= 32 and ksz - 1 <= SUBLANES
    rows = bsz * lp
    n_head = meta_pad + n_meta
    head = jnp.concatenate([jnp.zeros((meta_pad, d), x.dtype), meta_tokens.astype(x.dtype)], axis=0)

    wi = w_in[0]
    o_a = 2 * qk_w + vw
    o_go = o_a + 2 * heads
    o_glu = o_go + vw
    w_qkv = wi[:, :o_a].astype(BF16)
    w_ab = jnp.pad(wi[:, o_a:o_go], ((0, 0), (0, LANES - 2 * heads))).astype(BF16)
    w_go = wi[:, o_go:o_glu].astype(BF16)
    w_glu = wi[:, o_glu:].astype(BF16)
    row2 = lambda v: v.astype(F32).reshape(1, -1)

    tm1 = _pick_tile(lp, min(320, seq), BF16_ROWS)
    nt1 = lp // tm1
    assert n_head % SUBLANES == 0 and n_head <= tm1 <= seq
    rspec = lambda cols, tm, nt: pl.BlockSpec((tm, cols), lambda b, j: (b * nt + j, 0))
    full = lambda a: pl.BlockSpec(a.shape, lambda *_: (0,) * a.ndim)
    x_rows = pl.BlockSpec((pl.Element(1), pl.Element(tm1), pl.Element(d)),
                          lambda b, j: (b, pl.multiple_of(jnp.maximum(j * tm1 - n_head, 0), SUBLANES), 0))
    front_in = [head, x, row2(norm_mix_g[0]), w_qkv, w_ab, w_go, w_glu, w_merge[0].astype(BF16),
                row2(b_glu[0]), row2(b_merge[0]), _pad_lanes(a_log[0]), _pad_lanes(dt_bias[0])]
    hp, qkv_pre, gbeta, go_act, u_glu, gates = pl.pallas_call(
        functools.partial(_front_kernel, tm=tm1, meta_pad=meta_pad, heads=heads, conv_ch=conv_ch),
        grid=(bsz, nt1),
        in_specs=[full(head), x_rows] + [full(a) for a in front_in[2:]],
        out_specs=[rspec(d, tm1, nt1), rspec(o_a, tm1, nt1), rspec(LANES, tm1, nt1), rspec(vw, tm1, nt1),
                   rspec(conv_ch, tm1, nt1), rspec(2 * d, tm1, nt1)],
        out_shape=[jax.ShapeDtypeStruct((rows, d), F32),
                   jax.ShapeDtypeStruct((rows, o_a), BF16), jax.ShapeDtypeStruct((rows, LANES), F32),
                   jax.ShapeDtypeStruct((rows, vw), BF16), jax.ShapeDtypeStruct((rows, conv_ch), BF16),
                   jax.ShapeDtypeStruct((rows, 2 * d), BF16)],
        compiler_params=_params(("arbitrary", "arbitrary")),
        name="front",
    )(*front_in)

    tmc = _pick_tile(lp, 320, 32)
    ntc = lp // tmc
    halo, lead = 32, BF16_ROWS
    conv_w8 = jnp.repeat(conv_dw_w[0].astype(F32), SUBLANES, axis=0)
    conv_in = [u_glu, gates, conv_w8, row2(conv_dw_b[0]), row2(conv_ln_g[0]),
               row2(conv_ln_b[0]), w_conv_out[0].astype(BF16), row2(b_conv_out[0])]
    ybg = pl.pallas_call(
        functools.partial(_conv_kernel, tm=tmc, kw=kw, halo=halo, lead=lead, rb=32),
        grid=(bsz, ntc),
        in_specs=[rspec(conv_ch, tmc, ntc),
                  pl.BlockSpec((tmc, d), lambda b, j: (b * ntc + j, 1))] + [full(a) for a in conv_in[2:]],
        out_specs=rspec(d, tmc, ntc),
        out_shape=jax.ShapeDtypeStruct((rows, d), BF16),
        scratch_shapes=[pltpu.VMEM((lead + halo + tmc, conv_ch), BF16),
                        pltpu.VMEM((SUBLANES, lead + halo + tmc, conv_ch), F32),
                        pltpu.VMEM((tmc, conv_ch), F32)],
        compiler_params=_params(("arbitrary", "arbitrary")),
        name="convmod",
    )(*conv_in)

    nc = lp // CHUNK
    nbd = max(n for n in (1, 2, 4) if bsz % n == 0)
    cspec = lambda cols: pl.BlockSpec((nbd, CHUNK, cols), lambda b, c: (b, c, 0))
    delta_in = [qkv_pre.reshape(bsz, lp, o_a), gbeta.reshape(bsz, lp, LANES), go_act.reshape(bsz, lp, vw),
                conv_qkv_w[0].astype(F32), row2(dn_norm_g[0])]
    o_dn = pl.pallas_call(
        functools.partial(_delta_kernel, nb=nbd, heads=heads, dk=dk, dv=dv, ch=CHUNK, ksz=ksz, halo=BF16_ROWS),
        grid=(bsz // nbd, nc),
        in_specs=[cspec(o_a), cspec(LANES), cspec(vw), full(delta_in[3]), full(delta_in[4])],
        out_specs=cspec(vw),
        out_shape=jax.ShapeDtypeStruct((bsz, lp, vw), BF16),
        scratch_shapes=[pltpu.VMEM((nbd, heads, dk, dv), F32),
                        pltpu.VMEM((nbd, CHUNK + BF16_ROWS, o_a), BF16)],
        compiler_params=_params(("arbitrary", "arbitrary")),
        name="delta",
    )(*delta_in).reshape(rows, vw)

    tm3 = _pick_tile(lp, MID_TILE, BF16_ROWS)
    nt3 = lp // tm3
    w_r = jnp.pad(w_router[0], ((0, 0), (0, LANES - n_exp))).astype(BF16)
    mid_in = [hp, o_dn, gates, ybg, w_dn_out[0].astype(BF16), w_out[0].astype(BF16),
              row2(norm_ffn_g[0]), w_r, _pad_lanes(b_router[0])]
    h2, hn2, topi, topg = pl.pallas_call(
        functools.partial(_mid_kernel, tm=tm3, n_exp=n_exp, parts=MID_PARTS if tm3 % (MID_PARTS * BF16_ROWS) == 0 else 1),
        grid=(bsz, nt3),
        in_specs=[rspec(d, tm3, nt3), rspec(vw, tm3, nt3), rspec(d, tm3, nt3), rspec(d, tm3, nt3)]
        + [full(a) for a in mid_in[4:]],
        out_specs=[rspec(d, tm3, nt3), rspec(d, tm3, nt3), rspec(LANES, tm3, nt3), rspec(LANES, tm3, nt3)],
        out_shape=[jax.ShapeDtypeStruct((rows, d), F32), jax.ShapeDtypeStruct((rows, d), F32),
                   jax.ShapeDtypeStruct((rows, LANES), I32), jax.ShapeDtypeStruct((rows, LANES), F32)],
        compiler_params=_params(("arbitrary", "arbitrary")),
        name="mid",
    )(*mid_in)

    tmd = _pick_tile(seq, TOKEN_TILE, SUBLANES)
    tpb = seq // tmd
    ntd = bsz * tpb
    tokens = bsz * seq
    width = tmd * TOP_K + n_exp * SUBLANES
    row0 = lambda i: pl.multiple_of((i // tpb) * lp + n_head + (i % tpb) * tmd, SUBLANES)
    prow = lambda r, cols, f=row0: pl.BlockSpec((pl.Element(r), pl.Element(cols)), lambda i, *_: (f(i), 0))
    tspec = pl.BlockSpec((tmd, LANES), lambda i, *_: (i, 0))
    ptiles = _pick_tile(tpb, 5, 1)
    pspec = pl.BlockSpec((ptiles * tmd, LANES), lambda i: (i, 0))
    tab = pl.BlockSpec((ptiles, 1, LANES), lambda i: (i, 0, 0))
    tab_shape = jax.ShapeDtypeStruct((ntd, 1, LANES), I32)
    pos, pos_t, n8_t, off8_t, carry8_t, tot8 = pl.pallas_call(
        functools.partial(_plan_kernel, tm=tmd, tiles=ptiles, n_exp=n_exp),
        grid=(ntd // ptiles,),
        in_specs=[prow(ptiles * tmd, LANES, lambda i: row0(i * ptiles))],
        out_specs=[pspec, pl.BlockSpec((ptiles, SUBLANES, tmd), lambda i: (i, 0, 0)), tab, tab, tab,
                   pl.BlockSpec((1, LANES), lambda i: (0, 0))],
        out_shape=[jax.ShapeDtypeStruct((tokens, LANES), I32), jax.ShapeDtypeStruct((ntd, SUBLANES, tmd), I32),
                   tab_shape, tab_shape, tab_shape, jax.ShapeDtypeStruct((1, LANES), F32)],
        scratch_shapes=[pltpu.VMEM((1, LANES), F32)],
        compiler_params=_params(("arbitrary",)),
        name="plan",
    )(topi)

    n_blocks = -(-(tokens * TOP_K + n_exp * (SUBLANES - 1) * ntd) // MOE_BLOCK) + n_exp
    cap = n_blocks * MOE_BLOCK
    counts = tot8[0, :n_exp].astype(I32)
    padded = (counts + MOE_BLOCK - 1) // MOE_BLOCK * MOE_BLOCK
    pend = jnp.cumsum(padded).astype(I32)
    pstart = pend - padded
    n_used = (pend[-1] // MOE_BLOCK).astype(I32).reshape(1)
    block_start = jnp.arange(n_blocks, dtype=I32) * MOE_BLOCK
    block_e = jnp.minimum(jnp.sum((pend[None, :] <= block_start[:, None]).astype(I32), axis=1), n_exp - 1)
    dbase = (carry8_t.reshape(ntd, LANES) + jnp.pad(pstart, (0, LANES - n_exp))[None, :]).reshape(ntd * LANES)
    n8_f = n8_t.reshape(ntd * LANES)
    off8_f = off8_t.reshape(ntd * LANES)

    anyspec = pl.BlockSpec(memory_space=pl.ANY)
    stab = lambda f: pl.BlockSpec((LANES,), lambda i, *_: (f(i),), memory_space=pltpu.SMEM)
    cur = lambda i: i
    xs = pl.pallas_call(
        functools.partial(_dispatch_kernel, tm=tmd, width=width, n_exp=n_exp, n_steps=ntd, n_blocks=n_blocks),
        grid_spec=pltpu.PrefetchScalarGridSpec(
            num_scalar_prefetch=2, grid=(ntd,),
            in_specs=[stab(cur), stab(cur), stab(cur), prow(tmd, d),
                      pl.BlockSpec((1, SUBLANES, tmd), lambda i, *_: (i, 0, 0))],
            out_specs=anyspec,
            scratch_shapes=[pltpu.VMEM((MOE_BLOCK, d), F32), pltpu.VMEM((2, width, d), F32),
                            pltpu.SMEM((2,), I32), pltpu.SemaphoreType.DMA(()), pltpu.SemaphoreType.DMA((2,))]),
        out_shape=jax.ShapeDtypeStruct((cap, d), F32),
        compiler_params=_params(("arbitrary",)),
        name="dispatch",
    )(pend, padded, n8_f, off8_f, dbase, hn2, pos_t)

    last_used = lambda i, nbu: jnp.minimum(i, nbu[0] - 1)
    ew = lambda a: pl.BlockSpec((1,) + a.shape[1:], lambda i, be, nbu: (be[i],) + (0,) * (a.ndim - 1))
    e_in = [w_gate_up[0], b_gate_up[0].astype(F32)[:, None, :], w_down[0], b_down[0].astype(F32)[:, None, :]]
    ys = pl.pallas_call(
        functools.partial(_expert_kernel, d_ff=d_ff, parts=2),
        grid_spec=pltpu.PrefetchScalarGridSpec(
            num_scalar_prefetch=2, grid=(n_blocks,),
            in_specs=[pl.BlockSpec((MOE_BLOCK, d), lambda i, be, nbu: (last_used(i, nbu), 0))]
            + [ew(a) for a in e_in],
            out_specs=pl.BlockSpec((MOE_BLOCK, d), lambda i, be, nbu: (i, 0)),
            scratch_shapes=[pltpu.VMEM((d, 2 * d_ff), BF16), pltpu.VMEM((d_ff, d), BF16)]),
        out_shape=jax.ShapeDtypeStruct((cap, d), F32),
        compiler_params=_params(("arbitrary",)),
        name="experts",
    )(block_e, n_used, xs, *e_in)

    nxt = lambda i: jnp.minimum(i + 1, ntd - 1)
    out = pl.pallas_call(
        functools.partial(_combine_kernel, tm=tmd, width=width, n_exp=n_exp, n_steps=ntd),
        grid=(ntd,),
        in_specs=[stab(cur), stab(cur), stab(cur), stab(nxt), stab(nxt), stab(nxt), tspec,
                  prow(tmd, d), prow(tmd, LANES), pl.BlockSpec((1, d), lambda i: (0, 0)), anyspec],
        out_specs=pl.BlockSpec((tmd, d), lambda i: (i, 0)),
        out_shape=jax.ShapeDtypeStruct((tokens, d), x.dtype),
        scratch_shapes=[pltpu.VMEM((2, width, d), F32), pltpu.SMEM((2,), I32), pltpu.SemaphoreType.DMA((2,))],
        compiler_params=_params(("arbitrary",)),
        name="combine",
    )(n8_f, off8_f, dbase, n8_f, off8_f, dbase, pos, h2, topg, row2(final_norm_g), ys)
    return out.reshape(bsz, seq, d)
```

```python
import functools

import jax
import jax.numpy as jnp
from jax import lax
from jax.experimental import pallas as pl
from jax.experimental.pallas import tpu as pltpu

F32 = jnp.float32
BF16 = jnp.bfloat16
I32 = jnp.int32

CHUNK = 64
TOP_K = 4
MOE_BLOCK = 512
TOKEN_TILE = 256
MID_TILE = 832
MID_PARTS = 4
SWIGLU_LIMIT = 7.0
SWIGLU_ALPHA = 1.702
EPS = 1e-6
LN_EPS = 1e-5
LANES = 128
SUBLANES = 8
BF16_ROWS = 16
VMEM_LIMIT = 56 * 1024 * 1024


def _dot(a, b):
    return jnp.dot(a, b, preferred_element_type=F32)


def _dot_nt(a, b):
    return lax.dot_general(a, b, (((1,), (1,)), ((), ())), preferred_element_type=F32)


def _dot_tn(a, b):
    return lax.dot_general(a, b, (((0,), (0,)), ((), ())), preferred_element_type=F32)


def _sigmoid(x):
    return 1.0 / (1.0 + jnp.exp(-x))


def _silu(x):
    return x * _sigmoid(x)


def _softplus(x):
    return jnp.maximum(x, 0.0) + jnp.log(1.0 + jnp.exp(-jnp.abs(x)))


def _rms(x, g):
    return x * lax.rsqrt(jnp.mean(x * x, axis=-1, keepdims=True) + EPS) * g


def _iota(shape, dim):
    return lax.broadcasted_iota(I32, shape, dim)


def _pick_tile(n, target, mult):
    best = None
    for t in range(mult, min(n, target) + 1, mult):
        if n % t == 0:
            best = t
    assert best is not None, (n, target, mult)
    return best


def _params(sem, flags=None):
    return pltpu.CompilerParams(dimension_semantics=sem, vmem_limit_bytes=VMEM_LIMIT, flags=flags)


def _front_kernel(head_ref, x_ref, g_ref, wqkv_ref, wab_ref, wgo_ref, wglu_ref, wmg_ref, bglu_ref, bmg_ref,
                  alog_ref, dtb_ref, h_ref, qkv_ref, gb_ref, go_ref, u_ref, gates_ref,
                  *, tm, meta_pad, heads, conv_ch):
    j = pl.program_id(1)
    xb = x_ref[0]
    n_head = head_ref.shape[0]
    h = jnp.where(j == 0, jnp.concatenate([head_ref[...], xb[:tm - n_head]], axis=0), xb)
    h_ref[...] = h
    hn = _rms(h, g_ref[...]).astype(BF16)
    qkv_ref[...] = _dot(hn, wqkv_ref[...]).astype(BF16)
    ab = _dot(hn, wab_ref[...])
    valid = (j * tm + _iota((tm, 1), 0)) >= meta_pad
    lane = _iota((1, LANES), 1)
    g = -jnp.exp(alog_ref[...]) * _softplus(ab + dtb_ref[...])
    gb = jnp.where(lane < heads, g, _sigmoid(ab))
    gb_ref[...] = jnp.where(valid, gb, 0.0)
    go_ref[...] = _silu(_dot(hn, wgo_ref[...])).astype(BF16)
    glu = _dot(hn, wglu_ref[...]) + bglu_ref[...]
    u = glu[:, :conv_ch] * _sigmoid(glu[:, conv_ch:])
    u_ref[...] = jnp.where(valid, u, 0.0).astype(BF16)
    gates_ref[...] = _sigmoid(_dot(hn, wmg_ref[...]) + bmg_ref[...]).astype(BF16)


def _conv_kernel(u_ref, gate_ref, w_ref, b_ref, lng_ref, lnb_ref, wo_ref, bo_ref, y_ref,
                 xcat_ref, xs_ref, cbuf_ref, *, tm, kw, halo, lead, rb):
    j = pl.program_id(1)
    base = lead + halo
    total = base + tm
    n_c = xcat_ref.shape[1]

    @pl.when(j == 0)
    def _():
        xcat_ref[0:base, :] = jnp.zeros((base, n_c), BF16)

    @pl.when(j > 0)
    def _():
        xcat_ref[lead:base, :] = xcat_ref[tm + lead:tm + base, :]

    xcat_ref[base:total, :] = u_ref[...]
    sr = _iota((SUBLANES * rb, lead + rb), 0)
    sc = _iota((SUBLANES * rb, lead + rb), 1)
    shift = (sc == (sr & (rb - 1)) + lead - (sr >> (rb.bit_length() - 1))).astype(BF16)
    bias = b_ref[...]
    for t0 in range(lead, total, rb):
        res = _dot(shift, xcat_ref[t0 - lead:t0 + rb, :])
        for r in range(SUBLANES):
            xs_ref[r, t0:t0 + rb, :] = res[r * rb:(r + 1) * rb, :]
        if t0 < base:
            continue
        i0 = t0
        blk = (i0 - base) // rb
        acc = jnp.zeros((rb, n_c), F32)
        for s in range(kw):
            a, r = divmod(s, SUBLANES)
            wt = w_ref[(kw - 1 - s) * SUBLANES:(kw - s) * SUBLANES, :]
            acc = acc + xs_ref[r, i0 - SUBLANES * a:i0 - SUBLANES * a + rb, :] * jnp.concatenate(
                [wt] * (rb // SUBLANES), axis=0)
        cbuf_ref[blk * rb:(blk + 1) * rb, :] = acc + bias
    c = cbuf_ref[...]
    mu = jnp.mean(c, axis=-1, keepdims=True)
    xc = c - mu
    ln = xc * lax.rsqrt(jnp.mean(xc * xc, axis=-1, keepdims=True) + LN_EPS) * lng_ref[...] + lnb_ref[...]
    yb = _dot(_silu(ln).astype(BF16), wo_ref[...]) + bo_ref[...]
    y_ref[...] = (gate_ref[...].astype(F32) * yb).astype(BF16)


def _bdot(a, b):
    return jnp.dot(a.astype(BF16), b.astype(BF16), preferred_element_type=F32)


def _split3(x):
    x1 = x.astype(BF16)
    r1 = x - x1.astype(F32)
    x2 = r1.astype(BF16)
    x3 = (r1 - x2.astype(F32)).astype(BF16)
    return x1, x2, x3


def _unit_lower_inverses_minus_eye(mats, row, col):
    ch = mats[0].shape[0]
    blk16 = (row >> 4) == (col >> 4)
    blk32 = (row >> 5) == (col >> 5)
    n1 = [jnp.where(blk16, a, 0.0) for a in mats]
    n2 = [_bdot(n, n) for n in n1]
    r = [_bdot(jnp.concatenate([n, m], axis=0), m) for n, m in zip(n1, n2)]
    n4 = [x[ch:] for x in r]
    y = [m - n - x[:ch] for n, m, x in zip(n1, n2, r)]
    r = [_bdot(jnp.concatenate([yy, m], axis=0), m) for yy, m in zip(y, n4)]
    n8 = [x[ch:] for x in r]
    y = [yy + m + x[:ch] for yy, m, x in zip(y, n4, r)]
    y = [yy + m + _bdot(yy, m) for yy, m in zip(y, n8)]
    for mask in (blk32 & jnp.logical_not(blk16), jnp.logical_not(blk32)):
        ls = [jnp.where(mask, a, 0.0) for a in mats]
        ms = [l + _bdot(l, yy) for l, yy in zip(ls, y)]
        y = [yy - m - _bdot(yy, m) for yy, m in zip(y, ms)]
    return y


def _delta_kernel(qkv_ref, gb_ref, go_ref, cw_ref, ng_ref, o_ref, s_ref, xc_ref,
                  *, nb, heads, dk, dv, ch, ksz, halo):
    c = pl.program_id(1)
    qk_w = heads * dk

    @pl.when(c == 0)
    def _():
        s_ref[...] = jnp.zeros(s_ref.shape, F32)
        xc_ref[:, 0:halo, :] = jnp.zeros((nb, halo, xc_ref.shape[2]), BF16)

    @pl.when(c > 0)
    def _():
        xc_ref[:, 0:halo, :] = xc_ref[:, ch:ch + halo, :]

    xc_ref[:, halo:halo + ch, :] = qkv_ref[...]
    row = _iota((ch, ch), 0)
    col = _iota((ch, ch), 1)
    incl = row >= col
    strict = row > col
    tril = incl.astype(BF16)
    triu = (row <= col).astype(BF16)
    sr = _iota(((ksz - 1) * ch, halo + ch), 0)
    sc = _iota(((ksz - 1) * ch, halo + ch), 1)
    shift = (sc == (sr & (ch - 1)) + (sr >> (ch.bit_length() - 1)) + (halo - (ksz - 1))).astype(BF16)

    qn, kn, kb, vb, gamma, eg, ekd, elast = [], [], [], [], [], [], [], []
    shifted = [_dot(shift, xc_ref[b]) for b in range(nb)]
    for b in range(nb):
        acc = xc_ref[b, halo:halo + ch, :].astype(F32) * cw_ref[ksz - 1:ksz, :]
        for t in range(ksz - 1):
            acc = acc + shifted[b][t * ch:(t + 1) * ch, :] * cw_ref[t:t + 1, :]
        qkv = _silu(acc)
        gb = gb_ref[b]
        parts = _split3(gb)
        gc = sum(_dot(tril, p) for p in parts)
        gct = sum(_dot_tn(p, triu) for p in parts)
        glast = gc[ch - 1:ch, :]
        e_g = jnp.exp(gc)
        e_kd = jnp.exp(glast - gc)
        e_last = jnp.exp(glast)
        for h in range(heads):
            q = qkv[:, h * dk:(h + 1) * dk]
            k = qkv[:, qk_w + h * dk:qk_w + (h + 1) * dk]
            v = qkv[:, 2 * qk_w + h * dv:2 * qk_w + (h + 1) * dv]
            qn.append(q * lax.rsqrt(jnp.sum(q * q, axis=-1, keepdims=True) + EPS) * (dk ** -0.5))
            kn.append(k * lax.rsqrt(jnp.sum(k * k, axis=-1, keepdims=True) + EPS))
            beta = gb[:, heads + h:heads + h + 1]
            kb.append(kn[-1] * beta)
            vb.append(v * beta)
            diff = gc[:, h:h + 1] - gct[h:h + 1, :]
            gamma.append(jnp.where(incl, jnp.exp(jnp.where(incl, diff, 0.0)), 0.0))
            eg.append(e_g[:, h:h + 1])
            ekd.append(e_kd[:, h:h + 1])
            elast.append(e_last[:, h:h + 1])

    n = nb * heads
    kq = [_dot_nt(jnp.concatenate([kb[i], qn[i]], axis=0).astype(BF16), kn[i].astype(BF16))
          for i in range(n)]
    a = [jnp.where(strict, kq[i][:ch] * gamma[i], 0.0) for i in range(n)]
    qk = [kq[i][ch:] * gamma[i] for i in range(n)]
    y = _unit_lower_inverses_minus_eye(a, row, col)
    rhs = [jnp.concatenate([vb[i], kb[i] * eg[i]], axis=-1) for i in range(n)]
    uw = [rhs[i] + _bdot(y[i], rhs[i]) for i in range(n)]
    s = [s_ref[i // heads, i % heads] for i in range(n)]
    ws = [_bdot(jnp.concatenate([uw[i][:, dv:], qn[i] * eg[i]], axis=0), s[i]) for i in range(n)]
    v_new = [uw[i][:, :dv] - ws[i][:ch] for i in range(n)]
    o = [ws[i][ch:] + _bdot(qk[i], v_new[i]) for i in range(n)]
    s_new = [s[i] * elast[i] + _dot_tn((kn[i] * ekd[i]).astype(BF16), v_new[i].astype(BF16))
             for i in range(n)]
    for i in range(n):
        b, h = i // heads, i % heads
        s_ref[b, h] = s_new[i]
        on = _rms(o[i], ng_ref[...])
        o_ref[b, :, h * dv:(h + 1) * dv] = (
            on * go_ref[b, :, h * dv:(h + 1) * dv].astype(F32)).astype(BF16)


def _mid_kernel(h_ref, o_ref, ga_ref, ybg_ref, wdn_ref, wout_ref, g2_ref, wr_ref, br_ref,
                h2_ref, hn_ref, topi_ref, topg_ref, *, tm, n_exp, parts):
    rs = tm // parts
    sl = [pl.ds(p * rs, rs) for p in range(parts)]
    ya = [_dot(o_ref[s, :], wdn_ref[...]) for s in sl]
    y = [ga_ref[s, :].astype(F32) * a + ybg_ref[s, :].astype(F32) for s, a in zip(sl, ya)]
    h2 = [h_ref[s, :] + _dot(v.astype(BF16), wout_ref[...]) for s, v in zip(sl, y)]
    hn = [_rms(v, g2_ref[...]) for v in h2]
    for s, v, w in zip(sl, h2, hn):
        h2_ref[s, :] = v
        hn_ref[s, :] = w
    logits = [_dot(v.astype(BF16), wr_ref[...]) + br_ref[...] for v in hn]
    lane = _iota((rs, LANES), 1)
    lane_f = lane.astype(F32)
    l = [jnp.where(lane < n_exp, v, -jnp.inf) for v in logits]
    vals = [[] for _ in range(parts)]
    idxs = [[] for _ in range(parts)]
    for _ in range(TOP_K):
        m = [jnp.max(v, axis=-1, keepdims=True) for v in l]
        idx = [jnp.min(jnp.where(v == mm, lane_f, float(LANES)), axis=-1, keepdims=True)
               for v, mm in zip(l, m)]
        l = [jnp.where(lane_f == ii, -jnp.inf, v) for v, ii in zip(l, idx)]
        for p in range(parts):
            vals[p].append(m[p])
            idxs[p].append(idx[p])
    for p in range(parts):
        es = [jnp.exp(v - vals[p][0]) for v in vals[p]]
        den = es[0]
        for e in es[1:]:
            den = den + e
        topi = jnp.zeros((rs, LANES), F32)
        topg = jnp.zeros((rs, LANES), F32)
        for k in range(TOP_K):
            topi = jnp.where(lane == k, idxs[p][k], topi)
            topg = jnp.where(lane == k, es[k] / den, topg)
        topi_ref[sl[p], :] = topi.astype(I32)
        topg_ref[sl[p], :] = topg


def _choice_mask(topi, lane, n_exp):
    m = jnp.zeros(topi.shape, F32)
    for k in range(TOP_K):
        m = m + (lane == topi[:, k:k + 1]).astype(F32)
    return jnp.where(lane < n_exp, m, 0.0)


def _plan_kernel(topi_ref, pos_ref, post_ref, n8_ref, off8_ref, carry8_ref, tot8_ref, carry_ref,
                 *, tm, tiles, n_exp):
    i = pl.program_id(0)

    @pl.when(i == 0)
    def _():
        carry_ref[...] = jnp.zeros(carry_ref.shape, F32)

    lane = _iota((tm, LANES), 1)
    tri = (_iota((tm, tm), 0) >= _iota((tm, tm), 1)).astype(BF16)
    before = (_iota((LANES, LANES), 0) < _iota((LANES, LANES), 1)).astype(BF16)
    carry = carry_ref[...]
    for j in range(tiles):
        rows = pl.ds(j * tm, tm)
        topi = topi_ref[rows, :]
        m = _choice_mask(topi, lane, n_exp)
        cum = _dot(tri, m.astype(BF16))
        n8 = jnp.floor((cum[tm - 1:tm, :] + (SUBLANES - 1)) * (1.0 / SUBLANES)) * SUBLANES
        off8 = _dot(jnp.broadcast_to(n8, (SUBLANES, LANES)).astype(BF16), before)[0:1]
        posf = cum - m + off8
        pos = jnp.zeros((tm, LANES), F32)
        for k in range(TOP_K):
            pk = jnp.sum(jnp.where(lane == topi[:, k:k + 1], posf, 0.0), axis=-1, keepdims=True)
            pos = jnp.where(lane == k, pk, pos)
        pos_ref[rows, :] = pos.astype(I32)
        post_ref[j] = jnp.transpose(pos)[0:SUBLANES, :].astype(I32)
        n8_ref[j] = n8.astype(I32)
        off8_ref[j] = off8.astype(I32)
        carry8_ref[j] = carry.astype(I32)
        carry = carry + n8
    carry_ref[...] = carry
    tot8_ref[...] = carry


def _pow2_floor(n):
    return 1 << (n.bit_length() - 1)


def _strip_pieces(n8, max_rows):
    pieces = []
    b = max_rows
    while b >= SUBLANES:
        pieces.append(((n8 & b) != 0, n8 & ~(2 * b - 1), b))
        b //= 2
    return pieces


LONG_PIECE = 64


def _copy_strip(n8, start_piece):
    n_long = n8 >> (LONG_PIECE.bit_length() - 1)

    def long_piece(j, carry):
        start_piece(j * LONG_PIECE, LONG_PIECE)
        return carry

    lax.fori_loop(0, n_long, long_piece, 0)
    for cond, ofs, rows in _strip_pieces(n8 & (LONG_PIECE - 1), LONG_PIECE // 2):
        @pl.when(cond)
        def _():
            start_piece(n_long * LONG_PIECE + ofs, rows)


def _wait_rows(total, make_copy, max_rows):
    b = max_rows
    while b >= SUBLANES:
        @pl.when((total & b) != 0)
        def _():
            make_copy(b).wait()
        b //= 2


def _aligned(start, rows):
    return pl.ds(pl.multiple_of(start, SUBLANES), rows)


def _dispatch_kernel(pend_ref, padded_ref, n8_ref, off8_ref, dbase_ref, hn_ref, post_ref, xs_ref,
                     zbuf_ref, sbuf_ref, tot_ref, zsem, sems, *, tm, width, n_exp, n_steps, n_blocks):
    i = pl.program_id(0)
    slot = i % 2

    def zero_block(start):
        return pltpu.make_async_copy(zbuf_ref, xs_ref.at[_aligned(start, MOE_BLOCK), :], zsem)

    @pl.when(i == 0)
    def _():
        zbuf_ref[...] = jnp.zeros(zbuf_ref.shape, F32)
        for e in range(n_exp):
            @pl.when(padded_ref[e] > 0)
            def _():
                zero_block(pend_ref[e] - MOE_BLOCK).start()
        first_unused = pend_ref[n_exp - 1] // MOE_BLOCK

        def start_tail(j, carry):
            zero_block(j * MOE_BLOCK).start()
            return carry

        lax.fori_loop(first_unused, n_blocks, start_tail, 0)
        for e in range(n_exp):
            @pl.when(padded_ref[e] > 0)
            def _():
                zero_block(0).wait()

        def wait_tail(j, carry):
            zero_block(0).wait()
            return carry

        lax.fori_loop(first_unused, n_blocks, wait_tail, 0)

    def strip_wait(s):
        _wait_rows(tot_ref[s], lambda b: pltpu.make_async_copy(
            sbuf_ref.at[s, pl.ds(0, b), :], xs_ref.at[pl.ds(0, b), :], sems.at[s]), _pow2_floor(width))

    @pl.when(i >= 2)
    def _():
        strip_wait(slot)

    post = post_ref[0]
    orow = _iota((width, tm), 0)
    sel = orow == post[0:1, :]
    for k in range(1, TOP_K):
        sel = jnp.logical_or(sel, orow == post[k:k + 1, :])
    sbuf_ref[slot] = _dot(sel.astype(BF16), hn_ref[...].astype(BF16))

    total = 0
    for e in range(n_exp):
        n8, so, do = n8_ref[e], off8_ref[e], dbase_ref[e]
        total = total + n8
        _copy_strip(n8, lambda ofs, rows, so=so, do=do: pltpu.make_async_copy(
            sbuf_ref.at[slot, _aligned(so + ofs, rows), :], xs_ref.at[_aligned(do + ofs, rows), :],
            sems.at[slot]).start())
    tot_ref[slot] = total

    @pl.when(i == n_steps - 1)
    def _():
        if n_steps >= 2:
            strip_wait(1 - slot)
        strip_wait(slot)


def _expert_kernel(be_ref, nbu_ref, x_ref, wgu_ref, bgu_ref, wd_ref, bd_ref, y_ref, wgu16_ref, wd16_ref,
                   *, d_ff, parts):
    i = pl.program_id(0)
    used = i < nbu_ref[0]
    new_expert = jnp.logical_or(i == 0, be_ref[i] != be_ref[jnp.maximum(i - 1, 0)])

    @pl.when(jnp.logical_and(used, new_expert))
    def _():
        wgu16_ref[...] = wgu_ref[0].astype(BF16)
        wd16_ref[...] = wd_ref[0].astype(BF16)

    @pl.when(used)
    def _():
        rs = x_ref.shape[0] // parts
        sl = [pl.ds(p * rs, rs) for p in range(parts)]
        gu = [_dot(x_ref[s, :].astype(BF16), wgu16_ref[...]) + bgu_ref[0] for s in sl]
        act = []
        for v in gu:
            gate = jnp.minimum(v[:, :d_ff], SWIGLU_LIMIT)
            up = jnp.clip(v[:, d_ff:], -SWIGLU_LIMIT, SWIGLU_LIMIT)
            act.append(((up + 1.0) * gate * _sigmoid(SWIGLU_ALPHA * gate)).astype(BF16))
        for s, a in zip(sl, act):
            y_ref[s, :] = _dot(a, wd16_ref[...]) + bd_ref[0]

    @pl.when(jnp.logical_not(used))
    def _():
        y_ref[...] = jnp.zeros(y_ref.shape, F32)


def _combine_kernel(n8c_ref, off8c_ref, dbc_ref, n8n_ref, off8n_ref, dbn_ref, pos_ref, h2_ref, topg_ref, fg_ref,
                    ys_ref, out_ref, gbuf_ref, tot_ref, sems, *, tm, width, n_exp, n_steps):
    i = pl.program_id(0)
    slot = i % 2

    def fetch(n8_ref, off8_ref, db_ref, s):
        total = 0
        for e in range(n_exp):
            n8, so, do = n8_ref[e], off8_ref[e], db_ref[e]
            total = total + n8
            _copy_strip(n8, lambda ofs, rows, so=so, do=do: pltpu.make_async_copy(
                ys_ref.at[_aligned(do + ofs, rows), :], gbuf_ref.at[s, _aligned(so + ofs, rows), :],
                sems.at[s]).start())
        tot_ref[s] = total

    @pl.when(i == 0)
    def _():
        gbuf_ref[...] = jnp.zeros(gbuf_ref.shape, F32)
        fetch(n8c_ref, off8c_ref, dbc_ref, 0)

    @pl.when(i + 1 < n_steps)
    def _():
        fetch(n8n_ref, off8n_ref, dbn_ref, 1 - slot)

    _wait_rows(tot_ref[slot], lambda b: pltpu.make_async_copy(
        ys_ref.at[pl.ds(0, b), :], gbuf_ref.at[slot, pl.ds(0, b), :], sems.at[slot]), _pow2_floor(width))
    pos = pos_ref[...]
    topg = topg_ref[...]
    lane = _iota((tm, width), 1)
    g = jnp.zeros((tm, width), F32)
    for k in range(TOP_K):
        g = g + jnp.where(lane == pos[:, k:k + 1], topg[:, k:k + 1], 0.0)
    moe = _dot(g.astype(BF16), gbuf_ref[slot].astype(BF16))
    out_ref[...] = _rms(h2_ref[...] + moe, fg_ref[...])


def _pad_lanes(v, fill=0.0):
    v = v.astype(F32)
    return jnp.concatenate([v, jnp.full((LANES - v.shape[0],), fill, F32)])[None, :]


def kernel(x, meta_tokens, norm_mix_g, w_in, conv_qkv_w, a_log, dt_bias, dn_norm_g, w_dn_out, b_glu,
           conv_dw_w, conv_dw_b, conv_ln_g, conv_ln_b, w_conv_out, b_conv_out, w_merge, b_merge, w_out,
           norm_ffn_g, w_router, b_router, w_gate_up, b_gate_up, w_down, b_down, final_norm_g):
    bsz, seq, d = x.shape
    depth = w_in.shape[0]
    n_meta = meta_tokens.shape[0]
    heads = a_log.shape[1]
    vw = w_dn_out.shape[1]
    qk_w = (conv_qkv_w.shape[2] - vw) // 2
    dk, dv = qk_w // heads, vw // heads
    ksz = conv_qkv_w.shape[1]
    kw, conv_ch = conv_dw_w.shape[1], conv_dw_w.shape[2]
    n_exp, d_ff = w_gate_up.shape[1], w_down.shape[2]
    meta_pad = CHUNK - n_meta
    lp = seq + n_meta + meta_pad
    assert depth == 1 and lp % CHUNK == 0 and dk == LANES and dv == LANES and d % LANES == 0
    assert 2 * heads <= LANES and n_exp <= LANES and kw - 1 <= 32 and ksz - 1 <= SUBLANES
    rows = bsz * lp
    n_head = meta_pad + n_meta
    head = jnp.concatenate([jnp.zeros((meta_pad, d), x.dtype), meta_tokens.astype(x.dtype)], axis=0)

    wi = w_in[0]
    o_a = 2 * qk_w + vw
    o_go = o_a + 2 * heads
    o_glu = o_go + vw
    w_qkv = wi[:, :o_a].astype(BF16)
    w_ab = jnp.pad(wi[:, o_a:o_go], ((0, 0), (0, LANES - 2 * heads))).astype(BF16)
    w_go = wi[:, o_go:o_glu].astype(BF16)
    w_glu = wi[:, o_glu:].astype(BF16)
    row2 = lambda v: v.astype(F32).reshape(1, -1)

    tm1 = _pick_tile(lp, min(320, seq), BF16_ROWS)
    nt1 = lp // tm1
    assert n_head % SUBLANES == 0 and n_head <= tm1 <= seq
    rspec = lambda cols, tm, nt: pl.BlockSpec((tm, cols), lambda b, j: (b * nt + j, 0))
    full = lambda a: pl.BlockSpec(a.shape, lambda *_: (0,) * a.ndim)
    x_rows = pl.BlockSpec((pl.Element(1), pl.Element(tm1), pl.Element(d)),
                          lambda b, j: (b, pl.multiple_of(jnp.maximum(j * tm1 - n_head, 0), SUBLANES), 0))
    front_in = [head, x, row2(norm_mix_g[0]), w_qkv, w_ab, w_go, w_glu, w_merge[0].astype(BF16),
                row2(b_glu[0]), row2(b_merge[0]), _pad_lanes(a_log[0]), _pad_lanes(dt_bias[0])]
    hp, qkv_pre, gbeta, go_act, u_glu, gates = pl.pallas_call(
        functools.partial(_front_kernel, tm=tm1, meta_pad=meta_pad, heads=heads, conv_ch=conv_ch),
        grid=(bsz, nt1),
        in_specs=[full(head), x_rows] + [full(a) for a in front_in[2:]],
        out_specs=[rspec(d, tm1, nt1), rspec(o_a, tm1, nt1), rspec(LANES, tm1, nt1), rspec(vw, tm1, nt1),
                   rspec(conv_ch, tm1, nt1), rspec(2 * d, tm1, nt1)],
        out_shape=[jax.ShapeDtypeStruct((rows, d), F32),
                   jax.ShapeDtypeStruct((rows, o_a), BF16), jax.ShapeDtypeStruct((rows, LANES), F32),
                   jax.ShapeDtypeStruct((rows, vw), BF16), jax.ShapeDtypeStruct((rows, conv_ch), BF16),
                   jax.ShapeDtypeStruct((rows, 2 * d), BF16)],
        compiler_params=_params(("arbitrary", "arbitrary")),
        name="front",
    )(*front_in)

    tmc = _pick_tile(lp, 832, 32)
    ntc = lp // tmc
    halo, lead = 32, BF16_ROWS
    conv_w8 = jnp.repeat(conv_dw_w[0].astype(F32), SUBLANES, axis=0)
    conv_in = [u_glu, gates, conv_w8, row2(conv_dw_b[0]), row2(conv_ln_g[0]),
               row2(conv_ln_b[0]), w_conv_out[0].astype(BF16), row2(b_conv_out[0])]
    ybg = pl.pallas_call(
        functools.partial(_conv_kernel, tm=tmc, kw=kw, halo=halo, lead=lead, rb=32),
        grid=(bsz, ntc),
        in_specs=[rspec(conv_ch, tmc, ntc),
                  pl.BlockSpec((tmc, d), lambda b, j: (b * ntc + j, 1))] + [full(a) for a in conv_in[2:]],
        out_specs=rspec(d, tmc, ntc),
        out_shape=jax.ShapeDtypeStruct((rows, d), BF16),
        scratch_shapes=[pltpu.VMEM((lead + halo + tmc, conv_ch), BF16),
                        pltpu.VMEM((SUBLANES, lead + halo + tmc, conv_ch), F32),
                        pltpu.VMEM((tmc, conv_ch), F32)],
        compiler_params=_params(("arbitrary", "arbitrary")),
        name="convmod",
    )(*conv_in)

    nc = lp // CHUNK
    nbd = max(n for n in (1, 2, 4) if bsz % n == 0)
    cspec = lambda cols: pl.BlockSpec((nbd, CHUNK, cols), lambda b, c: (b, c, 0))
    delta_in = [qkv_pre.reshape(bsz, lp, o_a), gbeta.reshape(bsz, lp, LANES), go_act.reshape(bsz, lp, vw),
                conv_qkv_w[0].astype(F32), row2(dn_norm_g[0])]
    o_dn = pl.pallas_call(
        functools.partial(_delta_kernel, nb=nbd, heads=heads, dk=dk, dv=dv, ch=CHUNK, ksz=ksz, halo=BF16_ROWS),
        grid=(bsz // nbd, nc),
        in_specs=[cspec(o_a), cspec(LANES), cspec(vw), full(delta_in[3]), full(delta_in[4])],
        out_specs=cspec(vw),
        out_shape=jax.ShapeDtypeStruct((bsz, lp, vw), BF16),
        scratch_shapes=[pltpu.VMEM((nbd, heads, dk, dv), F32),
                        pltpu.VMEM((nbd, CHUNK + BF16_ROWS, o_a), BF16)],
        compiler_params=_params(("arbitrary", "arbitrary")),
        name="delta",
    )(*delta_in).reshape(rows, vw)

    tm3 = _pick_tile(lp, MID_TILE, BF16_ROWS)
    nt3 = lp // tm3
    w_r = jnp.pad(w_router[0], ((0, 0), (0, LANES - n_exp))).astype(BF16)
    mid_in = [hp, o_dn, gates, ybg, w_dn_out[0].astype(BF16), w_out[0].astype(BF16),
              row2(norm_ffn_g[0]), w_r, _pad_lanes(b_router[0])]
    h2, hn2, topi, topg = pl.pallas_call(
        functools.partial(_mid_kernel, tm=tm3, n_exp=n_exp, parts=MID_PARTS if tm3 % (MID_PARTS * BF16_ROWS) == 0 else 1),
        grid=(bsz, nt3),
        in_specs=[rspec(d, tm3, nt3), rspec(vw, tm3, nt3), rspec(d, tm3, nt3), rspec(d, tm3, nt3)]
        + [full(a) for a in mid_in[4:]],
        out_specs=[rspec(d, tm3, nt3), rspec(d, tm3, nt3), rspec(LANES, tm3, nt3), rspec(LANES, tm3, nt3)],
        out_shape=[jax.ShapeDtypeStruct((rows, d), F32), jax.ShapeDtypeStruct((rows, d), F32),
                   jax.ShapeDtypeStruct((rows, LANES), I32), jax.ShapeDtypeStruct((rows, LANES), F32)],
        compiler_params=_params(("arbitrary", "arbitrary")),
        name="mid",
    )(*mid_in)

    tmd = _pick_tile(seq, TOKEN_TILE, SUBLANES)
    tpb = seq // tmd
    ntd = bsz * tpb
    tokens = bsz * seq
    width = tmd * TOP_K + n_exp * SUBLANES
    row0 = lambda i: pl.multiple_of((i // tpb) * lp + n_head + (i % tpb) * tmd, SUBLANES)
    prow = lambda r, cols, f=row0: pl.BlockSpec((pl.Element(r), pl.Element(cols)), lambda i, *_: (f(i), 0))
    tspec = pl.BlockSpec((tmd, LANES), lambda i, *_: (i, 0))
    ptiles = _pick_tile(tpb, 8, 1)
    pspec = pl.BlockSpec((ptiles * tmd, LANES), lambda i: (i, 0))
    tab = pl.BlockSpec((ptiles, 1, LANES), lambda i: (i, 0, 0))
    tab_shape = jax.ShapeDtypeStruct((ntd, 1, LANES), I32)
    pos, pos_t, n8_t, off8_t, carry8_t, tot8 = pl.pallas_call(
        functools.partial(_plan_kernel, tm=tmd, tiles=ptiles, n_exp=n_exp),
        grid=(ntd // ptiles,),
        in_specs=[prow(ptiles * tmd, LANES, lambda i: row0(i * ptiles))],
        out_specs=[pspec, pl.BlockSpec((ptiles, SUBLANES, tmd), lambda i: (i, 0, 0)), tab, tab, tab,
                   pl.BlockSpec((1, LANES), lambda i: (0, 0))],
        out_shape=[jax.ShapeDtypeStruct((tokens, LANES), I32), jax.ShapeDtypeStruct((ntd, SUBLANES, tmd), I32),
                   tab_shape, tab_shape, tab_shape, jax.ShapeDtypeStruct((1, LANES), F32)],
        scratch_shapes=[pltpu.VMEM((1, LANES), F32)],
        compiler_params=_params(("arbitrary",)),
        name="plan",
    )(topi)

    n_blocks = -(-(tokens * TOP_K + n_exp * (SUBLANES - 1) * ntd) // MOE_BLOCK) + n_exp
    cap = n_blocks * MOE_BLOCK
    counts = tot8[0, :n_exp].astype(I32)
    padded = (counts + MOE_BLOCK - 1) // MOE_BLOCK * MOE_BLOCK
    pend = jnp.cumsum(padded).astype(I32)
    pstart = pend - padded
    n_used = (pend[-1] // MOE_BLOCK).astype(I32).reshape(1)
    block_start = jnp.arange(n_blocks, dtype=I32) * MOE_BLOCK
    block_e = jnp.minimum(jnp.sum((pend[None, :] <= block_start[:, None]).astype(I32), axis=1), n_exp - 1)
    dbase = (carry8_t.reshape(ntd, LANES) + jnp.pad(pstart, (0, LANES - n_exp))[None, :]).reshape(ntd * LANES)
    n8_f = n8_t.reshape(ntd * LANES)
    off8_f = off8_t.reshape(ntd * LANES)

    anyspec = pl.BlockSpec(memory_space=pl.ANY)
    stab = lambda f: pl.BlockSpec((LANES,), lambda i, *_: (f(i),), memory_space=pltpu.SMEM)
    cur = lambda i: i
    xs = pl.pallas_call(
        functools.partial(_dispatch_kernel, tm=tmd, width=width, n_exp=n_exp, n_steps=ntd, n_blocks=n_blocks),
        grid_spec=pltpu.PrefetchScalarGridSpec(
            num_scalar_prefetch=2, grid=(ntd,),
            in_specs=[stab(cur), stab(cur), stab(cur), prow(tmd, d),
                      pl.BlockSpec((1, SUBLANES, tmd), lambda i, *_: (i, 0, 0))],
            out_specs=anyspec,
            scratch_shapes=[pltpu.VMEM((MOE_BLOCK, d), F32), pltpu.VMEM((2, width, d), F32),
                            pltpu.SMEM((2,), I32), pltpu.SemaphoreType.DMA(()), pltpu.SemaphoreType.DMA((2,))]),
        out_shape=jax.ShapeDtypeStruct((cap, d), F32),
        compiler_params=_params(("arbitrary",)),
        name="dispatch",
    )(pend, padded, n8_f, off8_f, dbase, hn2, pos_t)

    last_used = lambda i, nbu: jnp.minimum(i, nbu[0] - 1)
    ew = lambda a: pl.BlockSpec((1,) + a.shape[1:], lambda i, be, nbu: (be[i],) + (0,) * (a.ndim - 1))
    e_in = [w_gate_up[0], b_gate_up[0].astype(F32)[:, None, :], w_down[0], b_down[0].astype(F32)[:, None, :]]
    ys = pl.pallas_call(
        functools.partial(_expert_kernel, d_ff=d_ff, parts=2),
        grid_spec=pltpu.PrefetchScalarGridSpec(
            num_scalar_prefetch=2, grid=(n_blocks,),
            in_specs=[pl.BlockSpec((MOE_BLOCK, d), lambda i, be, nbu: (last_used(i, nbu), 0))]
            + [ew(a) for a in e_in],
            out_specs=pl.BlockSpec((MOE_BLOCK, d), lambda i, be, nbu: (i, 0)),
            scratch_shapes=[pltpu.VMEM((d, 2 * d_ff), BF16), pltpu.VMEM((d_ff, d), BF16)]),
        out_shape=jax.ShapeDtypeStruct((cap, d), F32),
        compiler_params=_params(("arbitrary",)),
        name="experts",
    )(block_e, n_used, xs, *e_in)

    nxt = lambda i: jnp.minimum(i + 1, ntd - 1)
    out = pl.pallas_call(
        functools.partial(_combine_kernel, tm=tmd, width=width, n_exp=n_exp, n_steps=ntd),
        grid=(ntd,),
        in_specs=[stab(cur), stab(cur), stab(cur), stab(nxt), stab(nxt), stab(nxt), tspec,
                  prow(tmd, d), prow(tmd, LANES), pl.BlockSpec((1, d), lambda i: (0, 0)), anyspec],
        out_specs=pl.BlockSpec((tmd, d), lambda i: (i, 0)),
        out_shape=jax.ShapeDtypeStruct((tokens, d), x.dtype),
        scratch_shapes=[pltpu.VMEM((2, width, d), F32), pltpu.SMEM((2,), I32), pltpu.SemaphoreType.DMA((2,))],
        compiler_params=_params(("arbitrary",)),
        name="combine",
    )(n8_f, off8_f, dbase, n8_f, off8_f, dbase, pos, h2, topg, row2(final_norm_g), ys)
    return out.reshape(bsz, seq, d)
```

```python
import functools

import jax
import jax.numpy as jnp
from jax import lax
from jax.experimental import pallas as pl
from jax.experimental.pallas import tpu as pltpu

F32 = jnp.float32
BF16 = jnp.bfloat16
I32 = jnp.int32

CHUNK = 64
TOP_K = 4
MOE_BLOCK = 512
TOKEN_TILE = 256
MID_TILE = 832
MID_PARTS = 4
SWIGLU_LIMIT = 7.0
SWIGLU_ALPHA = 1.702
EPS = 1e-6
LN_EPS = 1e-5
LANES = 128
SUBLANES = 8
BF16_ROWS = 16
VMEM_LIMIT = 56 * 1024 * 1024


def _dot(a, b):
    return jnp.dot(a, b, preferred_element_type=F32)


def _dot_nt(a, b):
    return lax.dot_general(a, b, (((1,), (1,)), ((), ())), preferred_element_type=F32)


def _dot_tn(a, b):
    return lax.dot_general(a, b, (((0,), (0,)), ((), ())), preferred_element_type=F32)


def _sigmoid(x):
    return 1.0 / (1.0 + jnp.exp(-x))


def _silu(x):
    return x * _sigmoid(x)


def _softplus(x):
    return jnp.maximum(x, 0.0) + jnp.log(1.0 + jnp.exp(-jnp.abs(x)))


def _rms(x, g):
    return x * lax.rsqrt(jnp.mean(x * x, axis=-1, keepdims=True) + EPS) * g


def _iota(shape, dim):
    return lax.broadcasted_iota(I32, shape, dim)


def _pick_tile(n, target, mult):
    best = None
    for t in range(mult, min(n, target) + 1, mult):
        if n % t == 0:
            best = t
    assert best is not None, (n, target, mult)
    return best


def _params(sem, flags=None):
    return pltpu.CompilerParams(dimension_semantics=sem, vmem_limit_bytes=VMEM_LIMIT, flags=flags)


def _front_kernel(head_ref, x_ref, g_ref, wqkv_ref, wab_ref, wgo_ref, wglu_ref, wmg_ref, bglu_ref, bmg_ref,
                  alog_ref, dtb_ref, h_ref, qkv_ref, gb_ref, go_ref, u_ref, gates_ref,
                  *, tm, meta_pad, heads, conv_ch):
    j = pl.program_id(1)
    xb = x_ref[0]
    n_head = head_ref.shape[0]
    h = jnp.where(j == 0, jnp.concatenate([head_ref[...], xb[:tm - n_head]], axis=0), xb)
    h_ref[...] = h
    hn = _rms(h, g_ref[...]).astype(BF16)
    qkv_ref[...] = _dot(hn, wqkv_ref[...]).astype(BF16)
    ab = _dot(hn, wab_ref[...])
    valid = (j * tm + _iota((tm, 1), 0)) >= meta_pad
    lane = _iota((1, LANES), 1)
    g = -jnp.exp(alog_ref[...]) * _softplus(ab + dtb_ref[...])
    gb = jnp.where(lane < heads, g, _sigmoid(ab))
    gb_ref[...] = jnp.where(valid, gb, 0.0)
    go_ref[...] = _silu(_dot(hn, wgo_ref[...])).astype(BF16)
    glu = _dot(hn, wglu_ref[...]) + bglu_ref[...]
    u = glu[:, :conv_ch] * _sigmoid(glu[:, conv_ch:])
    u_ref[...] = jnp.where(valid, u, 0.0).astype(BF16)
    gates_ref[...] = _sigmoid(_dot(hn, wmg_ref[...]) + bmg_ref[...]).astype(BF16)


def _conv_kernel(u_ref, gate_ref, w_ref, b_ref, lng_ref, lnb_ref, wo_ref, bo_ref, y_ref,
                 xcat_ref, xs_ref, cbuf_ref, *, tm, kw, halo, lead, rb):
    j = pl.program_id(1)
    base = lead + halo
    total = base + tm
    n_c = xcat_ref.shape[1]

    @pl.when(j == 0)
    def _():
        xcat_ref[0:base, :] = jnp.zeros((base, n_c), BF16)

    @pl.when(j > 0)
    def _():
        xcat_ref[lead:base, :] = xcat_ref[tm + lead:tm + base, :]

    xcat_ref[base:total, :] = u_ref[...]
    sr = _iota((SUBLANES * rb, lead + rb), 0)
    sc = _iota((SUBLANES * rb, lead + rb), 1)
    shift = (sc == (sr & (rb - 1)) + lead - (sr >> (rb.bit_length() - 1))).astype(BF16)
    bias = b_ref[...]
    for t0 in range(lead, total, rb):
        res = _dot(shift, xcat_ref[t0 - lead:t0 + rb, :])
        for r in range(SUBLANES):
            xs_ref[r, t0:t0 + rb, :] = res[r * rb:(r + 1) * rb, :]
        if t0 < base:
            continue
        i0 = t0
        blk = (i0 - base) // rb
        acc = jnp.zeros((rb, n_c), F32)
        for s in range(kw):
            a, r = divmod(s, SUBLANES)
            wt = w_ref[(kw - 1 - s) * SUBLANES:(kw - s) * SUBLANES, :]
            acc = acc + xs_ref[r, i0 - SUBLANES * a:i0 - SUBLANES * a + rb, :] * jnp.concatenate(
                [wt] * (rb // SUBLANES), axis=0)
        cbuf_ref[blk * rb:(blk + 1) * rb, :] = acc + bias
    c = cbuf_ref[...]
    mu = jnp.mean(c, axis=-1, keepdims=True)
    xc = c - mu
    ln = xc * lax.rsqrt(jnp.mean(xc * xc, axis=-1, keepdims=True) + LN_EPS) * lng_ref[...] + lnb_ref[...]
    yb = _dot(_silu(ln).astype(BF16), wo_ref[...]) + bo_ref[...]
    y_ref[...] = (gate_ref[...].astype(F32) * yb).astype(BF16)


def _bdot(a, b):
    return jnp.dot(a.astype(BF16), b.astype(BF16), preferred_element_type=F32)


def _split3(x):
    x1 = x.astype(BF16)
    r1 = x - x1.astype(F32)
    x2 = r1.astype(BF16)
    x3 = (r1 - x2.astype(F32)).astype(BF16)
    return x1, x2, x3


def _unit_lower_inverses_minus_eye(mats, row, col):
    ch = mats[0].shape[0]
    blk16 = (row >> 4) == (col >> 4)
    blk32 = (row >> 5) == (col >> 5)
    n1 = [jnp.where(blk16, a, 0.0) for a in mats]
    n2 = [_bdot(n, n) for n in n1]
    r = [_bdot(jnp.concatenate([n, m], axis=0), m) for n, m in zip(n1, n2)]
    n4 = [x[ch:] for x in r]
    y = [m - n - x[:ch] for n, m, x in zip(n1, n2, r)]
    r = [_bdot(jnp.concatenate([yy, m], axis=0), m) for yy, m in zip(y, n4)]
    n8 = [x[ch:] for x in r]
    y = [yy + m + x[:ch] for yy, m, x in zip(y, n4, r)]
    y = [yy + m + _bdot(yy, m) for yy, m in zip(y, n8)]
    for mask in (blk32 & jnp.logical_not(blk16), jnp.logical_not(blk32)):
        ls = [jnp.where(mask, a, 0.0) for a in mats]
        ms = [l + _bdot(l, yy) for l, yy in zip(ls, y)]
        y = [yy - m - _bdot(yy, m) for yy, m in zip(y, ms)]
    return y


def _delta_kernel(qkv_ref, gb_ref, go_ref, cw_ref, ng_ref, o_ref, s_ref, xc_ref,
                  *, nb, heads, dk, dv, ch, ksz, halo):
    c = pl.program_id(1)
    qk_w = heads * dk

    @pl.when(c == 0)
    def _():
        s_ref[...] = jnp.zeros(s_ref.shape, F32)
        xc_ref[:, 0:halo, :] = jnp.zeros((nb, halo, xc_ref.shape[2]), BF16)

    @pl.when(c > 0)
    def _():
        xc_ref[:, 0:halo, :] = xc_ref[:, ch:ch + halo, :]

    xc_ref[:, halo:halo + ch, :] = qkv_ref[...]
    row = _iota((ch, ch), 0)
    col = _iota((ch, ch), 1)
    incl = row >= col
    strict = row > col
    tril = incl.astype(BF16)
    triu = (row <= col).astype(BF16)
    sr = _iota(((ksz - 1) * ch, halo + ch), 0)
    sc = _iota(((ksz - 1) * ch, halo + ch), 1)
    shift = (sc == (sr & (ch - 1)) + (sr >> (ch.bit_length() - 1)) + (halo - (ksz - 1))).astype(BF16)

    qn, kn, kb, vb, gamma, eg, ekd, elast = [], [], [], [], [], [], [], []
    shifted = [_dot(shift, xc_ref[b]) for b in range(nb)]
    for b in range(nb):
        acc = xc_ref[b, halo:halo + ch, :].astype(F32) * cw_ref[ksz - 1:ksz, :]
        for t in range(ksz - 1):
            acc = acc + shifted[b][t * ch:(t + 1) * ch, :] * cw_ref[t:t + 1, :]
        qkv = _silu(acc)
        gb = gb_ref[b]
        parts = _split3(gb)
        gc = sum(_dot(tril, p) for p in parts)
        gct = sum(_dot_tn(p, triu) for p in parts)
        glast = gc[ch - 1:ch, :]
        e_g = jnp.exp(gc)
        e_kd = jnp.exp(glast - gc)
        e_last = jnp.exp(glast)
        for h in range(heads):
            q = qkv[:, h * dk:(h + 1) * dk]
            k = qkv[:, qk_w + h * dk:qk_w + (h + 1) * dk]
            v = qkv[:, 2 * qk_w + h * dv:2 * qk_w + (h + 1) * dv]
            qn.append(q * lax.rsqrt(jnp.sum(q * q, axis=-1, keepdims=True) + EPS) * (dk ** -0.5))
            kn.append(k * lax.rsqrt(jnp.sum(k * k, axis=-1, keepdims=True) + EPS))
            beta = gb[:, heads + h:heads + h + 1]
            kb.append(kn[-1] * beta)
            vb.append(v * beta)
            diff = gc[:, h:h + 1] - gct[h:h + 1, :]
            gamma.append(jnp.where(incl, jnp.exp(jnp.where(incl, diff, 0.0)), 0.0))
            eg.append(e_g[:, h:h + 1])
            ekd.append(e_kd[:, h:h + 1])
            elast.append(e_last[:, h:h + 1])

    n = nb * heads
    kq = [_dot_nt(jnp.concatenate([kb[i], qn[i]], axis=0).astype(BF16), kn[i].astype(BF16))
          for i in range(n)]
    a = [jnp.where(strict, kq[i][:ch] * gamma[i], 0.0) for i in range(n)]
    qk = [kq[i][ch:] * gamma[i] for i in range(n)]
    y = _unit_lower_inverses_minus_eye(a, row, col)
    rhs = [jnp.concatenate([vb[i], kb[i] * eg[i]], axis=-1) for i in range(n)]
    uw = [rhs[i] + _bdot(y[i], rhs[i]) for i in range(n)]
    s = [s_ref[i // heads, i % heads] for i in range(n)]
    ws = [_bdot(jnp.concatenate([uw[i][:, dv:], qn[i] * eg[i]], axis=0), s[i]) for i in range(n)]
    v_new = [uw[i][:, :dv] - ws[i][:ch] for i in range(n)]
    o = [ws[i][ch:] + _bdot(qk[i], v_new[i]) for i in range(n)]
    s_new = [s[i] * elast[i] + _dot_tn((kn[i] * ekd[i]).astype(BF16), v_new[i].astype(BF16))
             for i in range(n)]
    for i in range(n):
        b, h = i // heads, i % heads
        s_ref[b, h] = s_new[i]
        on = _rms(o[i], ng_ref[...])
        o_ref[b, :, h * dv:(h + 1) * dv] = (
            on * go_ref[b, :, h * dv:(h + 1) * dv].astype(F32)).astype(BF16)


def _mid_kernel(h_ref, o_ref, ga_ref, ybg_ref, wdn_ref, wout_ref, g2_ref, wr_ref, br_ref,
                h2_ref, hn_ref, topi_ref, topg_ref, *, tm, n_exp, parts):
    rs = tm // parts
    sl = [pl.ds(p * rs, rs) for p in range(parts)]
    ya = [_dot(o_ref[s, :], wdn_ref[...]) for s in sl]
    y = [ga_ref[s, :].astype(F32) * a + ybg_ref[s, :].astype(F32) for s, a in zip(sl, ya)]
    h2 = [h_ref[s, :] + _dot(v.astype(BF16), wout_ref[...]) for s, v in zip(sl, y)]
    hn = [_rms(v, g2_ref[...]) for v in h2]
    for s, v, w in zip(sl, h2, hn):
        h2_ref[s, :] = v
        hn_ref[s, :] = w
    logits = [_dot(v.astype(BF16), wr_ref[...]) + br_ref[...] for v in hn]
    lane = _iota((rs, LANES), 1)
    lane_f = lane.astype(F32)
    l = [jnp.where(lane < n_exp, v, -jnp.inf) for v in logits]
    vals = [[] for _ in range(parts)]
    idxs = [[] for _ in range(parts)]
    for _ in range(TOP_K):
        m = [jnp.max(v, axis=-1, keepdims=True) for v in l]
        idx = [jnp.min(jnp.where(v == mm, lane_f, float(LANES)), axis=-1, keepdims=True)
               for v, mm in zip(l, m)]
        l = [jnp.where(lane_f == ii, -jnp.inf, v) for v, ii in zip(l, idx)]
        for p in range(parts):
            vals[p].append(m[p])
            idxs[p].append(idx[p])
    for p in range(parts):
        es = [jnp.exp(v - vals[p][0]) for v in vals[p]]
        den = es[0]
        for e in es[1:]:
            den = den + e
        topi = jnp.zeros((rs, LANES), F32)
        topg = jnp.zeros((rs, LANES), F32)
        for k in range(TOP_K):
            topi = jnp.where(lane == k, idxs[p][k], topi)
            topg = jnp.where(lane == k, es[k] / den, topg)
        topi_ref[sl[p], :] = topi.astype(I32)
        topg_ref[sl[p], :] = topg


def _choice_mask(topi, lane, n_exp):
    m = jnp.zeros(topi.shape, F32)
    for k in range(TOP_K):
        m = m + (lane == topi[:, k:k + 1]).astype(F32)
    return jnp.where(lane < n_exp, m, 0.0)


def _plan_kernel(topi_ref, pos_ref, post_ref, n8_ref, off8_ref, carry8_ref, tot8_ref, carry_ref,
                 *, tm, tiles, n_exp):
    i = pl.program_id(0)

    @pl.when(i == 0)
    def _():
        carry_ref[...] = jnp.zeros(carry_ref.shape, F32)

    lane = _iota((tm, LANES), 1)
    tri = (_iota((tm, tm), 0) >= _iota((tm, tm), 1)).astype(BF16)
    before = (_iota((LANES, LANES), 0) < _iota((LANES, LANES), 1)).astype(BF16)
    js = range(tiles)
    rows = [pl.ds(j * tm, tm) for j in js]
    topi = [topi_ref[r, :] for r in rows]
    m = [_choice_mask(t, lane, n_exp) for t in topi]
    cum = [_dot(tri, v.astype(BF16)) for v in m]
    n8 = [jnp.floor((c[tm - 1:tm, :] + (SUBLANES - 1)) * (1.0 / SUBLANES)) * SUBLANES for c in cum]
    off8 = [_dot(jnp.broadcast_to(v, (SUBLANES, LANES)).astype(BF16), before)[0:1] for v in n8]
    posf = [c - v + o for c, v, o in zip(cum, m, off8)]
    pos = [jnp.zeros((tm, LANES), F32) for _ in js]
    for k in range(TOP_K):
        pk = [jnp.sum(jnp.where(lane == t[:, k:k + 1], p, 0.0), axis=-1, keepdims=True)
              for t, p in zip(topi, posf)]
        pos = [jnp.where(lane == k, a, b) for a, b in zip(pk, pos)]
    carry = carry_ref[...]
    for j in js:
        pos_ref[rows[j], :] = pos[j].astype(I32)
        post_ref[j] = jnp.transpose(pos[j])[0:SUBLANES, :].astype(I32)
        n8_ref[j] = n8[j].astype(I32)
        off8_ref[j] = off8[j].astype(I32)
        carry8_ref[j] = carry.astype(I32)
        carry = carry + n8[j]
    carry_ref[...] = carry
    tot8_ref[...] = carry


def _pow2_floor(n):
    return 1 << (n.bit_length() - 1)


def _strip_pieces(n8, max_rows):
    pieces = []
    b = max_rows
    while b >= SUBLANES:
        pieces.append(((n8 & b) != 0, n8 & ~(2 * b - 1), b))
        b //= 2
    return pieces


LONG_PIECE = 64


def _copy_strip(n8, start_piece):
    n_long = n8 >> (LONG_PIECE.bit_length() - 1)

    def long_piece(j, carry):
        start_piece(j * LONG_PIECE, LONG_PIECE)
        return carry

    lax.fori_loop(0, n_long, long_piece, 0)
    for cond, ofs, rows in _strip_pieces(n8 & (LONG_PIECE - 1), LONG_PIECE // 2):
        @pl.when(cond)
        def _():
            start_piece(n_long * LONG_PIECE + ofs, rows)


def _wait_rows(total, make_copy, max_rows):
    b = max_rows
    while b >= SUBLANES:
        @pl.when((total & b) != 0)
        def _():
            make_copy(b).wait()
        b //= 2


def _aligned(start, rows):
    return pl.ds(pl.multiple_of(start, SUBLANES), rows)


def _dispatch_kernel(pend_ref, padded_ref, n8_ref, off8_ref, dbase_ref, hn_ref, post_ref, xs_ref,
                     zbuf_ref, sbuf_ref, tot_ref, zsem, sems, *, tm, width, n_exp, n_steps, n_blocks):
    i = pl.program_id(0)
    slot = i % 2

    def zero_block(start):
        return pltpu.make_async_copy(zbuf_ref, xs_ref.at[_aligned(start, MOE_BLOCK), :], zsem)

    @pl.when(i == 0)
    def _():
        zbuf_ref[...] = jnp.zeros(zbuf_ref.shape, F32)
        for e in range(n_exp):
            @pl.when(padded_ref[e] > 0)
            def _():
                zero_block(pend_ref[e] - MOE_BLOCK).start()
        first_unused = pend_ref[n_exp - 1] // MOE_BLOCK

        def start_tail(j, carry):
            zero_block(j * MOE_BLOCK).start()
            return carry

        lax.fori_loop(first_unused, n_blocks, start_tail, 0)
        for e in range(n_exp):
            @pl.when(padded_ref[e] > 0)
            def _():
                zero_block(0).wait()

        def wait_tail(j, carry):
            zero_block(0).wait()
            return carry

        lax.fori_loop(first_unused, n_blocks, wait_tail, 0)

    def strip_wait(s):
        _wait_rows(tot_ref[s], lambda b: pltpu.make_async_copy(
            sbuf_ref.at[s, pl.ds(0, b), :], xs_ref.at[pl.ds(0, b), :], sems.at[s]), _pow2_floor(width))

    @pl.when(i >= 2)
    def _():
        strip_wait(slot)

    post = post_ref[0]
    orow = _iota((width, tm), 0)
    sel = orow == post[0:1, :]
    for k in range(1, TOP_K):
        sel = jnp.logical_or(sel, orow == post[k:k + 1, :])
    sbuf_ref[slot] = _dot(sel.astype(BF16), hn_ref[...].astype(BF16))

    total = 0
    for e in range(n_exp):
        n8, so, do = n8_ref[e], off8_ref[e], dbase_ref[e]
        total = total + n8
        _copy_strip(n8, lambda ofs, rows, so=so, do=do: pltpu.make_async_copy(
            sbuf_ref.at[slot, _aligned(so + ofs, rows), :], xs_ref.at[_aligned(do + ofs, rows), :],
            sems.at[slot]).start())
    tot_ref[slot] = total

    @pl.when(i == n_steps - 1)
    def _():
        if n_steps >= 2:
            strip_wait(1 - slot)
        strip_wait(slot)


def _expert_kernel(be_ref, nbu_ref, x_ref, wgu_ref, bgu_ref, wd_ref, bd_ref, y_ref, wgu16_ref, wd16_ref,
                   *, d_ff, parts):
    i = pl.program_id(0)
    used = i < nbu_ref[0]
    new_expert = jnp.logical_or(i == 0, be_ref[i] != be_ref[jnp.maximum(i - 1, 0)])

    @pl.when(jnp.logical_and(used, new_expert))
    def _():
        wgu16_ref[...] = wgu_ref[0].astype(BF16)
        wd16_ref[...] = wd_ref[0].astype(BF16)

    @pl.when(used)
    def _():
        rs = x_ref.shape[0] // parts
        sl = [pl.ds(p * rs, rs) for p in range(parts)]
        gu = [_dot(x_ref[s, :].astype(BF16), wgu16_ref[...]) + bgu_ref[0] for s in sl]
        act = []
        for v in gu:
            gate = jnp.minimum(v[:, :d_ff], SWIGLU_LIMIT)
            up = jnp.clip(v[:, d_ff:], -SWIGLU_LIMIT, SWIGLU_LIMIT)
            act.append(((up + 1.0) * gate * _sigmoid(SWIGLU_ALPHA * gate)).astype(BF16))
        for s, a in zip(sl, act):
            y_ref[s, :] = _dot(a, wd16_ref[...]) + bd_ref[0]

    @pl.when(jnp.logical_not(used))
    def _():
        y_ref[...] = jnp.zeros(y_ref.shape, F32)


def _combine_kernel(n8c_ref, off8c_ref, dbc_ref, n8n_ref, off8n_ref, dbn_ref, pos_ref, h2_ref, topg_ref, fg_ref,
                    ys_ref, out_ref, gbuf_ref, tot_ref, sems, *, tm, width, n_exp, n_steps):
    i = pl.program_id(0)
    slot = i % 2

    def fetch(n8_ref, off8_ref, db_ref, s):
        total = 0
        for e in range(n_exp):
            n8, so, do = n8_ref[e], off8_ref[e], db_ref[e]
            total = total + n8
            _copy_strip(n8, lambda ofs, rows, so=so, do=do: pltpu.make_async_copy(
                ys_ref.at[_aligned(do + ofs, rows), :], gbuf_ref.at[s, _aligned(so + ofs, rows), :],
                sems.at[s]).start())
        tot_ref[s] = total

    @pl.when(i == 0)
    def _():
        gbuf_ref[...] = jnp.zeros(gbuf_ref.shape, F32)
        fetch(n8c_ref, off8c_ref, dbc_ref, 0)

    @pl.when(i + 1 < n_steps)
    def _():
        fetch(n8n_ref, off8n_ref, dbn_ref, 1 - slot)

    _wait_rows(tot_ref[slot], lambda b: pltpu.make_async_copy(
        ys_ref.at[pl.ds(0, b), :], gbuf_ref.at[slot, pl.ds(0, b), :], sems.at[slot]), _pow2_floor(width))
    pos = pos_ref[...]
    topg = topg_ref[...]
    lane = _iota((tm, width), 1)
    g = jnp.zeros((tm, width), F32)
    for k in range(TOP_K):
        g = g + jnp.where(lane == pos[:, k:k + 1], topg[:, k:k + 1], 0.0)
    moe = _dot(g.astype(BF16), gbuf_ref[slot].astype(BF16))
    out_ref[...] = _rms(h2_ref[...] + moe, fg_ref[...])


def _pad_lanes(v, fill=0.0):
    v = v.astype(F32)
    return jnp.concatenate([v, jnp.full((LANES - v.shape[0],), fill, F32)])[None, :]


def kernel(x, meta_tokens, norm_mix_g, w_in, conv_qkv_w, a_log, dt_bias, dn_norm_g, w_dn_out, b_glu,
           conv_dw_w, conv_dw_b, conv_ln_g, conv_ln_b, w_conv_out, b_conv_out, w_merge, b_merge, w_out,
           norm_ffn_g, w_router, b_router, w_gate_up, b_gate_up, w_down, b_down, final_norm_g):
    bsz, seq, d = x.shape
    depth = w_in.shape[0]
    n_meta = meta_tokens.shape[0]
    heads = a_log.shape[1]
    vw = w_dn_out.shape[1]
    qk_w = (conv_qkv_w.shape[2] - vw) // 2
    dk, dv = qk_w // heads, vw // heads
    ksz = conv_qkv_w.shape[1]
    kw, conv_ch = conv_dw_w.shape[1], conv_dw_w.shape[2]
    n_exp, d_ff = w_gate_up.shape[1], w_down.shape[2]
    meta_pad = CHUNK - n_meta
    lp = seq + n_meta + meta_pad
    assert depth == 1 and lp % CHUNK == 0 and dk == LANES and dv == LANES and d % LANES == 0
    assert 2 * heads <= LANES and n_exp <= LANES and kw - 1 <= 32 and ksz - 1 <= SUBLANES
    rows = bsz * lp
    n_head = meta_pad + n_meta
    head = jnp.concatenate([jnp.zeros((meta_pad, d), x.dtype), meta_tokens.astype(x.dtype)], axis=0)

    wi = w_in[0]
    o_a = 2 * qk_w + vw
    o_go = o_a + 2 * heads
    o_glu = o_go + vw
    w_qkv = wi[:, :o_a].astype(BF16)
    w_ab = jnp.pad(wi[:, o_a:o_go], ((0, 0), (0, LANES - 2 * heads))).astype(BF16)
    w_go = wi[:, o_go:o_glu].astype(BF16)
    w_glu = wi[:, o_glu:].astype(BF16)
    row2 = lambda v: v.astype(F32).reshape(1, -1)

    tm1 = _pick_tile(lp, min(320, seq), BF16_ROWS)
    nt1 = lp // tm1
    assert n_head % SUBLANES == 0 and n_head <= tm1 <= seq
    rspec = lambda cols, tm, nt: pl.BlockSpec((tm, cols), lambda b, j: (b * nt + j, 0))
    full = lambda a: pl.BlockSpec(a.shape, lambda *_: (0,) * a.ndim)
    x_rows = pl.BlockSpec((pl.Element(1), pl.Element(tm1), pl.Element(d)),
                          lambda b, j: (b, pl.multiple_of(jnp.maximum(j * tm1 - n_head, 0), SUBLANES), 0))
    front_in = [head, x, row2(norm_mix_g[0]), w_qkv, w_ab, w_go, w_glu, w_merge[0].astype(BF16),
                row2(b_glu[0]), row2(b_merge[0]), _pad_lanes(a_log[0]), _pad_lanes(dt_bias[0])]
    hp, qkv_pre, gbeta, go_act, u_glu, gates = pl.pallas_call(
        functools.partial(_front_kernel, tm=tm1, meta_pad=meta_pad, heads=heads, conv_ch=conv_ch),
        grid=(bsz, nt1),
        in_specs=[full(head), x_rows] + [full(a) for a in front_in[2:]],
        out_specs=[rspec(d, tm1, nt1), rspec(o_a, tm1, nt1), rspec(LANES, tm1, nt1), rspec(vw, tm1, nt1),
                   rspec(conv_ch, tm1, nt1), rspec(2 * d, tm1, nt1)],
        out_shape=[jax.ShapeDtypeStruct((rows, d), F32),
                   jax.ShapeDtypeStruct((rows, o_a), BF16), jax.ShapeDtypeStruct((rows, LANES), F32),
                   jax.ShapeDtypeStruct((rows, vw), BF16), jax.ShapeDtypeStruct((rows, conv_ch), BF16),
                   jax.ShapeDtypeStruct((rows, 2 * d), BF16)],
        compiler_params=_params(("arbitrary", "arbitrary")),
        name="front",
    )(*front_in)

    tmc = _pick_tile(lp, 832, 32)
    ntc = lp // tmc
    halo, lead = 32, BF16_ROWS
    conv_w8 = jnp.repeat(conv_dw_w[0].astype(F32), SUBLANES, axis=0)
    conv_in = [u_glu, gates, conv_w8, row2(conv_dw_b[0]), row2(conv_ln_g[0]),
               row2(conv_ln_b[0]), w_conv_out[0].astype(BF16), row2(b_conv_out[0])]
    ybg = pl.pallas_call(
        functools.partial(_conv_kernel, tm=tmc, kw=kw, halo=halo, lead=lead, rb=32),
        grid=(bsz, ntc),
        in_specs=[rspec(conv_ch, tmc, ntc),
                  pl.BlockSpec((tmc, d), lambda b, j: (b * ntc + j, 1))] + [full(a) for a in conv_in[2:]],
        out_specs=rspec(d, tmc, ntc),
        out_shape=jax.ShapeDtypeStruct((rows, d), BF16),
        scratch_shapes=[pltpu.VMEM((lead + halo + tmc, conv_ch), BF16),
                        pltpu.VMEM((SUBLANES, lead + halo + tmc, conv_ch), F32),
                        pltpu.VMEM((tmc, conv_ch), F32)],
        compiler_params=_params(("arbitrary", "arbitrary")),
        name="convmod",
    )(*conv_in)

    nc = lp // CHUNK
    nbd = max(n for n in (1, 2, 4) if bsz % n == 0)
    cspec = lambda cols: pl.BlockSpec((nbd, CHUNK, cols), lambda b, c: (b, c, 0))
    delta_in = [qkv_pre.reshape(bsz, lp, o_a), gbeta.reshape(bsz, lp, LANES), go_act.reshape(bsz, lp, vw),
                conv_qkv_w[0].astype(F32), row2(dn_norm_g[0])]
    o_dn = pl.pallas_call(
        functools.partial(_delta_kernel, nb=nbd, heads=heads, dk=dk, dv=dv, ch=CHUNK, ksz=ksz, halo=BF16_ROWS),
        grid=(bsz // nbd, nc),
        in_specs=[cspec(o_a), cspec(LANES), cspec(vw), full(delta_in[3]), full(delta_in[4])],
        out_specs=cspec(vw),
        out_shape=jax.ShapeDtypeStruct((bsz, lp, vw), BF16),
        scratch_shapes=[pltpu.VMEM((nbd, heads, dk, dv), F32),
                        pltpu.VMEM((nbd, CHUNK + BF16_ROWS, o_a), BF16)],
        compiler_params=_params(("arbitrary", "arbitrary")),
        name="delta",
    )(*delta_in).reshape(rows, vw)

    tm3 = _pick_tile(lp, MID_TILE, BF16_ROWS)
    nt3 = lp // tm3
    w_r = jnp.pad(w_router[0], ((0, 0), (0, LANES - n_exp))).astype(BF16)
    mid_in = [hp, o_dn, gates, ybg, w_dn_out[0].astype(BF16), w_out[0].astype(BF16),
              row2(norm_ffn_g[0]), w_r, _pad_lanes(b_router[0])]
    h2, hn2, topi, topg = pl.pallas_call(
        functools.partial(_mid_kernel, tm=tm3, n_exp=n_exp, parts=MID_PARTS if tm3 % (MID_PARTS * BF16_ROWS) == 0 else 1),
        grid=(bsz, nt3),
        in_specs=[rspec(d, tm3, nt3), rspec(vw, tm3, nt3), rspec(d, tm3, nt3), rspec(d, tm3, nt3)]
        + [full(a) for a in mid_in[4:]],
        out_specs=[rspec(d, tm3, nt3), rspec(d, tm3, nt3), rspec(LANES, tm3, nt3), rspec(LANES, tm3, nt3)],
        out_shape=[jax.ShapeDtypeStruct((rows, d), F32), jax.ShapeDtypeStruct((rows, d), F32),
                   jax.ShapeDtypeStruct((rows, LANES), I32), jax.ShapeDtypeStruct((rows, LANES), F32)],
        compiler_params=_params(("arbitrary", "arbitrary")),
        name="mid",
    )(*mid_in)

    tmd = _pick_tile(seq, TOKEN_TILE, SUBLANES)
    tpb = seq // tmd
    ntd = bsz * tpb
    tokens = bsz * seq
    width = tmd * TOP_K + n_exp * SUBLANES
    row0 = lambda i: pl.multiple_of((i // tpb) * lp + n_head + (i % tpb) * tmd, SUBLANES)
    prow = lambda r, cols, f=row0: pl.BlockSpec((pl.Element(r), pl.Element(cols)), lambda i, *_: (f(i), 0))
    tspec = pl.BlockSpec((tmd, LANES), lambda i, *_: (i, 0))
    ptiles = _pick_tile(tpb, 8, 1)
    pspec = pl.BlockSpec((ptiles * tmd, LANES), lambda i: (i, 0))
    tab = pl.BlockSpec((ptiles, 1, LANES), lambda i: (i, 0, 0))
    tab_shape = jax.ShapeDtypeStruct((ntd, 1, LANES), I32)
    pos, pos_t, n8_t, off8_t, carry8_t, tot8 = pl.pallas_call(
        functools.partial(_plan_kernel, tm=tmd, tiles=ptiles, n_exp=n_exp),
        grid=(ntd // ptiles,),
        in_specs=[prow(ptiles * tmd, LANES, lambda i: row0(i * ptiles))],
        out_specs=[pspec, pl.BlockSpec((ptiles, SUBLANES, tmd), lambda i: (i, 0, 0)), tab, tab, tab,
                   pl.BlockSpec((1, LANES), lambda i: (0, 0))],
        out_shape=[jax.ShapeDtypeStruct((tokens, LANES), I32), jax.ShapeDtypeStruct((ntd, SUBLANES, tmd), I32),
                   tab_shape, tab_shape, tab_shape, jax.ShapeDtypeStruct((1, LANES), F32)],
        scratch_shapes=[pltpu.VMEM((1, LANES), F32)],
        compiler_params=_params(("arbitrary",)),
        name="plan",
    )(topi)

    n_blocks = -(-(tokens * TOP_K + n_exp * (SUBLANES - 1) * ntd) // MOE_BLOCK) + n_exp
    cap = n_blocks * MOE_BLOCK
    counts = tot8[0, :n_exp].astype(I32)
    padded = (counts + MOE_BLOCK - 1) // MOE_BLOCK * MOE_BLOCK
    pend = jnp.cumsum(padded).astype(I32)
    pstart = pend - padded
    n_used = (pend[-1] // MOE_BLOCK).astype(I32).reshape(1)
    block_start = jnp.arange(n_blocks, dtype=I32) * MOE_BLOCK
    block_e = jnp.minimum(jnp.sum((pend[None, :] <= block_start[:, None]).astype(I32), axis=1), n_exp - 1)
    dbase = (carry8_t.reshape(ntd, LANES) + jnp.pad(pstart, (0, LANES - n_exp))[None, :]).reshape(ntd * LANES)
    n8_f = n8_t.reshape(ntd * LANES)
    off8_f = off8_t.reshape(ntd * LANES)

    anyspec = pl.BlockSpec(memory_space=pl.ANY)
    stab = lambda f: pl.BlockSpec((LANES,), lambda i, *_: (f(i),), memory_space=pltpu.SMEM)
    cur = lambda i: i
    xs = pl.pallas_call(
        functools.partial(_dispatch_kernel, tm=tmd, width=width, n_exp=n_exp, n_steps=ntd, n_blocks=n_blocks),
        grid_spec=pltpu.PrefetchScalarGridSpec(
            num_scalar_prefetch=2, grid=(ntd,),
            in_specs=[stab(cur), stab(cur), stab(cur), prow(tmd, d),
                      pl.BlockSpec((1, SUBLANES, tmd), lambda i, *_: (i, 0, 0))],
            out_specs=anyspec,
            scratch_shapes=[pltpu.VMEM((MOE_BLOCK, d), F32), pltpu.VMEM((2, width, d), F32),
                            pltpu.SMEM((2,), I32), pltpu.SemaphoreType.DMA(()), pltpu.SemaphoreType.DMA((2,))]),
        out_shape=jax.ShapeDtypeStruct((cap, d), F32),
        compiler_params=_params(("arbitrary",)),
        name="dispatch",
    )(pend, padded, n8_f, off8_f, dbase, hn2, pos_t)

    last_used = lambda i, nbu: jnp.minimum(i, nbu[0] - 1)
    ew = lambda a: pl.BlockSpec((1,) + a.shape[1:], lambda i, be, nbu: (be[i],) + (0,) * (a.ndim - 1))
    e_in = [w_gate_up[0], b_gate_up[0].astype(F32)[:, None, :], w_down[0], b_down[0].astype(F32)[:, None, :]]
    ys = pl.pallas_call(
        functools.partial(_expert_kernel, d_ff=d_ff, parts=2),
        grid_spec=pltpu.PrefetchScalarGridSpec(
            num_scalar_prefetch=2, grid=(n_blocks,),
            in_specs=[pl.BlockSpec((MOE_BLOCK, d), lambda i, be, nbu: (last_used(i, nbu), 0))]
            + [ew(a) for a in e_in],
            out_specs=pl.BlockSpec((MOE_BLOCK, d), lambda i, be, nbu: (i, 0)),
            scratch_shapes=[pltpu.VMEM((d, 2 * d_ff), BF16), pltpu.VMEM((d_ff, d), BF16)]),
        out_shape=jax.ShapeDtypeStruct((cap, d), F32),
        compiler_params=_params(("arbitrary",)),
        name="experts",
    )(block_e, n_used, xs, *e_in)

    nxt = lambda i: jnp.minimum(i + 1, ntd - 1)
    out = pl.pallas_call(
        functools.partial(_combine_kernel, tm=tmd, width=width, n_exp=n_exp, n_steps=ntd),
        grid=(ntd,),
        in_specs=[stab(cur), stab(cur), stab(cur), stab(nxt), stab(nxt), stab(nxt), tspec,
                  prow(tmd, d), prow(tmd, LANES), pl.BlockSpec((1, d), lambda i: (0, 0)), anyspec],
        out_specs=pl.BlockSpec((tmd, d), lambda i: (i, 0)),
        out_shape=jax.ShapeDtypeStruct((tokens, d), x.dtype),
        scratch_shapes=[pltpu.VMEM((2, width, d), F32), pltpu.SMEM((2,), I32), pltpu.SemaphoreType.DMA((2,))],
        compiler_params=_params(("arbitrary",)),
        name="combine",
    )(n8_f, off8_f, dbase, n8_f, off8_f, dbase, pos, h2, topg, row2(final_norm_g), ys)
    return out.reshape(bsz, seq, d)
```

```python
import functools

import jax
import jax.numpy as jnp
from jax import lax
from jax.experimental import pallas as pl
from jax.experimental.pallas import tpu as pltpu

F32 = jnp.float32
BF16 = jnp.bfloat16
I32 = jnp.int32

CHUNK = 64
TOP_K = 4
MOE_BLOCK = 512
TOKEN_TILE = 256
MID_TILE = 832
MID_PARTS = 4
SWIGLU_LIMIT = 7.0
SWIGLU_ALPHA = 1.702
EPS = 1e-6
LN_EPS = 1e-5
LANES = 128
SUBLANES = 8
BF16_ROWS = 16
VMEM_LIMIT = 56 * 1024 * 1024


def _dot(a, b):
    return jnp.dot(a, b, preferred_element_type=F32)


def _dot_nt(a, b):
    return lax.dot_general(a, b, (((1,), (1,)), ((), ())), preferred_element_type=F32)


def _dot_tn(a, b):
    return lax.dot_general(a, b, (((0,), (0,)), ((), ())), preferred_element_type=F32)


def _sigmoid(x):
    return 1.0 / (1.0 + jnp.exp(-x))


def _silu(x):
    return x * _sigmoid(x)


def _softplus(x):
    return jnp.maximum(x, 0.0) + jnp.log(1.0 + jnp.exp(-jnp.abs(x)))


def _rms(x, g):
    return x * lax.rsqrt(jnp.mean(x * x, axis=-1, keepdims=True) + EPS) * g


def _iota(shape, dim):
    return lax.broadcasted_iota(I32, shape, dim)


def _pick_tile(n, target, mult):
    best = None
    for t in range(mult, min(n, target) + 1, mult):
        if n % t == 0:
            best = t
    assert best is not None, (n, target, mult)
    return best


def _params(sem, flags=None):
    return pltpu.CompilerParams(dimension_semantics=sem, vmem_limit_bytes=VMEM_LIMIT, flags=flags)


def _front_kernel(head_ref, x_ref, g_ref, wqkv_ref, wab_ref, wgo_ref, wglu_ref, wmg_ref, bglu_ref, bmg_ref,
                  alog_ref, dtb_ref, h_ref, qkv_ref, gb_ref, go_ref, u_ref, gates_ref,
                  *, tm, meta_pad, heads, conv_ch):
    j = pl.program_id(1)
    xb = x_ref[0]
    n_head = head_ref.shape[0]
    h = jnp.where(j == 0, jnp.concatenate([head_ref[...], xb[:tm - n_head]], axis=0), xb)
    h_ref[...] = h
    hn = _rms(h, g_ref[...]).astype(BF16)
    qkv_ref[...] = _dot(hn, wqkv_ref[...]).astype(BF16)
    ab = _dot(hn, wab_ref[...])
    valid = (j * tm + _iota((tm, 1), 0)) >= meta_pad
    lane = _iota((1, LANES), 1)
    g = -jnp.exp(alog_ref[...]) * _softplus(ab + dtb_ref[...])
    gb = jnp.where(lane < heads, g, _sigmoid(ab))
    gb_ref[...] = jnp.where(valid, gb, 0.0)
    go_ref[...] = _silu(_dot(hn, wgo_ref[...])).astype(BF16)
    glu = _dot(hn, wglu_ref[...]) + bglu_ref[...]
    u = glu[:, :conv_ch] * _sigmoid(glu[:, conv_ch:])
    u_ref[...] = jnp.where(valid, u, 0.0).astype(BF16)
    gates_ref[...] = _sigmoid(_dot(hn, wmg_ref[...]) + bmg_ref[...]).astype(BF16)


def _conv_kernel(u_ref, gate_ref, w_ref, b_ref, lng_ref, lnb_ref, wo_ref, bo_ref, y_ref,
                 xcat_ref, xs_ref, cbuf_ref, *, tm, kw, halo, lead, rb):
    j = pl.program_id(1)
    base = lead + halo
    total = base + tm
    n_c = xcat_ref.shape[1]

    @pl.when(j == 0)
    def _():
        xcat_ref[0:base, :] = jnp.zeros((base, n_c), BF16)

    @pl.when(j > 0)
    def _():
        xcat_ref[lead:base, :] = xcat_ref[tm + lead:tm + base, :]

    xcat_ref[base:total, :] = u_ref[...]
    sr = _iota((SUBLANES * rb, lead + rb), 0)
    sc = _iota((SUBLANES * rb, lead + rb), 1)
    shift = (sc == (sr & (rb - 1)) + lead - (sr >> (rb.bit_length() - 1))).astype(BF16)
    bias = b_ref[...]
    for t0 in range(lead, total, rb):
        res = _dot(shift, xcat_ref[t0 - lead:t0 + rb, :])
        for r in range(SUBLANES):
            xs_ref[r, t0:t0 + rb, :] = res[r * rb:(r + 1) * rb, :]
        if t0 < base:
            continue
        i0 = t0
        blk = (i0 - base) // rb
        acc = jnp.zeros((rb, n_c), F32)
        for s in range(kw):
            a, r = divmod(s, SUBLANES)
            wt = w_ref[(kw - 1 - s) * SUBLANES:(kw - s) * SUBLANES, :]
            acc = acc + xs_ref[r, i0 - SUBLANES * a:i0 - SUBLANES * a + rb, :] * jnp.concatenate(
                [wt] * (rb // SUBLANES), axis=0)
        cbuf_ref[blk * rb:(blk + 1) * rb, :] = acc + bias
    c = cbuf_ref[...]
    mu = jnp.mean(c, axis=-1, keepdims=True)
    xc = c - mu
    ln = xc * lax.rsqrt(jnp.mean(xc * xc, axis=-1, keepdims=True) + LN_EPS) * lng_ref[...] + lnb_ref[...]
    yb = _dot(_silu(ln).astype(BF16), wo_ref[...]) + bo_ref[...]
    y_ref[...] = (gate_ref[...].astype(F32) * yb).astype(BF16)


def _bdot(a, b):
    return jnp.dot(a.astype(BF16), b.astype(BF16), preferred_element_type=F32)


def _split3(x):
    x1 = x.astype(BF16)
    r1 = x - x1.astype(F32)
    x2 = r1.astype(BF16)
    x3 = (r1 - x2.astype(F32)).astype(BF16)
    return x1, x2, x3


def _unit_lower_inverses_minus_eye(mats, row, col):
    ch = mats[0].shape[0]
    blk16 = (row >> 4) == (col >> 4)
    blk32 = (row >> 5) == (col >> 5)
    n1 = [jnp.where(blk16, a, 0.0) for a in mats]
    n2 = [_bdot(n, n) for n in n1]
    r = [_bdot(jnp.concatenate([n, m], axis=0), m) for n, m in zip(n1, n2)]
    n4 = [x[ch:] for x in r]
    y = [m - n - x[:ch] for n, m, x in zip(n1, n2, r)]
    r = [_bdot(jnp.concatenate([yy, m], axis=0), m) for yy, m in zip(y, n4)]
    n8 = [x[ch:] for x in r]
    y = [yy + m + x[:ch] for yy, m, x in zip(y, n4, r)]
    y = [yy + m + _bdot(yy, m) for yy, m in zip(y, n8)]
    for mask in (blk32 & jnp.logical_not(blk16), jnp.logical_not(blk32)):
        ls = [jnp.where(mask, a, 0.0) for a in mats]
        ms = [l + _bdot(l, yy) for l, yy in zip(ls, y)]
        y = [yy - m - _bdot(yy, m) for yy, m in zip(y, ms)]
    return y


def _delta_kernel(qkv_ref, gb_ref, go_ref, cw_ref, ng_ref, o_ref, s_ref, xc_ref,
                  *, nb, heads, dk, dv, ch, ksz, halo):
    c = pl.program_id(1)
    qk_w = heads * dk

    @pl.when(c == 0)
    def _():
        s_ref[...] = jnp.zeros(s_ref.shape, F32)
        xc_ref[:, 0:halo, :] = jnp.zeros((nb, halo, xc_ref.shape[2]), BF16)

    @pl.when(c > 0)
    def _():
        xc_ref[:, 0:halo, :] = xc_ref[:, ch:ch + halo, :]

    xc_ref[:, halo:halo + ch, :] = qkv_ref[...]
    row = _iota((ch, ch), 0)
    col = _iota((ch, ch), 1)
    incl = row >= col
    strict = row > col
    tril = incl.astype(BF16)
    triu = (row <= col).astype(BF16)
    sr = _iota(((ksz - 1) * ch, halo + ch), 0)
    sc = _iota(((ksz - 1) * ch, halo + ch), 1)
    shift = (sc == (sr & (ch - 1)) + (sr >> (ch.bit_length() - 1)) + (halo - (ksz - 1))).astype(BF16)

    qn, kn, kb, vb, gamma, eg, ekd, elast = [], [], [], [], [], [], [], []
    shifted = [_dot(shift, xc_ref[b]) for b in range(nb)]
    for b in range(nb):
        acc = xc_ref[b, halo:halo + ch, :].astype(F32) * cw_ref[ksz - 1:ksz, :]
        for t in range(ksz - 1):
            acc = acc + shifted[b][t * ch:(t + 1) * ch, :] * cw_ref[t:t + 1, :]
        qkv = _silu(acc)
        gb = gb_ref[b]
        parts = _split3(gb)
        gc = sum(_dot(tril, p) for p in parts)
        gct = sum(_dot_tn(p, triu) for p in parts)
        glast = gc[ch - 1:ch, :]
        e_g = jnp.exp(gc)
        e_kd = jnp.exp(glast - gc)
        e_last = jnp.exp(glast)
        for h in range(heads):
            q = qkv[:, h * dk:(h + 1) * dk]
            k = qkv[:, qk_w + h * dk:qk_w + (h + 1) * dk]
            v = qkv[:, 2 * qk_w + h * dv:2 * qk_w + (h + 1) * dv]
            qn.append(q * lax.rsqrt(jnp.sum(q * q, axis=-1, keepdims=True) + EPS) * (dk ** -0.5))
            kn.append(k * lax.rsqrt(jnp.sum(k * k, axis=-1, keepdims=True) + EPS))
            beta = gb[:, heads + h:heads + h + 1]
            kb.append(kn[-1] * beta)
            vb.append(v * beta)
            diff = gc[:, h:h + 1] - gct[h:h + 1, :]
            gamma.append(jnp.where(incl, jnp.exp(jnp.where(incl, diff, 0.0)), 0.0))
            eg.append(e_g[:, h:h + 1])
            ekd.append(e_kd[:, h:h + 1])
            elast.append(e_last[:, h:h + 1])

    n = nb * heads
    kq = [_dot_nt(jnp.concatenate([kb[i], qn[i]], axis=0).astype(BF16), kn[i].astype(BF16))
          for i in range(n)]
    a = [jnp.where(strict, kq[i][:ch] * gamma[i], 0.0) for i in range(n)]
    qk = [kq[i][ch:] * gamma[i] for i in range(n)]
    y = _unit_lower_inverses_minus_eye(a, row, col)
    rhs = [jnp.concatenate([vb[i], kb[i] * eg[i]], axis=-1) for i in range(n)]
    uw = [rhs[i] + _bdot(y[i], rhs[i]) for i in range(n)]
    s = [s_ref[i // heads, i % heads] for i in range(n)]
    ws = [_bdot(jnp.concatenate([uw[i][:, dv:], qn[i] * eg[i]], axis=0), s[i]) for i in range(n)]
    v_new = [uw[i][:, :dv] - ws[i][:ch] for i in range(n)]
    o = [ws[i][ch:] + _bdot(qk[i], v_new[i]) for i in range(n)]
    s_new = [s[i] * elast[i] + _dot_tn((kn[i] * ekd[i]).astype(BF16), v_new[i].astype(BF16))
             for i in range(n)]
    for i in range(n):
        b, h = i // heads, i % heads
        s_ref[b, h] = s_new[i]
        on = _rms(o[i], ng_ref[...])
        o_ref[b, :, h * dv:(h + 1) * dv] = (
            on * go_ref[b, :, h * dv:(h + 1) * dv].astype(F32)).astype(BF16)


def _mid_kernel(h_ref, o_ref, ga_ref, ybg_ref, wdn_ref, wout_ref, g2_ref, wr_ref, br_ref,
                h2_ref, hn_ref, topi_ref, topg_ref, *, tm, n_exp, parts):
    rs = tm // parts
    sl = [pl.ds(p * rs, rs) for p in range(parts)]
    ya = [_dot(o_ref[s, :], wdn_ref[...]) for s in sl]
    y = [ga_ref[s, :].astype(F32) * a + ybg_ref[s, :].astype(F32) for s, a in zip(sl, ya)]
    h2 = [h_ref[s, :] + _dot(v.astype(BF16), wout_ref[...]) for s, v in zip(sl, y)]
    hn = [_rms(v, g2_ref[...]) for v in h2]
    for s, v, w in zip(sl, h2, hn):
        h2_ref[s, :] = v
        hn_ref[s, :] = w
    logits = [_dot(v.astype(BF16), wr_ref[...]) + br_ref[...] for v in hn]
    lane = _iota((rs, LANES), 1)
    lane_f = lane.astype(F32)
    l = [jnp.where(lane < n_exp, v, -jnp.inf) for v in logits]
    vals = [[] for _ in range(parts)]
    idxs = [[] for _ in range(parts)]
    for _ in range(TOP_K):
        m = [jnp.max(v, axis=-1, keepdims=True) for v in l]
        idx = [jnp.min(jnp.where(v == mm, lane_f, float(LANES)), axis=-1, keepdims=True)
               for v, mm in zip(l, m)]
        l = [jnp.where(lane_f == ii, -jnp.inf, v) for v, ii in zip(l, idx)]
        for p in range(parts):
            vals[p].append(m[p])
            idxs[p].append(idx[p])
    for p in range(parts):
        es = [jnp.exp(v - vals[p][0]) for v in vals[p]]
        den = es[0]
        for e in es[1:]:
            den = den + e
        topi = jnp.zeros((rs, LANES), F32)
        topg = jnp.zeros((rs, LANES), F32)
        for k in range(TOP_K):
            topi = jnp.where(lane == k, idxs[p][k], topi)
            topg = jnp.where(lane == k, es[k] / den, topg)
        topi_ref[sl[p], :] = topi.astype(I32)
        topg_ref[sl[p], :] = topg


def _choice_mask(topi, lane, n_exp):
    m = jnp.zeros(topi.shape, F32)
    for k in range(TOP_K):
        m = m + (lane == topi[:, k:k + 1]).astype(F32)
    return jnp.where(lane < n_exp, m, 0.0)


def _plan_kernel(topi_ref, pos_ref, post_ref, n8_ref, off8_ref, carry8_ref, tot8_ref, carry_ref,
                 *, tm, tiles, n_exp):
    i = pl.program_id(0)

    @pl.when(i == 0)
    def _():
        carry_ref[...] = jnp.zeros(carry_ref.shape, F32)

    lane = _iota((tm, LANES), 1)
    tri = (_iota((tm, tm), 0) >= _iota((tm, tm), 1)).astype(BF16)
    before = (_iota((LANES, LANES), 0) < _iota((LANES, LANES), 1)).astype(BF16)
    js = range(tiles)
    rows = [pl.ds(j * tm, tm) for j in js]
    topi = [topi_ref[r, :] for r in rows]
    m = [_choice_mask(t, lane, n_exp) for t in topi]
    cum = [_dot(tri, v.astype(BF16)) for v in m]
    n8 = [jnp.floor((c[tm - 1:tm, :] + (SUBLANES - 1)) * (1.0 / SUBLANES)) * SUBLANES for c in cum]
    off8 = [_dot(jnp.broadcast_to(v, (SUBLANES, LANES)).astype(BF16), before)[0:1] for v in n8]
    posf = [c - v + o for c, v, o in zip(cum, m, off8)]
    pos = [jnp.zeros((tm, LANES), F32) for _ in js]
    for k in range(TOP_K):
        pk = [jnp.sum(jnp.where(lane == t[:, k:k + 1], p, 0.0), axis=-1, keepdims=True)
              for t, p in zip(topi, posf)]
        pos = [jnp.where(lane == k, a, b) for a, b in zip(pk, pos)]
    carry = carry_ref[...]
    for j in js:
        pos_ref[rows[j], :] = pos[j].astype(I32)
        post_ref[j] = jnp.transpose(pos[j])[0:SUBLANES, :].astype(I32)
        n8_ref[j] = n8[j].astype(I32)
        off8_ref[j] = off8[j].astype(I32)
        carry8_ref[j] = carry.astype(I32)
        carry = carry + n8[j]
    carry_ref[...] = carry
    tot8_ref[...] = carry


def _pow2_floor(n):
    return 1 << (n.bit_length() - 1)


def _strip_pieces(n8, max_rows):
    pieces = []
    b = max_rows
    while b >= SUBLANES:
        pieces.append(((n8 & b) != 0, n8 & ~(2 * b - 1), b))
        b //= 2
    return pieces


LONG_PIECE = 64


def _copy_strip(n8, start_piece):
    n_long = n8 >> (LONG_PIECE.bit_length() - 1)

    def long_piece(j, carry):
        start_piece(j * LONG_PIECE, LONG_PIECE)
        return carry

    lax.fori_loop(0, n_long, long_piece, 0)
    for cond, ofs, rows in _strip_pieces(n8 & (LONG_PIECE - 1), LONG_PIECE // 2):
        @pl.when(cond)
        def _():
            start_piece(n_long * LONG_PIECE + ofs, rows)


def _wait_rows(total, make_copy, max_rows):
    b = max_rows
    while b >= SUBLANES:
        @pl.when((total & b) != 0)
        def _():
            make_copy(b).wait()
        b //= 2


def _aligned(start, rows):
    return pl.ds(pl.multiple_of(start, SUBLANES), rows)


def _dispatch_kernel(pend_ref, padded_ref, n8_ref, off8_ref, dbase_ref, hn_ref, post_ref, xs_ref,
                     zbuf_ref, sbuf_ref, tot_ref, zsem, sems, *, tm, width, n_exp, n_steps, n_blocks):
    i = pl.program_id(0)
    slot = i % 2

    def zero_block(start):
        return pltpu.make_async_copy(zbuf_ref, xs_ref.at[_aligned(start, MOE_BLOCK), :], zsem)

    @pl.when(i == 0)
    def _():
        zbuf_ref[...] = jnp.zeros(zbuf_ref.shape, F32)
        for e in range(n_exp):
            @pl.when(padded_ref[e] > 0)
            def _():
                zero_block(pend_ref[e] - MOE_BLOCK).start()
        first_unused = pend_ref[n_exp - 1] // MOE_BLOCK

        def start_tail(j, carry):
            zero_block(j * MOE_BLOCK).start()
            return carry

        lax.fori_loop(first_unused, n_blocks, start_tail, 0)
        for e in range(n_exp):
            @pl.when(padded_ref[e] > 0)
            def _():
                zero_block(0).wait()

        def wait_tail(j, carry):
            zero_block(0).wait()
            return carry

        lax.fori_loop(first_unused, n_blocks, wait_tail, 0)

    def strip_wait(s):
        _wait_rows(tot_ref[s], lambda b: pltpu.make_async_copy(
            sbuf_ref.at[s, pl.ds(0, b), :], xs_ref.at[pl.ds(0, b), :], sems.at[s]), _pow2_floor(width))

    @pl.when(i >= 2)
    def _():
        strip_wait(slot)

    post = post_ref[0]
    orow = _iota((width, tm), 0)
    sel = orow == post[0:1, :]
    for k in range(1, TOP_K):
        sel = jnp.logical_or(sel, orow == post[k:k + 1, :])
    sbuf_ref[slot] = _dot(sel.astype(BF16), hn_ref[...].astype(BF16))

    total = 0
    for e in range(n_exp):
        n8, so, do = n8_ref[e], off8_ref[e], dbase_ref[e]
        total = total + n8
        _copy_strip(n8, lambda ofs, rows, so=so, do=do: pltpu.make_async_copy(
            sbuf_ref.at[slot, _aligned(so + ofs, rows), :], xs_ref.at[_aligned(do + ofs, rows), :],
            sems.at[slot]).start())
    tot_ref[slot] = total

    @pl.when(i == n_steps - 1)
    def _():
        if n_steps >= 2:
            strip_wait(1 - slot)
        strip_wait(slot)


def _expert_kernel(be_ref, nbu_ref, x_ref, wgu_ref, bgu_ref, wd_ref, bd_ref, y_ref, wgu16_ref, wd16_ref,
                   *, d_ff, parts):
    i = pl.program_id(0)
    used = i < nbu_ref[0]
    new_expert = jnp.logical_or(i == 0, be_ref[i] != be_ref[jnp.maximum(i - 1, 0)])

    @pl.when(jnp.logical_and(used, new_expert))
    def _():
        wgu16_ref[...] = wgu_ref[0].astype(BF16)
        wd16_ref[...] = wd_ref[0].astype(BF16)

    @pl.when(used)
    def _():
        rs = x_ref.shape[0] // parts
        sl = [pl.ds(p * rs, rs) for p in range(parts)]
        gu = [_dot(x_ref[s, :].astype(BF16), wgu16_ref[...]) + bgu_ref[0] for s in sl]
        act = []
        for v in gu:
            gate = jnp.minimum(v[:, :d_ff], SWIGLU_LIMIT)
            up = jnp.clip(v[:, d_ff:], -SWIGLU_LIMIT, SWIGLU_LIMIT)
            act.append(((up + 1.0) * gate * _sigmoid(SWIGLU_ALPHA * gate)).astype(BF16))
        for s, a in zip(sl, act):
            y_ref[s, :] = _dot(a, wd16_ref[...]) + bd_ref[0]

    @pl.when(jnp.logical_not(used))
    def _():
        y_ref[...] = jnp.zeros(y_ref.shape, F32)


def _combine_kernel(n8c_ref, off8c_ref, dbc_ref, n8n_ref, off8n_ref, dbn_ref, pos_ref, h2_ref, topg_ref, fg_ref,
                    ys_ref, out_ref, gbuf_ref, tot_ref, sems, *, tm, width, n_exp, n_steps):
    i = pl.program_id(0)
    slot = i % 2

    def fetch(n8_ref, off8_ref, db_ref, s):
        total = 0
        for e in range(n_exp):
            n8, so, do = n8_ref[e], off8_ref[e], db_ref[e]
            total = total + n8
            _copy_strip(n8, lambda ofs, rows, so=so, do=do: pltpu.make_async_copy(
                ys_ref.at[_aligned(do + ofs, rows), :], gbuf_ref.at[s, _aligned(so + ofs, rows), :],
                sems.at[s]).start())
        tot_ref[s] = total

    @pl.when(i == 0)
    def _():
        gbuf_ref[...] = jnp.zeros(gbuf_ref.shape, F32)
        fetch(n8c_ref, off8c_ref, dbc_ref, 0)

    @pl.when(i + 1 < n_steps)
    def _():
        fetch(n8n_ref, off8n_ref, dbn_ref, 1 - slot)

    _wait_rows(tot_ref[slot], lambda b: pltpu.make_async_copy(
        ys_ref.at[pl.ds(0, b), :], gbuf_ref.at[slot, pl.ds(0, b), :], sems.at[slot]), _pow2_floor(width))
    pos = pos_ref[...]
    topg = topg_ref[...]
    lane = _iota((tm, width), 1)
    g = jnp.zeros((tm, width), F32)
    for k in range(TOP_K):
        g = g + jnp.where(lane == pos[:, k:k + 1], topg[:, k:k + 1], 0.0)
    moe = _dot(g.astype(BF16), gbuf_ref[slot].astype(BF16))
    out_ref[...] = _rms(h2_ref[...] + moe, fg_ref[...])


def _pad_lanes(v, fill=0.0):
    v = v.astype(F32)
    return jnp.concatenate([v, jnp.full((LANES - v.shape[0],), fill, F32)])[None, :]


def kernel(x, meta_tokens, norm_mix_g, w_in, conv_qkv_w, a_log, dt_bias, dn_norm_g, w_dn_out, b_glu,
           conv_dw_w, conv_dw_b, conv_ln_g, conv_ln_b, w_conv_out, b_conv_out, w_merge, b_merge, w_out,
           norm_ffn_g, w_router, b_router, w_gate_up, b_gate_up, w_down, b_down, final_norm_g):
    bsz, seq, d = x.shape
    depth = w_in.shape[0]
    n_meta = meta_tokens.shape[0]
    heads = a_log.shape[1]
    vw = w_dn_out.shape[1]
    qk_w = (conv_qkv_w.shape[2] - vw) // 2
    dk, dv = qk_w // heads, vw // heads
    ksz = conv_qkv_w.shape[1]
    kw, conv_ch = conv_dw_w.shape[1], conv_dw_w.shape[2]
    n_exp, d_ff = w_gate_up.shape[1], w_down.shape[2]
    meta_pad = CHUNK - n_meta
    lp = seq + n_meta + meta_pad
    assert depth == 1 and lp % CHUNK == 0 and dk == LANES and dv == LANES and d % LANES == 0
    assert 2 * heads <= LANES and n_exp <= LANES and kw - 1 <= 32 and ksz - 1 <= SUBLANES
    rows = bsz * lp
    n_head = meta_pad + n_meta
    head = jnp.concatenate([jnp.zeros((meta_pad, d), x.dtype), meta_tokens.astype(x.dtype)], axis=0)

    wi = w_in[0]
    o_a = 2 * qk_w + vw
    o_go = o_a + 2 * heads
    o_glu = o_go + vw
    w_qkv = wi[:, :o_a].astype(BF16)
    w_ab = jnp.pad(wi[:, o_a:o_go], ((0, 0), (0, LANES - 2 * heads))).astype(BF16)
    w_go = wi[:, o_go:o_glu].astype(BF16)
    w_glu = wi[:, o_glu:].astype(BF16)
    row2 = lambda v: v.astype(F32).reshape(1, -1)

    tm1 = _pick_tile(lp, min(320, seq), BF16_ROWS)
    nt1 = lp // tm1
    assert n_head % SUBLANES == 0 and n_head <= tm1 <= seq
    rspec = lambda cols, tm, nt: pl.BlockSpec((tm, cols), lambda b, j: (b * nt + j, 0))
    full = lambda a: pl.BlockSpec(a.shape, lambda *_: (0,) * a.ndim)
    x_rows = pl.BlockSpec((pl.Element(1), pl.Element(tm1), pl.Element(d)),
                          lambda b, j: (b, pl.multiple_of(jnp.maximum(j * tm1 - n_head, 0), SUBLANES), 0))
    front_in = [head, x, row2(norm_mix_g[0]), w_qkv, w_ab, w_go, w_glu, w_merge[0].astype(BF16),
                row2(b_glu[0]), row2(b_merge[0]), _pad_lanes(a_log[0]), _pad_lanes(dt_bias[0])]
    hp, qkv_pre, gbeta, go_act, u_glu, gates = pl.pallas_call(
        functools.partial(_front_kernel, tm=tm1, meta_pad=meta_pad, heads=heads, conv_ch=conv_ch),
        grid=(bsz, nt1),
        in_specs=[full(head), x_rows] + [full(a) for a in front_in[2:]],
        out_specs=[rspec(d, tm1, nt1), rspec(o_a, tm1, nt1), rspec(LANES, tm1, nt1), rspec(vw, tm1, nt1),
                   rspec(conv_ch, tm1, nt1), rspec(2 * d, tm1, nt1)],
        out_shape=[jax.ShapeDtypeStruct((rows, d), F32),
                   jax.ShapeDtypeStruct((rows, o_a), BF16), jax.ShapeDtypeStruct((rows, LANES), F32),
                   jax.ShapeDtypeStruct((rows, vw), BF16), jax.ShapeDtypeStruct((rows, conv_ch), BF16),
                   jax.ShapeDtypeStruct((rows, 2 * d), BF16)],
        compiler_params=_params(("arbitrary", "arbitrary")),
        name="front",
    )(*front_in)

    tmc = _pick_tile(lp, 832, 32)
    ntc = lp // tmc
    halo, lead = 32, BF16_ROWS
    conv_w8 = jnp.repeat(conv_dw_w[0].astype(F32), SUBLANES, axis=0)
    conv_in = [u_glu, gates, conv_w8, row2(conv_dw_b[0]), row2(conv_ln_g[0]),
               row2(conv_ln_b[0]), w_conv_out[0].astype(BF16), row2(b_conv_out[0])]
    ybg = pl.pallas_call(
        functools.partial(_conv_kernel, tm=tmc, kw=kw, halo=halo, lead=lead, rb=32),
        grid=(bsz, ntc),
        in_specs=[rspec(conv_ch, tmc, ntc),
                  pl.BlockSpec((tmc, d), lambda b, j: (b * ntc + j, 1))] + [full(a) for a in conv_in[2:]],
        out_specs=rspec(d, tmc, ntc),
        out_shape=jax.ShapeDtypeStruct((rows, d), BF16),
        scratch_shapes=[pltpu.VMEM((lead + halo + tmc, conv_ch), BF16),
                        pltpu.VMEM((SUBLANES, lead + halo + tmc, conv_ch), F32),
                        pltpu.VMEM((tmc, conv_ch), F32)],
        compiler_params=_params(("arbitrary", "arbitrary")),
        name="convmod",
    )(*conv_in)

    nc = lp // CHUNK
    nbd = max(n for n in (1, 2, 4, 8) if bsz % n == 0)
    cspec = lambda cols: pl.BlockSpec((nbd, CHUNK, cols), lambda b, c: (b, c, 0))
    delta_in = [qkv_pre.reshape(bsz, lp, o_a), gbeta.reshape(bsz, lp, LANES), go_act.reshape(bsz, lp, vw),
                conv_qkv_w[0].astype(F32), row2(dn_norm_g[0])]
    o_dn = pl.pallas_call(
        functools.partial(_delta_kernel, nb=nbd, heads=heads, dk=dk, dv=dv, ch=CHUNK, ksz=ksz, halo=BF16_ROWS),
        grid=(bsz // nbd, nc),
        in_specs=[cspec(o_a), cspec(LANES), cspec(vw), full(delta_in[3]), full(delta_in[4])],
        out_specs=cspec(vw),
        out_shape=jax.ShapeDtypeStruct((bsz, lp, vw), BF16),
        scratch_shapes=[pltpu.VMEM((nbd, heads, dk, dv), F32),
                        pltpu.VMEM((nbd, CHUNK + BF16_ROWS, o_a), BF16)],
        compiler_params=_params(("arbitrary", "arbitrary")),
        name="delta",
    )(*delta_in).reshape(rows, vw)

    tm3 = _pick_tile(lp, MID_TILE, BF16_ROWS)
    nt3 = lp // tm3
    w_r = jnp.pad(w_router[0], ((0, 0), (0, LANES - n_exp))).astype(BF16)
    mid_in = [hp, o_dn, gates, ybg, w_dn_out[0].astype(BF16), w_out[0].astype(BF16),
              row2(norm_ffn_g[0]), w_r, _pad_lanes(b_router[0])]
    h2, hn2, topi, topg = pl.pallas_call(
        functools.partial(_mid_kernel, tm=tm3, n_exp=n_exp, parts=MID_PARTS if tm3 % (MID_PARTS * BF16_ROWS) == 0 else 1),
        grid=(bsz, nt3),
        in_specs=[rspec(d, tm3, nt3), rspec(vw, tm3, nt3), rspec(d, tm3, nt3), rspec(d, tm3, nt3)]
        + [full(a) for a in mid_in[4:]],
        out_specs=[rspec(d, tm3, nt3), rspec(d, tm3, nt3), rspec(LANES, tm3, nt3), rspec(LANES, tm3, nt3)],
        out_shape=[jax.ShapeDtypeStruct((rows, d), F32), jax.ShapeDtypeStruct((rows, d), F32),
                   jax.ShapeDtypeStruct((rows, LANES), I32), jax.ShapeDtypeStruct((rows, LANES), F32)],
        compiler_params=_params(("arbitrary", "arbitrary")),
        name="mid",
    )(*mid_in)

    tmd = _pick_tile(seq, TOKEN_TILE, SUBLANES)
    tpb = seq // tmd
    ntd = bsz * tpb
    tokens = bsz * seq
    width = tmd * TOP_K + n_exp * SUBLANES
    row0 = lambda i: pl.multiple_of((i // tpb) * lp + n_head + (i % tpb) * tmd, SUBLANES)
    prow = lambda r, cols, f=row0: pl.BlockSpec((pl.Element(r), pl.Element(cols)), lambda i, *_: (f(i), 0))
    tspec = pl.BlockSpec((tmd, LANES), lambda i, *_: (i, 0))
    ptiles = _pick_tile(tpb, 8, 1)
    pspec = pl.BlockSpec((ptiles * tmd, LANES), lambda i: (i, 0))
    tab = pl.BlockSpec((ptiles, 1, LANES), lambda i: (i, 0, 0))
    tab_shape = jax.ShapeDtypeStruct((ntd, 1, LANES), I32)
    pos, pos_t, n8_t, off8_t, carry8_t, tot8 = pl.pallas_call(
        functools.partial(_plan_kernel, tm=tmd, tiles=ptiles, n_exp=n_exp),
        grid=(ntd // ptiles,),
        in_specs=[prow(ptiles * tmd, LANES, lambda i: row0(i * ptiles))],
        out_specs=[pspec, pl.BlockSpec((ptiles, SUBLANES, tmd), lambda i: (i, 0, 0)), tab, tab, tab,
                   pl.BlockSpec((1, LANES), lambda i: (0, 0))],
        out_shape=[jax.ShapeDtypeStruct((tokens, LANES), I32), jax.ShapeDtypeStruct((ntd, SUBLANES, tmd), I32),
                   tab_shape, tab_shape, tab_shape, jax.ShapeDtypeStruct((1, LANES), F32)],
        scratch_shapes=[pltpu.VMEM((1, LANES), F32)],
        compiler_params=_params(("arbitrary",)),
        name="plan",
    )(topi)

    n_blocks = -(-(tokens * TOP_K + n_exp * (SUBLANES - 1) * ntd) // MOE_BLOCK) + n_exp
    cap = n_blocks * MOE_BLOCK
    counts = tot8[0, :n_exp].astype(I32)
    padded = (counts + MOE_BLOCK - 1) // MOE_BLOCK * MOE_BLOCK
    pend = jnp.cumsum(padded).astype(I32)
    pstart = pend - padded
    n_used = (pend[-1] // MOE_BLOCK).astype(I32).reshape(1)
    block_start = jnp.arange(n_blocks, dtype=I32) * MOE_BLOCK
    block_e = jnp.minimum(jnp.sum((pend[None, :] <= block_start[:, None]).astype(I32), axis=1), n_exp - 1)
    dbase = (carry8_t.reshape(ntd, LANES) + jnp.pad(pstart, (0, LANES - n_exp))[None, :]).reshape(ntd * LANES)
    n8_f = n8_t.reshape(ntd * LANES)
    off8_f = off8_t.reshape(ntd * LANES)

    anyspec = pl.BlockSpec(memory_space=pl.ANY)
    stab = lambda f: pl.BlockSpec((LANES,), lambda i, *_: (f(i),), memory_space=pltpu.SMEM)
    cur = lambda i: i
    xs = pl.pallas_call(
        functools.partial(_dispatch_kernel, tm=tmd, width=width, n_exp=n_exp, n_steps=ntd, n_blocks=n_blocks),
        grid_spec=pltpu.PrefetchScalarGridSpec(
            num_scalar_prefetch=2, grid=(ntd,),
            in_specs=[stab(cur), stab(cur), stab(cur), prow(tmd, d),
                      pl.BlockSpec((1, SUBLANES, tmd), lambda i, *_: (i, 0, 0))],
            out_specs=anyspec,
            scratch_shapes=[pltpu.VMEM((MOE_BLOCK, d), F32), pltpu.VMEM((2, width, d), F32),
                            pltpu.SMEM((2,), I32), pltpu.SemaphoreType.DMA(()), pltpu.SemaphoreType.DMA((2,))]),
        out_shape=jax.ShapeDtypeStruct((cap, d), F32),
        compiler_params=_params(("arbitrary",)),
        name="dispatch",
    )(pend, padded, n8_f, off8_f, dbase, hn2, pos_t)

    last_used = lambda i, nbu: jnp.minimum(i, nbu[0] - 1)
    ew = lambda a: pl.BlockSpec((1,) + a.shape[1:], lambda i, be, nbu: (be[i],) + (0,) * (a.ndim - 1))
    e_in = [w_gate_up[0], b_gate_up[0].astype(F32)[:, None, :], w_down[0], b_down[0].astype(F32)[:, None, :]]
    ys = pl.pallas_call(
        functools.partial(_expert_kernel, d_ff=d_ff, parts=2),
        grid_spec=pltpu.PrefetchScalarGridSpec(
            num_scalar_prefetch=2, grid=(n_blocks,),
            in_specs=[pl.BlockSpec((MOE_BLOCK, d), lambda i, be, nbu: (last_used(i, nbu), 0))]
            + [ew(a) for a in e_in],
            out_specs=pl.BlockSpec((MOE_BLOCK, d), lambda i, be, nbu: (i, 0)),
            scratch_shapes=[pltpu.VMEM((d, 2 * d_ff), BF16), pltpu.VMEM((d_ff, d), BF16)]),
        out_shape=jax.ShapeDtypeStruct((cap, d), F32),
        compiler_params=_params(("arbitrary",)),
        name="experts",
    )(block_e, n_used, xs, *e_in)

    nxt = lambda i: jnp.minimum(i + 1, ntd - 1)
    out = pl.pallas_call(
        functools.partial(_combine_kernel, tm=tmd, width=width, n_exp=n_exp, n_steps=ntd),
        grid=(ntd,),
        in_specs=[stab(cur), stab(cur), stab(cur), stab(nxt), stab(nxt), stab(nxt), tspec,
                  prow(tmd, d), prow(tmd, LANES), pl.BlockSpec((1, d), lambda i: (0, 0)), anyspec],
        out_specs=pl.BlockSpec((tmd, d), lambda i: (i, 0)),
        out_shape=jax.ShapeDtypeStruct((tokens, d), x.dtype),
        scratch_shapes=[pltpu.VMEM((2, width, d), F32), pltpu.SMEM((2,), I32), pltpu.SemaphoreType.DMA((2,))],
        compiler_params=_params(("arbitrary",)),
        name="combine",
    )(n8_f, off8_f, dbase, n8_f, off8_f, dbase, pos, h2, topg, row2(final_norm_g), ys)
    return out.reshape(bsz, seq, d)
```

```python
import functools

import jax
import jax.numpy as jnp
from jax import lax
from jax.experimental import pallas as pl
from jax.experimental.pallas import tpu as pltpu

F32 = jnp.float32
BF16 = jnp.bfloat16
I32 = jnp.int32

CHUNK = 64
TOP_K = 4
MOE_BLOCK = 512
TOKEN_TILE = 256
MID_TILE = 832
MID_PARTS = 4
SWIGLU_LIMIT = 7.0
SWIGLU_ALPHA = 1.702
EPS = 1e-6
LN_EPS = 1e-5
LANES = 128
SUBLANES = 8
BF16_ROWS = 16
VMEM_LIMIT = 56 * 1024 * 1024


def _dot(a, b):
    return jnp.dot(a, b, preferred_element_type=F32)


def _dot_nt(a, b):
    return lax.dot_general(a, b, (((1,), (1,)), ((), ())), preferred_element_type=F32)


def _dot_tn(a, b):
    return lax.dot_general(a, b, (((0,), (0,)), ((), ())), preferred_element_type=F32)


def _sigmoid(x):
    return 1.0 / (1.0 + jnp.exp(-x))


def _silu(x):
    return x * _sigmoid(x)


def _softplus(x):
    return jnp.maximum(x, 0.0) + jnp.log(1.0 + jnp.exp(-jnp.abs(x)))


def _rms(x, g):
    return x * lax.rsqrt(jnp.mean(x * x, axis=-1, keepdims=True) + EPS) * g


def _iota(shape, dim):
    return lax.broadcasted_iota(I32, shape, dim)


def _pick_tile(n, target, mult):
    best = None
    for t in range(mult, min(n, target) + 1, mult):
        if n % t == 0:
            best = t
    assert best is not None, (n, target, mult)
    return best


def _params(sem, flags=None):
    return pltpu.CompilerParams(dimension_semantics=sem, vmem_limit_bytes=VMEM_LIMIT, flags=flags)


def _front_kernel(head_ref, x_ref, g_ref, wqkv_ref, wab_ref, wgo_ref, wglu_ref, wmg_ref, bglu_ref, bmg_ref,
                  alog_ref, dtb_ref, h_ref, qkv_ref, gb_ref, go_ref, u_ref, gates_ref,
                  *, tm, meta_pad, heads, conv_ch):
    j = pl.program_id(1)
    xb = x_ref[0]
    n_head = head_ref.shape[0]
    h = jnp.where(j == 0, jnp.concatenate([head_ref[...], xb[:tm - n_head]], axis=0), xb)
    h_ref[...] = h
    hn = _rms(h, g_ref[...]).astype(BF16)
    qkv_ref[...] = _dot(hn, wqkv_ref[...]).astype(BF16)
    ab = _dot(hn, wab_ref[...])
    valid = (j * tm + _iota((tm, 1), 0)) >= meta_pad
    lane = _iota((1, LANES), 1)
    g = -jnp.exp(alog_ref[...]) * _softplus(ab + dtb_ref[...])
    gb = jnp.where(lane < heads, g, _sigmoid(ab))
    gb_ref[...] = jnp.where(valid, gb, 0.0)
    go_ref[...] = _silu(_dot(hn, wgo_ref[...])).astype(BF16)
    glu = _dot(hn, wglu_ref[...]) + bglu_ref[...]
    u = glu[:, :conv_ch] * _sigmoid(glu[:, conv_ch:])
    u_ref[...] = jnp.where(valid, u, 0.0).astype(BF16)
    gates_ref[...] = _sigmoid(_dot(hn, wmg_ref[...]) + bmg_ref[...]).astype(BF16)


def _conv_kernel(u_ref, gate_ref, w_ref, b_ref, lng_ref, lnb_ref, wo_ref, bo_ref, y_ref,
                 xcat_ref, xs_ref, cbuf_ref, *, tm, kw, halo, lead, rb):
    j = pl.program_id(1)
    base = lead + halo
    total = base + tm
    n_c = xcat_ref.shape[1]

    @pl.when(j == 0)
    def _():
        xcat_ref[0:base, :] = jnp.zeros((base, n_c), BF16)

    @pl.when(j > 0)
    def _():
        xcat_ref[lead:base, :] = xcat_ref[tm + lead:tm + base, :]

    xcat_ref[base:total, :] = u_ref[...]
    sr = _iota((SUBLANES * rb, lead + rb), 0)
    sc = _iota((SUBLANES * rb, lead + rb), 1)
    shift = (sc == (sr & (rb - 1)) + lead - (sr >> (rb.bit_length() - 1))).astype(BF16)
    bias = b_ref[...]
    for t0 in range(lead, total, rb):
        res = _dot(shift, xcat_ref[t0 - lead:t0 + rb, :])
        for r in range(SUBLANES):
            xs_ref[r, t0:t0 + rb, :] = res[r * rb:(r + 1) * rb, :]
        if t0 < base:
            continue
        i0 = t0
        blk = (i0 - base) // rb
        acc = jnp.zeros((rb, n_c), F32)
        for s in range(kw):
            a, r = divmod(s, SUBLANES)
            wt = w_ref[(kw - 1 - s) * SUBLANES:(kw - s) * SUBLANES, :]
            acc = acc + xs_ref[r, i0 - SUBLANES * a:i0 - SUBLANES * a + rb, :] * jnp.concatenate(
                [wt] * (rb // SUBLANES), axis=0)
        cbuf_ref[blk * rb:(blk + 1) * rb, :] = acc + bias
    c = cbuf_ref[...]
    mu = jnp.mean(c, axis=-1, keepdims=True)
    xc = c - mu
    ln = xc * lax.rsqrt(jnp.mean(xc * xc, axis=-1, keepdims=True) + LN_EPS) * lng_ref[...] + lnb_ref[...]
    yb = _dot(_silu(ln).astype(BF16), wo_ref[...]) + bo_ref[...]
    y_ref[...] = (gate_ref[...].astype(F32) * yb).astype(BF16)


def _bdot(a, b):
    return jnp.dot(a.astype(BF16), b.astype(BF16), preferred_element_type=F32)


def _split3(x):
    x1 = x.astype(BF16)
    r1 = x - x1.astype(F32)
    x2 = r1.astype(BF16)
    x3 = (r1 - x2.astype(F32)).astype(BF16)
    return x1, x2, x3


def _unit_lower_inverses_minus_eye(mats, row, col):
    ch = mats[0].shape[0]
    blk16 = (row >> 4) == (col >> 4)
    blk32 = (row >> 5) == (col >> 5)
    n1 = [jnp.where(blk16, a, 0.0) for a in mats]
    n2 = [_bdot(n, n) for n in n1]
    r = [_bdot(jnp.concatenate([n, m], axis=0), m) for n, m in zip(n1, n2)]
    n4 = [x[ch:] for x in r]
    y = [m - n - x[:ch] for n, m, x in zip(n1, n2, r)]
    r = [_bdot(jnp.concatenate([yy, m], axis=0), m) for yy, m in zip(y, n4)]
    n8 = [x[ch:] for x in r]
    y = [yy + m + x[:ch] for yy, m, x in zip(y, n4, r)]
    y = [yy + m + _bdot(yy, m) for yy, m in zip(y, n8)]
    for mask in (blk32 & jnp.logical_not(blk16), jnp.logical_not(blk32)):
        ls = [jnp.where(mask, a, 0.0) for a in mats]
        ms = [l + _bdot(l, yy) for l, yy in zip(ls, y)]
        y = [yy - m - _bdot(yy, m) for yy, m in zip(y, ms)]
    return y


def _delta_kernel(qkv_ref, gb_ref, go_ref, cw_ref, ng_ref, o_ref, s_ref, xc_ref,
                  *, nb, heads, dk, dv, ch, ksz, halo):
    c = pl.program_id(1)
    qk_w = heads * dk

    @pl.when(c == 0)
    def _():
        s_ref[...] = jnp.zeros(s_ref.shape, F32)
        xc_ref[:, 0:halo, :] = jnp.zeros((nb, halo, xc_ref.shape[2]), BF16)

    @pl.when(c > 0)
    def _():
        xc_ref[:, 0:halo, :] = xc_ref[:, ch:ch + halo, :]

    xc_ref[:, halo:halo + ch, :] = qkv_ref[...]
    row = _iota((ch, ch), 0)
    col = _iota((ch, ch), 1)
    incl = row >= col
    strict = row > col
    tril = incl.astype(BF16)
    triu = (row <= col).astype(BF16)
    sr = _iota(((ksz - 1) * ch, halo + ch), 0)
    sc = _iota(((ksz - 1) * ch, halo + ch), 1)
    shift = (sc == (sr & (ch - 1)) + (sr >> (ch.bit_length() - 1)) + (halo - (ksz - 1))).astype(BF16)

    qn, kn, kb, vb, gamma, eg, ekd, elast = [], [], [], [], [], [], [], []
    shifted = [_dot(shift, xc_ref[b]) for b in range(nb)]
    for b in range(nb):
        acc = xc_ref[b, halo:halo + ch, :].astype(F32) * cw_ref[ksz - 1:ksz, :]
        for t in range(ksz - 1):
            acc = acc + shifted[b][t * ch:(t + 1) * ch, :] * cw_ref[t:t + 1, :]
        qkv = _silu(acc)
        gb = gb_ref[b]
        parts = _split3(gb)
        gc = sum(_dot(tril, p) for p in parts)
        gct = sum(_dot_tn(p, triu) for p in parts)
        glast = gc[ch - 1:ch, :]
        e_g = jnp.exp(gc)
        e_kd = jnp.exp(glast - gc)
        e_last = jnp.exp(glast)
        for h in range(heads):
            q = qkv[:, h * dk:(h + 1) * dk]
            k = qkv[:, qk_w + h * dk:qk_w + (h + 1) * dk]
            v = qkv[:, 2 * qk_w + h * dv:2 * qk_w + (h + 1) * dv]
            qn.append(q * lax.rsqrt(jnp.sum(q * q, axis=-1, keepdims=True) + EPS) * (dk ** -0.5))
            kn.append(k * lax.rsqrt(jnp.sum(k * k, axis=-1, keepdims=True) + EPS))
            beta = gb[:, heads + h:heads + h + 1]
            kb.append(kn[-1] * beta)
            vb.append(v * beta)
            diff = gc[:, h:h + 1] - gct[h:h + 1, :]
            gamma.append(jnp.where(incl, jnp.exp(jnp.where(incl, diff, 0.0)), 0.0))
            eg.append(e_g[:, h:h + 1])
            ekd.append(e_kd[:, h:h + 1])
            elast.append(e_last[:, h:h + 1])

    n = nb * heads
    kq = [_dot_nt(jnp.concatenate([kb[i], qn[i]], axis=0).astype(BF16), kn[i].astype(BF16))
          for i in range(n)]
    a = [jnp.where(strict, kq[i][:ch] * gamma[i], 0.0) for i in range(n)]
    qk = [kq[i][ch:] * gamma[i] for i in range(n)]
    y = _unit_lower_inverses_minus_eye(a, row, col)
    rhs = [jnp.concatenate([vb[i], kb[i] * eg[i]], axis=-1) for i in range(n)]
    uw = [rhs[i] + _bdot(y[i], rhs[i]) for i in range(n)]
    s = [s_ref[i // heads, i % heads] for i in range(n)]
    ws = [_bdot(jnp.concatenate([uw[i][:, dv:], qn[i] * eg[i]], axis=0), s[i]) for i in range(n)]
    v_new = [uw[i][:, :dv] - ws[i][:ch] for i in range(n)]
    o = [ws[i][ch:] + _bdot(qk[i], v_new[i]) for i in range(n)]
    s_new = [s[i] * elast[i] + _dot_tn((kn[i] * ekd[i]).astype(BF16), v_new[i].astype(BF16))
             for i in range(n)]
    for i in range(n):
        b, h = i // heads, i % heads
        s_ref[b, h] = s_new[i]
        on = _rms(o[i], ng_ref[...])
        o_ref[b, :, h * dv:(h + 1) * dv] = (
            on * go_ref[b, :, h * dv:(h + 1) * dv].astype(F32)).astype(BF16)


def _mid_kernel(h_ref, o_ref, ga_ref, ybg_ref, wdn_ref, wout_ref, g2_ref, wr_ref, br_ref,
                h2_ref, hn_ref, topi_ref, topg_ref, *, tm, n_exp, parts):
    rs = tm // parts
    sl = [pl.ds(p * rs, rs) for p in range(parts)]
    ya = [_dot(o_ref[s, :], wdn_ref[...]) for s in sl]
    y = [ga_ref[s, :].astype(F32) * a + ybg_ref[s, :].astype(F32) for s, a in zip(sl, ya)]
    h2 = [h_ref[s, :] + _dot(v.astype(BF16), wout_ref[...]) for s, v in zip(sl, y)]
    hn = [_rms(v, g2_ref[...]) for v in h2]
    for s, v, w in zip(sl, h2, hn):
        h2_ref[s, :] = v
        hn_ref[s, :] = w
    logits = [_dot(v.astype(BF16), wr_ref[...]) + br_ref[...] for v in hn]
    lane = _iota((rs, LANES), 1)
    lane_f = lane.astype(F32)
    l = [jnp.where(lane < n_exp, v, -jnp.inf) for v in logits]
    vals = [[] for _ in range(parts)]
    idxs = [[] for _ in range(parts)]
    for _ in range(TOP_K):
        m = [jnp.max(v, axis=-1, keepdims=True) for v in l]
        idx = [jnp.min(jnp.where(v == mm, lane_f, float(LANES)), axis=-1, keepdims=True)
               for v, mm in zip(l, m)]
        l = [jnp.where(lane_f == ii, -jnp.inf, v) for v, ii in zip(l, idx)]
        for p in range(parts):
            vals[p].append(m[p])
            idxs[p].append(idx[p])
    for p in range(parts):
        es = [jnp.exp(v - vals[p][0]) for v in vals[p]]
        den = es[0]
        for e in es[1:]:
            den = den + e
        topi = jnp.zeros((rs, LANES), F32)
        topg = jnp.zeros((rs, LANES), F32)
        for k in range(TOP_K):
            topi = jnp.where(lane == k, idxs[p][k], topi)
            topg = jnp.where(lane == k, es[k] / den, topg)
        topi_ref[sl[p], :] = topi.astype(I32)
        topg_ref[sl[p], :] = topg


def _choice_mask(topi, lane, n_exp):
    m = jnp.zeros(topi.shape, F32)
    for k in range(TOP_K):
        m = m + (lane == topi[:, k:k + 1]).astype(F32)
    return jnp.where(lane < n_exp, m, 0.0)


def _plan_kernel(topi_ref, pos_ref, post_ref, n8_ref, off8_ref, carry8_ref, tot8_ref, carry_ref,
                 *, tm, tiles, n_exp):
    i = pl.program_id(0)

    @pl.when(i == 0)
    def _():
        carry_ref[...] = jnp.zeros(carry_ref.shape, F32)

    lane = _iota((tm, LANES), 1)
    tri = (_iota((tm, tm), 0) >= _iota((tm, tm), 1)).astype(BF16)
    before = (_iota((LANES, LANES), 0) < _iota((LANES, LANES), 1)).astype(BF16)
    js = range(tiles)
    rows = [pl.ds(j * tm, tm) for j in js]
    topi = [topi_ref[r, :] for r in rows]
    m = [_choice_mask(t, lane, n_exp) for t in topi]
    cum = [_dot(tri, v.astype(BF16)) for v in m]
    n8 = [jnp.floor((c[tm - 1:tm, :] + (SUBLANES - 1)) * (1.0 / SUBLANES)) * SUBLANES for c in cum]
    off8 = [_dot(jnp.broadcast_to(v, (SUBLANES, LANES)).astype(BF16), before)[0:1] for v in n8]
    posf = [c - v + o for c, v, o in zip(cum, m, off8)]
    pos = [jnp.zeros((tm, LANES), F32) for _ in js]
    for k in range(TOP_K):
        pk = [jnp.sum(jnp.where(lane == t[:, k:k + 1], p, 0.0), axis=-1, keepdims=True)
              for t, p in zip(topi, posf)]
        pos = [jnp.where(lane == k, a, b) for a, b in zip(pk, pos)]
    carry = carry_ref[...]
    for j in js:
        pos_ref[rows[j], :] = pos[j].astype(I32)
        post_ref[j] = jnp.transpose(pos[j])[0:SUBLANES, :].astype(I32)
        n8_ref[j] = n8[j].astype(I32)
        off8_ref[j] = off8[j].astype(I32)
        carry8_ref[j] = carry.astype(I32)
        carry = carry + n8[j]
    carry_ref[...] = carry
    tot8_ref[...] = carry


def _pow2_floor(n):
    return 1 << (n.bit_length() - 1)


def _strip_pieces(n8, max_rows):
    pieces = []
    b = max_rows
    while b >= SUBLANES:
        pieces.append(((n8 & b) != 0, n8 & ~(2 * b - 1), b))
        b //= 2
    return pieces


LONG_PIECE = 64


def _copy_strip(n8, start_piece):
    n_long = n8 >> (LONG_PIECE.bit_length() - 1)

    def long_piece(j, carry):
        start_piece(j * LONG_PIECE, LONG_PIECE)
        return carry

    lax.fori_loop(0, n_long, long_piece, 0)
    for cond, ofs, rows in _strip_pieces(n8 & (LONG_PIECE - 1), LONG_PIECE // 2):
        @pl.when(cond)
        def _():
            start_piece(n_long * LONG_PIECE + ofs, rows)


def _wait_rows(total, make_copy, max_rows):
    b = max_rows
    while b >= SUBLANES:
        @pl.when((total & b) != 0)
        def _():
            make_copy(b).wait()
        b //= 2


def _aligned(start, rows):
    return pl.ds(pl.multiple_of(start, SUBLANES), rows)


def _dispatch_kernel(pend_ref, padded_ref, n8_ref, off8_ref, dbase_ref, hn_ref, post_ref, xs_ref,
                     zbuf_ref, sbuf_ref, tot_ref, zsem, sems, *, tm, width, chunk, n_exp, n_steps, n_blocks):
    i = pl.program_id(0)
    slot = i % 2

    def zero_block(start):
        return pltpu.make_async_copy(zbuf_ref, xs_ref.at[_aligned(start, MOE_BLOCK), :], zsem)

    @pl.when(i == 0)
    def _():
        zbuf_ref[...] = jnp.zeros(zbuf_ref.shape, F32)
        for e in range(n_exp):
            @pl.when(padded_ref[e] > 0)
            def _():
                zero_block(pend_ref[e] - MOE_BLOCK).start()
        first_unused = pend_ref[n_exp - 1] // MOE_BLOCK

        def start_tail(j, carry):
            zero_block(j * MOE_BLOCK).start()
            return carry

        lax.fori_loop(first_unused, n_blocks, start_tail, 0)
        for e in range(n_exp):
            @pl.when(padded_ref[e] > 0)
            def _():
                zero_block(0).wait()

        def wait_tail(j, carry):
            zero_block(0).wait()
            return carry

        lax.fori_loop(first_unused, n_blocks, wait_tail, 0)

    def strip_wait(s):
        _wait_rows(tot_ref[s], lambda b: pltpu.make_async_copy(
            sbuf_ref.at[s, pl.ds(0, b), :], xs_ref.at[pl.ds(0, b), :], sems.at[s]), _pow2_floor(width))

    @pl.when(i >= 2)
    def _():
        strip_wait(slot)

    post = post_ref[0]
    x16 = hn_ref[...].astype(BF16)
    for c0 in range(0, width, chunk):
        orow = c0 + _iota((chunk, tm), 0)
        sel = orow == post[0:1, :]
        for k in range(1, TOP_K):
            sel = jnp.logical_or(sel, orow == post[k:k + 1, :])
        sbuf_ref[slot, c0:c0 + chunk, :] = _dot(sel.astype(BF16), x16)

    total = 0
    for e in range(n_exp):
        n8, so, do = n8_ref[e], off8_ref[e], dbase_ref[e]
        total = total + n8
        _copy_strip(n8, lambda ofs, rows, so=so, do=do: pltpu.make_async_copy(
            sbuf_ref.at[slot, _aligned(so + ofs, rows), :], xs_ref.at[_aligned(do + ofs, rows), :],
            sems.at[slot]).start())
    tot_ref[slot] = total

    @pl.when(i == n_steps - 1)
    def _():
        if n_steps >= 2:
            strip_wait(1 - slot)
        strip_wait(slot)


def _expert_kernel(be_ref, nbu_ref, x_ref, wgu_ref, bgu_ref, wd_ref, bd_ref, y_ref, wgu16_ref, wd16_ref,
                   *, d_ff, parts):
    i = pl.program_id(0)
    used = i < nbu_ref[0]
    new_expert = jnp.logical_or(i == 0, be_ref[i] != be_ref[jnp.maximum(i - 1, 0)])

    @pl.when(jnp.logical_and(used, new_expert))
    def _():
        wgu16_ref[...] = wgu_ref[0].astype(BF16)
        wd16_ref[...] = wd_ref[0].astype(BF16)

    @pl.when(used)
    def _():
        rs = x_ref.shape[0] // parts
        sl = [pl.ds(p * rs, rs) for p in range(parts)]
        gu = [_dot(x_ref[s, :].astype(BF16), wgu16_ref[...]) + bgu_ref[0] for s in sl]
        act = []
        for v in gu:
            gate = jnp.minimum(v[:, :d_ff], SWIGLU_LIMIT)
            up = jnp.clip(v[:, d_ff:], -SWIGLU_LIMIT, SWIGLU_LIMIT)
            act.append(((up + 1.0) * gate * _sigmoid(SWIGLU_ALPHA * gate)).astype(BF16))
        for s, a in zip(sl, act):
            y_ref[s, :] = _dot(a, wd16_ref[...]) + bd_ref[0]

    @pl.when(jnp.logical_not(used))
    def _():
        y_ref[...] = jnp.zeros(y_ref.shape, F32)


def _combine_kernel(n8c_ref, off8c_ref, dbc_ref, n8n_ref, off8n_ref, dbn_ref, pos_ref, h2_ref, topg_ref, fg_ref,
                    ys_ref, out_ref, gbuf_ref, tot_ref, sems, *, tm, width, chunk, n_exp, n_steps):
    i = pl.program_id(0)
    slot = i % 2

    def fetch(n8_ref, off8_ref, db_ref, s):
        total = 0
        for e in range(n_exp):
            n8, so, do = n8_ref[e], off8_ref[e], db_ref[e]
            total = total + n8
            _copy_strip(n8, lambda ofs, rows, so=so, do=do: pltpu.make_async_copy(
                ys_ref.at[_aligned(do + ofs, rows), :], gbuf_ref.at[s, _aligned(so + ofs, rows), :],
                sems.at[s]).start())
        tot_ref[s] = total

    @pl.when(i == 0)
    def _():
        gbuf_ref[...] = jnp.zeros(gbuf_ref.shape, F32)
        fetch(n8c_ref, off8c_ref, dbc_ref, 0)

    @pl.when(i + 1 < n_steps)
    def _():
        fetch(n8n_ref, off8n_ref, dbn_ref, 1 - slot)

    _wait_rows(tot_ref[slot], lambda b: pltpu.make_async_copy(
        ys_ref.at[pl.ds(0, b), :], gbuf_ref.at[slot, pl.ds(0, b), :], sems.at[slot]), _pow2_floor(width))
    pos = pos_ref[...]
    topg = topg_ref[...]
    acc = h2_ref[...]
    for c0 in range(0, width, chunk):
        lane = c0 + _iota((tm, chunk), 1)
        g = jnp.zeros((tm, chunk), F32)
        for k in range(TOP_K):
            g = g + jnp.where(lane == pos[:, k:k + 1], topg[:, k:k + 1], 0.0)
        acc = acc + _dot(g.astype(BF16), gbuf_ref[slot, c0:c0 + chunk, :].astype(BF16))
    out_ref[...] = _rms(acc, fg_ref[...])


def _pad_lanes(v, fill=0.0):
    v = v.astype(F32)
    return jnp.concatenate([v, jnp.full((LANES - v.shape[0],), fill, F32)])[None, :]


def kernel(x, meta_tokens, norm_mix_g, w_in, conv_qkv_w, a_log, dt_bias, dn_norm_g, w_dn_out, b_glu,
           conv_dw_w, conv_dw_b, conv_ln_g, conv_ln_b, w_conv_out, b_conv_out, w_merge, b_merge, w_out,
           norm_ffn_g, w_router, b_router, w_gate_up, b_gate_up, w_down, b_down, final_norm_g):
    bsz, seq, d = x.shape
    depth = w_in.shape[0]
    n_meta = meta_tokens.shape[0]
    heads = a_log.shape[1]
    vw = w_dn_out.shape[1]
    qk_w = (conv_qkv_w.shape[2] - vw) // 2
    dk, dv = qk_w // heads, vw // heads
    ksz = conv_qkv_w.shape[1]
    kw, conv_ch = conv_dw_w.shape[1], conv_dw_w.shape[2]
    n_exp, d_ff = w_gate_up.shape[1], w_down.shape[2]
    meta_pad = CHUNK - n_meta
    lp = seq + n_meta + meta_pad
    assert depth == 1 and lp % CHUNK == 0 and dk == LANES and dv == LANES and d % LANES == 0
    assert 2 * heads <= LANES and n_exp <= LANES and kw - 1 <= 32 and ksz - 1 <= SUBLANES
    rows = bsz * lp
    n_head = meta_pad + n_meta
    head = jnp.concatenate([jnp.zeros((meta_pad, d), x.dtype), meta_tokens.astype(x.dtype)], axis=0)

    wi = w_in[0]
    o_a = 2 * qk_w + vw
    o_go = o_a + 2 * heads
    o_glu = o_go + vw
    w_qkv = wi[:, :o_a].astype(BF16)
    w_ab = jnp.pad(wi[:, o_a:o_go], ((0, 0), (0, LANES - 2 * heads))).astype(BF16)
    w_go = wi[:, o_go:o_glu].astype(BF16)
    w_glu = wi[:, o_glu:].astype(BF16)
    row2 = lambda v: v.astype(F32).reshape(1, -1)

    tm1 = _pick_tile(lp, min(320, seq), BF16_ROWS)
    nt1 = lp // tm1
    assert n_head % SUBLANES == 0 and n_head <= tm1 <= seq
    rspec = lambda cols, tm, nt: pl.BlockSpec((tm, cols), lambda b, j: (b * nt + j, 0))
    full = lambda a: pl.BlockSpec(a.shape, lambda *_: (0,) * a.ndim)
    x_rows = pl.BlockSpec((pl.Element(1), pl.Element(tm1), pl.Element(d)),
                          lambda b, j: (b, pl.multiple_of(jnp.maximum(j * tm1 - n_head, 0), SUBLANES), 0))
    front_in = [head, x, row2(norm_mix_g[0]), w_qkv, w_ab, w_go, w_glu, w_merge[0].astype(BF16),
                row2(b_glu[0]), row2(b_merge[0]), _pad_lanes(a_log[0]), _pad_lanes(dt_bias[0])]
    hp, qkv_pre, gbeta, go_act, u_glu, gates = pl.pallas_call(
        functools.partial(_front_kernel, tm=tm1, meta_pad=meta_pad, heads=heads, conv_ch=conv_ch),
        grid=(bsz, nt1),
        in_specs=[full(head), x_rows] + [full(a) for a in front_in[2:]],
        out_specs=[rspec(d, tm1, nt1), rspec(o_a, tm1, nt1), rspec(LANES, tm1, nt1), rspec(vw, tm1, nt1),
                   rspec(conv_ch, tm1, nt1), rspec(2 * d, tm1, nt1)],
        out_shape=[jax.ShapeDtypeStruct((rows, d), F32),
                   jax.ShapeDtypeStruct((rows, o_a), BF16), jax.ShapeDtypeStruct((rows, LANES), F32),
                   jax.ShapeDtypeStruct((rows, vw), BF16), jax.ShapeDtypeStruct((rows, conv_ch), BF16),
                   jax.ShapeDtypeStruct((rows, 2 * d), BF16)],
        compiler_params=_params(("arbitrary", "arbitrary")),
        name="front",
    )(*front_in)

    tmc = _pick_tile(lp, 832, 32)
    ntc = lp // tmc
    halo, lead = 32, BF16_ROWS
    conv_w8 = jnp.repeat(conv_dw_w[0].astype(F32), SUBLANES, axis=0)
    conv_in = [u_glu, gates, conv_w8, row2(conv_dw_b[0]), row2(conv_ln_g[0]),
               row2(conv_ln_b[0]), w_conv_out[0].astype(BF16), row2(b_conv_out[0])]
    ybg = pl.pallas_call(
        functools.partial(_conv_kernel, tm=tmc, kw=kw, halo=halo, lead=lead, rb=32),
        grid=(bsz, ntc),
        in_specs=[rspec(conv_ch, tmc, ntc),
                  pl.BlockSpec((tmc, d), lambda b, j: (b * ntc + j, 1))] + [full(a) for a in conv_in[2:]],
        out_specs=rspec(d, tmc, ntc),
        out_shape=jax.ShapeDtypeStruct((rows, d), BF16),
        scratch_shapes=[pltpu.VMEM((lead + halo + tmc, conv_ch), BF16),
                        pltpu.VMEM((SUBLANES, lead + halo + tmc, conv_ch), F32),
                        pltpu.VMEM((tmc, conv_ch), F32)],
        compiler_params=_params(("arbitrary", "arbitrary")),
        name="convmod",
    )(*conv_in)

    nc = lp // CHUNK
    nbd = max(n for n in (1, 2, 4, 8) if bsz % n == 0)
    cspec = lambda cols: pl.BlockSpec((nbd, CHUNK, cols), lambda b, c: (b, c, 0))
    delta_in = [qkv_pre.reshape(bsz, lp, o_a), gbeta.reshape(bsz, lp, LANES), go_act.reshape(bsz, lp, vw),
                conv_qkv_w[0].astype(F32), row2(dn_norm_g[0])]
    o_dn = pl.pallas_call(
        functools.partial(_delta_kernel, nb=nbd, heads=heads, dk=dk, dv=dv, ch=CHUNK, ksz=ksz, halo=BF16_ROWS),
        grid=(bsz // nbd, nc),
        in_specs=[cspec(o_a), cspec(LANES), cspec(vw), full(delta_in[3]), full(delta_in[4])],
        out_specs=cspec(vw),
        out_shape=jax.ShapeDtypeStruct((bsz, lp, vw), BF16),
        scratch_shapes=[pltpu.VMEM((nbd, heads, dk, dv), F32),
                        pltpu.VMEM((nbd, CHUNK + BF16_ROWS, o_a), BF16)],
        compiler_params=_params(("arbitrary", "arbitrary")),
        name="delta",
    )(*delta_in).reshape(rows, vw)

    tm3 = _pick_tile(lp, MID_TILE, BF16_ROWS)
    nt3 = lp // tm3
    w_r = jnp.pad(w_router[0], ((0, 0), (0, LANES - n_exp))).astype(BF16)
    mid_in = [hp, o_dn, gates, ybg, w_dn_out[0].astype(BF16), w_out[0].astype(BF16),
              row2(norm_ffn_g[0]), w_r, _pad_lanes(b_router[0])]
    h2, hn2, topi, topg = pl.pallas_call(
        functools.partial(_mid_kernel, tm=tm3, n_exp=n_exp, parts=MID_PARTS if tm3 % (MID_PARTS * BF16_ROWS) == 0 else 1),
        grid=(bsz, nt3),
        in_specs=[rspec(d, tm3, nt3), rspec(vw, tm3, nt3), rspec(d, tm3, nt3), rspec(d, tm3, nt3)]
        + [full(a) for a in mid_in[4:]],
        out_specs=[rspec(d, tm3, nt3), rspec(d, tm3, nt3), rspec(LANES, tm3, nt3), rspec(LANES, tm3, nt3)],
        out_shape=[jax.ShapeDtypeStruct((rows, d), F32), jax.ShapeDtypeStruct((rows, d), F32),
                   jax.ShapeDtypeStruct((rows, LANES), I32), jax.ShapeDtypeStruct((rows, LANES), F32)],
        compiler_params=_params(("arbitrary", "arbitrary")),
        name="mid",
    )(*mid_in)

    tmd = _pick_tile(seq, TOKEN_TILE, SUBLANES)
    tpb = seq // tmd
    ntd = bsz * tpb
    tokens = bsz * seq
    width = tmd * TOP_K + n_exp * SUBLANES
    ochunk = _pick_tile(width, 256, BF16_ROWS)
    row0 = lambda i: pl.multiple_of((i // tpb) * lp + n_head + (i % tpb) * tmd, SUBLANES)
    prow = lambda r, cols, f=row0: pl.BlockSpec((pl.Element(r), pl.Element(cols)), lambda i, *_: (f(i), 0))
    tspec = pl.BlockSpec((tmd, LANES), lambda i, *_: (i, 0))
    ptiles = _pick_tile(tpb, 8, 1)
    pspec = pl.BlockSpec((ptiles * tmd, LANES), lambda i: (i, 0))
    tab = pl.BlockSpec((ptiles, 1, LANES), lambda i: (i, 0, 0))
    tab_shape = jax.ShapeDtypeStruct((ntd, 1, LANES), I32)
    pos, pos_t, n8_t, off8_t, carry8_t, tot8 = pl.pallas_call(
        functools.partial(_plan_kernel, tm=tmd, tiles=ptiles, n_exp=n_exp),
        grid=(ntd // ptiles,),
        in_specs=[prow(ptiles * tmd, LANES, lambda i: row0(i * ptiles))],
        out_specs=[pspec, pl.BlockSpec((ptiles, SUBLANES, tmd), lambda i: (i, 0, 0)), tab, tab, tab,
                   pl.BlockSpec((1, LANES), lambda i: (0, 0))],
        out_shape=[jax.ShapeDtypeStruct((tokens, LANES), I32), jax.ShapeDtypeStruct((ntd, SUBLANES, tmd), I32),
                   tab_shape, tab_shape, tab_shape, jax.ShapeDtypeStruct((1, LANES), F32)],
        scratch_shapes=[pltpu.VMEM((1, LANES), F32)],
        compiler_params=_params(("arbitrary",)),
        name="plan",
    )(topi)

    n_blocks = -(-(tokens * TOP_K + n_exp * (SUBLANES - 1) * ntd) // MOE_BLOCK) + n_exp
    cap = n_blocks * MOE_BLOCK
    counts = tot8[0, :n_exp].astype(I32)
    padded = (counts + MOE_BLOCK - 1) // MOE_BLOCK * MOE_BLOCK
    pend = jnp.cumsum(padded).astype(I32)
    pstart = pend - padded
    n_used = (pend[-1] // MOE_BLOCK).astype(I32).reshape(1)
    block_start = jnp.arange(n_blocks, dtype=I32) * MOE_BLOCK
    block_e = jnp.minimum(jnp.sum((pend[None, :] <= block_start[:, None]).astype(I32), axis=1), n_exp - 1)
    dbase = (carry8_t.reshape(ntd, LANES) + jnp.pad(pstart, (0, LANES - n_exp))[None, :]).reshape(ntd * LANES)
    n8_f = n8_t.reshape(ntd * LANES)
    off8_f = off8_t.reshape(ntd * LANES)

    anyspec = pl.BlockSpec(memory_space=pl.ANY)
    stab = lambda f: pl.BlockSpec((LANES,), lambda i, *_: (f(i),), memory_space=pltpu.SMEM)
    cur = lambda i: i
    xs = pl.pallas_call(
        functools.partial(_dispatch_kernel, tm=tmd, width=width, chunk=ochunk, n_exp=n_exp, n_steps=ntd,
                          n_blocks=n_blocks),
        grid_spec=pltpu.PrefetchScalarGridSpec(
            num_scalar_prefetch=2, grid=(ntd,),
            in_specs=[stab(cur), stab(cur), stab(cur), prow(tmd, d),
                      pl.BlockSpec((1, SUBLANES, tmd), lambda i, *_: (i, 0, 0))],
            out_specs=anyspec,
            scratch_shapes=[pltpu.VMEM((MOE_BLOCK, d), F32), pltpu.VMEM((2, width, d), F32),
                            pltpu.SMEM((2,), I32), pltpu.SemaphoreType.DMA(()), pltpu.SemaphoreType.DMA((2,))]),
        out_shape=jax.ShapeDtypeStruct((cap, d), F32),
        compiler_params=_params(("arbitrary",)),
        name="dispatch",
    )(pend, padded, n8_f, off8_f, dbase, hn2, pos_t)

    last_used = lambda i, nbu: jnp.minimum(i, nbu[0] - 1)
    ew = lambda a: pl.BlockSpec((1,) + a.shape[1:], lambda i, be, nbu: (be[i],) + (0,) * (a.ndim - 1))
    e_in = [w_gate_up[0], b_gate_up[0].astype(F32)[:, None, :], w_down[0], b_down[0].astype(F32)[:, None, :]]
    ys = pl.pallas_call(
        functools.partial(_expert_kernel, d_ff=d_ff, parts=2),
        grid_spec=pltpu.PrefetchScalarGridSpec(
            num_scalar_prefetch=2, grid=(n_blocks,),
            in_specs=[pl.BlockSpec((MOE_BLOCK, d), lambda i, be, nbu: (last_used(i, nbu), 0))]
            + [ew(a) for a in e_in],
            out_specs=pl.BlockSpec((MOE_BLOCK, d), lambda i, be, nbu: (i, 0)),
            scratch_shapes=[pltpu.VMEM((d, 2 * d_ff), BF16), pltpu.VMEM((d_ff, d), BF16)]),
        out_shape=jax.ShapeDtypeStruct((cap, d), F32),
        compiler_params=_params(("arbitrary",)),
        name="experts",
    )(block_e, n_used, xs, *e_in)

    nxt = lambda i: jnp.minimum(i + 1, ntd - 1)
    out = pl.pallas_call(
        functools.partial(_combine_kernel, tm=tmd, width=width, chunk=ochunk, n_exp=n_exp, n_steps=ntd),
        grid=(ntd,),
        in_specs=[stab(cur), stab(cur), stab(cur), stab(nxt), stab(nxt), stab(nxt), tspec,
                  prow(tmd, d), prow(tmd, LANES), pl.BlockSpec((1, d), lambda i: (0, 0)), anyspec],
        out_specs=pl.BlockSpec((tmd, d), lambda i: (i, 0)),
        out_shape=jax.ShapeDtypeStruct((tokens, d), x.dtype),
        scratch_shapes=[pltpu.VMEM((2, width, d), F32), pltpu.SMEM((2,), I32), pltpu.SemaphoreType.DMA((2,))],
        compiler_params=_params(("arbitrary",)),
        name="combine",
    )(n8_f, off8_f, dbase, n8_f, off8_f, dbase, pos, h2, topg, row2(final_norm_g), ys)
    return out.reshape(bsz, seq, d)
```

```python
import functools

import jax
import jax.numpy as jnp
from jax import lax
from jax.experimental import pallas as pl
from jax.experimental.pallas import tpu as pltpu

F32 = jnp.float32
BF16 = jnp.bfloat16
I32 = jnp.int32

CHUNK = 64
TOP_K = 4
MOE_BLOCK = 512
TOKEN_TILE = 256
MID_TILE = 832
MID_PARTS = 4
SWIGLU_LIMIT = 7.0
SWIGLU_ALPHA = 1.702
EPS = 1e-6
LN_EPS = 1e-5
LANES = 128
SUBLANES = 8
BF16_ROWS = 16
VMEM_LIMIT = 56 * 1024 * 1024


def _dot(a, b):
    return jnp.dot(a, b, preferred_element_type=F32)


def _dot_nt(a, b):
    return lax.dot_general(a, b, (((1,), (1,)), ((), ())), preferred_element_type=F32)


def _dot_tn(a, b):
    return lax.dot_general(a, b, (((0,), (0,)), ((), ())), preferred_element_type=F32)


def _sigmoid(x):
    return 1.0 / (1.0 + jnp.exp(-x))


def _silu(x):
    return x * _sigmoid(x)


def _softplus(x):
    return jnp.maximum(x, 0.0) + jnp.log(1.0 + jnp.exp(-jnp.abs(x)))


def _rms(x, g):
    return x * lax.rsqrt(jnp.mean(x * x, axis=-1, keepdims=True) + EPS) * g


def _iota(shape, dim):
    return lax.broadcasted_iota(I32, shape, dim)


def _pick_tile(n, target, mult):
    best = None
    for t in range(mult, min(n, target) + 1, mult):
        if n % t == 0:
            best = t
    assert best is not None, (n, target, mult)
    return best


def _params(sem, flags=None):
    return pltpu.CompilerParams(dimension_semantics=sem, vmem_limit_bytes=VMEM_LIMIT, flags=flags)


def _front_kernel(head_ref, x_ref, g_ref, wqkv_ref, wab_ref, wgo_ref, wglu_ref, wmg_ref, bglu_ref, bmg_ref,
                  alog_ref, dtb_ref, h_ref, qkv_ref, gb_ref, go_ref, u_ref, gates_ref,
                  *, tm, meta_pad, heads, conv_ch):
    j = pl.program_id(1)
    xb = x_ref[0]
    n_head = head_ref.shape[0]
    h = jnp.where(j == 0, jnp.concatenate([head_ref[...], xb[:tm - n_head]], axis=0), xb)
    h_ref[...] = h
    hn = _rms(h, g_ref[...]).astype(BF16)
    qkv_ref[...] = _dot(hn, wqkv_ref[...]).astype(BF16)
    ab = _dot(hn, wab_ref[...])
    valid = (j * tm + _iota((tm, 1), 0)) >= meta_pad
    lane = _iota((1, LANES), 1)
    g = -jnp.exp(alog_ref[...]) * _softplus(ab + dtb_ref[...])
    gb = jnp.where(lane < heads, g, _sigmoid(ab))
    gb_ref[...] = jnp.where(valid, gb, 0.0)
    go_ref[...] = _silu(_dot(hn, wgo_ref[...])).astype(BF16)
    glu = _dot(hn, wglu_ref[...]) + bglu_ref[...]
    u = glu[:, :conv_ch] * _sigmoid(glu[:, conv_ch:])
    u_ref[...] = jnp.where(valid, u, 0.0).astype(BF16)
    gates_ref[...] = _sigmoid(_dot(hn, wmg_ref[...]) + bmg_ref[...]).astype(BF16)


def _conv_kernel(u_ref, gate_ref, w_ref, b_ref, lng_ref, lnb_ref, wo_ref, bo_ref, y_ref,
                 xcat_ref, xs_ref, cbuf_ref, *, tm, kw, halo, lead, rb):
    j = pl.program_id(1)
    base = lead + halo
    total = base + tm
    n_c = xcat_ref.shape[1]

    @pl.when(j == 0)
    def _():
        xcat_ref[0:base, :] = jnp.zeros((base, n_c), BF16)

    @pl.when(j > 0)
    def _():
        xcat_ref[lead:base, :] = xcat_ref[tm + lead:tm + base, :]

    xcat_ref[base:total, :] = u_ref[...]
    sr = _iota((SUBLANES * rb, lead + rb), 0)
    sc = _iota((SUBLANES * rb, lead + rb), 1)
    shift = (sc == (sr & (rb - 1)) + lead - (sr >> (rb.bit_length() - 1))).astype(BF16)
    bias = b_ref[...]
    for t0 in range(lead, total, rb):
        res = _dot(shift, xcat_ref[t0 - lead:t0 + rb, :])
        for r in range(SUBLANES):
            xs_ref[r, t0:t0 + rb, :] = res[r * rb:(r + 1) * rb, :]
        if t0 < base:
            continue
        i0 = t0
        blk = (i0 - base) // rb
        acc = jnp.zeros((rb, n_c), F32)
        for s in range(kw):
            a, r = divmod(s, SUBLANES)
            wt = w_ref[(kw - 1 - s) * SUBLANES:(kw - s) * SUBLANES, :]
            acc = acc + xs_ref[r, i0 - SUBLANES * a:i0 - SUBLANES * a + rb, :] * jnp.concatenate(
                [wt] * (rb // SUBLANES), axis=0)
        cbuf_ref[blk * rb:(blk + 1) * rb, :] = acc + bias
    c = cbuf_ref[...]
    mu = jnp.mean(c, axis=-1, keepdims=True)
    xc = c - mu
    ln = xc * lax.rsqrt(jnp.mean(xc * xc, axis=-1, keepdims=True) + LN_EPS) * lng_ref[...] + lnb_ref[...]
    yb = _dot(_silu(ln).astype(BF16), wo_ref[...]) + bo_ref[...]
    y_ref[...] = (gate_ref[...].astype(F32) * yb).astype(BF16)


def _bdot(a, b):
    return jnp.dot(a.astype(BF16), b.astype(BF16), preferred_element_type=F32)


def _split3(x):
    x1 = x.astype(BF16)
    r1 = x - x1.astype(F32)
    x2 = r1.astype(BF16)
    x3 = (r1 - x2.astype(F32)).astype(BF16)
    return x1, x2, x3


def _unit_lower_inverses_minus_eye(mats, row, col):
    ch = mats[0].shape[0]
    blk16 = (row >> 4) == (col >> 4)
    blk32 = (row >> 5) == (col >> 5)
    n1 = [jnp.where(blk16, a, 0.0) for a in mats]
    n2 = [_bdot(n, n) for n in n1]
    r = [_bdot(jnp.concatenate([n, m], axis=0), m) for n, m in zip(n1, n2)]
    n4 = [x[ch:] for x in r]
    y = [m - n - x[:ch] for n, m, x in zip(n1, n2, r)]
    r = [_bdot(jnp.concatenate([yy, m], axis=0), m) for yy, m in zip(y, n4)]
    n8 = [x[ch:] for x in r]
    y = [yy + m + x[:ch] for yy, m, x in zip(y, n4, r)]
    y = [yy + m + _bdot(yy, m) for yy, m in zip(y, n8)]
    for mask in (blk32 & jnp.logical_not(blk16), jnp.logical_not(blk32)):
        ls = [jnp.where(mask, a, 0.0) for a in mats]
        ms = [l + _bdot(l, yy) for l, yy in zip(ls, y)]
        y = [yy - m - _bdot(yy, m) for yy, m in zip(y, ms)]
    return y


def _delta_kernel(qkv_ref, gb_ref, go_ref, cw_ref, ng_ref, o_ref, s_ref, xc_ref,
                  *, nb, heads, dk, dv, ch, ksz, halo):
    c = pl.program_id(1)
    qk_w = heads * dk

    @pl.when(c == 0)
    def _():
        s_ref[...] = jnp.zeros(s_ref.shape, F32)
        xc_ref[:, 0:halo, :] = jnp.zeros((nb, halo, xc_ref.shape[2]), BF16)

    @pl.when(c > 0)
    def _():
        xc_ref[:, 0:halo, :] = xc_ref[:, ch:ch + halo, :]

    xc_ref[:, halo:halo + ch, :] = qkv_ref[...]
    row = _iota((ch, ch), 0)
    col = _iota((ch, ch), 1)
    incl = row >= col
    strict = row > col
    tril = incl.astype(BF16)
    triu = (row <= col).astype(BF16)
    sr = _iota(((ksz - 1) * ch, halo + ch), 0)
    sc = _iota(((ksz - 1) * ch, halo + ch), 1)
    shift = (sc == (sr & (ch - 1)) + (sr >> (ch.bit_length() - 1)) + (halo - (ksz - 1))).astype(BF16)

    qn, kn, kb, vb, gamma, eg, ekd, elast = [], [], [], [], [], [], [], []
    shifted = [_dot(shift, xc_ref[b]) for b in range(nb)]
    for b in range(nb):
        acc = xc_ref[b, halo:halo + ch, :].astype(F32) * cw_ref[ksz - 1:ksz, :]
        for t in range(ksz - 1):
            acc = acc + shifted[b][t * ch:(t + 1) * ch, :] * cw_ref[t:t + 1, :]
        qkv = _silu(acc)
        gb = gb_ref[b]
        parts = _split3(gb)
        gc = sum(_dot(tril, p) for p in parts)
        gct = sum(_dot_tn(p, triu) for p in parts)
        glast = gc[ch - 1:ch, :]
        e_g = jnp.exp(gc)
        e_kd = jnp.exp(glast - gc)
        e_last = jnp.exp(glast)
        for h in range(heads):
            q = qkv[:, h * dk:(h + 1) * dk]
            k = qkv[:, qk_w + h * dk:qk_w + (h + 1) * dk]
            v = qkv[:, 2 * qk_w + h * dv:2 * qk_w + (h + 1) * dv]
            qn.append(q * lax.rsqrt(jnp.sum(q * q, axis=-1, keepdims=True) + EPS) * (dk ** -0.5))
            kn.append(k * lax.rsqrt(jnp.sum(k * k, axis=-1, keepdims=True) + EPS))
            beta = gb[:, heads + h:heads + h + 1]
            kb.append(kn[-1] * beta)
            vb.append(v * beta)
            diff = gc[:, h:h + 1] - gct[h:h + 1, :]
            gamma.append(jnp.where(incl, jnp.exp(jnp.where(incl, diff, 0.0)), 0.0))
            eg.append(e_g[:, h:h + 1])
            ekd.append(e_kd[:, h:h + 1])
            elast.append(e_last[:, h:h + 1])

    n = nb * heads
    kq = [_dot_nt(jnp.concatenate([kb[i], qn[i]], axis=0).astype(BF16), kn[i].astype(BF16))
          for i in range(n)]
    a = [jnp.where(strict, kq[i][:ch] * gamma[i], 0.0) for i in range(n)]
    qk = [kq[i][ch:] * gamma[i] for i in range(n)]
    y = _unit_lower_inverses_minus_eye(a, row, col)
    rhs = [jnp.concatenate([vb[i], kb[i] * eg[i]], axis=-1) for i in range(n)]
    uw = [rhs[i] + _bdot(y[i], rhs[i]) for i in range(n)]
    s = [s_ref[i // heads, i % heads] for i in range(n)]
    ws = [_bdot(jnp.concatenate([uw[i][:, dv:], qn[i] * eg[i]], axis=0), s[i]) for i in range(n)]
    v_new = [uw[i][:, :dv] - ws[i][:ch] for i in range(n)]
    o = [ws[i][ch:] + _bdot(qk[i], v_new[i]) for i in range(n)]
    s_new = [s[i] * elast[i] + _dot_tn((kn[i] * ekd[i]).astype(BF16), v_new[i].astype(BF16))
             for i in range(n)]
    for i in range(n):
        b, h = i // heads, i % heads
        s_ref[b, h] = s_new[i]
        on = _rms(o[i], ng_ref[...])
        o_ref[b, :, h * dv:(h + 1) * dv] = (
            on * go_ref[b, :, h * dv:(h + 1) * dv].astype(F32)).astype(BF16)


def _mid_kernel(h_ref, o_ref, ga_ref, ybg_ref, wdn_ref, wout_ref, g2_ref, wr_ref, br_ref,
                h2_ref, hn_ref, topi_ref, topg_ref, *, tm, n_exp, parts):
    rs = tm // parts
    sl = [pl.ds(p * rs, rs) for p in range(parts)]
    ya = [_dot(o_ref[s, :], wdn_ref[...]) for s in sl]
    y = [ga_ref[s, :].astype(F32) * a + ybg_ref[s, :].astype(F32) for s, a in zip(sl, ya)]
    h2 = [h_ref[s, :] + _dot(v.astype(BF16), wout_ref[...]) for s, v in zip(sl, y)]
    hn = [_rms(v, g2_ref[...]) for v in h2]
    for s, v, w in zip(sl, h2, hn):
        h2_ref[s, :] = v
        hn_ref[s, :] = w
    logits = [_dot(v.astype(BF16), wr_ref[...]) + br_ref[...] for v in hn]
    lane = _iota((rs, LANES), 1)
    lane_f = lane.astype(F32)
    l = [jnp.where(lane < n_exp, v, -jnp.inf) for v in logits]
    vals = [[] for _ in range(parts)]
    idxs = [[] for _ in range(parts)]
    for _ in range(TOP_K):
        m = [jnp.max(v, axis=-1, keepdims=True) for v in l]
        idx = [jnp.min(jnp.where(v == mm, lane_f, float(LANES)), axis=-1, keepdims=True)
               for v, mm in zip(l, m)]
        l = [jnp.where(lane_f == ii, -jnp.inf, v) for v, ii in zip(l, idx)]
        for p in range(parts):
            vals[p].append(m[p])
            idxs[p].append(idx[p])
    for p in range(parts):
        es = [jnp.exp(v - vals[p][0]) for v in vals[p]]
        den = es[0]
        for e in es[1:]:
            den = den + e
        topi = jnp.zeros((rs, LANES), F32)
        topg = jnp.zeros((rs, LANES), F32)
        for k in range(TOP_K):
            topi = jnp.where(lane == k, idxs[p][k], topi)
            topg = jnp.where(lane == k, es[k] / den, topg)
        topi_ref[sl[p], :] = topi.astype(I32)
        topg_ref[sl[p], :] = topg


def _choice_mask(topi, lane, n_exp):
    m = jnp.zeros(topi.shape, F32)
    for k in range(TOP_K):
        m = m + (lane == topi[:, k:k + 1]).astype(F32)
    return jnp.where(lane < n_exp, m, 0.0)


def _plan_kernel(topi_ref, pos_ref, post_ref, n8_ref, off8_ref, carry8_ref, tot8_ref, carry_ref,
                 *, tm, tiles, n_exp):
    i = pl.program_id(0)

    @pl.when(i == 0)
    def _():
        carry_ref[...] = jnp.zeros(carry_ref.shape, F32)

    lane = _iota((tm, LANES), 1)
    tri = (_iota((tm, tm), 0) >= _iota((tm, tm), 1)).astype(BF16)
    before = (_iota((LANES, LANES), 0) < _iota((LANES, LANES), 1)).astype(BF16)
    js = range(tiles)
    rows = [pl.ds(j * tm, tm) for j in js]
    topi = [topi_ref[r, :] for r in rows]
    m = [_choice_mask(t, lane, n_exp) for t in topi]
    cum = [_dot(tri, v.astype(BF16)) for v in m]
    n8 = [jnp.floor((c[tm - 1:tm, :] + (SUBLANES - 1)) * (1.0 / SUBLANES)) * SUBLANES for c in cum]
    off8 = [_dot(jnp.broadcast_to(v, (SUBLANES, LANES)).astype(BF16), before)[0:1] for v in n8]
    posf = [c - v + o for c, v, o in zip(cum, m, off8)]
    pos = [jnp.zeros((tm, LANES), F32) for _ in js]
    for k in range(TOP_K):
        pk = [jnp.sum(jnp.where(lane == t[:, k:k + 1], p, 0.0), axis=-1, keepdims=True)
              for t, p in zip(topi, posf)]
        pos = [jnp.where(lane == k, a, b) for a, b in zip(pk, pos)]
    carry = carry_ref[...]
    for j in js:
        pos_ref[rows[j], :] = pos[j].astype(I32)
        post_ref[j] = jnp.transpose(pos[j])[0:SUBLANES, :].astype(I32)
        n8_ref[j] = n8[j].astype(I32)
        off8_ref[j] = off8[j].astype(I32)
        carry8_ref[j] = carry.astype(I32)
        carry = carry + n8[j]
    carry_ref[...] = carry
    tot8_ref[...] = carry


def _pow2_floor(n):
    return 1 << (n.bit_length() - 1)


def _strip_pieces(n8, max_rows):
    pieces = []
    b = max_rows
    while b >= SUBLANES:
        pieces.append(((n8 & b) != 0, n8 & ~(2 * b - 1), b))
        b //= 2
    return pieces


LONG_PIECE = 64


def _copy_strip(n8, start_piece):
    n_long = n8 >> (LONG_PIECE.bit_length() - 1)

    def long_piece(j, carry):
        start_piece(j * LONG_PIECE, LONG_PIECE)
        return carry

    lax.fori_loop(0, n_long, long_piece, 0)
    for cond, ofs, rows in _strip_pieces(n8 & (LONG_PIECE - 1), LONG_PIECE // 2):
        @pl.when(cond)
        def _():
            start_piece(n_long * LONG_PIECE + ofs, rows)


def _wait_rows(total, make_copy, max_rows):
    b = max_rows
    while b >= SUBLANES:
        @pl.when((total & b) != 0)
        def _():
            make_copy(b).wait()
        b //= 2


def _aligned(start, rows):
    return pl.ds(pl.multiple_of(start, SUBLANES), rows)


def _dispatch_kernel(pend_ref, padded_ref, n8_ref, off8_ref, dbase_ref, hn_ref, post_ref, xs_ref,
                     zbuf_ref, sbuf_ref, tot_ref, zsem, sems, *, tm, width, n_exp, n_steps, n_blocks):
    i = pl.program_id(0)
    slot = i % 2

    def zero_block(start):
        return pltpu.make_async_copy(zbuf_ref, xs_ref.at[_aligned(start, MOE_BLOCK), :], zsem)

    @pl.when(i == 0)
    def _():
        zbuf_ref[...] = jnp.zeros(zbuf_ref.shape, F32)
        for e in range(n_exp):
            @pl.when(padded_ref[e] > 0)
            def _():
                zero_block(pend_ref[e] - MOE_BLOCK).start()
        first_unused = pend_ref[n_exp - 1] // MOE_BLOCK

        def start_tail(j, carry):
            zero_block(j * MOE_BLOCK).start()
            return carry

        lax.fori_loop(first_unused, n_blocks, start_tail, 0)
        for e in range(n_exp):
            @pl.when(padded_ref[e] > 0)
            def _():
                zero_block(0).wait()

        def wait_tail(j, carry):
            zero_block(0).wait()
            return carry

        lax.fori_loop(first_unused, n_blocks, wait_tail, 0)

    def strip_wait(s):
        _wait_rows(tot_ref[s], lambda b: pltpu.make_async_copy(
            sbuf_ref.at[s, pl.ds(0, b), :], xs_ref.at[pl.ds(0, b), :], sems.at[s]), _pow2_floor(width))

    @pl.when(i >= 2)
    def _():
        strip_wait(slot)

    post = post_ref[0]
    orow = _iota((width, tm), 0)
    sel = orow == post[0:1, :]
    for k in range(1, TOP_K):
        sel = jnp.logical_or(sel, orow == post[k:k + 1, :])
    sbuf_ref[slot] = _dot(sel.astype(BF16), hn_ref[...].astype(BF16))

    total = 0
    for e in range(n_exp):
        n8, so, do = n8_ref[e], off8_ref[e], dbase_ref[e]
        total = total + n8
        _copy_strip(n8, lambda ofs, rows, so=so, do=do: pltpu.make_async_copy(
            sbuf_ref.at[slot, _aligned(so + ofs, rows), :], xs_ref.at[_aligned(do + ofs, rows), :],
            sems.at[slot]).start())
    tot_ref[slot] = total

    @pl.when(i == n_steps - 1)
    def _():
        if n_steps >= 2:
            strip_wait(1 - slot)
        strip_wait(slot)


def _expert_kernel(be_ref, nbu_ref, x_ref, wgu_ref, bgu_ref, wd_ref, bd_ref, y_ref, wgu16_ref, wd16_ref,
                   *, d_ff, parts):
    i = pl.program_id(0)
    used = i < nbu_ref[0]
    new_expert = jnp.logical_or(i == 0, be_ref[i] != be_ref[jnp.maximum(i - 1, 0)])

    @pl.when(jnp.logical_and(used, new_expert))
    def _():
        wgu16_ref[...] = wgu_ref[0].astype(BF16)
        wd16_ref[...] = wd_ref[0].astype(BF16)

    @pl.when(used)
    def _():
        rs = x_ref.shape[0] // parts
        sl = [pl.ds(p * rs, rs) for p in range(parts)]
        gu = [_dot(x_ref[s, :].astype(BF16), wgu16_ref[...]) + bgu_ref[0] for s in sl]
        act = []
        for v in gu:
            gate = jnp.minimum(v[:, :d_ff], SWIGLU_LIMIT)
            up = jnp.clip(v[:, d_ff:], -SWIGLU_LIMIT, SWIGLU_LIMIT)
            act.append(((up + 1.0) * gate * _sigmoid(SWIGLU_ALPHA * gate)).astype(BF16))
        for s, a in zip(sl, act):
            y_ref[s, :] = _dot(a, wd16_ref[...]) + bd_ref[0]

    @pl.when(jnp.logical_not(used))
    def _():
        y_ref[...] = jnp.zeros(y_ref.shape, F32)


def _combine_kernel(n8c_ref, off8c_ref, dbc_ref, n8n_ref, off8n_ref, dbn_ref, pos_ref, h2_ref, topg_ref, fg_ref,
                    ys_ref, out_ref, gbuf_ref, tot_ref, sems, *, tm, width, n_exp, n_steps):
    i = pl.program_id(0)
    slot = i % 2

    def fetch(n8_ref, off8_ref, db_ref, s):
        total = 0
        for e in range(n_exp):
            n8, so, do = n8_ref[e], off8_ref[e], db_ref[e]
            total = total + n8
            _copy_strip(n8, lambda ofs, rows, so=so, do=do: pltpu.make_async_copy(
                ys_ref.at[_aligned(do + ofs, rows), :], gbuf_ref.at[s, _aligned(so + ofs, rows), :],
                sems.at[s]).start())
        tot_ref[s] = total

    @pl.when(i == 0)
    def _():
        gbuf_ref[...] = jnp.zeros(gbuf_ref.shape, F32)
        fetch(n8c_ref, off8c_ref, dbc_ref, 0)

    @pl.when(i + 1 < n_steps)
    def _():
        fetch(n8n_ref, off8n_ref, dbn_ref, 1 - slot)

    _wait_rows(tot_ref[slot], lambda b: pltpu.make_async_copy(
        ys_ref.at[pl.ds(0, b), :], gbuf_ref.at[slot, pl.ds(0, b), :], sems.at[slot]), _pow2_floor(width))
    pos = pos_ref[...]
    topg = topg_ref[...]
    lane = _iota((tm, width), 1)
    g = jnp.zeros((tm, width), F32)
    for k in range(TOP_K):
        g = g + jnp.where(lane == pos[:, k:k + 1], topg[:, k:k + 1], 0.0)
    moe = _dot(g.astype(BF16), gbuf_ref[slot].astype(BF16))
    out_ref[...] = _rms(h2_ref[...] + moe, fg_ref[...])


def _pad_lanes(v, fill=0.0):
    v = v.astype(F32)
    return jnp.concatenate([v, jnp.full((LANES - v.shape[0],), fill, F32)])[None, :]


def kernel(x, meta_tokens, norm_mix_g, w_in, conv_qkv_w, a_log, dt_bias, dn_norm_g, w_dn_out, b_glu,
           conv_dw_w, conv_dw_b, conv_ln_g, conv_ln_b, w_conv_out, b_conv_out, w_merge, b_merge, w_out,
           norm_ffn_g, w_router, b_router, w_gate_up, b_gate_up, w_down, b_down, final_norm_g):
    bsz, seq, d = x.shape
    depth = w_in.shape[0]
    n_meta = meta_tokens.shape[0]
    heads = a_log.shape[1]
    vw = w_dn_out.shape[1]
    qk_w = (conv_qkv_w.shape[2] - vw) // 2
    dk, dv = qk_w // heads, vw // heads
    ksz = conv_qkv_w.shape[1]
    kw, conv_ch = conv_dw_w.shape[1], conv_dw_w.shape[2]
    n_exp, d_ff = w_gate_up.shape[1], w_down.shape[2]
    meta_pad = CHUNK - n_meta
    lp = seq + n_meta + meta_pad
    assert depth == 1 and lp % CHUNK == 0 and dk == LANES and dv == LANES and d % LANES == 0
    assert 2 * heads <= LANES and n_exp <= LANES and kw - 1 <= 32 and ksz - 1 <= SUBLANES
    rows = bsz * lp
    n_head = meta_pad + n_meta
    head = jnp.concatenate([jnp.zeros((meta_pad, d), x.dtype), meta_tokens.astype(x.dtype)], axis=0)

    wi = w_in[0]
    o_a = 2 * qk_w + vw
    o_go = o_a + 2 * heads
    o_glu = o_go + vw
    w_qkv = wi[:, :o_a].astype(BF16)
    w_ab = jnp.pad(wi[:, o_a:o_go], ((0, 0), (0, LANES - 2 * heads))).astype(BF16)
    w_go = wi[:, o_go:o_glu].astype(BF16)
    w_glu = wi[:, o_glu:].astype(BF16)
    row2 = lambda v: v.astype(F32).reshape(1, -1)

    tm1 = _pick_tile(lp, min(832, seq), BF16_ROWS)
    nt1 = lp // tm1
    assert n_head % SUBLANES == 0 and n_head <= tm1 <= seq
    rspec = lambda cols, tm, nt: pl.BlockSpec((tm, cols), lambda b, j: (b * nt + j, 0))
    full = lambda a: pl.BlockSpec(a.shape, lambda *_: (0,) * a.ndim)
    full1 = lambda a: pl.BlockSpec(a.shape, lambda *_: (0,) * a.ndim, pipeline_mode=pl.Buffered(1))
    x_rows = pl.BlockSpec((pl.Element(1), pl.Element(tm1), pl.Element(d)),
                          lambda b, j: (b, pl.multiple_of(jnp.maximum(j * tm1 - n_head, 0), SUBLANES), 0))
    front_in = [head, x, row2(norm_mix_g[0]), w_qkv, w_ab, w_go, w_glu, w_merge[0].astype(BF16),
                row2(b_glu[0]), row2(b_merge[0]), _pad_lanes(a_log[0]), _pad_lanes(dt_bias[0])]
    hp, qkv_pre, gbeta, go_act, u_glu, gates = pl.pallas_call(
        functools.partial(_front_kernel, tm=tm1, meta_pad=meta_pad, heads=heads, conv_ch=conv_ch),
        grid=(bsz, nt1),
        in_specs=[full1(head), x_rows] + [full1(a) for a in front_in[2:]],
        out_specs=[rspec(d, tm1, nt1), rspec(o_a, tm1, nt1), rspec(LANES, tm1, nt1), rspec(vw, tm1, nt1),
                   rspec(conv_ch, tm1, nt1), rspec(2 * d, tm1, nt1)],
        out_shape=[jax.ShapeDtypeStruct((rows, d), F32),
                   jax.ShapeDtypeStruct((rows, o_a), BF16), jax.ShapeDtypeStruct((rows, LANES), F32),
                   jax.ShapeDtypeStruct((rows, vw), BF16), jax.ShapeDtypeStruct((rows, conv_ch), BF16),
                   jax.ShapeDtypeStruct((rows, 2 * d), BF16)],
        compiler_params=_params(("arbitrary", "arbitrary")),
        name="front",
    )(*front_in)

    tmc = _pick_tile(lp, 832, 32)
    ntc = lp // tmc
    halo, lead = 32, BF16_ROWS
    conv_w8 = jnp.repeat(conv_dw_w[0].astype(F32), SUBLANES, axis=0)
    conv_in = [u_glu, gates, conv_w8, row2(conv_dw_b[0]), row2(conv_ln_g[0]),
               row2(conv_ln_b[0]), w_conv_out[0].astype(BF16), row2(b_conv_out[0])]
    ybg = pl.pallas_call(
        functools.partial(_conv_kernel, tm=tmc, kw=kw, halo=halo, lead=lead, rb=32),
        grid=(bsz, ntc),
        in_specs=[rspec(conv_ch, tmc, ntc),
                  pl.BlockSpec((tmc, d), lambda b, j: (b * ntc + j, 1))] + [full(a) for a in conv_in[2:]],
        out_specs=rspec(d, tmc, ntc),
        out_shape=jax.ShapeDtypeStruct((rows, d), BF16),
        scratch_shapes=[pltpu.VMEM((lead + halo + tmc, conv_ch), BF16),
                        pltpu.VMEM((SUBLANES, lead + halo + tmc, conv_ch), F32),
                        pltpu.VMEM((tmc, conv_ch), F32)],
        compiler_params=_params(("arbitrary", "arbitrary")),
        name="convmod",
    )(*conv_in)

    nc = lp // CHUNK
    nbd = max(n for n in (1, 2, 4, 8) if bsz % n == 0)
    cspec = lambda cols: pl.BlockSpec((nbd, CHUNK, cols), lambda b, c: (b, c, 0))
    delta_in = [qkv_pre.reshape(bsz, lp, o_a), gbeta.reshape(bsz, lp, LANES), go_act.reshape(bsz, lp, vw),
                conv_qkv_w[0].astype(F32), row2(dn_norm_g[0])]
    o_dn = pl.pallas_call(
        functools.partial(_delta_kernel, nb=nbd, heads=heads, dk=dk, dv=dv, ch=CHUNK, ksz=ksz, halo=BF16_ROWS),
        grid=(bsz // nbd, nc),
        in_specs=[cspec(o_a), cspec(LANES), cspec(vw), full(delta_in[3]), full(delta_in[4])],
        out_specs=cspec(vw),
        out_shape=jax.ShapeDtypeStruct((bsz, lp, vw), BF16),
        scratch_shapes=[pltpu.VMEM((nbd, heads, dk, dv), F32),
                        pltpu.VMEM((nbd, CHUNK + BF16_ROWS, o_a), BF16)],
        compiler_params=_params(("arbitrary", "arbitrary")),
        name="delta",
    )(*delta_in).reshape(rows, vw)

    tm3 = _pick_tile(lp, MID_TILE, BF16_ROWS)
    nt3 = lp // tm3
    w_r = jnp.pad(w_router[0], ((0, 0), (0, LANES - n_exp))).astype(BF16)
    mid_in = [hp, o_dn, gates, ybg, w_dn_out[0].astype(BF16), w_out[0].astype(BF16),
              row2(norm_ffn_g[0]), w_r, _pad_lanes(b_router[0])]
    h2, hn2, topi, topg = pl.pallas_call(
        functools.partial(_mid_kernel, tm=tm3, n_exp=n_exp, parts=MID_PARTS if tm3 % (MID_PARTS * BF16_ROWS) == 0 else 1),
        grid=(bsz, nt3),
        in_specs=[rspec(d, tm3, nt3), rspec(vw, tm3, nt3), rspec(d, tm3, nt3), rspec(d, tm3, nt3)]
        + [full(a) for a in mid_in[4:]],
        out_specs=[rspec(d, tm3, nt3), rspec(d, tm3, nt3), rspec(LANES, tm3, nt3), rspec(LANES, tm3, nt3)],
        out_shape=[jax.ShapeDtypeStruct((rows, d), F32), jax.ShapeDtypeStruct((rows, d), F32),
                   jax.ShapeDtypeStruct((rows, LANES), I32), jax.ShapeDtypeStruct((rows, LANES), F32)],
        compiler_params=_params(("arbitrary", "arbitrary")),
        name="mid",
    )(*mid_in)

    tmd = _pick_tile(seq, TOKEN_TILE, SUBLANES)
    tpb = seq // tmd
    ntd = bsz * tpb
    tokens = bsz * seq
    width = tmd * TOP_K + n_exp * SUBLANES
    row0 = lambda i: pl.multiple_of((i // tpb) * lp + n_head + (i % tpb) * tmd, SUBLANES)
    prow = lambda r, cols, f=row0: pl.BlockSpec((pl.Element(r), pl.Element(cols)), lambda i, *_: (f(i), 0))
    tspec = pl.BlockSpec((tmd, LANES), lambda i, *_: (i, 0))
    ptiles = _pick_tile(tpb, 8, 1)
    pspec = pl.BlockSpec((ptiles * tmd, LANES), lambda i: (i, 0))
    tab = pl.BlockSpec((ptiles, 1, LANES), lambda i: (i, 0, 0))
    tab_shape = jax.ShapeDtypeStruct((ntd, 1, LANES), I32)
    pos, pos_t, n8_t, off8_t, carry8_t, tot8 = pl.pallas_call(
        functools.partial(_plan_kernel, tm=tmd, tiles=ptiles, n_exp=n_exp),
        grid=(ntd // ptiles,),
        in_specs=[prow(ptiles * tmd, LANES, lambda i: row0(i * ptiles))],
        out_specs=[pspec, pl.BlockSpec((ptiles, SUBLANES, tmd), lambda i: (i, 0, 0)), tab, tab, tab,
                   pl.BlockSpec((1, LANES), lambda i: (0, 0))],
        out_shape=[jax.ShapeDtypeStruct((tokens, LANES), I32), jax.ShapeDtypeStruct((ntd, SUBLANES, tmd), I32),
                   tab_shape, tab_shape, tab_shape, jax.ShapeDtypeStruct((1, LANES), F32)],
        scratch_shapes=[pltpu.VMEM((1, LANES), F32)],
        compiler_params=_params(("arbitrary",)),
        name="plan",
    )(topi)

    n_blocks = -(-(tokens * TOP_K + n_exp * (SUBLANES - 1) * ntd) // MOE_BLOCK) + n_exp
    cap = n_blocks * MOE_BLOCK
    counts = tot8[0, :n_exp].astype(I32)
    padded = (counts + MOE_BLOCK - 1) // MOE_BLOCK * MOE_BLOCK
    pend = jnp.cumsum(padded).astype(I32)
    pstart = pend - padded
    n_used = (pend[-1] // MOE_BLOCK).astype(I32).reshape(1)
    block_start = jnp.arange(n_blocks, dtype=I32) * MOE_BLOCK
    block_e = jnp.minimum(jnp.sum((pend[None, :] <= block_start[:, None]).astype(I32), axis=1), n_exp - 1)
    dbase = (carry8_t.reshape(ntd, LANES) + jnp.pad(pstart, (0, LANES - n_exp))[None, :]).reshape(ntd * LANES)
    n8_f = n8_t.reshape(ntd * LANES)
    off8_f = off8_t.reshape(ntd * LANES)

    anyspec = pl.BlockSpec(memory_space=pl.ANY)
    stab = lambda f: pl.BlockSpec((LANES,), lambda i, *_: (f(i),), memory_space=pltpu.SMEM)
    cur = lambda i: i
    xs = pl.pallas_call(
        functools.partial(_dispatch_kernel, tm=tmd, width=width, n_exp=n_exp, n_steps=ntd, n_blocks=n_blocks),
        grid_spec=pltpu.PrefetchScalarGridSpec(
            num_scalar_prefetch=2, grid=(ntd,),
            in_specs=[stab(cur), stab(cur), stab(cur), prow(tmd, d),
                      pl.BlockSpec((1, SUBLANES, tmd), lambda i, *_: (i, 0, 0))],
            out_specs=anyspec,
            scratch_shapes=[pltpu.VMEM((MOE_BLOCK, d), F32), pltpu.VMEM((2, width, d), F32),
                            pltpu.SMEM((2,), I32), pltpu.SemaphoreType.DMA(()), pltpu.SemaphoreType.DMA((2,))]),
        out_shape=jax.ShapeDtypeStruct((cap, d), F32),
        compiler_params=_params(("arbitrary",)),
        name="dispatch",
    )(pend, padded, n8_f, off8_f, dbase, hn2, pos_t)

    last_used = lambda i, nbu: jnp.minimum(i, nbu[0] - 1)
    ew = lambda a: pl.BlockSpec((1,) + a.shape[1:], lambda i, be, nbu: (be[i],) + (0,) * (a.ndim - 1))
    e_in = [w_gate_up[0], b_gate_up[0].astype(F32)[:, None, :], w_down[0], b_down[0].astype(F32)[:, None, :]]
    ys = pl.pallas_call(
        functools.partial(_expert_kernel, d_ff=d_ff, parts=2),
        grid_spec=pltpu.PrefetchScalarGridSpec(
            num_scalar_prefetch=2, grid=(n_blocks,),
            in_specs=[pl.BlockSpec((MOE_BLOCK, d), lambda i, be, nbu: (last_used(i, nbu), 0))]
            + [ew(a) for a in e_in],
            out_specs=pl.BlockSpec((MOE_BLOCK, d), lambda i, be, nbu: (i, 0)),
            scratch_shapes=[pltpu.VMEM((d, 2 * d_ff), BF16), pltpu.VMEM((d_ff, d), BF16)]),
        out_shape=jax.ShapeDtypeStruct((cap, d), F32),
        compiler_params=_params(("arbitrary",)),
        name="experts",
    )(block_e, n_used, xs, *e_in)

    nxt = lambda i: jnp.minimum(i + 1, ntd - 1)
    out = pl.pallas_call(
        functools.partial(_combine_kernel, tm=tmd, width=width, n_exp=n_exp, n_steps=ntd),
        grid=(ntd,),
        in_specs=[stab(cur), stab(cur), stab(cur), stab(nxt), stab(nxt), stab(nxt), tspec,
                  prow(tmd, d), prow(tmd, LANES), pl.BlockSpec((1, d), lambda i: (0, 0)), anyspec],
        out_specs=pl.BlockSpec((tmd, d), lambda i: (i, 0)),
        out_shape=jax.ShapeDtypeStruct((tokens, d), x.dtype),
        scratch_shapes=[pltpu.VMEM((2, width, d), F32), pltpu.SMEM((2,), I32), pltpu.SemaphoreType.DMA((2,))],
        compiler_params=_params(("arbitrary",)),
        name="combine",
    )(n8_f, off8_f, dbase, n8_f, off8_f, dbase, pos, h2, topg, row2(final_norm_g), ys)
    return out.reshape(bsz, seq, d)
```

```python
import functools

import jax
import jax.numpy as jnp
from jax import lax
from jax.experimental import pallas as pl
from jax.experimental.pallas import tpu as pltpu

F32 = jnp.float32
BF16 = jnp.bfloat16
I32 = jnp.int32

CHUNK = 64
TOP_K = 4
MOE_BLOCK = 512
TOKEN_TILE = 256
MID_TILE = 832
MID_PARTS = 4
SWIGLU_LIMIT = 7.0
SWIGLU_ALPHA = 1.702
EPS = 1e-6
LN_EPS = 1e-5
LANES = 128
SUBLANES = 8
BF16_ROWS = 16
VMEM_LIMIT = 56 * 1024 * 1024


def _dot(a, b):
    return jnp.dot(a, b, preferred_element_type=F32)


def _dot_nt(a, b):
    return lax.dot_general(a, b, (((1,), (1,)), ((), ())), preferred_element_type=F32)


def _dot_tn(a, b):
    return lax.dot_general(a, b, (((0,), (0,)), ((), ())), preferred_element_type=F32)


def _sigmoid(x):
    return 1.0 / (1.0 + jnp.exp(-x))


def _silu(x):
    return x * _sigmoid(x)


def _softplus(x):
    return jnp.maximum(x, 0.0) + jnp.log(1.0 + jnp.exp(-jnp.abs(x)))


def _rms(x, g):
    return x * lax.rsqrt(jnp.mean(x * x, axis=-1, keepdims=True) + EPS) * g


def _iota(shape, dim):
    return lax.broadcasted_iota(I32, shape, dim)


def _pick_tile(n, target, mult):
    best = None
    for t in range(mult, min(n, target) + 1, mult):
        if n % t == 0:
            best = t
    assert best is not None, (n, target, mult)
    return best


def _params(sem, flags=None):
    return pltpu.CompilerParams(dimension_semantics=sem, vmem_limit_bytes=VMEM_LIMIT, flags=flags)


def _front_kernel(head_ref, x_ref, g_ref, wqkv_ref, wab_ref, wgo_ref, wglu_ref, wmg_ref, bglu_ref, bmg_ref,
                  alog_ref, dtb_ref, h_ref, qkv_ref, gb_ref, go_ref, u_ref, gates_ref,
                  *, tm, meta_pad, heads, conv_ch):
    j = pl.program_id(1)
    xb = x_ref[0]
    n_head = head_ref.shape[0]
    h = jnp.where(j == 0, jnp.concatenate([head_ref[...], xb[:tm - n_head]], axis=0), xb)
    h_ref[...] = h
    hn = _rms(h, g_ref[...]).astype(BF16)
    qkv_ref[...] = _dot(hn, wqkv_ref[...]).astype(BF16)
    ab = _dot(hn, wab_ref[...])
    valid = (j * tm + _iota((tm, 1), 0)) >= meta_pad
    lane = _iota((1, LANES), 1)
    g = -jnp.exp(alog_ref[...]) * _softplus(ab + dtb_ref[...])
    gb = jnp.where(lane < heads, g, _sigmoid(ab))
    gb_ref[...] = jnp.where(valid, gb, 0.0)
    go_ref[...] = _silu(_dot(hn, wgo_ref[...])).astype(BF16)
    glu = _dot(hn, wglu_ref[...]) + bglu_ref[...]
    u = glu[:, :conv_ch] * _sigmoid(glu[:, conv_ch:])
    u_ref[...] = jnp.where(valid, u, 0.0).astype(BF16)
    gates_ref[...] = _sigmoid(_dot(hn, wmg_ref[...]) + bmg_ref[...]).astype(BF16)


def _conv_kernel(u_ref, gate_ref, w_ref, b_ref, lng_ref, lnb_ref, wo_ref, bo_ref, y_ref,
                 xcat_ref, xs_ref, cbuf_ref, *, tm, kw, halo, lead, rb):
    j = pl.program_id(1)
    base = lead + halo
    total = base + tm
    n_c = xcat_ref.shape[1]

    @pl.when(j == 0)
    def _():
        xcat_ref[0:base, :] = jnp.zeros((base, n_c), BF16)

    @pl.when(j > 0)
    def _():
        xcat_ref[lead:base, :] = xcat_ref[tm + lead:tm + base, :]

    xcat_ref[base:total, :] = u_ref[...]
    sr = _iota((SUBLANES * rb, lead + rb), 0)
    sc = _iota((SUBLANES * rb, lead + rb), 1)
    shift = (sc == (sr & (rb - 1)) + lead - (sr >> (rb.bit_length() - 1))).astype(BF16)
    bias = b_ref[...]
    for t0 in range(lead, total, rb):
        res = _dot(shift, xcat_ref[t0 - lead:t0 + rb, :])
        for r in range(SUBLANES):
            xs_ref[r, t0:t0 + rb, :] = res[r * rb:(r + 1) * rb, :]
        if t0 < base:
            continue
        i0 = t0
        blk = (i0 - base) // rb
        acc = jnp.zeros((rb, n_c), F32)
        for s in range(kw):
            a, r = divmod(s, SUBLANES)
            wt = w_ref[(kw - 1 - s) * SUBLANES:(kw - s) * SUBLANES, :]
            acc = acc + xs_ref[r, i0 - SUBLANES * a:i0 - SUBLANES * a + rb, :] * jnp.concatenate(
                [wt] * (rb // SUBLANES), axis=0)
        cbuf_ref[blk * rb:(blk + 1) * rb, :] = acc + bias
    c = cbuf_ref[...]
    mu = jnp.mean(c, axis=-1, keepdims=True)
    xc = c - mu
    ln = xc * lax.rsqrt(jnp.mean(xc * xc, axis=-1, keepdims=True) + LN_EPS) * lng_ref[...] + lnb_ref[...]
    yb = _dot(_silu(ln).astype(BF16), wo_ref[...]) + bo_ref[...]
    y_ref[...] = (gate_ref[...].astype(F32) * yb).astype(BF16)


def _bdot(a, b):
    return jnp.dot(a.astype(BF16), b.astype(BF16), preferred_element_type=F32)


def _split3(x):
    x1 = x.astype(BF16)
    r1 = x - x1.astype(F32)
    x2 = r1.astype(BF16)
    x3 = (r1 - x2.astype(F32)).astype(BF16)
    return x1, x2, x3


def _unit_lower_inverses_minus_eye(mats, row, col):
    ch = mats[0].shape[0]
    blk16 = (row >> 4) == (col >> 4)
    blk32 = (row >> 5) == (col >> 5)
    n1 = [jnp.where(blk16, a, 0.0) for a in mats]
    n2 = [_bdot(n, n) for n in n1]
    r = [_bdot(jnp.concatenate([n, m], axis=0), m) for n, m in zip(n1, n2)]
    n4 = [x[ch:] for x in r]
    y = [m - n - x[:ch] for n, m, x in zip(n1, n2, r)]
    r = [_bdot(jnp.concatenate([yy, m], axis=0), m) for yy, m in zip(y, n4)]
    n8 = [x[ch:] for x in r]
    y = [yy + m + x[:ch] for yy, m, x in zip(y, n4, r)]
    y = [yy + m + _bdot(yy, m) for yy, m in zip(y, n8)]
    for mask in (blk32 & jnp.logical_not(blk16), jnp.logical_not(blk32)):
        ls = [jnp.where(mask, a, 0.0) for a in mats]
        ms = [l + _bdot(l, yy) for l, yy in zip(ls, y)]
        y = [yy - m - _bdot(yy, m) for yy, m in zip(y, ms)]
    return y


def _delta_kernel(qkv_ref, gb_ref, go_ref, cw_ref, ng_ref, o_ref, s_ref, xc_ref,
                  *, nb, heads, dk, dv, ch, ksz, halo):
    c = pl.program_id(1)
    qk_w = heads * dk

    @pl.when(c == 0)
    def _():
        s_ref[...] = jnp.zeros(s_ref.shape, F32)
        xc_ref[:, 0:halo, :] = jnp.zeros((nb, halo, xc_ref.shape[2]), BF16)

    @pl.when(c > 0)
    def _():
        xc_ref[:, 0:halo, :] = xc_ref[:, ch:ch + halo, :]

    xc_ref[:, halo:halo + ch, :] = qkv_ref[...]
    row = _iota((ch, ch), 0)
    col = _iota((ch, ch), 1)
    incl = row >= col
    strict = row > col
    tril = incl.astype(BF16)
    triu = (row <= col).astype(BF16)
    sr = _iota(((ksz - 1) * ch, halo + ch), 0)
    sc = _iota(((ksz - 1) * ch, halo + ch), 1)
    shift = (sc == (sr & (ch - 1)) + (sr >> (ch.bit_length() - 1)) + (halo - (ksz - 1))).astype(BF16)

    qn, kn, kb, vb, gamma, eg, ekd, elast = [], [], [], [], [], [], [], []
    shifted = [_dot(shift, xc_ref[b]) for b in range(nb)]
    for b in range(nb):
        acc = xc_ref[b, halo:halo + ch, :].astype(F32) * cw_ref[ksz - 1:ksz, :]
        for t in range(ksz - 1):
            acc = acc + shifted[b][t * ch:(t + 1) * ch, :] * cw_ref[t:t + 1, :]
        qkv = _silu(acc)
        gb = gb_ref[b]
        parts = _split3(gb)
        gc = sum(_dot(tril, p) for p in parts)
        gct = sum(_dot_tn(p, triu) for p in parts)
        glast = gc[ch - 1:ch, :]
        e_g = jnp.exp(gc)
        e_kd = jnp.exp(glast - gc)
        e_last = jnp.exp(glast)
        for h in range(heads):
            q = qkv[:, h * dk:(h + 1) * dk]
            k = qkv[:, qk_w + h * dk:qk_w + (h + 1) * dk]
            v = qkv[:, 2 * qk_w + h * dv:2 * qk_w + (h + 1) * dv]
            qn.append(q * lax.rsqrt(jnp.sum(q * q, axis=-1, keepdims=True) + EPS) * (dk ** -0.5))
            kn.append(k * lax.rsqrt(jnp.sum(k * k, axis=-1, keepdims=True) + EPS))
            beta = gb[:, heads + h:heads + h + 1]
            kb.append(kn[-1] * beta)
            vb.append(v * beta)
            diff = gc[:, h:h + 1] - gct[h:h + 1, :]
            gamma.append(jnp.where(incl, jnp.exp(jnp.where(incl, diff, 0.0)), 0.0))
            eg.append(e_g[:, h:h + 1])
            ekd.append(e_kd[:, h:h + 1])
            elast.append(e_last[:, h:h + 1])

    n = nb * heads
    kq = [_dot_nt(jnp.concatenate([kb[i], qn[i]], axis=0).astype(BF16), kn[i].astype(BF16))
          for i in range(n)]
    a = [jnp.where(strict, kq[i][:ch] * gamma[i], 0.0) for i in range(n)]
    qk = [kq[i][ch:] * gamma[i] for i in range(n)]
    y = _unit_lower_inverses_minus_eye(a, row, col)
    rhs = [jnp.concatenate([vb[i], kb[i] * eg[i]], axis=-1) for i in range(n)]
    uw = [rhs[i] + _bdot(y[i], rhs[i]) for i in range(n)]
    s = [s_ref[i // heads, i % heads] for i in range(n)]
    ws = [_bdot(jnp.concatenate([uw[i][:, dv:], qn[i] * eg[i]], axis=0), s[i]) for i in range(n)]
    v_new = [uw[i][:, :dv] - ws[i][:ch] for i in range(n)]
    o = [ws[i][ch:] + _bdot(qk[i], v_new[i]) for i in range(n)]
    s_new = [s[i] * elast[i] + _dot_tn((kn[i] * ekd[i]).astype(BF16), v_new[i].astype(BF16))
             for i in range(n)]
    for i in range(n):
        b, h = i // heads, i % heads
        s_ref[b, h] = s_new[i]
        on = _rms(o[i], ng_ref[...])
        o_ref[b, :, h * dv:(h + 1) * dv] = (
            on * go_ref[b, :, h * dv:(h + 1) * dv].astype(F32)).astype(BF16)


def _mid_kernel(h_ref, o_ref, ga_ref, ybg_ref, wdn_ref, wout_ref, g2_ref, wr_ref, br_ref,
                h2_ref, hn_ref, topi_ref, topg_ref, *, tm, n_exp, parts):
    rs = tm // parts
    sl = [pl.ds(p * rs, rs) for p in range(parts)]
    ya = [_dot(o_ref[s, :], wdn_ref[...]) for s in sl]
    y = [ga_ref[s, :].astype(F32) * a + ybg_ref[s, :].astype(F32) for s, a in zip(sl, ya)]
    h2 = [h_ref[s, :] + _dot(v.astype(BF16), wout_ref[...]) for s, v in zip(sl, y)]
    hn = [_rms(v, g2_ref[...]) for v in h2]
    for s, v, w in zip(sl, h2, hn):
        h2_ref[s, :] = v
        hn_ref[s, :] = w
    logits = [_dot(v.astype(BF16), wr_ref[...]) + br_ref[...] for v in hn]
    lane = _iota((rs, LANES), 1)
    lane_f = lane.astype(F32)
    l = [jnp.where(lane < n_exp, v, -jnp.inf) for v in logits]
    vals = [[] for _ in range(parts)]
    idxs = [[] for _ in range(parts)]
    for _ in range(TOP_K):
        m = [jnp.max(v, axis=-1, keepdims=True) for v in l]
        idx = [jnp.min(jnp.where(v == mm, lane_f, float(LANES)), axis=-1, keepdims=True)
               for v, mm in zip(l, m)]
        l = [jnp.where(lane_f == ii, -jnp.inf, v) for v, ii in zip(l, idx)]
        for p in range(parts):
            vals[p].append(m[p])
            idxs[p].append(idx[p])
    for p in range(parts):
        es = [jnp.exp(v - vals[p][0]) for v in vals[p]]
        den = es[0]
        for e in es[1:]:
            den = den + e
        topi = jnp.zeros((rs, LANES), F32)
        topg = jnp.zeros((rs, LANES), F32)
        for k in range(TOP_K):
            topi = jnp.where(lane == k, idxs[p][k], topi)
            topg = jnp.where(lane == k, es[k] / den, topg)
        topi_ref[sl[p], :] = topi.astype(I32)
        topg_ref[sl[p], :] = topg


def _choice_mask(topi, lane, n_exp):
    m = jnp.zeros(topi.shape, F32)
    for k in range(TOP_K):
        m = m + (lane == topi[:, k:k + 1]).astype(F32)
    return jnp.where(lane < n_exp, m, 0.0)


def _plan_kernel(topi_ref, pos_ref, post_ref, n8_ref, off8_ref, carry8_ref, tot8_ref, carry_ref,
                 *, tm, tiles, n_exp):
    i = pl.program_id(0)

    @pl.when(i == 0)
    def _():
        carry_ref[...] = jnp.zeros(carry_ref.shape, F32)

    lane = _iota((tm, LANES), 1)
    tri = (_iota((tm, tm), 0) >= _iota((tm, tm), 1)).astype(BF16)
    before = (_iota((LANES, LANES), 0) < _iota((LANES, LANES), 1)).astype(BF16)
    js = range(tiles)
    rows = [pl.ds(j * tm, tm) for j in js]
    topi = [topi_ref[r, :] for r in rows]
    m = [_choice_mask(t, lane, n_exp) for t in topi]
    cum = [_dot(tri, v.astype(BF16)) for v in m]
    n8 = [jnp.floor((c[tm - 1:tm, :] + (SUBLANES - 1)) * (1.0 / SUBLANES)) * SUBLANES for c in cum]
    off8 = [_dot(jnp.broadcast_to(v, (SUBLANES, LANES)).astype(BF16), before)[0:1] for v in n8]
    posf = [c - v + o for c, v, o in zip(cum, m, off8)]
    pos = [jnp.zeros((tm, LANES), F32) for _ in js]
    for k in range(TOP_K):
        pk = [jnp.sum(jnp.where(lane == t[:, k:k + 1], p, 0.0), axis=-1, keepdims=True)
              for t, p in zip(topi, posf)]
        pos = [jnp.where(lane == k, a, b) for a, b in zip(pk, pos)]
    carry = carry_ref[...]
    for j in js:
        pos_ref[rows[j], :] = pos[j].astype(I32)
        post_ref[j] = jnp.transpose(pos[j])[0:SUBLANES, :].astype(I32)
        n8_ref[j] = n8[j].astype(I32)
        off8_ref[j] = off8[j].astype(I32)
        carry8_ref[j] = carry.astype(I32)
        carry = carry + n8[j]
    carry_ref[...] = carry
    tot8_ref[...] = carry


def _pow2_floor(n):
    return 1 << (n.bit_length() - 1)


def _strip_pieces(n8, max_rows):
    pieces = []
    b = max_rows
    while b >= SUBLANES:
        pieces.append(((n8 & b) != 0, n8 & ~(2 * b - 1), b))
        b //= 2
    return pieces


LONG_PIECE = 64


def _copy_strip(n8, start_piece):
    n_long = n8 >> (LONG_PIECE.bit_length() - 1)

    def long_piece(j, carry):
        start_piece(j * LONG_PIECE, LONG_PIECE)
        return carry

    lax.fori_loop(0, n_long, long_piece, 0)
    for cond, ofs, rows in _strip_pieces(n8 & (LONG_PIECE - 1), LONG_PIECE // 2):
        @pl.when(cond)
        def _():
            start_piece(n_long * LONG_PIECE + ofs, rows)


def _wait_rows(total, make_copy, max_rows):
    b = max_rows
    while b >= SUBLANES:
        @pl.when((total & b) != 0)
        def _():
            make_copy(b).wait()
        b //= 2


def _aligned(start, rows):
    return pl.ds(pl.multiple_of(start, SUBLANES), rows)


def _dispatch_kernel(pend_ref, padded_ref, n8_ref, off8_ref, dbase_ref, hn_ref, post_ref, xs_ref,
                     zbuf_ref, sbuf_ref, tot_ref, zsem, sems, *, tm, width, n_exp, n_steps, n_blocks):
    i = pl.program_id(0)
    slot = i % 2

    def zero_block(start):
        return pltpu.make_async_copy(zbuf_ref, xs_ref.at[_aligned(start, MOE_BLOCK), :], zsem)

    @pl.when(i == 0)
    def _():
        zbuf_ref[...] = jnp.zeros(zbuf_ref.shape, F32)
        for e in range(n_exp):
            @pl.when(padded_ref[e] > 0)
            def _():
                zero_block(pend_ref[e] - MOE_BLOCK).start()
        first_unused = pend_ref[n_exp - 1] // MOE_BLOCK

        def start_tail(j, carry):
            zero_block(j * MOE_BLOCK).start()
            return carry

        lax.fori_loop(first_unused, n_blocks, start_tail, 0)
        for e in range(n_exp):
            @pl.when(padded_ref[e] > 0)
            def _():
                zero_block(0).wait()

        def wait_tail(j, carry):
            zero_block(0).wait()
            return carry

        lax.fori_loop(first_unused, n_blocks, wait_tail, 0)

    def strip_wait(s):
        _wait_rows(tot_ref[s], lambda b: pltpu.make_async_copy(
            sbuf_ref.at[s, pl.ds(0, b), :], xs_ref.at[pl.ds(0, b), :], sems.at[s]), _pow2_floor(width))

    @pl.when(i >= 2)
    def _():
        strip_wait(slot)

    post = post_ref[0]
    orow = _iota((width, tm), 0)
    sel = orow == post[0:1, :]
    for k in range(1, TOP_K):
        sel = jnp.logical_or(sel, orow == post[k:k + 1, :])
    sbuf_ref[slot] = _dot(sel.astype(BF16), hn_ref[...].astype(BF16))

    total = 0
    for e in range(n_exp):
        n8, so, do = n8_ref[e], off8_ref[e], dbase_ref[e]
        total = total + n8
        _copy_strip(n8, lambda ofs, rows, so=so, do=do: pltpu.make_async_copy(
            sbuf_ref.at[slot, _aligned(so + ofs, rows), :], xs_ref.at[_aligned(do + ofs, rows), :],
            sems.at[slot]).start())
    tot_ref[slot] = total

    @pl.when(i == n_steps - 1)
    def _():
        if n_steps >= 2:
            strip_wait(1 - slot)
        strip_wait(slot)


def _expert_kernel(be_ref, nbu_ref, fill_ref, x_ref, wgu_ref, bgu_ref, wd_ref, bd_ref, y_ref, wgu16_ref, wd16_ref,
                   *, d_ff, parts):
    i = pl.program_id(0)
    used = i < nbu_ref[0]
    new_expert = jnp.logical_or(i == 0, be_ref[i] != be_ref[jnp.maximum(i - 1, 0)])
    rs = x_ref.shape[0] // parts

    @pl.when(jnp.logical_and(used, new_expert))
    def _():
        wgu16_ref[...] = wgu_ref[0].astype(BF16)
        wd16_ref[...] = wd_ref[0].astype(BF16)

    def mlp(groups):
        sl = [pl.ds(p * rs, rs) for p in groups]
        gu = [_dot(x_ref[s, :].astype(BF16), wgu16_ref[...]) + bgu_ref[0] for s in sl]
        act = []
        for v in gu:
            gate = jnp.minimum(v[:, :d_ff], SWIGLU_LIMIT)
            up = jnp.clip(v[:, d_ff:], -SWIGLU_LIMIT, SWIGLU_LIMIT)
            act.append(((up + 1.0) * gate * _sigmoid(SWIGLU_ALPHA * gate)).astype(BF16))
        for s, a in zip(sl, act):
            y_ref[s, :] = _dot(a, wd16_ref[...]) + bd_ref[0]

    short = fill_ref[i] <= rs

    @pl.when(jnp.logical_and(used, jnp.logical_not(short)))
    def _():
        mlp(range(parts))

    @pl.when(jnp.logical_and(used, short))
    def _():
        mlp(range(1))
        y_ref[rs:, :] = jnp.zeros((x_ref.shape[0] - rs, y_ref.shape[1]), F32)

    @pl.when(jnp.logical_not(used))
    def _():
        y_ref[...] = jnp.zeros(y_ref.shape, F32)


def _combine_kernel(n8c_ref, off8c_ref, dbc_ref, n8n_ref, off8n_ref, dbn_ref, pos_ref, h2_ref, topg_ref, fg_ref,
                    ys_ref, out_ref, gbuf_ref, tot_ref, sems, *, tm, width, n_exp, n_steps):
    i = pl.program_id(0)
    slot = i % 2

    def fetch(n8_ref, off8_ref, db_ref, s):
        total = 0
        for e in range(n_exp):
            n8, so, do = n8_ref[e], off8_ref[e], db_ref[e]
            total = total + n8
            _copy_strip(n8, lambda ofs, rows, so=so, do=do: pltpu.make_async_copy(
                ys_ref.at[_aligned(do + ofs, rows), :], gbuf_ref.at[s, _aligned(so + ofs, rows), :],
                sems.at[s]).start())
        tot_ref[s] = total

    @pl.when(i == 0)
    def _():
        gbuf_ref[...] = jnp.zeros(gbuf_ref.shape, F32)
        fetch(n8c_ref, off8c_ref, dbc_ref, 0)

    @pl.when(i + 1 < n_steps)
    def _():
        fetch(n8n_ref, off8n_ref, dbn_ref, 1 - slot)

    _wait_rows(tot_ref[slot], lambda b: pltpu.make_async_copy(
        ys_ref.at[pl.ds(0, b), :], gbuf_ref.at[slot, pl.ds(0, b), :], sems.at[slot]), _pow2_floor(width))
    pos = pos_ref[...]
    topg = topg_ref[...]
    lane = _iota((tm, width), 1)
    g = jnp.zeros((tm, width), F32)
    for k in range(TOP_K):
        g = g + jnp.where(lane == pos[:, k:k + 1], topg[:, k:k + 1], 0.0)
    moe = _dot(g.astype(BF16), gbuf_ref[slot].astype(BF16))
    out_ref[...] = _rms(h2_ref[...] + moe, fg_ref[...])


def _pad_lanes(v, fill=0.0):
    v = v.astype(F32)
    return jnp.concatenate([v, jnp.full((LANES - v.shape[0],), fill, F32)])[None, :]


def kernel(x, meta_tokens, norm_mix_g, w_in, conv_qkv_w, a_log, dt_bias, dn_norm_g, w_dn_out, b_glu,
           conv_dw_w, conv_dw_b, conv_ln_g, conv_ln_b, w_conv_out, b_conv_out, w_merge, b_merge, w_out,
           norm_ffn_g, w_router, b_router, w_gate_up, b_gate_up, w_down, b_down, final_norm_g):
    bsz, seq, d = x.shape
    depth = w_in.shape[0]
    n_meta = meta_tokens.shape[0]
    heads = a_log.shape[1]
    vw = w_dn_out.shape[1]
    qk_w = (conv_qkv_w.shape[2] - vw) // 2
    dk, dv = qk_w // heads, vw // heads
    ksz = conv_qkv_w.shape[1]
    kw, conv_ch = conv_dw_w.shape[1], conv_dw_w.shape[2]
    n_exp, d_ff = w_gate_up.shape[1], w_down.shape[2]
    meta_pad = CHUNK - n_meta
    lp = seq + n_meta + meta_pad
    assert depth == 1 and lp % CHUNK == 0 and dk == LANES and dv == LANES and d % LANES == 0
    assert 2 * heads <= LANES and n_exp <= LANES and kw - 1 <= 32 and ksz - 1 <= SUBLANES
    rows = bsz * lp
    n_head = meta_pad + n_meta
    head = jnp.concatenate([jnp.zeros((meta_pad, d), x.dtype), meta_tokens.astype(x.dtype)], axis=0)

    wi = w_in[0]
    o_a = 2 * qk_w + vw
    o_go = o_a + 2 * heads
    o_glu = o_go + vw
    w_qkv = wi[:, :o_a].astype(BF16)
    w_ab = jnp.pad(wi[:, o_a:o_go], ((0, 0), (0, LANES - 2 * heads))).astype(BF16)
    w_go = wi[:, o_go:o_glu].astype(BF16)
    w_glu = wi[:, o_glu:].astype(BF16)
    row2 = lambda v: v.astype(F32).reshape(1, -1)

    tm1 = _pick_tile(lp, min(832, seq), BF16_ROWS)
    nt1 = lp // tm1
    assert n_head % SUBLANES == 0 and n_head <= tm1 <= seq
    rspec = lambda cols, tm, nt: pl.BlockSpec((tm, cols), lambda b, j: (b * nt + j, 0))
    full = lambda a: pl.BlockSpec(a.shape, lambda *_: (0,) * a.ndim)
    full1 = lambda a: pl.BlockSpec(a.shape, lambda *_: (0,) * a.ndim, pipeline_mode=pl.Buffered(1))
    x_rows = pl.BlockSpec((pl.Element(1), pl.Element(tm1), pl.Element(d)),
                          lambda b, j: (b, pl.multiple_of(jnp.maximum(j * tm1 - n_head, 0), SUBLANES), 0))
    front_in = [head, x, row2(norm_mix_g[0]), w_qkv, w_ab, w_go, w_glu, w_merge[0].astype(BF16),
                row2(b_glu[0]), row2(b_merge[0]), _pad_lanes(a_log[0]), _pad_lanes(dt_bias[0])]
    hp, qkv_pre, gbeta, go_act, u_glu, gates = pl.pallas_call(
        functools.partial(_front_kernel, tm=tm1, meta_pad=meta_pad, heads=heads, conv_ch=conv_ch),
        grid=(bsz, nt1),
        in_specs=[full1(head), x_rows] + [full1(a) for a in front_in[2:]],
        out_specs=[rspec(d, tm1, nt1), rspec(o_a, tm1, nt1), rspec(LANES, tm1, nt1), rspec(vw, tm1, nt1),
                   rspec(conv_ch, tm1, nt1), rspec(2 * d, tm1, nt1)],
        out_shape=[jax.ShapeDtypeStruct((rows, d), F32),
                   jax.ShapeDtypeStruct((rows, o_a), BF16), jax.ShapeDtypeStruct((rows, LANES), F32),
                   jax.ShapeDtypeStruct((rows, vw), BF16), jax.ShapeDtypeStruct((rows, conv_ch), BF16),
                   jax.ShapeDtypeStruct((rows, 2 * d), BF16)],
        compiler_params=_params(("arbitrary", "arbitrary")),
        name="front",
    )(*front_in)

    tmc = _pick_tile(lp, 832, 32)
    ntc = lp // tmc
    halo, lead = 32, BF16_ROWS
    conv_w8 = jnp.repeat(conv_dw_w[0].astype(F32), SUBLANES, axis=0)
    conv_in = [u_glu, gates, conv_w8, row2(conv_dw_b[0]), row2(conv_ln_g[0]),
               row2(conv_ln_b[0]), w_conv_out[0].astype(BF16), row2(b_conv_out[0])]
    ybg = pl.pallas_call(
        functools.partial(_conv_kernel, tm=tmc, kw=kw, halo=halo, lead=lead, rb=32),
        grid=(bsz, ntc),
        in_specs=[rspec(conv_ch, tmc, ntc),
                  pl.BlockSpec((tmc, d), lambda b, j: (b * ntc + j, 1))] + [full(a) for a in conv_in[2:]],
        out_specs=rspec(d, tmc, ntc),
        out_shape=jax.ShapeDtypeStruct((rows, d), BF16),
        scratch_shapes=[pltpu.VMEM((lead + halo + tmc, conv_ch), BF16),
                        pltpu.VMEM((SUBLANES, lead + halo + tmc, conv_ch), F32),
                        pltpu.VMEM((tmc, conv_ch), F32)],
        compiler_params=_params(("arbitrary", "arbitrary")),
        name="convmod",
    )(*conv_in)

    nc = lp // CHUNK
    nbd = max(n for n in (1, 2, 4, 8) if bsz % n == 0)
    cspec = lambda cols: pl.BlockSpec((nbd, CHUNK, cols), lambda b, c: (b, c, 0))
    delta_in = [qkv_pre.reshape(bsz, lp, o_a), gbeta.reshape(bsz, lp, LANES), go_act.reshape(bsz, lp, vw),
                conv_qkv_w[0].astype(F32), row2(dn_norm_g[0])]
    o_dn = pl.pallas_call(
        functools.partial(_delta_kernel, nb=nbd, heads=heads, dk=dk, dv=dv, ch=CHUNK, ksz=ksz, halo=BF16_ROWS),
        grid=(bsz // nbd, nc),
        in_specs=[cspec(o_a), cspec(LANES), cspec(vw), full(delta_in[3]), full(delta_in[4])],
        out_specs=cspec(vw),
        out_shape=jax.ShapeDtypeStruct((bsz, lp, vw), BF16),
        scratch_shapes=[pltpu.VMEM((nbd, heads, dk, dv), F32),
                        pltpu.VMEM((nbd, CHUNK + BF16_ROWS, o_a), BF16)],
        compiler_params=_params(("arbitrary", "arbitrary")),
        name="delta",
    )(*delta_in).reshape(rows, vw)

    tm3 = _pick_tile(lp, MID_TILE, BF16_ROWS)
    nt3 = lp // tm3
    w_r = jnp.pad(w_router[0], ((0, 0), (0, LANES - n_exp))).astype(BF16)
    mid_in = [hp, o_dn, gates, ybg, w_dn_out[0].astype(BF16), w_out[0].astype(BF16),
              row2(norm_ffn_g[0]), w_r, _pad_lanes(b_router[0])]
    h2, hn2, topi, topg = pl.pallas_call(
        functools.partial(_mid_kernel, tm=tm3, n_exp=n_exp, parts=MID_PARTS if tm3 % (MID_PARTS * BF16_ROWS) == 0 else 1),
        grid=(bsz, nt3),
        in_specs=[rspec(d, tm3, nt3), rspec(vw, tm3, nt3), rspec(d, tm3, nt3), rspec(d, tm3, nt3)]
        + [full(a) for a in mid_in[4:]],
        out_specs=[rspec(d, tm3, nt3), rspec(d, tm3, nt3), rspec(LANES, tm3, nt3), rspec(LANES, tm3, nt3)],
        out_shape=[jax.ShapeDtypeStruct((rows, d), F32), jax.ShapeDtypeStruct((rows, d), F32),
                   jax.ShapeDtypeStruct((rows, LANES), I32), jax.ShapeDtypeStruct((rows, LANES), F32)],
        compiler_params=_params(("arbitrary", "arbitrary")),
        name="mid",
    )(*mid_in)

    tmd = _pick_tile(seq, TOKEN_TILE, SUBLANES)
    tpb = seq // tmd
    ntd = bsz * tpb
    tokens = bsz * seq
    width = tmd * TOP_K + n_exp * SUBLANES
    row0 = lambda i: pl.multiple_of((i // tpb) * lp + n_head + (i % tpb) * tmd, SUBLANES)
    prow = lambda r, cols, f=row0: pl.BlockSpec((pl.Element(r), pl.Element(cols)), lambda i, *_: (f(i), 0))
    tspec = pl.BlockSpec((tmd, LANES), lambda i, *_: (i, 0))
    ptiles = _pick_tile(tpb, 8, 1)
    pspec = pl.BlockSpec((ptiles * tmd, LANES), lambda i: (i, 0))
    tab = pl.BlockSpec((ptiles, 1, LANES), lambda i: (i, 0, 0))
    tab_shape = jax.ShapeDtypeStruct((ntd, 1, LANES), I32)
    pos, pos_t, n8_t, off8_t, carry8_t, tot8 = pl.pallas_call(
        functools.partial(_plan_kernel, tm=tmd, tiles=ptiles, n_exp=n_exp),
        grid=(ntd // ptiles,),
        in_specs=[prow(ptiles * tmd, LANES, lambda i: row0(i * ptiles))],
        out_specs=[pspec, pl.BlockSpec((ptiles, SUBLANES, tmd), lambda i: (i, 0, 0)), tab, tab, tab,
                   pl.BlockSpec((1, LANES), lambda i: (0, 0))],
        out_shape=[jax.ShapeDtypeStruct((tokens, LANES), I32), jax.ShapeDtypeStruct((ntd, SUBLANES, tmd), I32),
                   tab_shape, tab_shape, tab_shape, jax.ShapeDtypeStruct((1, LANES), F32)],
        scratch_shapes=[pltpu.VMEM((1, LANES), F32)],
        compiler_params=_params(("arbitrary",)),
        name="plan",
    )(topi)

    n_blocks = -(-(tokens * TOP_K + n_exp * (SUBLANES - 1) * ntd) // MOE_BLOCK) + n_exp
    cap = n_blocks * MOE_BLOCK
    counts = tot8[0, :n_exp].astype(I32)
    padded = (counts + MOE_BLOCK - 1) // MOE_BLOCK * MOE_BLOCK
    pend = jnp.cumsum(padded).astype(I32)
    pstart = pend - padded
    n_used = (pend[-1] // MOE_BLOCK).astype(I32).reshape(1)
    block_start = jnp.arange(n_blocks, dtype=I32) * MOE_BLOCK
    block_e = jnp.minimum(jnp.sum((pend[None, :] <= block_start[:, None]).astype(I32), axis=1), n_exp - 1)
    dbase = (carry8_t.reshape(ntd, LANES) + jnp.pad(pstart, (0, LANES - n_exp))[None, :]).reshape(ntd * LANES)
    n8_f = n8_t.reshape(ntd * LANES)
    off8_f = off8_t.reshape(ntd * LANES)

    anyspec = pl.BlockSpec(memory_space=pl.ANY)
    stab = lambda f: pl.BlockSpec((LANES,), lambda i, *_: (f(i),), memory_space=pltpu.SMEM)
    cur = lambda i: i
    xs = pl.pallas_call(
        functools.partial(_dispatch_kernel, tm=tmd, width=width, n_exp=n_exp, n_steps=ntd, n_blocks=n_blocks),
        grid_spec=pltpu.PrefetchScalarGridSpec(
            num_scalar_prefetch=2, grid=(ntd,),
            in_specs=[stab(cur), stab(cur), stab(cur), prow(tmd, d),
                      pl.BlockSpec((1, SUBLANES, tmd), lambda i, *_: (i, 0, 0))],
            out_specs=anyspec,
            scratch_shapes=[pltpu.VMEM((MOE_BLOCK, d), F32), pltpu.VMEM((2, width, d), F32),
                            pltpu.SMEM((2,), I32), pltpu.SemaphoreType.DMA(()), pltpu.SemaphoreType.DMA((2,))]),
        out_shape=jax.ShapeDtypeStruct((cap, d), F32),
        compiler_params=_params(("arbitrary",)),
        name="dispatch",
    )(pend, padded, n8_f, off8_f, dbase, hn2, pos_t)

    last_used = lambda i, nbu: jnp.minimum(i, nbu[0] - 1)
    ew = lambda a: pl.BlockSpec((1,) + a.shape[1:], lambda i, be, nbu, fl: (be[i],) + (0,) * (a.ndim - 1))
    e_in = [w_gate_up[0], b_gate_up[0].astype(F32)[:, None, :], w_down[0], b_down[0].astype(F32)[:, None, :]]
    block_fill = jnp.clip((pstart + counts)[block_e] - block_start, 0, MOE_BLOCK).astype(I32)
    ys = pl.pallas_call(
        functools.partial(_expert_kernel, d_ff=d_ff, parts=2),
        grid_spec=pltpu.PrefetchScalarGridSpec(
            num_scalar_prefetch=3, grid=(n_blocks,),
            in_specs=[pl.BlockSpec((MOE_BLOCK, d), lambda i, be, nbu, fl: (last_used(i, nbu), 0))]
            + [ew(a) for a in e_in],
            out_specs=pl.BlockSpec((MOE_BLOCK, d), lambda i, be, nbu, fl: (i, 0)),
            scratch_shapes=[pltpu.VMEM((d, 2 * d_ff), BF16), pltpu.VMEM((d_ff, d), BF16)]),
        out_shape=jax.ShapeDtypeStruct((cap, d), F32),
        compiler_params=_params(("arbitrary",)),
        name="experts",
    )(block_e, n_used, block_fill, xs, *e_in)

    nxt = lambda i: jnp.minimum(i + 1, ntd - 1)
    out = pl.pallas_call(
        functools.partial(_combine_kernel, tm=tmd, width=width, n_exp=n_exp, n_steps=ntd),
        grid=(ntd,),
        in_specs=[stab(cur), stab(cur), stab(cur), stab(nxt), stab(nxt), stab(nxt), tspec,
                  prow(tmd, d), prow(tmd, LANES), pl.BlockSpec((1, d), lambda i: (0, 0)), anyspec],
        out_specs=pl.BlockSpec((tmd, d), lambda i: (i, 0)),
        out_shape=jax.ShapeDtypeStruct((tokens, d), x.dtype),
        scratch_shapes=[pltpu.VMEM((2, width, d), F32), pltpu.SMEM((2,), I32), pltpu.SemaphoreType.DMA((2,))],
        compiler_params=_params(("arbitrary",)),
        name="combine",
    )(n8_f, off8_f, dbase, n8_f, off8_f, dbase, pos, h2, topg, row2(final_norm_g), ys)
    return out.reshape(bsz, seq, d)
```

```python
import functools

import jax
import jax.numpy as jnp
from jax import lax
from jax.experimental import pallas as pl
from jax.experimental.pallas import tpu as pltpu

F32 = jnp.float32
BF16 = jnp.bfloat16
I32 = jnp.int32

CHUNK = 64
TOP_K = 4
MOE_BLOCK = 512
TOKEN_TILE = 256
MID_TILE = 832
MID_PARTS = 4
SWIGLU_LIMIT = 7.0
SWIGLU_ALPHA = 1.702
EPS = 1e-6
LN_EPS = 1e-5
LANES = 128
SUBLANES = 8
BF16_ROWS = 16
VMEM_LIMIT = 56 * 1024 * 1024


def _dot(a, b):
    return jnp.dot(a, b, preferred_element_type=F32)


def _dot_nt(a, b):
    return lax.dot_general(a, b, (((1,), (1,)), ((), ())), preferred_element_type=F32)


def _dot_tn(a, b):
    return lax.dot_general(a, b, (((0,), (0,)), ((), ())), preferred_element_type=F32)


def _sigmoid(x):
    return 1.0 / (1.0 + jnp.exp(-x))


def _silu(x):
    return x * _sigmoid(x)


def _softplus(x):
    return jnp.maximum(x, 0.0) + jnp.log(1.0 + jnp.exp(-jnp.abs(x)))


def _rms(x, g):
    return x * lax.rsqrt(jnp.mean(x * x, axis=-1, keepdims=True) + EPS) * g


def _iota(shape, dim):
    return lax.broadcasted_iota(I32, shape, dim)


def _pick_tile(n, target, mult):
    best = None
    for t in range(mult, min(n, target) + 1, mult):
        if n % t == 0:
            best = t
    assert best is not None, (n, target, mult)
    return best


def _params(sem, flags=None):
    return pltpu.CompilerParams(dimension_semantics=sem, vmem_limit_bytes=VMEM_LIMIT, flags=flags)


def _front_kernel(head_ref, x_ref, g_ref, wqkv_ref, wab_ref, wgo_ref, wglu_ref, wmg_ref, bglu_ref, bmg_ref,
                  alog_ref, dtb_ref, h_ref, qkv_ref, gb_ref, go_ref, u_ref, gates_ref,
                  *, tm, meta_pad, heads, conv_ch):
    j = pl.program_id(1)
    xb = x_ref[0]
    n_head = head_ref.shape[0]
    h = jnp.where(j == 0, jnp.concatenate([head_ref[...], xb[:tm - n_head]], axis=0), xb)
    h_ref[...] = h
    hn = _rms(h, g_ref[...]).astype(BF16)
    qkv_ref[...] = _dot(hn, wqkv_ref[...]).astype(BF16)
    ab = _dot(hn, wab_ref[...])
    valid = (j * tm + _iota((tm, 1), 0)) >= meta_pad
    lane = _iota((1, LANES), 1)
    g = -jnp.exp(alog_ref[...]) * _softplus(ab + dtb_ref[...])
    gb = jnp.where(lane < heads, g, _sigmoid(ab))
    gb_ref[...] = jnp.where(valid, gb, 0.0)
    go_ref[...] = _silu(_dot(hn, wgo_ref[...])).astype(BF16)
    glu = _dot(hn, wglu_ref[...]) + bglu_ref[...]
    u = glu[:, :conv_ch] * _sigmoid(glu[:, conv_ch:])
    u_ref[...] = jnp.where(valid, u, 0.0).astype(BF16)
    gates_ref[...] = _sigmoid(_dot(hn, wmg_ref[...]) + bmg_ref[...]).astype(BF16)


def _conv_kernel(u_ref, gate_ref, w_ref, b_ref, lng_ref, lnb_ref, wo_ref, bo_ref, y_ref,
                 xcat_ref, xs_ref, cbuf_ref, *, tm, kw, halo, lead, rb):
    j = pl.program_id(1)
    base = lead + halo
    total = base + tm
    n_c = xcat_ref.shape[1]

    @pl.when(j == 0)
    def _():
        xcat_ref[0:base, :] = jnp.zeros((base, n_c), BF16)

    @pl.when(j > 0)
    def _():
        xcat_ref[lead:base, :] = xcat_ref[tm + lead:tm + base, :]

    xcat_ref[base:total, :] = u_ref[...]
    sr = _iota((SUBLANES * rb, lead + rb), 0)
    sc = _iota((SUBLANES * rb, lead + rb), 1)
    shift = (sc == (sr & (rb - 1)) + lead - (sr >> (rb.bit_length() - 1))).astype(BF16)
    bias = b_ref[...]
    for t0 in range(lead, total, rb):
        res = _dot(shift, xcat_ref[t0 - lead:t0 + rb, :])
        for r in range(SUBLANES):
            xs_ref[r, t0:t0 + rb, :] = res[r * rb:(r + 1) * rb, :]
        if t0 < base:
            continue
        i0 = t0
        blk = (i0 - base) // rb
        acc = jnp.zeros((rb, n_c), F32)
        for s in range(kw):
            a, r = divmod(s, SUBLANES)
            wt = w_ref[(kw - 1 - s) * SUBLANES:(kw - s) * SUBLANES, :]
            acc = acc + xs_ref[r, i0 - SUBLANES * a:i0 - SUBLANES * a + rb, :] * jnp.concatenate(
                [wt] * (rb // SUBLANES), axis=0)
        cbuf_ref[blk * rb:(blk + 1) * rb, :] = acc + bias
    c = cbuf_ref[...]
    mu = jnp.mean(c, axis=-1, keepdims=True)
    xc = c - mu
    ln = xc * lax.rsqrt(jnp.mean(xc * xc, axis=-1, keepdims=True) + LN_EPS) * lng_ref[...] + lnb_ref[...]
    yb = _dot(_silu(ln).astype(BF16), wo_ref[...]) + bo_ref[...]
    y_ref[...] = (gate_ref[...].astype(F32) * yb).astype(BF16)


def _bdot(a, b):
    return jnp.dot(a.astype(BF16), b.astype(BF16), preferred_element_type=F32)


def _split3(x):
    x1 = x.astype(BF16)
    r1 = x - x1.astype(F32)
    x2 = r1.astype(BF16)
    x3 = (r1 - x2.astype(F32)).astype(BF16)
    return x1, x2, x3


def _unit_lower_inverses_minus_eye(mats, row, col):
    ch = mats[0].shape[0]
    blk16 = (row >> 4) == (col >> 4)
    blk32 = (row >> 5) == (col >> 5)
    n1 = [jnp.where(blk16, a, 0.0) for a in mats]
    n2 = [_bdot(n, n) for n in n1]
    r = [_bdot(jnp.concatenate([n, m], axis=0), m) for n, m in zip(n1, n2)]
    n4 = [x[ch:] for x in r]
    y = [m - n - x[:ch] for n, m, x in zip(n1, n2, r)]
    r = [_bdot(jnp.concatenate([yy, m], axis=0), m) for yy, m in zip(y, n4)]
    n8 = [x[ch:] for x in r]
    y = [yy + m + x[:ch] for yy, m, x in zip(y, n4, r)]
    y = [yy + m + _bdot(yy, m) for yy, m in zip(y, n8)]
    for mask in (blk32 & jnp.logical_not(blk16), jnp.logical_not(blk32)):
        ls = [jnp.where(mask, a, 0.0) for a in mats]
        ms = [l + _bdot(l, yy) for l, yy in zip(ls, y)]
        y = [yy - m - _bdot(yy, m) for yy, m in zip(y, ms)]
    return y


def _delta_kernel(qkv_ref, gb_ref, go_ref, cw_ref, ng_ref, o_ref, s_ref, xc_ref,
                  *, nb, heads, dk, dv, ch, ksz, halo):
    c = pl.program_id(1)
    qk_w = heads * dk

    @pl.when(c == 0)
    def _():
        s_ref[...] = jnp.zeros(s_ref.shape, F32)
        xc_ref[:, 0:halo, :] = jnp.zeros((nb, halo, xc_ref.shape[2]), BF16)

    @pl.when(c > 0)
    def _():
        xc_ref[:, 0:halo, :] = xc_ref[:, ch:ch + halo, :]

    xc_ref[:, halo:halo + ch, :] = qkv_ref[...]
    row = _iota((ch, ch), 0)
    col = _iota((ch, ch), 1)
    incl = row >= col
    strict = row > col
    tril = incl.astype(BF16)
    triu = (row <= col).astype(BF16)
    sr = _iota(((ksz - 1) * ch, halo + ch), 0)
    sc = _iota(((ksz - 1) * ch, halo + ch), 1)
    shift = (sc == (sr & (ch - 1)) + (sr >> (ch.bit_length() - 1)) + (halo - (ksz - 1))).astype(BF16)

    qn, kn, kb, vb, gamma, eg, ekd, elast = [], [], [], [], [], [], [], []
    shifted = [_dot(shift, xc_ref[b]) for b in range(nb)]
    for b in range(nb):
        acc = xc_ref[b, halo:halo + ch, :].astype(F32) * cw_ref[ksz - 1:ksz, :]
        for t in range(ksz - 1):
            acc = acc + shifted[b][t * ch:(t + 1) * ch, :] * cw_ref[t:t + 1, :]
        qkv = _silu(acc)
        gb = gb_ref[b]
        parts = _split3(gb)
        gc = sum(_dot(tril, p) for p in parts)
        gct = sum(_dot_tn(p, triu) for p in parts)
        glast = gc[ch - 1:ch, :]
        e_g = jnp.exp(gc)
        e_kd = jnp.exp(glast - gc)
        e_last = jnp.exp(glast)
        for h in range(heads):
            q = qkv[:, h * dk:(h + 1) * dk]
            k = qkv[:, qk_w + h * dk:qk_w + (h + 1) * dk]
            v = qkv[:, 2 * qk_w + h * dv:2 * qk_w + (h + 1) * dv]
            qn.append(q * lax.rsqrt(jnp.sum(q * q, axis=-1, keepdims=True) + EPS) * (dk ** -0.5))
            kn.append(k * lax.rsqrt(jnp.sum(k * k, axis=-1, keepdims=True) + EPS))
            beta = gb[:, heads + h:heads + h + 1]
            kb.append(kn[-1] * beta)
            vb.append(v * beta)
            diff = gc[:, h:h + 1] - gct[h:h + 1, :]
            gamma.append(jnp.where(incl, jnp.exp(jnp.where(incl, diff, 0.0)), 0.0))
            eg.append(e_g[:, h:h + 1])
            ekd.append(e_kd[:, h:h + 1])
            elast.append(e_last[:, h:h + 1])

    n = nb * heads
    kq = [_dot_nt(jnp.concatenate([kb[i], qn[i]], axis=0).astype(BF16), kn[i].astype(BF16))
          for i in range(n)]
    a = [jnp.where(strict, kq[i][:ch] * gamma[i], 0.0) for i in range(n)]
    qk = [kq[i][ch:] * gamma[i] for i in range(n)]
    y = _unit_lower_inverses_minus_eye(a, row, col)
    rhs = [jnp.concatenate([vb[i], kb[i] * eg[i]], axis=-1) for i in range(n)]
    uw = [rhs[i] + _bdot(y[i], rhs[i]) for i in range(n)]
    s = [s_ref[i // heads, i % heads] for i in range(n)]
    ws = [_bdot(jnp.concatenate([uw[i][:, dv:], qn[i] * eg[i]], axis=0), s[i]) for i in range(n)]
    v_new = [uw[i][:, :dv] - ws[i][:ch] for i in range(n)]
    o = [ws[i][ch:] + _bdot(qk[i], v_new[i]) for i in range(n)]
    s_new = [s[i] * elast[i] + _dot_tn((kn[i] * ekd[i]).astype(BF16), v_new[i].astype(BF16))
             for i in range(n)]
    for i in range(n):
        b, h = i // heads, i % heads
        s_ref[b, h] = s_new[i]
        on = _rms(o[i], ng_ref[...])
        o_ref[b, :, h * dv:(h + 1) * dv] = (
            on * go_ref[b, :, h * dv:(h + 1) * dv].astype(F32)).astype(BF16)


def _mid_kernel(h_ref, o_ref, ga_ref, ybg_ref, wdn_ref, wout_ref, g2_ref, wr_ref, br_ref,
                h2_ref, hn_ref, topi_ref, topg_ref, *, tm, n_exp, parts):
    rs = tm // parts
    sl = [pl.ds(p * rs, rs) for p in range(parts)]
    ya = [_dot(o_ref[s, :], wdn_ref[...]) for s in sl]
    y = [ga_ref[s, :].astype(F32) * a + ybg_ref[s, :].astype(F32) for s, a in zip(sl, ya)]
    h2 = [h_ref[s, :] + _dot(v.astype(BF16), wout_ref[...]) for s, v in zip(sl, y)]
    hn = [_rms(v, g2_ref[...]) for v in h2]
    for s, v, w in zip(sl, h2, hn):
        h2_ref[s, :] = v
        hn_ref[s, :] = w
    logits = [_dot(v.astype(BF16), wr_ref[...]) + br_ref[...] for v in hn]
    lane = _iota((rs, LANES), 1)
    lane_f = lane.astype(F32)
    l = [jnp.where(lane < n_exp, v, -jnp.inf) for v in logits]
    vals = [[] for _ in range(parts)]
    idxs = [[] for _ in range(parts)]
    for _ in range(TOP_K):
        m = [jnp.max(v, axis=-1, keepdims=True) for v in l]
        idx = [jnp.min(jnp.where(v == mm, lane_f, float(LANES)), axis=-1, keepdims=True)
               for v, mm in zip(l, m)]
        l = [jnp.where(lane_f == ii, -jnp.inf, v) for v, ii in zip(l, idx)]
        for p in range(parts):
            vals[p].append(m[p])
            idxs[p].append(idx[p])
    for p in range(parts):
        es = [jnp.exp(v - vals[p][0]) for v in vals[p]]
        den = es[0]
        for e in es[1:]:
            den = den + e
        topi = jnp.zeros((rs, LANES), F32)
        topg = jnp.zeros((rs, LANES), F32)
        for k in range(TOP_K):
            topi = jnp.where(lane == k, idxs[p][k], topi)
            topg = jnp.where(lane == k, es[k] / den, topg)
        topi_ref[sl[p], :] = topi.astype(I32)
        topg_ref[sl[p], :] = topg


def _choice_mask(topi, lane, n_exp):
    m = jnp.zeros(topi.shape, F32)
    for k in range(TOP_K):
        m = m + (lane == topi[:, k:k + 1]).astype(F32)
    return jnp.where(lane < n_exp, m, 0.0)


def _plan_kernel(topi_ref, pos_ref, post_ref, n8_ref, off8_ref, carry8_ref, tot8_ref, carry_ref,
                 *, tm, tiles, n_exp):
    i = pl.program_id(0)

    @pl.when(i == 0)
    def _():
        carry_ref[...] = jnp.zeros(carry_ref.shape, F32)

    lane = _iota((tm, LANES), 1)
    tri = (_iota((tm, tm), 0) >= _iota((tm, tm), 1)).astype(BF16)
    before = (_iota((LANES, LANES), 0) < _iota((LANES, LANES), 1)).astype(BF16)
    js = range(tiles)
    rows = [pl.ds(j * tm, tm) for j in js]
    topi = [topi_ref[r, :] for r in rows]
    m = [_choice_mask(t, lane, n_exp) for t in topi]
    cum = [_dot(tri, v.astype(BF16)) for v in m]
    n8 = [jnp.floor((c[tm - 1:tm, :] + (SUBLANES - 1)) * (1.0 / SUBLANES)) * SUBLANES for c in cum]
    off8 = [_dot(jnp.broadcast_to(v, (SUBLANES, LANES)).astype(BF16), before)[0:1] for v in n8]
    posf = [c - v + o for c, v, o in zip(cum, m, off8)]
    pos = [jnp.zeros((tm, LANES), F32) for _ in js]
    for k in range(TOP_K):
        pk = [jnp.sum(jnp.where(lane == t[:, k:k + 1], p, 0.0), axis=-1, keepdims=True)
              for t, p in zip(topi, posf)]
        pos = [jnp.where(lane == k, a, b) for a, b in zip(pk, pos)]
    carry = carry_ref[...]
    for j in js:
        pos_ref[rows[j], :] = pos[j].astype(I32)
        post_ref[j] = jnp.transpose(pos[j])[0:SUBLANES, :].astype(I32)
        n8_ref[j] = n8[j].astype(I32)
        off8_ref[j] = off8[j].astype(I32)
        carry8_ref[j] = carry.astype(I32)
        carry = carry + n8[j]
    carry_ref[...] = carry
    tot8_ref[...] = carry


def _pow2_floor(n):
    return 1 << (n.bit_length() - 1)


def _strip_pieces(n8, max_rows):
    pieces = []
    b = max_rows
    while b >= SUBLANES:
        pieces.append(((n8 & b) != 0, n8 & ~(2 * b - 1), b))
        b //= 2
    return pieces


LONG_PIECE = 32


def _copy_strip(n8, start_piece):
    n_long = n8 >> (LONG_PIECE.bit_length() - 1)

    def long_piece(j, carry):
        start_piece(j * LONG_PIECE, LONG_PIECE)
        return carry

    lax.fori_loop(0, n_long, long_piece, 0)
    for cond, ofs, rows in _strip_pieces(n8 & (LONG_PIECE - 1), LONG_PIECE // 2):
        @pl.when(cond)
        def _():
            start_piece(n_long * LONG_PIECE + ofs, rows)


def _wait_rows(total, make_copy, max_rows):
    b = max_rows
    while b >= SUBLANES:
        @pl.when((total & b) != 0)
        def _():
            make_copy(b).wait()
        b //= 2


def _aligned(start, rows):
    return pl.ds(pl.multiple_of(start, SUBLANES), rows)


def _dispatch_kernel(pend_ref, padded_ref, n8_ref, off8_ref, dbase_ref, hn_ref, post_ref, xs_ref,
                     zbuf_ref, sbuf_ref, tot_ref, zsem, sems, *, tm, width, n_exp, n_steps, n_blocks):
    i = pl.program_id(0)
    slot = i % 2

    def zero_block(start):
        return pltpu.make_async_copy(zbuf_ref, xs_ref.at[_aligned(start, MOE_BLOCK), :], zsem)

    @pl.when(i == 0)
    def _():
        zbuf_ref[...] = jnp.zeros(zbuf_ref.shape, F32)
        for e in range(n_exp):
            @pl.when(padded_ref[e] > 0)
            def _():
                zero_block(pend_ref[e] - MOE_BLOCK).start()
        first_unused = pend_ref[n_exp - 1] // MOE_BLOCK

        def start_tail(j, carry):
            zero_block(j * MOE_BLOCK).start()
            return carry

        lax.fori_loop(first_unused, n_blocks, start_tail, 0)
        for e in range(n_exp):
            @pl.when(padded_ref[e] > 0)
            def _():
                zero_block(0).wait()

        def wait_tail(j, carry):
            zero_block(0).wait()
            return carry

        lax.fori_loop(first_unused, n_blocks, wait_tail, 0)

    def strip_wait(s):
        _wait_rows(tot_ref[s], lambda b: pltpu.make_async_copy(
            sbuf_ref.at[s, pl.ds(0, b), :], xs_ref.at[pl.ds(0, b), :], sems.at[s]), _pow2_floor(width))

    @pl.when(i >= 2)
    def _():
        strip_wait(slot)

    post = post_ref[0]
    orow = _iota((width, tm), 0)
    sel = orow == post[0:1, :]
    for k in range(1, TOP_K):
        sel = jnp.logical_or(sel, orow == post[k:k + 1, :])
    sbuf_ref[slot] = _dot(sel.astype(BF16), hn_ref[...].astype(BF16))

    total = 0
    for e in range(n_exp):
        n8, so, do = n8_ref[e], off8_ref[e], dbase_ref[e]
        total = total + n8
        _copy_strip(n8, lambda ofs, rows, so=so, do=do: pltpu.make_async_copy(
            sbuf_ref.at[slot, _aligned(so + ofs, rows), :], xs_ref.at[_aligned(do + ofs, rows), :],
            sems.at[slot]).start())
    tot_ref[slot] = total

    @pl.when(i == n_steps - 1)
    def _():
        if n_steps >= 2:
            strip_wait(1 - slot)
        strip_wait(slot)


def _expert_kernel(be_ref, nbu_ref, x_ref, wgu_ref, bgu_ref, wd_ref, bd_ref, y_ref, wgu16_ref, wd16_ref,
                   *, d_ff, parts):
    i = pl.program_id(0)
    used = i < nbu_ref[0]
    new_expert = jnp.logical_or(i == 0, be_ref[i] != be_ref[jnp.maximum(i - 1, 0)])

    @pl.when(jnp.logical_and(used, new_expert))
    def _():
        wgu16_ref[...] = wgu_ref[0].astype(BF16)
        wd16_ref[...] = wd_ref[0].astype(BF16)

    @pl.when(used)
    def _():
        rs = x_ref.shape[0] // parts
        sl = [pl.ds(p * rs, rs) for p in range(parts)]
        gu = [_dot(x_ref[s, :].astype(BF16), wgu16_ref[...]) + bgu_ref[0] for s in sl]
        act = []
        for v in gu:
            gate = jnp.minimum(v[:, :d_ff], SWIGLU_LIMIT)
            up = jnp.clip(v[:, d_ff:], -SWIGLU_LIMIT, SWIGLU_LIMIT)
            act.append(((up + 1.0) * gate * _sigmoid(SWIGLU_ALPHA * gate)).astype(BF16))
        for s, a in zip(sl, act):
            y_ref[s, :] = _dot(a, wd16_ref[...]) + bd_ref[0]

    @pl.when(jnp.logical_not(used))
    def _():
        y_ref[...] = jnp.zeros(y_ref.shape, F32)


def _combine_kernel(n8c_ref, off8c_ref, dbc_ref, n8n_ref, off8n_ref, dbn_ref, pos_ref, h2_ref, topg_ref, fg_ref,
                    ys_ref, out_ref, gbuf_ref, tot_ref, sems, *, tm, width, n_exp, n_steps):
    i = pl.program_id(0)
    slot = i % 2

    def fetch(n8_ref, off8_ref, db_ref, s):
        total = 0
        for e in range(n_exp):
            n8, so, do = n8_ref[e], off8_ref[e], db_ref[e]
            total = total + n8
            _copy_strip(n8, lambda ofs, rows, so=so, do=do: pltpu.make_async_copy(
                ys_ref.at[_aligned(do + ofs, rows), :], gbuf_ref.at[s, _aligned(so + ofs, rows), :],
                sems.at[s]).start())
        tot_ref[s] = total

    @pl.when(i == 0)
    def _():
        gbuf_ref[...] = jnp.zeros(gbuf_ref.shape, F32)
        fetch(n8c_ref, off8c_ref, dbc_ref, 0)

    @pl.when(i + 1 < n_steps)
    def _():
        fetch(n8n_ref, off8n_ref, dbn_ref, 1 - slot)

    _wait_rows(tot_ref[slot], lambda b: pltpu.make_async_copy(
        ys_ref.at[pl.ds(0, b), :], gbuf_ref.at[slot, pl.ds(0, b), :], sems.at[slot]), _pow2_floor(width))
    pos = pos_ref[...]
    topg = topg_ref[...]
    lane = _iota((tm, width), 1)
    g = jnp.zeros((tm, width), F32)
    for k in range(TOP_K):
        g = g + jnp.where(lane == pos[:, k:k + 1], topg[:, k:k + 1], 0.0)
    moe = _dot(g.astype(BF16), gbuf_ref[slot].astype(BF16))
    out_ref[...] = _rms(h2_ref[...] + moe, fg_ref[...])


def _pad_lanes(v, fill=0.0):
    v = v.astype(F32)
    return jnp.concatenate([v, jnp.full((LANES - v.shape[0],), fill, F32)])[None, :]


def kernel(x, meta_tokens, norm_mix_g, w_in, conv_qkv_w, a_log, dt_bias, dn_norm_g, w_dn_out, b_glu,
           conv_dw_w, conv_dw_b, conv_ln_g, conv_ln_b, w_conv_out, b_conv_out, w_merge, b_merge, w_out,
           norm_ffn_g, w_router, b_router, w_gate_up, b_gate_up, w_down, b_down, final_norm_g):
    bsz, seq, d = x.shape
    depth = w_in.shape[0]
    n_meta = meta_tokens.shape[0]
    heads = a_log.shape[1]
    vw = w_dn_out.shape[1]
    qk_w = (conv_qkv_w.shape[2] - vw) // 2
    dk, dv = qk_w // heads, vw // heads
    ksz = conv_qkv_w.shape[1]
    kw, conv_ch = conv_dw_w.shape[1], conv_dw_w.shape[2]
    n_exp, d_ff = w_gate_up.shape[1], w_down.shape[2]
    meta_pad = CHUNK - n_meta
    lp = seq + n_meta + meta_pad
    assert depth == 1 and lp % CHUNK == 0 and dk == LANES and dv == LANES and d % LANES == 0
    assert 2 * heads <= LANES and n_exp <= LANES and kw - 1 <= 32 and ksz - 1 <= SUBLANES
    rows = bsz * lp
    n_head = meta_pad + n_meta
    head = jnp.concatenate([jnp.zeros((meta_pad, d), x.dtype), meta_tokens.astype(x.dtype)], axis=0)

    wi = w_in[0]
    o_a = 2 * qk_w + vw
    o_go = o_a + 2 * heads
    o_glu = o_go + vw
    w_qkv = wi[:, :o_a].astype(BF16)
    w_ab = jnp.pad(wi[:, o_a:o_go], ((0, 0), (0, LANES - 2 * heads))).astype(BF16)
    w_go = wi[:, o_go:o_glu].astype(BF16)
    w_glu = wi[:, o_glu:].astype(BF16)
    row2 = lambda v: v.astype(F32).reshape(1, -1)

    tm1 = _pick_tile(lp, min(832, seq), BF16_ROWS)
    nt1 = lp // tm1
    assert n_head % SUBLANES == 0 and n_head <= tm1 <= seq
    rspec = lambda cols, tm, nt: pl.BlockSpec((tm, cols), lambda b, j: (b * nt + j, 0))
    full = lambda a: pl.BlockSpec(a.shape, lambda *_: (0,) * a.ndim)
    full1 = lambda a: pl.BlockSpec(a.shape, lambda *_: (0,) * a.ndim, pipeline_mode=pl.Buffered(1))
    x_rows = pl.BlockSpec((pl.Element(1), pl.Element(tm1), pl.Element(d)),
                          lambda b, j: (b, pl.multiple_of(jnp.maximum(j * tm1 - n_head, 0), SUBLANES), 0))
    front_in = [head, x, row2(norm_mix_g[0]), w_qkv, w_ab, w_go, w_glu, w_merge[0].astype(BF16),
                row2(b_glu[0]), row2(b_merge[0]), _pad_lanes(a_log[0]), _pad_lanes(dt_bias[0])]
    hp, qkv_pre, gbeta, go_act, u_glu, gates = pl.pallas_call(
        functools.partial(_front_kernel, tm=tm1, meta_pad=meta_pad, heads=heads, conv_ch=conv_ch),
        grid=(bsz, nt1),
        in_specs=[full1(head), x_rows] + [full1(a) for a in front_in[2:]],
        out_specs=[rspec(d, tm1, nt1), rspec(o_a, tm1, nt1), rspec(LANES, tm1, nt1), rspec(vw, tm1, nt1),
                   rspec(conv_ch, tm1, nt1), rspec(2 * d, tm1, nt1)],
        out_shape=[jax.ShapeDtypeStruct((rows, d), F32),
                   jax.ShapeDtypeStruct((rows, o_a), BF16), jax.ShapeDtypeStruct((rows, LANES), F32),
                   jax.ShapeDtypeStruct((rows, vw), BF16), jax.ShapeDtypeStruct((rows, conv_ch), BF16),
                   jax.ShapeDtypeStruct((rows, 2 * d), BF16)],
        compiler_params=_params(("arbitrary", "arbitrary")),
        name="front",
    )(*front_in)

    tmc = _pick_tile(lp, 832, 32)
    ntc = lp // tmc
    halo, lead = 32, BF16_ROWS
    conv_w8 = jnp.repeat(conv_dw_w[0].astype(F32), SUBLANES, axis=0)
    conv_in = [u_glu, gates, conv_w8, row2(conv_dw_b[0]), row2(conv_ln_g[0]),
               row2(conv_ln_b[0]), w_conv_out[0].astype(BF16), row2(b_conv_out[0])]
    ybg = pl.pallas_call(
        functools.partial(_conv_kernel, tm=tmc, kw=kw, halo=halo, lead=lead, rb=32),
        grid=(bsz, ntc),
        in_specs=[rspec(conv_ch, tmc, ntc),
                  pl.BlockSpec((tmc, d), lambda b, j: (b * ntc + j, 1))] + [full(a) for a in conv_in[2:]],
        out_specs=rspec(d, tmc, ntc),
        out_shape=jax.ShapeDtypeStruct((rows, d), BF16),
        scratch_shapes=[pltpu.VMEM((lead + halo + tmc, conv_ch), BF16),
                        pltpu.VMEM((SUBLANES, lead + halo + tmc, conv_ch), F32),
                        pltpu.VMEM((tmc, conv_ch), F32)],
        compiler_params=_params(("arbitrary", "arbitrary")),
        name="convmod",
    )(*conv_in)

    nc = lp // CHUNK
    nbd = max(n for n in (1, 2, 4, 8) if bsz % n == 0)
    cspec = lambda cols: pl.BlockSpec((nbd, CHUNK, cols), lambda b, c: (b, c, 0))
    delta_in = [qkv_pre.reshape(bsz, lp, o_a), gbeta.reshape(bsz, lp, LANES), go_act.reshape(bsz, lp, vw),
                conv_qkv_w[0].astype(F32), row2(dn_norm_g[0])]
    o_dn = pl.pallas_call(
        functools.partial(_delta_kernel, nb=nbd, heads=heads, dk=dk, dv=dv, ch=CHUNK, ksz=ksz, halo=BF16_ROWS),
        grid=(bsz // nbd, nc),
        in_specs=[cspec(o_a), cspec(LANES), cspec(vw), full(delta_in[3]), full(delta_in[4])],
        out_specs=cspec(vw),
        out_shape=jax.ShapeDtypeStruct((bsz, lp, vw), BF16),
        scratch_shapes=[pltpu.VMEM((nbd, heads, dk, dv), F32),
                        pltpu.VMEM((nbd, CHUNK + BF16_ROWS, o_a), BF16)],
        compiler_params=_params(("arbitrary", "arbitrary")),
        name="delta",
    )(*delta_in).reshape(rows, vw)

    tm3 = _pick_tile(lp, MID_TILE, BF16_ROWS)
    nt3 = lp // tm3
    w_r = jnp.pad(w_router[0], ((0, 0), (0, LANES - n_exp))).astype(BF16)
    mid_in = [hp, o_dn, gates, ybg, w_dn_out[0].astype(BF16), w_out[0].astype(BF16),
              row2(norm_ffn_g[0]), w_r, _pad_lanes(b_router[0])]
    h2, hn2, topi, topg = pl.pallas_call(
        functools.partial(_mid_kernel, tm=tm3, n_exp=n_exp, parts=MID_PARTS if tm3 % (MID_PARTS * BF16_ROWS) == 0 else 1),
        grid=(bsz, nt3),
        in_specs=[rspec(d, tm3, nt3), rspec(vw, tm3, nt3), rspec(d, tm3, nt3), rspec(d, tm3, nt3)]
        + [full(a) for a in mid_in[4:]],
        out_specs=[rspec(d, tm3, nt3), rspec(d, tm3, nt3), rspec(LANES, tm3, nt3), rspec(LANES, tm3, nt3)],
        out_shape=[jax.ShapeDtypeStruct((rows, d), F32), jax.ShapeDtypeStruct((rows, d), F32),
                   jax.ShapeDtypeStruct((rows, LANES), I32), jax.ShapeDtypeStruct((rows, LANES), F32)],
        compiler_params=_params(("arbitrary", "arbitrary")),
        name="mid",
    )(*mid_in)

    tmd = _pick_tile(seq, TOKEN_TILE, SUBLANES)
    tpb = seq // tmd
    ntd = bsz * tpb
    tokens = bsz * seq
    width = tmd * TOP_K + n_exp * SUBLANES
    row0 = lambda i: pl.multiple_of((i // tpb) * lp + n_head + (i % tpb) * tmd, SUBLANES)
    prow = lambda r, cols, f=row0: pl.BlockSpec((pl.Element(r), pl.Element(cols)), lambda i, *_: (f(i), 0))
    tspec = pl.BlockSpec((tmd, LANES), lambda i, *_: (i, 0))
    ptiles = _pick_tile(tpb, 8, 1)
    pspec = pl.BlockSpec((ptiles * tmd, LANES), lambda i: (i, 0))
    tab = pl.BlockSpec((ptiles, 1, LANES), lambda i: (i, 0, 0))
    tab_shape = jax.ShapeDtypeStruct((ntd, 1, LANES), I32)
    pos, pos_t, n8_t, off8_t, carry8_t, tot8 = pl.pallas_call(
        functools.partial(_plan_kernel, tm=tmd, tiles=ptiles, n_exp=n_exp),
        grid=(ntd // ptiles,),
        in_specs=[prow(ptiles * tmd, LANES, lambda i: row0(i * ptiles))],
        out_specs=[pspec, pl.BlockSpec((ptiles, SUBLANES, tmd), lambda i: (i, 0, 0)), tab, tab, tab,
                   pl.BlockSpec((1, LANES), lambda i: (0, 0))],
        out_shape=[jax.ShapeDtypeStruct((tokens, LANES), I32), jax.ShapeDtypeStruct((ntd, SUBLANES, tmd), I32),
                   tab_shape, tab_shape, tab_shape, jax.ShapeDtypeStruct((1, LANES), F32)],
        scratch_shapes=[pltpu.VMEM((1, LANES), F32)],
        compiler_params=_params(("arbitrary",)),
        name="plan",
    )(topi)

    n_blocks = -(-(tokens * TOP_K + n_exp * (SUBLANES - 1) * ntd) // MOE_BLOCK) + n_exp
    cap = n_blocks * MOE_BLOCK
    counts = tot8[0, :n_exp].astype(I32)
    padded = (counts + MOE_BLOCK - 1) // MOE_BLOCK * MOE_BLOCK
    pend = jnp.cumsum(padded).astype(I32)
    pstart = pend - padded
    n_used = (pend[-1] // MOE_BLOCK).astype(I32).reshape(1)
    block_start = jnp.arange(n_blocks, dtype=I32) * MOE_BLOCK
    block_e = jnp.minimum(jnp.sum((pend[None, :] <= block_start[:, None]).astype(I32), axis=1), n_exp - 1)
    dbase = (carry8_t.reshape(ntd, LANES) + jnp.pad(pstart, (0, LANES - n_exp))[None, :]).reshape(ntd * LANES)
    n8_f = n8_t.reshape(ntd * LANES)
    off8_f = off8_t.reshape(ntd * LANES)

    anyspec = pl.BlockSpec(memory_space=pl.ANY)
    stab = lambda f: pl.BlockSpec((LANES,), lambda i, *_: (f(i),), memory_space=pltpu.SMEM)
    cur = lambda i: i
    xs = pl.pallas_call(
        functools.partial(_dispatch_kernel, tm=tmd, width=width, n_exp=n_exp, n_steps=ntd, n_blocks=n_blocks),
        grid_spec=pltpu.PrefetchScalarGridSpec(
            num_scalar_prefetch=2, grid=(ntd,),
            in_specs=[stab(cur), stab(cur), stab(cur), prow(tmd, d),
                      pl.BlockSpec((1, SUBLANES, tmd), lambda i, *_: (i, 0, 0))],
            out_specs=anyspec,
            scratch_shapes=[pltpu.VMEM((MOE_BLOCK, d), F32), pltpu.VMEM((2, width, d), F32),
                            pltpu.SMEM((2,), I32), pltpu.SemaphoreType.DMA(()), pltpu.SemaphoreType.DMA((2,))]),
        out_shape=jax.ShapeDtypeStruct((cap, d), F32),
        compiler_params=_params(("arbitrary",)),
        name="dispatch",
    )(pend, padded, n8_f, off8_f, dbase, hn2, pos_t)

    last_used = lambda i, nbu: jnp.minimum(i, nbu[0] - 1)
    ew = lambda a: pl.BlockSpec((1,) + a.shape[1:], lambda i, be, nbu: (be[i],) + (0,) * (a.ndim - 1))
    e_in = [w_gate_up[0], b_gate_up[0].astype(F32)[:, None, :], w_down[0], b_down[0].astype(F32)[:, None, :]]
    ys = pl.pallas_call(
        functools.partial(_expert_kernel, d_ff=d_ff, parts=2),
        grid_spec=pltpu.PrefetchScalarGridSpec(
            num_scalar_prefetch=2, grid=(n_blocks,),
            in_specs=[pl.BlockSpec((MOE_BLOCK, d), lambda i, be, nbu: (last_used(i, nbu), 0))]
            + [ew(a) for a in e_in],
            out_specs=pl.BlockSpec((MOE_BLOCK, d), lambda i, be, nbu: (i, 0)),
            scratch_shapes=[pltpu.VMEM((d, 2 * d_ff), BF16), pltpu.VMEM((d_ff, d), BF16)]),
        out_shape=jax.ShapeDtypeStruct((cap, d), F32),
        compiler_params=_params(("arbitrary",)),
        name="experts",
    )(block_e, n_used, xs, *e_in)

    nxt = lambda i: jnp.minimum(i + 1, ntd - 1)
    out = pl.pallas_call(
        functools.partial(_combine_kernel, tm=tmd, width=width, n_exp=n_exp, n_steps=ntd),
        grid=(ntd,),
        in_specs=[stab(cur), stab(cur), stab(cur), stab(nxt), stab(nxt), stab(nxt), tspec,
                  prow(tmd, d), prow(tmd, LANES), pl.BlockSpec((1, d), lambda i: (0, 0)), anyspec],
        out_specs=pl.BlockSpec((tmd, d), lambda i: (i, 0)),
        out_shape=jax.ShapeDtypeStruct((tokens, d), x.dtype),
        scratch_shapes=[pltpu.VMEM((2, width, d), F32), pltpu.SMEM((2,), I32), pltpu.SemaphoreType.DMA((2,))],
        compiler_params=_params(("arbitrary",)),
        name="combine",
    )(n8_f, off8_f, dbase, n8_f, off8_f, dbase, pos, h2, topg, row2(final_norm_g), ys)
    return out.reshape(bsz, seq, d)
```

```python
import functools

import jax
import jax.numpy as jnp
from jax import lax
from jax.experimental import pallas as pl
from jax.experimental.pallas import tpu as pltpu

F32 = jnp.float32
BF16 = jnp.bfloat16
I32 = jnp.int32

CHUNK = 64
TOP_K = 4
MOE_BLOCK = 512
TOKEN_TILE = 256
MID_TILE = 832
MID_PARTS = 4
SWIGLU_LIMIT = 7.0
SWIGLU_ALPHA = 1.702
EPS = 1e-6
LN_EPS = 1e-5
LANES = 128
SUBLANES = 8
BF16_ROWS = 16
VMEM_LIMIT = 56 * 1024 * 1024


def _dot(a, b):
    return jnp.dot(a, b, preferred_element_type=F32)


def _dot_nt(a, b):
    return lax.dot_general(a, b, (((1,), (1,)), ((), ())), preferred_element_type=F32)


def _dot_tn(a, b):
    return lax.dot_general(a, b, (((0,), (0,)), ((), ())), preferred_element_type=F32)


def _sigmoid(x):
    return 1.0 / (1.0 + jnp.exp(-x))


def _silu(x):
    return x * _sigmoid(x)


def _softplus(x):
    return jnp.maximum(x, 0.0) + jnp.log(1.0 + jnp.exp(-jnp.abs(x)))


def _rms(x, g):
    return x * lax.rsqrt(jnp.mean(x * x, axis=-1, keepdims=True) + EPS) * g


def _iota(shape, dim):
    return lax.broadcasted_iota(I32, shape, dim)


def _pick_tile(n, target, mult):
    best = None
    for t in range(mult, min(n, target) + 1, mult):
        if n % t == 0:
            best = t
    assert best is not None, (n, target, mult)
    return best


def _params(sem, flags=None):
    return pltpu.CompilerParams(dimension_semantics=sem, vmem_limit_bytes=VMEM_LIMIT, flags=flags)


def _front_kernel(head_ref, x_ref, g_ref, wqkv_ref, wab_ref, wgo_ref, wglu_ref, wmg_ref, bglu_ref, bmg_ref,
                  alog_ref, dtb_ref, h_ref, qkv_ref, gb_ref, go_ref, u_ref, gates_ref,
                  *, tm, meta_pad, heads, conv_ch):
    j = pl.program_id(1)
    xb = x_ref[0]
    n_head = head_ref.shape[0]
    h = jnp.where(j == 0, jnp.concatenate([head_ref[...], xb[:tm - n_head]], axis=0), xb)
    h_ref[...] = h
    hn = _rms(h, g_ref[...]).astype(BF16)
    qkv_ref[...] = _dot(hn, wqkv_ref[...]).astype(BF16)
    ab = _dot(hn, wab_ref[...])
    valid = (j * tm + _iota((tm, 1), 0)) >= meta_pad
    lane = _iota((1, LANES), 1)
    g = -jnp.exp(alog_ref[...]) * _softplus(ab + dtb_ref[...])
    gb = jnp.where(lane < heads, g, _sigmoid(ab))
    gb_ref[...] = jnp.where(valid, gb, 0.0)
    go_ref[...] = _silu(_dot(hn, wgo_ref[...])).astype(BF16)
    glu = _dot(hn, wglu_ref[...]) + bglu_ref[...]
    u = glu[:, :conv_ch] * _sigmoid(glu[:, conv_ch:])
    u_ref[...] = jnp.where(valid, u, 0.0).astype(BF16)
    gates_ref[...] = _sigmoid(_dot(hn, wmg_ref[...]) + bmg_ref[...]).astype(BF16)


def _conv_kernel(u_ref, gate_ref, w_ref, b_ref, lng_ref, lnb_ref, wo_ref, bo_ref, y_ref,
                 xcat_ref, xs_ref, cbuf_ref, *, tm, kw, halo, lead, rb):
    j = pl.program_id(1)
    base = lead + halo
    total = base + tm
    n_c = xcat_ref.shape[1]

    @pl.when(j == 0)
    def _():
        xcat_ref[0:base, :] = jnp.zeros((base, n_c), BF16)

    @pl.when(j > 0)
    def _():
        xcat_ref[lead:base, :] = xcat_ref[tm + lead:tm + base, :]

    xcat_ref[base:total, :] = u_ref[...]
    sr = _iota((SUBLANES * rb, lead + rb), 0)
    sc = _iota((SUBLANES * rb, lead + rb), 1)
    shift = (sc == (sr & (rb - 1)) + lead - (sr >> (rb.bit_length() - 1))).astype(BF16)
    bias = b_ref[...]
    for t0 in range(lead, total, rb):
        res = _dot(shift, xcat_ref[t0 - lead:t0 + rb, :])
        for r in range(SUBLANES):
            xs_ref[r, t0:t0 + rb, :] = res[r * rb:(r + 1) * rb, :]
        if t0 < base:
            continue
        i0 = t0
        blk = (i0 - base) // rb
        acc = jnp.zeros((rb, n_c), F32)
        for s in range(kw):
            a, r = divmod(s, SUBLANES)
            wt = w_ref[(kw - 1 - s) * SUBLANES:(kw - s) * SUBLANES, :]
            acc = acc + xs_ref[r, i0 - SUBLANES * a:i0 - SUBLANES * a + rb, :] * jnp.concatenate(
                [wt] * (rb // SUBLANES), axis=0)
        cbuf_ref[blk * rb:(blk + 1) * rb, :] = acc + bias
    c = cbuf_ref[...]
    mu = jnp.mean(c, axis=-1, keepdims=True)
    xc = c - mu
    ln = xc * lax.rsqrt(jnp.mean(xc * xc, axis=-1, keepdims=True) + LN_EPS) * lng_ref[...] + lnb_ref[...]
    yb = _dot(_silu(ln).astype(BF16), wo_ref[...]) + bo_ref[...]
    y_ref[...] = (gate_ref[...].astype(F32) * yb).astype(BF16)


def _bdot(a, b):
    return jnp.dot(a.astype(BF16), b.astype(BF16), preferred_element_type=F32)


def _split3(x):
    x1 = x.astype(BF16)
    r1 = x - x1.astype(F32)
    x2 = r1.astype(BF16)
    x3 = (r1 - x2.astype(F32)).astype(BF16)
    return x1, x2, x3


def _unit_lower_inverses_minus_eye(mats, row, col):
    ch = mats[0].shape[0]
    blk16 = (row >> 4) == (col >> 4)
    blk32 = (row >> 5) == (col >> 5)
    n1 = [jnp.where(blk16, a, 0.0) for a in mats]
    n2 = [_bdot(n, n) for n in n1]
    r = [_bdot(jnp.concatenate([n, m], axis=0), m) for n, m in zip(n1, n2)]
    n4 = [x[ch:] for x in r]
    y = [m - n - x[:ch] for n, m, x in zip(n1, n2, r)]
    r = [_bdot(jnp.concatenate([yy, m], axis=0), m) for yy, m in zip(y, n4)]
    n8 = [x[ch:] for x in r]
    y = [yy + m + x[:ch] for yy, m, x in zip(y, n4, r)]
    y = [yy + m + _bdot(yy, m) for yy, m in zip(y, n8)]
    for mask in (blk32 & jnp.logical_not(blk16), jnp.logical_not(blk32)):
        ls = [jnp.where(mask, a, 0.0) for a in mats]
        ms = [l + _bdot(l, yy) for l, yy in zip(ls, y)]
        y = [yy - m - _bdot(yy, m) for yy, m in zip(y, ms)]
    return y


def _delta_kernel(qkv_ref, gb_ref, go_ref, cw_ref, ng_ref, o_ref, s_ref, xc_ref,
                  *, nb, heads, dk, dv, ch, ksz, halo):
    c = pl.program_id(1)
    qk_w = heads * dk

    @pl.when(c == 0)
    def _():
        s_ref[...] = jnp.zeros(s_ref.shape, F32)
        xc_ref[:, 0:halo, :] = jnp.zeros((nb, halo, xc_ref.shape[2]), BF16)

    @pl.when(c > 0)
    def _():
        xc_ref[:, 0:halo, :] = xc_ref[:, ch:ch + halo, :]

    xc_ref[:, halo:halo + ch, :] = qkv_ref[...]
    row = _iota((ch, ch), 0)
    col = _iota((ch, ch), 1)
    incl = row >= col
    strict = row > col
    tril = incl.astype(BF16)
    triu = (row <= col).astype(BF16)
    sr = _iota(((ksz - 1) * ch, halo + ch), 0)
    sc = _iota(((ksz - 1) * ch, halo + ch), 1)
    shift = (sc == (sr & (ch - 1)) + (sr >> (ch.bit_length() - 1)) + (halo - (ksz - 1))).astype(BF16)

    qn, kn, kb, vb, gamma, eg, ekd, elast = [], [], [], [], [], [], [], []
    shifted = [_dot(shift, xc_ref[b]) for b in range(nb)]
    for b in range(nb):
        acc = xc_ref[b, halo:halo + ch, :].astype(F32) * cw_ref[ksz - 1:ksz, :]
        for t in range(ksz - 1):
            acc = acc + shifted[b][t * ch:(t + 1) * ch, :] * cw_ref[t:t + 1, :]
        qkv = _silu(acc)
        gb = gb_ref[b]
        parts = _split3(gb)
        gc = sum(_dot(tril, p) for p in parts)
        gct = sum(_dot_tn(p, triu) for p in parts)
        glast = gc[ch - 1:ch, :]
        e_g = jnp.exp(gc)
        e_kd = jnp.exp(glast - gc)
        e_last = jnp.exp(glast)
        for h in range(heads):
            q = qkv[:, h * dk:(h + 1) * dk]
            k = qkv[:, qk_w + h * dk:qk_w + (h + 1) * dk]
            v = qkv[:, 2 * qk_w + h * dv:2 * qk_w + (h + 1) * dv]
            qn.append(q * lax.rsqrt(jnp.sum(q * q, axis=-1, keepdims=True) + EPS) * (dk ** -0.5))
            kn.append(k * lax.rsqrt(jnp.sum(k * k, axis=-1, keepdims=True) + EPS))
            beta = gb[:, heads + h:heads + h + 1]
            kb.append(kn[-1] * beta)
            vb.append(v * beta)
            diff = gc[:, h:h + 1] - gct[h:h + 1, :]
            gamma.append(jnp.where(incl, jnp.exp(jnp.where(incl, diff, 0.0)), 0.0))
            eg.append(e_g[:, h:h + 1])
            ekd.append(e_kd[:, h:h + 1])
            elast.append(e_last[:, h:h + 1])

    n = nb * heads
    kq = [_dot_nt(jnp.concatenate([kb[i], qn[i]], axis=0).astype(BF16), kn[i].astype(BF16))
          for i in range(n)]
    a = [jnp.where(strict, kq[i][:ch] * gamma[i], 0.0) for i in range(n)]
    qk = [kq[i][ch:] * gamma[i] for i in range(n)]
    y = _unit_lower_inverses_minus_eye(a, row, col)
    rhs = [jnp.concatenate([vb[i], kb[i] * eg[i]], axis=-1) for i in range(n)]
    uw = [rhs[i] + _bdot(y[i], rhs[i]) for i in range(n)]
    s = [s_ref[i // heads, i % heads] for i in range(n)]
    ws = [_bdot(jnp.concatenate([uw[i][:, dv:], qn[i] * eg[i]], axis=0), s[i]) for i in range(n)]
    v_new = [uw[i][:, :dv] - ws[i][:ch] for i in range(n)]
    o = [ws[i][ch:] + _bdot(qk[i], v_new[i]) for i in range(n)]
    s_new = [s[i] * elast[i] + _dot_tn((kn[i] * ekd[i]).astype(BF16), v_new[i].astype(BF16))
             for i in range(n)]
    for i in range(n):
        b, h = i // heads, i % heads
        s_ref[b, h] = s_new[i]
        on = _rms(o[i], ng_ref[...])
        o_ref[b, :, h * dv:(h + 1) * dv] = (
            on * go_ref[b, :, h * dv:(h + 1) * dv].astype(F32)).astype(BF16)


def _mid_kernel(h_ref, o_ref, ga_ref, ybg_ref, wdn_ref, wout_ref, g2_ref, wr_ref, br_ref,
                h2_ref, hn_ref, topi_ref, topg_ref, *, tm, n_exp, parts):
    rs = tm // parts
    sl = [pl.ds(p * rs, rs) for p in range(parts)]
    ya = [_dot(o_ref[s, :], wdn_ref[...]) for s in sl]
    y = [ga_ref[s, :].astype(F32) * a + ybg_ref[s, :].astype(F32) for s, a in zip(sl, ya)]
    h2 = [h_ref[s, :] + _dot(v.astype(BF16), wout_ref[...]) for s, v in zip(sl, y)]
    hn = [_rms(v, g2_ref[...]) for v in h2]
    for s, v, w in zip(sl, h2, hn):
        h2_ref[s, :] = v
        hn_ref[s, :] = w
    logits = [_dot(v.astype(BF16), wr_ref[...]) + br_ref[...] for v in hn]
    lane = _iota((rs, LANES), 1)
    lane_f = lane.astype(F32)
    l = [jnp.where(lane < n_exp, v, -jnp.inf) for v in logits]
    vals = [[] for _ in range(parts)]
    idxs = [[] for _ in range(parts)]
    for _ in range(TOP_K):
        m = [jnp.max(v, axis=-1, keepdims=True) for v in l]
        idx = [jnp.min(jnp.where(v == mm, lane_f, float(LANES)), axis=-1, keepdims=True)
               for v, mm in zip(l, m)]
        l = [jnp.where(lane_f == ii, -jnp.inf, v) for v, ii in zip(l, idx)]
        for p in range(parts):
            vals[p].append(m[p])
            idxs[p].append(idx[p])
    for p in range(parts):
        es = [jnp.exp(v - vals[p][0]) for v in vals[p]]
        den = es[0]
        for e in es[1:]:
            den = den + e
        topi = jnp.zeros((rs, LANES), F32)
        topg = jnp.zeros((rs, LANES), F32)
        for k in range(TOP_K):
            topi = jnp.where(lane == k, idxs[p][k], topi)
            topg = jnp.where(lane == k, es[k] / den, topg)
        topi_ref[sl[p], :] = topi.astype(I32)
        topg_ref[sl[p], :] = topg


def _choice_mask(topi, lane, n_exp):
    m = jnp.zeros(topi.shape, F32)
    for k in range(TOP_K):
        m = m + (lane == topi[:, k:k + 1]).astype(F32)
    return jnp.where(lane < n_exp, m, 0.0)


def _plan_kernel(topi_ref, pos_ref, post_ref, n8_ref, off8_ref, carry8_ref, tot8_ref, carry_ref,
                 *, tm, tiles, n_exp):
    i = pl.program_id(0)

    @pl.when(i == 0)
    def _():
        carry_ref[...] = jnp.zeros(carry_ref.shape, F32)

    lane = _iota((tm, LANES), 1)
    tri = (_iota((tm, tm), 0) >= _iota((tm, tm), 1)).astype(BF16)
    before = (_iota((LANES, LANES), 0) < _iota((LANES, LANES), 1)).astype(BF16)
    js = range(tiles)
    rows = [pl.ds(j * tm, tm) for j in js]
    topi = [topi_ref[r, :] for r in rows]
    m = [_choice_mask(t, lane, n_exp) for t in topi]
    cum = [_dot(tri, v.astype(BF16)) for v in m]
    n8 = [jnp.floor((c[tm - 1:tm, :] + (SUBLANES - 1)) * (1.0 / SUBLANES)) * SUBLANES for c in cum]
    off8 = [_dot(jnp.broadcast_to(v, (SUBLANES, LANES)).astype(BF16), before)[0:1] for v in n8]
    posf = [c - v + o for c, v, o in zip(cum, m, off8)]
    pos = [jnp.zeros((tm, LANES), F32) for _ in js]
    for k in range(TOP_K):
        pk = [jnp.sum(jnp.where(lane == t[:, k:k + 1], p, 0.0), axis=-1, keepdims=True)
              for t, p in zip(topi, posf)]
        pos = [jnp.where(lane == k, a, b) for a, b in zip(pk, pos)]
    carry = carry_ref[...]
    for j in js:
        pos_ref[rows[j], :] = pos[j].astype(I32)
        post_ref[j] = jnp.transpose(pos[j])[0:SUBLANES, :].astype(I32)
        n8_ref[j] = n8[j].astype(I32)
        off8_ref[j] = off8[j].astype(I32)
        carry8_ref[j] = carry.astype(I32)
        carry = carry + n8[j]
    carry_ref[...] = carry
    tot8_ref[...] = carry


def _pow2_floor(n):
    return 1 << (n.bit_length() - 1)


def _strip_pieces(n8, max_rows):
    pieces = []
    b = max_rows
    while b >= SUBLANES:
        pieces.append(((n8 & b) != 0, n8 & ~(2 * b - 1), b))
        b //= 2
    return pieces


LONG_PIECE = 64


def _copy_strip(n8, start_piece):
    n_long = n8 >> (LONG_PIECE.bit_length() - 1)

    def long_piece(j, carry):
        start_piece(j * LONG_PIECE, LONG_PIECE)
        return carry

    lax.fori_loop(0, n_long, long_piece, 0)
    for cond, ofs, rows in _strip_pieces(n8 & (LONG_PIECE - 1), LONG_PIECE // 2):
        @pl.when(cond)
        def _():
            start_piece(n_long * LONG_PIECE + ofs, rows)


def _wait_rows(total, make_copy, max_rows):
    b = max_rows
    while b >= SUBLANES:
        @pl.when((total & b) != 0)
        def _():
            make_copy(b).wait()
        b //= 2


def _aligned(start, rows):
    return pl.ds(pl.multiple_of(start, SUBLANES), rows)


def _dispatch_kernel(pend_ref, padded_ref, n8_ref, off8_ref, dbase_ref, hn_ref, post_ref, xs_ref,
                     zbuf_ref, sbuf_ref, tot_ref, zsem, tsem, sems, *, tm, width, n_exp, n_steps, n_blocks):
    i = pl.program_id(0)
    slot = i % 2
    first_unused = pend_ref[n_exp - 1] // MOE_BLOCK

    def zero_block(start, sem):
        return pltpu.make_async_copy(zbuf_ref, xs_ref.at[_aligned(start, MOE_BLOCK), :], sem)

    @pl.when(i == 0)
    def _():
        zbuf_ref[...] = jnp.zeros(zbuf_ref.shape, F32)
        for e in range(n_exp):
            @pl.when(padded_ref[e] > 0)
            def _():
                zero_block(pend_ref[e] - MOE_BLOCK, zsem).start()

        def start_tail(j, carry):
            zero_block(j * MOE_BLOCK, tsem).start()
            return carry

        lax.fori_loop(first_unused, n_blocks, start_tail, 0)
        for e in range(n_exp):
            @pl.when(padded_ref[e] > 0)
            def _():
                zero_block(0, zsem).wait()

    def strip_wait(s):
        _wait_rows(tot_ref[s], lambda b: pltpu.make_async_copy(
            sbuf_ref.at[s, pl.ds(0, b), :], xs_ref.at[pl.ds(0, b), :], sems.at[s]), _pow2_floor(width))

    @pl.when(i >= 2)
    def _():
        strip_wait(slot)

    post = post_ref[0]
    orow = _iota((width, tm), 0)
    sel = orow == post[0:1, :]
    for k in range(1, TOP_K):
        sel = jnp.logical_or(sel, orow == post[k:k + 1, :])
    sbuf_ref[slot] = _dot(sel.astype(BF16), hn_ref[...].astype(BF16))

    total = 0
    for e in range(n_exp):
        n8, so, do = n8_ref[e], off8_ref[e], dbase_ref[e]
        total = total + n8
        _copy_strip(n8, lambda ofs, rows, so=so, do=do: pltpu.make_async_copy(
            sbuf_ref.at[slot, _aligned(so + ofs, rows), :], xs_ref.at[_aligned(do + ofs, rows), :],
            sems.at[slot]).start())
    tot_ref[slot] = total

    @pl.when(i == n_steps - 1)
    def _():
        if n_steps >= 2:
            strip_wait(1 - slot)
        strip_wait(slot)

        def wait_tail(j, carry):
            zero_block(0, tsem).wait()
            return carry

        lax.fori_loop(first_unused, n_blocks, wait_tail, 0)


def _expert_kernel(be_ref, nbu_ref, x_ref, wgu_ref, bgu_ref, wd_ref, bd_ref, y_ref, wgu16_ref, wd16_ref,
                   *, d_ff, parts):
    i = pl.program_id(0)
    used = i < nbu_ref[0]
    new_expert = jnp.logical_or(i == 0, be_ref[i] != be_ref[jnp.maximum(i - 1, 0)])

    @pl.when(jnp.logical_and(used, new_expert))
    def _():
        wgu16_ref[...] = wgu_ref[0].astype(BF16)
        wd16_ref[...] = wd_ref[0].astype(BF16)

    @pl.when(used)
    def _():
        rs = x_ref.shape[0] // parts
        sl = [pl.ds(p * rs, rs) for p in range(parts)]
        gu = [_dot(x_ref[s, :].astype(BF16), wgu16_ref[...]) + bgu_ref[0] for s in sl]
        act = []
        for v in gu:
            gate = jnp.minimum(v[:, :d_ff], SWIGLU_LIMIT)
            up = jnp.clip(v[:, d_ff:], -SWIGLU_LIMIT, SWIGLU_LIMIT)
            act.append(((up + 1.0) * gate * _sigmoid(SWIGLU_ALPHA * gate)).astype(BF16))
        for s, a in zip(sl, act):
            y_ref[s, :] = _dot(a, wd16_ref[...]) + bd_ref[0]

    @pl.when(jnp.logical_not(used))
    def _():
        y_ref[...] = jnp.zeros(y_ref.shape, F32)


def _combine_kernel(n8c_ref, off8c_ref, dbc_ref, n8n_ref, off8n_ref, dbn_ref, pos_ref, h2_ref, topg_ref, fg_ref,
                    ys_ref, out_ref, gbuf_ref, tot_ref, sems, *, tm, width, n_exp, n_steps):
    i = pl.program_id(0)
    slot = i % 2

    def fetch(n8_ref, off8_ref, db_ref, s):
        total = 0
        for e in range(n_exp):
            n8, so, do = n8_ref[e], off8_ref[e], db_ref[e]
            total = total + n8
            _copy_strip(n8, lambda ofs, rows, so=so, do=do: pltpu.make_async_copy(
                ys_ref.at[_aligned(do + ofs, rows), :], gbuf_ref.at[s, _aligned(so + ofs, rows), :],
                sems.at[s]).start())
        tot_ref[s] = total

    @pl.when(i == 0)
    def _():
        gbuf_ref[...] = jnp.zeros(gbuf_ref.shape, F32)
        fetch(n8c_ref, off8c_ref, dbc_ref, 0)

    @pl.when(i + 1 < n_steps)
    def _():
        fetch(n8n_ref, off8n_ref, dbn_ref, 1 - slot)

    _wait_rows(tot_ref[slot], lambda b: pltpu.make_async_copy(
        ys_ref.at[pl.ds(0, b), :], gbuf_ref.at[slot, pl.ds(0, b), :], sems.at[slot]), _pow2_floor(width))
    pos = pos_ref[...]
    topg = topg_ref[...]
    lane = _iota((tm, width), 1)
    g = jnp.zeros((tm, width), F32)
    for k in range(TOP_K):
        g = g + jnp.where(lane == pos[:, k:k + 1], topg[:, k:k + 1], 0.0)
    moe = _dot(g.astype(BF16), gbuf_ref[slot].astype(BF16))
    out_ref[...] = _rms(h2_ref[...] + moe, fg_ref[...])


def _pad_lanes(v, fill=0.0):
    v = v.astype(F32)
    return jnp.concatenate([v, jnp.full((LANES - v.shape[0],), fill, F32)])[None, :]


def kernel(x, meta_tokens, norm_mix_g, w_in, conv_qkv_w, a_log, dt_bias, dn_norm_g, w_dn_out, b_glu,
           conv_dw_w, conv_dw_b, conv_ln_g, conv_ln_b, w_conv_out, b_conv_out, w_merge, b_merge, w_out,
           norm_ffn_g, w_router, b_router, w_gate_up, b_gate_up, w_down, b_down, final_norm_g):
    bsz, seq, d = x.shape
    depth = w_in.shape[0]
    n_meta = meta_tokens.shape[0]
    heads = a_log.shape[1]
    vw = w_dn_out.shape[1]
    qk_w = (conv_qkv_w.shape[2] - vw) // 2
    dk, dv = qk_w // heads, vw // heads
    ksz = conv_qkv_w.shape[1]
    kw, conv_ch = conv_dw_w.shape[1], conv_dw_w.shape[2]
    n_exp, d_ff = w_gate_up.shape[1], w_down.shape[2]
    meta_pad = CHUNK - n_meta
    lp = seq + n_meta + meta_pad
    assert depth == 1 and lp % CHUNK == 0 and dk == LANES and dv == LANES and d % LANES == 0
    assert 2 * heads <= LANES and n_exp <= LANES and kw - 1 <= 32 and ksz - 1 <= SUBLANES
    rows = bsz * lp
    n_head = meta_pad + n_meta
    head = jnp.concatenate([jnp.zeros((meta_pad, d), x.dtype), meta_tokens.astype(x.dtype)], axis=0)

    wi = w_in[0]
    o_a = 2 * qk_w + vw
    o_go = o_a + 2 * heads
    o_glu = o_go + vw
    w_qkv = wi[:, :o_a].astype(BF16)
    w_ab = jnp.pad(wi[:, o_a:o_go], ((0, 0), (0, LANES - 2 * heads))).astype(BF16)
    w_go = wi[:, o_go:o_glu].astype(BF16)
    w_glu = wi[:, o_glu:].astype(BF16)
    row2 = lambda v: v.astype(F32).reshape(1, -1)

    tm1 = _pick_tile(lp, min(832, seq), BF16_ROWS)
    nt1 = lp // tm1
    assert n_head % SUBLANES == 0 and n_head <= tm1 <= seq
    rspec = lambda cols, tm, nt: pl.BlockSpec((tm, cols), lambda b, j: (b * nt + j, 0))
    full = lambda a: pl.BlockSpec(a.shape, lambda *_: (0,) * a.ndim)
    full1 = lambda a: pl.BlockSpec(a.shape, lambda *_: (0,) * a.ndim, pipeline_mode=pl.Buffered(1))
    x_rows = pl.BlockSpec((pl.Element(1), pl.Element(tm1), pl.Element(d)),
                          lambda b, j: (b, pl.multiple_of(jnp.maximum(j * tm1 - n_head, 0), SUBLANES), 0))
    front_in = [head, x, row2(norm_mix_g[0]), w_qkv, w_ab, w_go, w_glu, w_merge[0].astype(BF16),
                row2(b_glu[0]), row2(b_merge[0]), _pad_lanes(a_log[0]), _pad_lanes(dt_bias[0])]
    hp, qkv_pre, gbeta, go_act, u_glu, gates = pl.pallas_call(
        functools.partial(_front_kernel, tm=tm1, meta_pad=meta_pad, heads=heads, conv_ch=conv_ch),
        grid=(bsz, nt1),
        in_specs=[full1(head), x_rows] + [full1(a) for a in front_in[2:]],
        out_specs=[rspec(d, tm1, nt1), rspec(o_a, tm1, nt1), rspec(LANES, tm1, nt1), rspec(vw, tm1, nt1),
                   rspec(conv_ch, tm1, nt1), rspec(2 * d, tm1, nt1)],
        out_shape=[jax.ShapeDtypeStruct((rows, d), F32),
                   jax.ShapeDtypeStruct((rows, o_a), BF16), jax.ShapeDtypeStruct((rows, LANES), F32),
                   jax.ShapeDtypeStruct((rows, vw), BF16), jax.ShapeDtypeStruct((rows, conv_ch), BF16),
                   jax.ShapeDtypeStruct((rows, 2 * d), BF16)],
        compiler_params=_params(("arbitrary", "arbitrary")),
        name="front",
    )(*front_in)

    tmc = _pick_tile(lp, 832, 32)
    ntc = lp // tmc
    halo, lead = 32, BF16_ROWS
    conv_w8 = jnp.repeat(conv_dw_w[0].astype(F32), SUBLANES, axis=0)
    conv_in = [u_glu, gates, conv_w8, row2(conv_dw_b[0]), row2(conv_ln_g[0]),
               row2(conv_ln_b[0]), w_conv_out[0].astype(BF16), row2(b_conv_out[0])]
    ybg = pl.pallas_call(
        functools.partial(_conv_kernel, tm=tmc, kw=kw, halo=halo, lead=lead, rb=32),
        grid=(bsz, ntc),
        in_specs=[rspec(conv_ch, tmc, ntc),
                  pl.BlockSpec((tmc, d), lambda b, j: (b * ntc + j, 1))] + [full(a) for a in conv_in[2:]],
        out_specs=rspec(d, tmc, ntc),
        out_shape=jax.ShapeDtypeStruct((rows, d), BF16),
        scratch_shapes=[pltpu.VMEM((lead + halo + tmc, conv_ch), BF16),
                        pltpu.VMEM((SUBLANES, lead + halo + tmc, conv_ch), F32),
                        pltpu.VMEM((tmc, conv_ch), F32)],
        compiler_params=_params(("arbitrary", "arbitrary")),
        name="convmod",
    )(*conv_in)

    nc = lp // CHUNK
    nbd = max(n for n in (1, 2, 4, 8) if bsz % n == 0)
    cspec = lambda cols: pl.BlockSpec((nbd, CHUNK, cols), lambda b, c: (b, c, 0))
    delta_in = [qkv_pre.reshape(bsz, lp, o_a), gbeta.reshape(bsz, lp, LANES), go_act.reshape(bsz, lp, vw),
                conv_qkv_w[0].astype(F32), row2(dn_norm_g[0])]
    o_dn = pl.pallas_call(
        functools.partial(_delta_kernel, nb=nbd, heads=heads, dk=dk, dv=dv, ch=CHUNK, ksz=ksz, halo=BF16_ROWS),
        grid=(bsz // nbd, nc),
        in_specs=[cspec(o_a), cspec(LANES), cspec(vw), full(delta_in[3]), full(delta_in[4])],
        out_specs=cspec(vw),
        out_shape=jax.ShapeDtypeStruct((bsz, lp, vw), BF16),
        scratch_shapes=[pltpu.VMEM((nbd, heads, dk, dv), F32),
                        pltpu.VMEM((nbd, CHUNK + BF16_ROWS, o_a), BF16)],
        compiler_params=_params(("arbitrary", "arbitrary")),
        name="delta",
    )(*delta_in).reshape(rows, vw)

    tm3 = _pick_tile(lp, MID_TILE, BF16_ROWS)
    nt3 = lp // tm3
    w_r = jnp.pad(w_router[0], ((0, 0), (0, LANES - n_exp))).astype(BF16)
    mid_in = [hp, o_dn, gates, ybg, w_dn_out[0].astype(BF16), w_out[0].astype(BF16),
              row2(norm_ffn_g[0]), w_r, _pad_lanes(b_router[0])]
    h2, hn2, topi, topg = pl.pallas_call(
        functools.partial(_mid_kernel, tm=tm3, n_exp=n_exp, parts=MID_PARTS if tm3 % (MID_PARTS * BF16_ROWS) == 0 else 1),
        grid=(bsz, nt3),
        in_specs=[rspec(d, tm3, nt3), rspec(vw, tm3, nt3), rspec(d, tm3, nt3), rspec(d, tm3, nt3)]
        + [full(a) for a in mid_in[4:]],
        out_specs=[rspec(d, tm3, nt3), rspec(d, tm3, nt3), rspec(LANES, tm3, nt3), rspec(LANES, tm3, nt3)],
        out_shape=[jax.ShapeDtypeStruct((rows, d), F32), jax.ShapeDtypeStruct((rows, d), F32),
                   jax.ShapeDtypeStruct((rows, LANES), I32), jax.ShapeDtypeStruct((rows, LANES), F32)],
        compiler_params=_params(("arbitrary", "arbitrary")),
        name="mid",
    )(*mid_in)

    tmd = _pick_tile(seq, TOKEN_TILE, SUBLANES)
    tpb = seq // tmd
    ntd = bsz * tpb
    tokens = bsz * seq
    width = tmd * TOP_K + n_exp * SUBLANES
    row0 = lambda i: pl.multiple_of((i // tpb) * lp + n_head + (i % tpb) * tmd, SUBLANES)
    prow = lambda r, cols, f=row0: pl.BlockSpec((pl.Element(r), pl.Element(cols)), lambda i, *_: (f(i), 0))
    tspec = pl.BlockSpec((tmd, LANES), lambda i, *_: (i, 0))
    ptiles = _pick_tile(tpb, 8, 1)
    pspec = pl.BlockSpec((ptiles * tmd, LANES), lambda i: (i, 0))
    tab = pl.BlockSpec((ptiles, 1, LANES), lambda i: (i, 0, 0))
    tab_shape = jax.ShapeDtypeStruct((ntd, 1, LANES), I32)
    pos, pos_t, n8_t, off8_t, carry8_t, tot8 = pl.pallas_call(
        functools.partial(_plan_kernel, tm=tmd, tiles=ptiles, n_exp=n_exp),
        grid=(ntd // ptiles,),
        in_specs=[prow(ptiles * tmd, LANES, lambda i: row0(i * ptiles))],
        out_specs=[pspec, pl.BlockSpec((ptiles, SUBLANES, tmd), lambda i: (i, 0, 0)), tab, tab, tab,
                   pl.BlockSpec((1, LANES), lambda i: (0, 0))],
        out_shape=[jax.ShapeDtypeStruct((tokens, LANES), I32), jax.ShapeDtypeStruct((ntd, SUBLANES, tmd), I32),
                   tab_shape, tab_shape, tab_shape, jax.ShapeDtypeStruct((1, LANES), F32)],
        scratch_shapes=[pltpu.VMEM((1, LANES), F32)],
        compiler_params=_params(("arbitrary",)),
        name="plan",
    )(topi)

    n_blocks = -(-(tokens * TOP_K + n_exp * (SUBLANES - 1) * ntd) // MOE_BLOCK) + n_exp
    cap = n_blocks * MOE_BLOCK
    counts = tot8[0, :n_exp].astype(I32)
    padded = (counts + MOE_BLOCK - 1) // MOE_BLOCK * MOE_BLOCK
    pend = jnp.cumsum(padded).astype(I32)
    pstart = pend - padded
    n_used = (pend[-1] // MOE_BLOCK).astype(I32).reshape(1)
    block_start = jnp.arange(n_blocks, dtype=I32) * MOE_BLOCK
    block_e = jnp.minimum(jnp.sum((pend[None, :] <= block_start[:, None]).astype(I32), axis=1), n_exp - 1)
    dbase = (carry8_t.reshape(ntd, LANES) + jnp.pad(pstart, (0, LANES - n_exp))[None, :]).reshape(ntd * LANES)
    n8_f = n8_t.reshape(ntd * LANES)
    off8_f = off8_t.reshape(ntd * LANES)

    anyspec = pl.BlockSpec(memory_space=pl.ANY)
    stab = lambda f: pl.BlockSpec((LANES,), lambda i, *_: (f(i),), memory_space=pltpu.SMEM)
    cur = lambda i: i
    xs = pl.pallas_call(
        functools.partial(_dispatch_kernel, tm=tmd, width=width, n_exp=n_exp, n_steps=ntd, n_blocks=n_blocks),
        grid_spec=pltpu.PrefetchScalarGridSpec(
            num_scalar_prefetch=2, grid=(ntd,),
            in_specs=[stab(cur), stab(cur), stab(cur), prow(tmd, d),
                      pl.BlockSpec((1, SUBLANES, tmd), lambda i, *_: (i, 0, 0))],
            out_specs=anyspec,
            scratch_shapes=[pltpu.VMEM((MOE_BLOCK, d), F32), pltpu.VMEM((2, width, d), F32),
                            pltpu.SMEM((2,), I32), pltpu.SemaphoreType.DMA(()), pltpu.SemaphoreType.DMA(()),
                            pltpu.SemaphoreType.DMA((2,))]),
        out_shape=jax.ShapeDtypeStruct((cap, d), F32),
        compiler_params=_params(("arbitrary",)),
        name="dispatch",
    )(pend, padded, n8_f, off8_f, dbase, hn2, pos_t)

    last_used = lambda i, nbu: jnp.minimum(i, nbu[0] - 1)
    ew = lambda a: pl.BlockSpec((1,) + a.shape[1:], lambda i, be, nbu: (be[i],) + (0,) * (a.ndim - 1))
    e_in = [w_gate_up[0], b_gate_up[0].astype(F32)[:, None, :], w_down[0], b_down[0].astype(F32)[:, None, :]]
    ys = pl.pallas_call(
        functools.partial(_expert_kernel, d_ff=d_ff, parts=2),
        grid_spec=pltpu.PrefetchScalarGridSpec(
            num_scalar_prefetch=2, grid=(n_blocks,),
            in_specs=[pl.BlockSpec((MOE_BLOCK, d), lambda i, be, nbu: (last_used(i, nbu), 0))]
            + [ew(a) for a in e_in],
            out_specs=pl.BlockSpec((MOE_BLOCK, d), lambda i, be, nbu: (i, 0)),
            scratch_shapes=[pltpu.VMEM((d, 2 * d_ff), BF16), pltpu.VMEM((d_ff, d), BF16)]),
        out_shape=jax.ShapeDtypeStruct((cap, d), F32),
        compiler_params=_params(("arbitrary",)),
        name="experts",
    )(block_e, n_used, xs, *e_in)

    nxt = lambda i: jnp.minimum(i + 1, ntd - 1)
    out = pl.pallas_call(
        functools.partial(_combine_kernel, tm=tmd, width=width, n_exp=n_exp, n_steps=ntd),
        grid=(ntd,),
        in_specs=[stab(cur), stab(cur), stab(cur), stab(nxt), stab(nxt), stab(nxt), tspec,
                  prow(tmd, d), prow(tmd, LANES), pl.BlockSpec((1, d), lambda i: (0, 0)), anyspec],
        out_specs=pl.BlockSpec((tmd, d), lambda i: (i, 0)),
        out_shape=jax.ShapeDtypeStruct((tokens, d), x.dtype),
        scratch_shapes=[pltpu.VMEM((2, width, d), F32), pltpu.SMEM((2,), I32), pltpu.SemaphoreType.DMA((2,))],
        compiler_params=_params(("arbitrary",)),
        name="combine",
    )(n8_f, off8_f, dbase, n8_f, off8_f, dbase, pos, h2, topg, row2(final_norm_g), ys)
    return out.reshape(bsz, seq, d)
```

```python
import functools

import jax
import jax.numpy as jnp
from jax import lax
from jax.experimental import pallas as pl
from jax.experimental.pallas import tpu as pltpu

F32 = jnp.float32
BF16 = jnp.bfloat16
I32 = jnp.int32

CHUNK = 64
TOP_K = 4
MOE_BLOCK = 512
TOKEN_TILE = 256
MID_TILE = 832
MID_PARTS = 4
SWIGLU_LIMIT = 7.0
SWIGLU_ALPHA = 1.702
EPS = 1e-6
LN_EPS = 1e-5
LANES = 128
SUBLANES = 8
BF16_ROWS = 16
VMEM_LIMIT = 56 * 1024 * 1024


def _dot(a, b):
    return jnp.dot(a, b, preferred_element_type=F32)


def _dot_nt(a, b):
    return lax.dot_general(a, b, (((1,), (1,)), ((), ())), preferred_element_type=F32)


def _dot_tn(a, b):
    return lax.dot_general(a, b, (((0,), (0,)), ((), ())), preferred_element_type=F32)


def _sigmoid(x):
    return 1.0 / (1.0 + jnp.exp(-x))


def _silu(x):
    return x * _sigmoid(x)


def _softplus(x):
    return jnp.maximum(x, 0.0) + jnp.log(1.0 + jnp.exp(-jnp.abs(x)))


def _rms(x, g):
    return x * lax.rsqrt(jnp.mean(x * x, axis=-1, keepdims=True) + EPS) * g


def _iota(shape, dim):
    return lax.broadcasted_iota(I32, shape, dim)


def _pick_tile(n, target, mult):
    best = None
    for t in range(mult, min(n, target) + 1, mult):
        if n % t == 0:
            best = t
    assert best is not None, (n, target, mult)
    return best


def _params(sem, flags=None):
    return pltpu.CompilerParams(dimension_semantics=sem, vmem_limit_bytes=VMEM_LIMIT, flags=flags)


def _front_kernel(head_ref, x_ref, g_ref, wqkv_ref, wab_ref, wgo_ref, wglu_ref, wmg_ref, bglu_ref, bmg_ref,
                  alog_ref, dtb_ref, h_ref, qkv_ref, gb_ref, go_ref, u_ref, gates_ref,
                  *, tm, meta_pad, heads, conv_ch):
    j = pl.program_id(1)
    xb = x_ref[0]
    n_head = head_ref.shape[0]
    h = jnp.where(j == 0, jnp.concatenate([head_ref[...], xb[:tm - n_head]], axis=0), xb)
    h_ref[...] = h
    hn = _rms(h, g_ref[...]).astype(BF16)
    qkv_ref[...] = _dot(hn, wqkv_ref[...]).astype(BF16)
    ab = _dot(hn, wab_ref[...])
    valid = (j * tm + _iota((tm, 1), 0)) >= meta_pad
    lane = _iota((1, LANES), 1)
    g = -jnp.exp(alog_ref[...]) * _softplus(ab + dtb_ref[...])
    gb = jnp.where(lane < heads, g, _sigmoid(ab))
    gb_ref[...] = jnp.where(valid, gb, 0.0)
    go_ref[...] = _silu(_dot(hn, wgo_ref[...])).astype(BF16)
    glu = _dot(hn, wglu_ref[...]) + bglu_ref[...]
    u = glu[:, :conv_ch] * _sigmoid(glu[:, conv_ch:])
    u_ref[...] = jnp.where(valid, u, 0.0).astype(BF16)
    gates_ref[...] = _sigmoid(_dot(hn, wmg_ref[...]) + bmg_ref[...]).astype(BF16)


def _conv_kernel(u_ref, gate_ref, w_ref, b_ref, lng_ref, lnb_ref, wo_ref, bo_ref, y_ref,
                 xcat_ref, xs_ref, cbuf_ref, *, tm, kw, halo, lead, rb):
    j = pl.program_id(1)
    base = lead + halo
    total = base + tm
    n_c = xcat_ref.shape[1]

    @pl.when(j == 0)
    def _():
        xcat_ref[0:base, :] = jnp.zeros((base, n_c), BF16)

    @pl.when(j > 0)
    def _():
        xcat_ref[lead:base, :] = xcat_ref[tm + lead:tm + base, :]

    xcat_ref[base:total, :] = u_ref[...]
    sr = _iota((SUBLANES * rb, lead + rb), 0)
    sc = _iota((SUBLANES * rb, lead + rb), 1)
    shift = (sc == (sr & (rb - 1)) + lead - (sr >> (rb.bit_length() - 1))).astype(BF16)
    bias = b_ref[...]
    for t0 in range(lead, total, rb):
        res = _dot(shift, xcat_ref[t0 - lead:t0 + rb, :])
        for r in range(SUBLANES):
            xs_ref[r, t0:t0 + rb, :] = res[r * rb:(r + 1) * rb, :]
        if t0 < base:
            continue
        i0 = t0
        blk = (i0 - base) // rb
        acc = jnp.zeros((rb, n_c), F32)
        for s in range(kw):
            a, r = divmod(s, SUBLANES)
            wt = w_ref[(kw - 1 - s) * SUBLANES:(kw - s) * SUBLANES, :]
            acc = acc + xs_ref[r, i0 - SUBLANES * a:i0 - SUBLANES * a + rb, :] * jnp.concatenate(
                [wt] * (rb // SUBLANES), axis=0)
        cbuf_ref[blk * rb:(blk + 1) * rb, :] = acc + bias
    c = cbuf_ref[...]
    mu = jnp.mean(c, axis=-1, keepdims=True)
    xc = c - mu
    ln = xc * lax.rsqrt(jnp.mean(xc * xc, axis=-1, keepdims=True) + LN_EPS) * lng_ref[...] + lnb_ref[...]
    yb = _dot(_silu(ln).astype(BF16), wo_ref[...]) + bo_ref[...]
    y_ref[...] = (gate_ref[...].astype(F32) * yb).astype(BF16)


def _bdot(a, b):
    return jnp.dot(a.astype(BF16), b.astype(BF16), preferred_element_type=F32)


def _split3(x):
    x1 = x.astype(BF16)
    r1 = x - x1.astype(F32)
    x2 = r1.astype(BF16)
    x3 = (r1 - x2.astype(F32)).astype(BF16)
    return x1, x2, x3


def _unit_lower_inverses_minus_eye(mats, row, col):
    ch = mats[0].shape[0]
    blk16 = (row >> 4) == (col >> 4)
    blk32 = (row >> 5) == (col >> 5)
    n1 = [jnp.where(blk16, a, 0.0) for a in mats]
    n2 = [_bdot(n, n) for n in n1]
    r = [_bdot(jnp.concatenate([n, m], axis=0), m) for n, m in zip(n1, n2)]
    n4 = [x[ch:] for x in r]
    y = [m - n - x[:ch] for n, m, x in zip(n1, n2, r)]
    r = [_bdot(jnp.concatenate([yy, m], axis=0), m) for yy, m in zip(y, n4)]
    n8 = [x[ch:] for x in r]
    y = [yy + m + x[:ch] for yy, m, x in zip(y, n4, r)]
    y = [yy + m + _bdot(yy, m) for yy, m in zip(y, n8)]
    for mask in (blk32 & jnp.logical_not(blk16), jnp.logical_not(blk32)):
        ls = [jnp.where(mask, a, 0.0) for a in mats]
        ms = [l + _bdot(l, yy) for l, yy in zip(ls, y)]
        y = [yy - m - _bdot(yy, m) for yy, m in zip(y, ms)]
    return y


def _delta_kernel(qkv_ref, gb_ref, go_ref, cw_ref, ng_ref, o_ref, s_ref, xc_ref,
                  *, nb, heads, dk, dv, ch, ksz, halo):
    c = pl.program_id(1)
    qk_w = heads * dk

    @pl.when(c == 0)
    def _():
        s_ref[...] = jnp.zeros(s_ref.shape, F32)
        xc_ref[:, 0:halo, :] = jnp.zeros((nb, halo, xc_ref.shape[2]), BF16)

    @pl.when(c > 0)
    def _():
        xc_ref[:, 0:halo, :] = xc_ref[:, ch:ch + halo, :]

    xc_ref[:, halo:halo + ch, :] = qkv_ref[...]
    row = _iota((ch, ch), 0)
    col = _iota((ch, ch), 1)
    incl = row >= col
    strict = row > col
    tril = incl.astype(BF16)
    triu = (row <= col).astype(BF16)
    sr = _iota(((ksz - 1) * ch, halo + ch), 0)
    sc = _iota(((ksz - 1) * ch, halo + ch), 1)
    shift = (sc == (sr & (ch - 1)) + (sr >> (ch.bit_length() - 1)) + (halo - (ksz - 1))).astype(BF16)

    qn, kn, kb, vb, gamma, eg, ekd, elast = [], [], [], [], [], [], [], []
    shifted = [_dot(shift, xc_ref[b]) for b in range(nb)]
    for b in range(nb):
        acc = xc_ref[b, halo:halo + ch, :].astype(F32) * cw_ref[ksz - 1:ksz, :]
        for t in range(ksz - 1):
            acc = acc + shifted[b][t * ch:(t + 1) * ch, :] * cw_ref[t:t + 1, :]
        qkv = _silu(acc)
        gb = gb_ref[b]
        parts = _split3(gb)
        gc = sum(_dot(tril, p) for p in parts)
        gct = sum(_dot_tn(p, triu) for p in parts)
        glast = gc[ch - 1:ch, :]
        e_g = jnp.exp(gc)
        e_kd = jnp.exp(glast - gc)
        e_last = jnp.exp(glast)
        for h in range(heads):
            q = qkv[:, h * dk:(h + 1) * dk]
            k = qkv[:, qk_w + h * dk:qk_w + (h + 1) * dk]
            v = qkv[:, 2 * qk_w + h * dv:2 * qk_w + (h + 1) * dv]
            qn.append(q * lax.rsqrt(jnp.sum(q * q, axis=-1, keepdims=True) + EPS) * (dk ** -0.5))
            kn.append(k * lax.rsqrt(jnp.sum(k * k, axis=-1, keepdims=True) + EPS))
            beta = gb[:, heads + h:heads + h + 1]
            kb.append(kn[-1] * beta)
            vb.append(v * beta)
            diff = gc[:, h:h + 1] - gct[h:h + 1, :]
            gamma.append(jnp.where(incl, jnp.exp(jnp.where(incl, diff, 0.0)), 0.0))
            eg.append(e_g[:, h:h + 1])
            ekd.append(e_kd[:, h:h + 1])
            elast.append(e_last[:, h:h + 1])

    n = nb * heads
    kq = [_dot_nt(jnp.concatenate([kb[i], qn[i]], axis=0).astype(BF16), kn[i].astype(BF16))
          for i in range(n)]
    a = [jnp.where(strict, kq[i][:ch] * gamma[i], 0.0) for i in range(n)]
    qk = [kq[i][ch:] * gamma[i] for i in range(n)]
    y = _unit_lower_inverses_minus_eye(a, row, col)
    rhs = [jnp.concatenate([vb[i], kb[i] * eg[i]], axis=-1) for i in range(n)]
    uw = [rhs[i] + _bdot(y[i], rhs[i]) for i in range(n)]
    s = [s_ref[i // heads, i % heads] for i in range(n)]
    ws = [_bdot(jnp.concatenate([uw[i][:, dv:], qn[i] * eg[i]], axis=0), s[i]) for i in range(n)]
    v_new = [uw[i][:, :dv] - ws[i][:ch] for i in range(n)]
    o = [ws[i][ch:] + _bdot(qk[i], v_new[i]) for i in range(n)]
    s_new = [s[i] * elast[i] + _dot_tn((kn[i] * ekd[i]).astype(BF16), v_new[i].astype(BF16))
             for i in range(n)]
    for i in range(n):
        b, h = i // heads, i % heads
        s_ref[b, h] = s_new[i]
        on = _rms(o[i], ng_ref[...])
        o_ref[b, :, h * dv:(h + 1) * dv] = (
            on * go_ref[b, :, h * dv:(h + 1) * dv].astype(F32)).astype(BF16)


def _mid_kernel(h_ref, o_ref, ga_ref, ybg_ref, wdn_ref, wout_ref, g2_ref, wr_ref, br_ref,
                h2_ref, hn_ref, topi_ref, topg_ref, *, tm, n_exp, parts):
    rs = tm // parts
    sl = [pl.ds(p * rs, rs) for p in range(parts)]
    ya = [_dot(o_ref[s, :], wdn_ref[...]) for s in sl]
    y = [ga_ref[s, :].astype(F32) * a + ybg_ref[s, :].astype(F32) for s, a in zip(sl, ya)]
    h2 = [h_ref[s, :] + _dot(v.astype(BF16), wout_ref[...]) for s, v in zip(sl, y)]
    hn = [_rms(v, g2_ref[...]) for v in h2]
    for s, v, w in zip(sl, h2, hn):
        h2_ref[s, :] = v
        hn_ref[s, :] = w
    logits = [_dot(v.astype(BF16), wr_ref[...]) + br_ref[...] for v in hn]
    lane = _iota((rs, LANES), 1)
    lane_f = lane.astype(F32)
    l = [jnp.where(lane < n_exp, v, -jnp.inf) for v in logits]
    vals = [[] for _ in range(parts)]
    idxs = [[] for _ in range(parts)]
    for _ in range(TOP_K):
        m = [jnp.max(v, axis=-1, keepdims=True) for v in l]
        idx = [jnp.min(jnp.where(v == mm, lane_f, float(LANES)), axis=-1, keepdims=True)
               for v, mm in zip(l, m)]
        l = [jnp.where(lane_f == ii, -jnp.inf, v) for v, ii in zip(l, idx)]
        for p in range(parts):
            vals[p].append(m[p])
            idxs[p].append(idx[p])
    for p in range(parts):
        es = [jnp.exp(v - vals[p][0]) for v in vals[p]]
        den = es[0]
        for e in es[1:]:
            den = den + e
        topi = jnp.zeros((rs, LANES), F32)
        topg = jnp.zeros((rs, LANES), F32)
        for k in range(TOP_K):
            topi = jnp.where(lane == k, idxs[p][k], topi)
            topg = jnp.where(lane == k, es[k] / den, topg)
        topi_ref[sl[p], :] = topi.astype(I32)
        topg_ref[sl[p], :] = topg


def _choice_mask(topi, lane, n_exp):
    m = jnp.zeros(topi.shape, F32)
    for k in range(TOP_K):
        m = m + (lane == topi[:, k:k + 1]).astype(F32)
    return jnp.where(lane < n_exp, m, 0.0)


def _plan_kernel(topi_ref, pos_ref, post_ref, n8_ref, off8_ref, carry8_ref, tot8_ref, carry_ref,
                 *, tm, tiles, n_exp):
    i = pl.program_id(0)

    @pl.when(i == 0)
    def _():
        carry_ref[...] = jnp.zeros(carry_ref.shape, F32)

    lane = _iota((tm, LANES), 1)
    tri = (_iota((tm, tm), 0) >= _iota((tm, tm), 1)).astype(BF16)
    before = (_iota((LANES, LANES), 0) < _iota((LANES, LANES), 1)).astype(BF16)
    js = range(tiles)
    rows = [pl.ds(j * tm, tm) for j in js]
    topi = [topi_ref[r, :] for r in rows]
    m = [_choice_mask(t, lane, n_exp) for t in topi]
    cum = [_dot(tri, v.astype(BF16)) for v in m]
    n8 = [jnp.floor((c[tm - 1:tm, :] + (SUBLANES - 1)) * (1.0 / SUBLANES)) * SUBLANES for c in cum]
    off8 = [_dot(jnp.broadcast_to(v, (SUBLANES, LANES)).astype(BF16), before)[0:1] for v in n8]
    posf = [c - v + o for c, v, o in zip(cum, m, off8)]
    pos = [jnp.zeros((tm, LANES), F32) for _ in js]
    for k in range(TOP_K):
        pk = [jnp.sum(jnp.where(lane == t[:, k:k + 1], p, 0.0), axis=-1, keepdims=True)
              for t, p in zip(topi, posf)]
        pos = [jnp.where(lane == k, a, b) for a, b in zip(pk, pos)]
    carry = carry_ref[...]
    for j in js:
        pos_ref[rows[j], :] = pos[j].astype(I32)
        post_ref[j] = jnp.transpose(pos[j])[0:SUBLANES, :].astype(I32)
        n8_ref[j] = n8[j].astype(I32)
        off8_ref[j] = off8[j].astype(I32)
        carry8_ref[j] = carry.astype(I32)
        carry = carry + n8[j]
    carry_ref[...] = carry
    tot8_ref[...] = carry


def _pow2_floor(n):
    return 1 << (n.bit_length() - 1)


def _strip_pieces(n8, max_rows):
    pieces = []
    b = max_rows
    while b >= SUBLANES:
        pieces.append(((n8 & b) != 0, n8 & ~(2 * b - 1), b))
        b //= 2
    return pieces


LONG_PIECE = 64


def _copy_strip(n8, start_piece):
    n_long = n8 >> (LONG_PIECE.bit_length() - 1)

    def long_piece(j, carry):
        start_piece(j * LONG_PIECE, LONG_PIECE)
        return carry

    lax.fori_loop(0, n_long, long_piece, 0)
    for cond, ofs, rows in _strip_pieces(n8 & (LONG_PIECE - 1), LONG_PIECE // 2):
        @pl.when(cond)
        def _():
            start_piece(n_long * LONG_PIECE + ofs, rows)


def _wait_rows(total, make_copy, max_rows):
    b = max_rows
    while b >= SUBLANES:
        @pl.when((total & b) != 0)
        def _():
            make_copy(b).wait()
        b //= 2


def _aligned(start, rows):
    return pl.ds(pl.multiple_of(start, SUBLANES), rows)


def _dispatch_kernel(pend_ref, padded_ref, n8_ref, off8_ref, dbase_ref, hn_ref, post_ref, xs_ref,
                     zbuf_ref, sbuf_ref, tot_ref, zsem, sems, *, tm, width, n_exp, n_steps, n_blocks):
    i = pl.program_id(0)
    slot = i % 2

    def zero_block(start):
        return pltpu.make_async_copy(zbuf_ref, xs_ref.at[_aligned(start, MOE_BLOCK), :], zsem)

    @pl.when(i == 0)
    def _():
        zbuf_ref[...] = jnp.zeros(zbuf_ref.shape, F32)
        for e in range(n_exp):
            @pl.when(padded_ref[e] > 0)
            def _():
                zero_block(pend_ref[e] - MOE_BLOCK).start()
        first_unused = pend_ref[n_exp - 1] // MOE_BLOCK

        def start_tail(j, carry):
            zero_block(j * MOE_BLOCK).start()
            return carry

        lax.fori_loop(first_unused, n_blocks, start_tail, 0)
        for e in range(n_exp):
            @pl.when(padded_ref[e] > 0)
            def _():
                zero_block(0).wait()

        def wait_tail(j, carry):
            zero_block(0).wait()
            return carry

        lax.fori_loop(first_unused, n_blocks, wait_tail, 0)

    def strip_wait(s):
        _wait_rows(tot_ref[s], lambda b: pltpu.make_async_copy(
            sbuf_ref.at[s, pl.ds(0, b), :], xs_ref.at[pl.ds(0, b), :], sems.at[s]), _pow2_floor(width))

    @pl.when(i >= 2)
    def _():
        strip_wait(slot)

    post = post_ref[0]
    orow = _iota((width, tm), 0)
    sel = orow == post[0:1, :]
    for k in range(1, TOP_K):
        sel = jnp.logical_or(sel, orow == post[k:k + 1, :])
    sbuf_ref[slot] = _dot(sel.astype(BF16), hn_ref[...].astype(BF16))

    total = 0
    for e in range(n_exp):
        n8, so, do = n8_ref[e], off8_ref[e], dbase_ref[e]
        total = total + n8
        _copy_strip(n8, lambda ofs, rows, so=so, do=do: pltpu.make_async_copy(
            sbuf_ref.at[slot, _aligned(so + ofs, rows), :], xs_ref.at[_aligned(do + ofs, rows), :],
            sems.at[slot]).start())
    tot_ref[slot] = total

    @pl.when(i == n_steps - 1)
    def _():
        if n_steps >= 2:
            strip_wait(1 - slot)
        strip_wait(slot)


def _expert_kernel(be_ref, nbu_ref, nxt_ref, xs_ref, wgu_ref, bgu_ref, wd_ref, bd_ref, ys_ref,
                   xbuf_ref, ybuf_ref, sgu_ref, sd_ref, wgu16_ref, wd16_ref, zbuf_ref, xsem, ysem, wsem, zsem,
                   *, d_ff, parts, n_blocks):
    bm = xbuf_ref.shape[1]
    n_used = nbu_ref[0]

    def rows(i):
        return pl.ds(pl.multiple_of(i * bm, bm), bm)

    def x_copy(i, s):
        return pltpu.make_async_copy(xs_ref.at[rows(i), :], xbuf_ref.at[s], xsem.at[s])

    def y_copy(i, s):
        return pltpu.make_async_copy(ybuf_ref.at[s], ys_ref.at[rows(i), :], ysem.at[s])

    def w_copies(e):
        return (pltpu.make_async_copy(wgu_ref.at[e], sgu_ref, wsem.at[0]),
                pltpu.make_async_copy(wd_ref.at[e], sd_ref, wsem.at[1]))

    for c in w_copies(be_ref[0]):
        c.start()
    x_copy(0, 0).start()

    zbuf_ref[...] = jnp.zeros(zbuf_ref.shape, F32)

    def z_copy(i):
        return pltpu.make_async_copy(zbuf_ref, ys_ref.at[rows(i), :], zsem.at[0])

    def z_start(i, c):
        z_copy(i).start()
        return c

    jax.lax.fori_loop(n_used, n_blocks, z_start, 0)

    def body(i, carry):
        slot = i % 2
        e = be_ref[i]
        new_expert = jnp.logical_or(i == 0, e != be_ref[jnp.maximum(i - 1, 0)])

        @pl.when(new_expert)
        def _():
            for c in w_copies(e):
                c.wait()
            wgu16_ref[...] = sgu_ref[...].astype(BF16)
            wd16_ref[...] = sd_ref[...].astype(BF16)
            nx = nxt_ref[e]

            @pl.when(nx != e)
            def _():
                for c in w_copies(nx):
                    c.start()

        x_copy(i, slot).wait()

        @pl.when(i + 1 < n_used)
        def _():
            x_copy(i + 1, 1 - slot).start()

        @pl.when(i >= 2)
        def _():
            y_copy(0, slot).wait()

        rs = bm // parts
        sl = [pl.ds(p * rs, rs) for p in range(parts)]
        gu = [_dot(xbuf_ref[slot, s, :].astype(BF16), wgu16_ref[...]) + bgu_ref[e] for s in sl]
        act = []
        for v in gu:
            gate = jnp.minimum(v[:, :d_ff], SWIGLU_LIMIT)
            up = jnp.clip(v[:, d_ff:], -SWIGLU_LIMIT, SWIGLU_LIMIT)
            act.append(((up + 1.0) * gate * _sigmoid(SWIGLU_ALPHA * gate)).astype(BF16))
        for s, a in zip(sl, act):
            ybuf_ref[slot, s, :] = _dot(a, wd16_ref[...]) + bd_ref[e]
        y_copy(i, slot).start()
        return carry

    jax.lax.fori_loop(0, n_used, body, 0)

    @pl.when(n_used >= 2)
    def _():
        y_copy(0, n_used % 2).wait()

    y_copy(0, (n_used + 1) % 2).wait()

    def z_wait(i, c):
        z_copy(0).wait()
        return c

    jax.lax.fori_loop(n_used, n_blocks, z_wait, 0)


def _combine_kernel(n8c_ref, off8c_ref, dbc_ref, n8n_ref, off8n_ref, dbn_ref, pos_ref, h2_ref, topg_ref, fg_ref,
                    ys_ref, out_ref, gbuf_ref, tot_ref, sems, *, tm, width, n_exp, n_steps):
    i = pl.program_id(0)
    slot = i % 2

    def fetch(n8_ref, off8_ref, db_ref, s):
        total = 0
        for e in range(n_exp):
            n8, so, do = n8_ref[e], off8_ref[e], db_ref[e]
            total = total + n8
            _copy_strip(n8, lambda ofs, rows, so=so, do=do: pltpu.make_async_copy(
                ys_ref.at[_aligned(do + ofs, rows), :], gbuf_ref.at[s, _aligned(so + ofs, rows), :],
                sems.at[s]).start())
        tot_ref[s] = total

    @pl.when(i == 0)
    def _():
        gbuf_ref[...] = jnp.zeros(gbuf_ref.shape, F32)
        fetch(n8c_ref, off8c_ref, dbc_ref, 0)

    @pl.when(i + 1 < n_steps)
    def _():
        fetch(n8n_ref, off8n_ref, dbn_ref, 1 - slot)

    _wait_rows(tot_ref[slot], lambda b: pltpu.make_async_copy(
        ys_ref.at[pl.ds(0, b), :], gbuf_ref.at[slot, pl.ds(0, b), :], sems.at[slot]), _pow2_floor(width))
    pos = pos_ref[...]
    topg = topg_ref[...]
    lane = _iota((tm, width), 1)
    g = jnp.zeros((tm, width), F32)
    for k in range(TOP_K):
        g = g + jnp.where(lane == pos[:, k:k + 1], topg[:, k:k + 1], 0.0)
    moe = _dot(g.astype(BF16), gbuf_ref[slot].astype(BF16))
    out_ref[...] = _rms(h2_ref[...] + moe, fg_ref[...])


def _pad_lanes(v, fill=0.0):
    v = v.astype(F32)
    return jnp.concatenate([v, jnp.full((LANES - v.shape[0],), fill, F32)])[None, :]


def kernel(x, meta_tokens, norm_mix_g, w_in, conv_qkv_w, a_log, dt_bias, dn_norm_g, w_dn_out, b_glu,
           conv_dw_w, conv_dw_b, conv_ln_g, conv_ln_b, w_conv_out, b_conv_out, w_merge, b_merge, w_out,
           norm_ffn_g, w_router, b_router, w_gate_up, b_gate_up, w_down, b_down, final_norm_g):
    bsz, seq, d = x.shape
    depth = w_in.shape[0]
    n_meta = meta_tokens.shape[0]
    heads = a_log.shape[1]
    vw = w_dn_out.shape[1]
    qk_w = (conv_qkv_w.shape[2] - vw) // 2
    dk, dv = qk_w // heads, vw // heads
    ksz = conv_qkv_w.shape[1]
    kw, conv_ch = conv_dw_w.shape[1], conv_dw_w.shape[2]
    n_exp, d_ff = w_gate_up.shape[1], w_down.shape[2]
    meta_pad = CHUNK - n_meta
    lp = seq + n_meta + meta_pad
    assert depth == 1 and lp % CHUNK == 0 and dk == LANES and dv == LANES and d % LANES == 0
    assert 2 * heads <= LANES and n_exp <= LANES and kw - 1 <= 32 and ksz - 1 <= SUBLANES
    rows = bsz * lp
    n_head = meta_pad + n_meta
    head = jnp.concatenate([jnp.zeros((meta_pad, d), x.dtype), meta_tokens.astype(x.dtype)], axis=0)

    wi = w_in[0]
    o_a = 2 * qk_w + vw
    o_go = o_a + 2 * heads
    o_glu = o_go + vw
    w_qkv = wi[:, :o_a].astype(BF16)
    w_ab = jnp.pad(wi[:, o_a:o_go], ((0, 0), (0, LANES - 2 * heads))).astype(BF16)
    w_go = wi[:, o_go:o_glu].astype(BF16)
    w_glu = wi[:, o_glu:].astype(BF16)
    row2 = lambda v: v.astype(F32).reshape(1, -1)

    tm1 = _pick_tile(lp, min(832, seq), BF16_ROWS)
    nt1 = lp // tm1
    assert n_head % SUBLANES == 0 and n_head <= tm1 <= seq
    rspec = lambda cols, tm, nt: pl.BlockSpec((tm, cols), lambda b, j: (b * nt + j, 0))
    full = lambda a: pl.BlockSpec(a.shape, lambda *_: (0,) * a.ndim)
    full1 = lambda a: pl.BlockSpec(a.shape, lambda *_: (0,) * a.ndim, pipeline_mode=pl.Buffered(1))
    x_rows = pl.BlockSpec((pl.Element(1), pl.Element(tm1), pl.Element(d)),
                          lambda b, j: (b, pl.multiple_of(jnp.maximum(j * tm1 - n_head, 0), SUBLANES), 0))
    front_in = [head, x, row2(norm_mix_g[0]), w_qkv, w_ab, w_go, w_glu, w_merge[0].astype(BF16),
                row2(b_glu[0]), row2(b_merge[0]), _pad_lanes(a_log[0]), _pad_lanes(dt_bias[0])]
    hp, qkv_pre, gbeta, go_act, u_glu, gates = pl.pallas_call(
        functools.partial(_front_kernel, tm=tm1, meta_pad=meta_pad, heads=heads, conv_ch=conv_ch),
        grid=(bsz, nt1),
        in_specs=[full1(head), x_rows] + [full1(a) for a in front_in[2:]],
        out_specs=[rspec(d, tm1, nt1), rspec(o_a, tm1, nt1), rspec(LANES, tm1, nt1), rspec(vw, tm1, nt1),
                   rspec(conv_ch, tm1, nt1), rspec(2 * d, tm1, nt1)],
        out_shape=[jax.ShapeDtypeStruct((rows, d), F32),
                   jax.ShapeDtypeStruct((rows, o_a), BF16), jax.ShapeDtypeStruct((rows, LANES), F32),
                   jax.ShapeDtypeStruct((rows, vw), BF16), jax.ShapeDtypeStruct((rows, conv_ch), BF16),
                   jax.ShapeDtypeStruct((rows, 2 * d), BF16)],
        compiler_params=_params(("arbitrary", "arbitrary")),
        name="front",
    )(*front_in)

    tmc = _pick_tile(lp, 832, 32)
    ntc = lp // tmc
    halo, lead = 32, BF16_ROWS
    conv_w8 = jnp.repeat(conv_dw_w[0].astype(F32), SUBLANES, axis=0)
    conv_in = [u_glu, gates, conv_w8, row2(conv_dw_b[0]), row2(conv_ln_g[0]),
               row2(conv_ln_b[0]), w_conv_out[0].astype(BF16), row2(b_conv_out[0])]
    ybg = pl.pallas_call(
        functools.partial(_conv_kernel, tm=tmc, kw=kw, halo=halo, lead=lead, rb=32),
        grid=(bsz, ntc),
        in_specs=[rspec(conv_ch, tmc, ntc),
                  pl.BlockSpec((tmc, d), lambda b, j: (b * ntc + j, 1))] + [full(a) for a in conv_in[2:]],
        out_specs=rspec(d, tmc, ntc),
        out_shape=jax.ShapeDtypeStruct((rows, d), BF16),
        scratch_shapes=[pltpu.VMEM((lead + halo + tmc, conv_ch), BF16),
                        pltpu.VMEM((SUBLANES, lead + halo + tmc, conv_ch), F32),
                        pltpu.VMEM((tmc, conv_ch), F32)],
        compiler_params=_params(("arbitrary", "arbitrary")),
        name="convmod",
    )(*conv_in)

    nc = lp // CHUNK
    nbd = max(n for n in (1, 2, 4, 8) if bsz % n == 0)
    cspec = lambda cols: pl.BlockSpec((nbd, CHUNK, cols), lambda b, c: (b, c, 0))
    delta_in = [qkv_pre.reshape(bsz, lp, o_a), gbeta.reshape(bsz, lp, LANES), go_act.reshape(bsz, lp, vw),
                conv_qkv_w[0].astype(F32), row2(dn_norm_g[0])]
    o_dn = pl.pallas_call(
        functools.partial(_delta_kernel, nb=nbd, heads=heads, dk=dk, dv=dv, ch=CHUNK, ksz=ksz, halo=BF16_ROWS),
        grid=(bsz // nbd, nc),
        in_specs=[cspec(o_a), cspec(LANES), cspec(vw), full(delta_in[3]), full(delta_in[4])],
        out_specs=cspec(vw),
        out_shape=jax.ShapeDtypeStruct((bsz, lp, vw), BF16),
        scratch_shapes=[pltpu.VMEM((nbd, heads, dk, dv), F32),
                        pltpu.VMEM((nbd, CHUNK + BF16_ROWS, o_a), BF16)],
        compiler_params=_params(("arbitrary", "arbitrary")),
        name="delta",
    )(*delta_in).reshape(rows, vw)

    tm3 = _pick_tile(lp, MID_TILE, BF16_ROWS)
    nt3 = lp // tm3
    w_r = jnp.pad(w_router[0], ((0, 0), (0, LANES - n_exp))).astype(BF16)
    mid_in = [hp, o_dn, gates, ybg, w_dn_out[0].astype(BF16), w_out[0].astype(BF16),
              row2(norm_ffn_g[0]), w_r, _pad_lanes(b_router[0])]
    h2, hn2, topi, topg = pl.pallas_call(
        functools.partial(_mid_kernel, tm=tm3, n_exp=n_exp, parts=MID_PARTS if tm3 % (MID_PARTS * BF16_ROWS) == 0 else 1),
        grid=(bsz, nt3),
        in_specs=[rspec(d, tm3, nt3), rspec(vw, tm3, nt3), rspec(d, tm3, nt3), rspec(d, tm3, nt3)]
        + [full(a) for a in mid_in[4:]],
        out_specs=[rspec(d, tm3, nt3), rspec(d, tm3, nt3), rspec(LANES, tm3, nt3), rspec(LANES, tm3, nt3)],
        out_shape=[jax.ShapeDtypeStruct((rows, d), F32), jax.ShapeDtypeStruct((rows, d), F32),
                   jax.ShapeDtypeStruct((rows, LANES), I32), jax.ShapeDtypeStruct((rows, LANES), F32)],
        compiler_params=_params(("arbitrary", "arbitrary")),
        name="mid",
    )(*mid_in)

    tmd = _pick_tile(seq, TOKEN_TILE, SUBLANES)
    tpb = seq // tmd
    ntd = bsz * tpb
    tokens = bsz * seq
    width = tmd * TOP_K + n_exp * SUBLANES
    row0 = lambda i: pl.multiple_of((i // tpb) * lp + n_head + (i % tpb) * tmd, SUBLANES)
    prow = lambda r, cols, f=row0: pl.BlockSpec((pl.Element(r), pl.Element(cols)), lambda i, *_: (f(i), 0))
    tspec = pl.BlockSpec((tmd, LANES), lambda i, *_: (i, 0))
    ptiles = _pick_tile(tpb, 8, 1)
    pspec = pl.BlockSpec((ptiles * tmd, LANES), lambda i: (i, 0))
    tab = pl.BlockSpec((ptiles, 1, LANES), lambda i: (i, 0, 0))
    tab_shape = jax.ShapeDtypeStruct((ntd, 1, LANES), I32)
    pos, pos_t, n8_t, off8_t, carry8_t, tot8 = pl.pallas_call(
        functools.partial(_plan_kernel, tm=tmd, tiles=ptiles, n_exp=n_exp),
        grid=(ntd // ptiles,),
        in_specs=[prow(ptiles * tmd, LANES, lambda i: row0(i * ptiles))],
        out_specs=[pspec, pl.BlockSpec((ptiles, SUBLANES, tmd), lambda i: (i, 0, 0)), tab, tab, tab,
                   pl.BlockSpec((1, LANES), lambda i: (0, 0))],
        out_shape=[jax.ShapeDtypeStruct((tokens, LANES), I32), jax.ShapeDtypeStruct((ntd, SUBLANES, tmd), I32),
                   tab_shape, tab_shape, tab_shape, jax.ShapeDtypeStruct((1, LANES), F32)],
        scratch_shapes=[pltpu.VMEM((1, LANES), F32)],
        compiler_params=_params(("arbitrary",)),
        name="plan",
    )(topi)

    n_blocks = -(-(tokens * TOP_K + n_exp * (SUBLANES - 1) * ntd) // MOE_BLOCK) + n_exp
    cap = n_blocks * MOE_BLOCK
    counts = tot8[0, :n_exp].astype(I32)
    padded = (counts + MOE_BLOCK - 1) // MOE_BLOCK * MOE_BLOCK
    pend = jnp.cumsum(padded).astype(I32)
    pstart = pend - padded
    n_used = (pend[-1] // MOE_BLOCK).astype(I32).reshape(1)
    block_start = jnp.arange(n_blocks, dtype=I32) * MOE_BLOCK
    block_e = jnp.minimum(jnp.sum((pend[None, :] <= block_start[:, None]).astype(I32), axis=1), n_exp - 1)
    dbase = (carry8_t.reshape(ntd, LANES) + jnp.pad(pstart, (0, LANES - n_exp))[None, :]).reshape(ntd * LANES)
    n8_f = n8_t.reshape(ntd * LANES)
    off8_f = off8_t.reshape(ntd * LANES)

    anyspec = pl.BlockSpec(memory_space=pl.ANY)
    stab = lambda f: pl.BlockSpec((LANES,), lambda i, *_: (f(i),), memory_space=pltpu.SMEM)
    cur = lambda i: i
    xs = pl.pallas_call(
        functools.partial(_dispatch_kernel, tm=tmd, width=width, n_exp=n_exp, n_steps=ntd, n_blocks=n_blocks),
        grid_spec=pltpu.PrefetchScalarGridSpec(
            num_scalar_prefetch=2, grid=(ntd,),
            in_specs=[stab(cur), stab(cur), stab(cur), prow(tmd, d),
                      pl.BlockSpec((1, SUBLANES, tmd), lambda i, *_: (i, 0, 0))],
            out_specs=anyspec,
            scratch_shapes=[pltpu.VMEM((MOE_BLOCK, d), F32), pltpu.VMEM((2, width, d), F32),
                            pltpu.SMEM((2,), I32), pltpu.SemaphoreType.DMA(()), pltpu.SemaphoreType.DMA((2,))]),
        out_shape=jax.ShapeDtypeStruct((cap, d), F32),
        compiler_params=_params(("arbitrary",)),
        name="dispatch",
    )(pend, padded, n8_f, off8_f, dbase, hn2, pos_t)

    eidx = jnp.arange(n_exp, dtype=I32)
    later = jnp.logical_and(eidx[None, :] > eidx[:, None], (padded > 0)[None, :])
    nxt = jnp.min(jnp.where(later, eidx[None, :], n_exp), axis=1)
    next_e = jnp.where(nxt == n_exp, eidx, nxt).astype(I32)
    whole = lambda a: pl.BlockSpec(a.shape, lambda i, *_: (0,) * a.ndim)
    e_in = [w_gate_up[0], b_gate_up[0].astype(F32)[:, None, :], w_down[0], b_down[0].astype(F32)[:, None, :]]
    ys = pl.pallas_call(
        functools.partial(_expert_kernel, d_ff=d_ff, parts=2, n_blocks=n_blocks),
        grid_spec=pltpu.PrefetchScalarGridSpec(
            num_scalar_prefetch=3, grid=(1,),
            in_specs=[anyspec, anyspec, whole(e_in[1]), anyspec, whole(e_in[3])],
            out_specs=anyspec,
            scratch_shapes=[pltpu.VMEM((2, MOE_BLOCK, d), F32), pltpu.VMEM((2, MOE_BLOCK, d), F32),
                            pltpu.VMEM((d, 2 * d_ff), F32), pltpu.VMEM((d_ff, d), F32),
                            pltpu.VMEM((d, 2 * d_ff), BF16), pltpu.VMEM((d_ff, d), BF16),
                            pltpu.VMEM((MOE_BLOCK, d), F32),
                            pltpu.SemaphoreType.DMA((2,)), pltpu.SemaphoreType.DMA((2,)),
                            pltpu.SemaphoreType.DMA((2,)), pltpu.SemaphoreType.DMA((1,))]),
        out_shape=jax.ShapeDtypeStruct((cap, d), F32),
        compiler_params=_params(("arbitrary",)),
        name="experts",
    )(block_e, n_used, next_e, xs, *e_in)

    nxt = lambda i: jnp.minimum(i + 1, ntd - 1)
    out = pl.pallas_call(
        functools.partial(_combine_kernel, tm=tmd, width=width, n_exp=n_exp, n_steps=ntd),
        grid=(ntd,),
        in_specs=[stab(cur), stab(cur), stab(cur), stab(nxt), stab(nxt), stab(nxt), tspec,
                  prow(tmd, d), prow(tmd, LANES), pl.BlockSpec((1, d), lambda i: (0, 0)), anyspec],
        out_specs=pl.BlockSpec((tmd, d), lambda i: (i, 0)),
        out_shape=jax.ShapeDtypeStruct((tokens, d), x.dtype),
        scratch_shapes=[pltpu.VMEM((2, width, d), F32), pltpu.SMEM((2,), I32), pltpu.SemaphoreType.DMA((2,))],
        compiler_params=_params(("arbitrary",)),
        name="combine",
    )(n8_f, off8_f, dbase, n8_f, off8_f, dbase, pos, h2, topg, row2(final_norm_g), ys)
    return out.reshape(bsz, seq, d)
```

```python
import functools

import jax
import jax.numpy as jnp
from jax import lax
from jax.experimental import pallas as pl
from jax.experimental.pallas import tpu as pltpu

F32 = jnp.float32
BF16 = jnp.bfloat16
I32 = jnp.int32

CHUNK = 64
TOP_K = 4
MOE_BLOCK = 512
TOKEN_TILE = 256
MID_TILE = 832
MID_PARTS = 4
SWIGLU_LIMIT = 7.0
SWIGLU_ALPHA = 1.702
EPS = 1e-6
LN_EPS = 1e-5
LANES = 128
SUBLANES = 8
BF16_ROWS = 16
VMEM_LIMIT = 56 * 1024 * 1024


def _dot(a, b):
    return jnp.dot(a, b, preferred_element_type=F32)


def _dot_nt(a, b):
    return lax.dot_general(a, b, (((1,), (1,)), ((), ())), preferred_element_type=F32)


def _dot_tn(a, b):
    return lax.dot_general(a, b, (((0,), (0,)), ((), ())), preferred_element_type=F32)


def _sigmoid(x):
    return 1.0 / (1.0 + jnp.exp(-x))


def _silu(x):
    return x * _sigmoid(x)


def _softplus(x):
    return jnp.maximum(x, 0.0) + jnp.log(1.0 + jnp.exp(-jnp.abs(x)))


def _rms(x, g):
    return x * lax.rsqrt(jnp.mean(x * x, axis=-1, keepdims=True) + EPS) * g


def _iota(shape, dim):
    return lax.broadcasted_iota(I32, shape, dim)


def _pick_tile(n, target, mult):
    best = None
    for t in range(mult, min(n, target) + 1, mult):
        if n % t == 0:
            best = t
    assert best is not None, (n, target, mult)
    return best


def _params(sem, flags=None):
    return pltpu.CompilerParams(dimension_semantics=sem, vmem_limit_bytes=VMEM_LIMIT, flags=flags)


def _front_kernel(head_ref, x_ref, g_ref, wqkv_ref, wab_ref, wgo_ref, wglu_ref, wmg_ref, bglu_ref, bmg_ref,
                  alog_ref, dtb_ref, h_ref, qkv_ref, gb_ref, go_ref, u_ref, gates_ref,
                  *, tm, meta_pad, heads, conv_ch):
    j = pl.program_id(1)
    xb = x_ref[0]
    n_head = head_ref.shape[0]
    h = jnp.where(j == 0, jnp.concatenate([head_ref[...], xb[:tm - n_head]], axis=0), xb)
    h_ref[...] = h
    hn = _rms(h, g_ref[...]).astype(BF16)
    qkv_ref[...] = _dot(hn, wqkv_ref[...]).astype(BF16)
    ab = _dot(hn, wab_ref[...])
    valid = (j * tm + _iota((tm, 1), 0)) >= meta_pad
    lane = _iota((1, LANES), 1)
    g = -jnp.exp(alog_ref[...]) * _softplus(ab + dtb_ref[...])
    gb = jnp.where(lane < heads, g, _sigmoid(ab))
    gb_ref[...] = jnp.where(valid, gb, 0.0)
    go_ref[...] = _silu(_dot(hn, wgo_ref[...])).astype(BF16)
    glu = _dot(hn, wglu_ref[...]) + bglu_ref[...]
    u = glu[:, :conv_ch] * _sigmoid(glu[:, conv_ch:])
    u_ref[...] = jnp.where(valid, u, 0.0).astype(BF16)
    gates_ref[...] = _sigmoid(_dot(hn, wmg_ref[...]) + bmg_ref[...]).astype(BF16)


def _conv_kernel(u_ref, gate_ref, w_ref, b_ref, lng_ref, lnb_ref, wo_ref, bo_ref, y_ref,
                 xcat_ref, xs_ref, cbuf_ref, *, tm, kw, halo, lead, rb):
    j = pl.program_id(1)
    base = lead + halo
    total = base + tm
    n_c = xcat_ref.shape[1]

    @pl.when(j == 0)
    def _():
        xcat_ref[0:base, :] = jnp.zeros((base, n_c), BF16)

    @pl.when(j > 0)
    def _():
        xcat_ref[lead:base, :] = xcat_ref[tm + lead:tm + base, :]

    xcat_ref[base:total, :] = u_ref[...]
    sr = _iota((SUBLANES * rb, lead + rb), 0)
    sc = _iota((SUBLANES * rb, lead + rb), 1)
    shift = (sc == (sr & (rb - 1)) + lead - (sr >> (rb.bit_length() - 1))).astype(BF16)
    bias = b_ref[...]
    for t0 in range(lead, total, rb):
        res = _dot(shift, xcat_ref[t0 - lead:t0 + rb, :])
        for r in range(SUBLANES):
            xs_ref[r, t0:t0 + rb, :] = res[r * rb:(r + 1) * rb, :]
        if t0 < base:
            continue
        i0 = t0
        blk = (i0 - base) // rb
        acc = jnp.zeros((rb, n_c), F32)
        for s in range(kw):
            a, r = divmod(s, SUBLANES)
            wt = w_ref[(kw - 1 - s) * SUBLANES:(kw - s) * SUBLANES, :]
            acc = acc + xs_ref[r, i0 - SUBLANES * a:i0 - SUBLANES * a + rb, :] * jnp.concatenate(
                [wt] * (rb // SUBLANES), axis=0)
        cbuf_ref[blk * rb:(blk + 1) * rb, :] = acc + bias
    c = cbuf_ref[...]
    mu = jnp.mean(c, axis=-1, keepdims=True)
    xc = c - mu
    ln = xc * lax.rsqrt(jnp.mean(xc * xc, axis=-1, keepdims=True) + LN_EPS) * lng_ref[...] + lnb_ref[...]
    yb = _dot(_silu(ln).astype(BF16), wo_ref[...]) + bo_ref[...]
    y_ref[...] = (gate_ref[...].astype(F32) * yb).astype(BF16)


def _bdot(a, b):
    return jnp.dot(a.astype(BF16), b.astype(BF16), preferred_element_type=F32)


def _split3(x):
    x1 = x.astype(BF16)
    r1 = x - x1.astype(F32)
    x2 = r1.astype(BF16)
    x3 = (r1 - x2.astype(F32)).astype(BF16)
    return x1, x2, x3


def _unit_lower_inverses_minus_eye(mats, row, col):
    ch = mats[0].shape[0]
    blk16 = (row >> 4) == (col >> 4)
    blk32 = (row >> 5) == (col >> 5)
    n1 = [jnp.where(blk16, a, 0.0) for a in mats]
    n2 = [_bdot(n, n) for n in n1]
    r = [_bdot(jnp.concatenate([n, m], axis=0), m) for n, m in zip(n1, n2)]
    n4 = [x[ch:] for x in r]
    y = [m - n - x[:ch] for n, m, x in zip(n1, n2, r)]
    r = [_bdot(jnp.concatenate([yy, m], axis=0), m) for yy, m in zip(y, n4)]
    n8 = [x[ch:] for x in r]
    y = [yy + m + x[:ch] for yy, m, x in zip(y, n4, r)]
    y = [yy + m + _bdot(yy, m) for yy, m in zip(y, n8)]
    for mask in (blk32 & jnp.logical_not(blk16), jnp.logical_not(blk32)):
        ls = [jnp.where(mask, a, 0.0) for a in mats]
        ms = [l + _bdot(l, yy) for l, yy in zip(ls, y)]
        y = [yy - m - _bdot(yy, m) for yy, m in zip(y, ms)]
    return y


def _delta_kernel(qkv_ref, gb_ref, go_ref, cw_ref, ng_ref, o_ref, s_ref, xc_ref,
                  *, nb, heads, dk, dv, ch, ksz, halo):
    c = pl.program_id(1)
    qk_w = heads * dk

    @pl.when(c == 0)
    def _():
        s_ref[...] = jnp.zeros(s_ref.shape, F32)
        xc_ref[:, 0:halo, :] = jnp.zeros((nb, halo, xc_ref.shape[2]), BF16)

    @pl.when(c > 0)
    def _():
        xc_ref[:, 0:halo, :] = xc_ref[:, ch:ch + halo, :]

    xc_ref[:, halo:halo + ch, :] = qkv_ref[...]
    row = _iota((ch, ch), 0)
    col = _iota((ch, ch), 1)
    incl = row >= col
    strict = row > col
    tril = incl.astype(BF16)
    triu = (row <= col).astype(BF16)
    sr = _iota(((ksz - 1) * ch, halo + ch), 0)
    sc = _iota(((ksz - 1) * ch, halo + ch), 1)
    shift = (sc == (sr & (ch - 1)) + (sr >> (ch.bit_length() - 1)) + (halo - (ksz - 1))).astype(BF16)

    qn, kn, kb, vb, gamma, eg, ekd, elast = [], [], [], [], [], [], [], []
    shifted = [_dot(shift, xc_ref[b]) for b in range(nb)]
    for b in range(nb):
        acc = xc_ref[b, halo:halo + ch, :].astype(F32) * cw_ref[ksz - 1:ksz, :]
        for t in range(ksz - 1):
            acc = acc + shifted[b][t * ch:(t + 1) * ch, :] * cw_ref[t:t + 1, :]
        qkv = _silu(acc)
        gb = gb_ref[b]
        parts = _split3(gb)
        gc = sum(_dot(tril, p) for p in parts)
        gct = sum(_dot_tn(p, triu) for p in parts)
        glast = gc[ch - 1:ch, :]
        e_g = jnp.exp(gc)
        e_kd = jnp.exp(glast - gc)
        e_last = jnp.exp(glast)
        for h in range(heads):
            q = qkv[:, h * dk:(h + 1) * dk]
            k = qkv[:, qk_w + h * dk:qk_w + (h + 1) * dk]
            v = qkv[:, 2 * qk_w + h * dv:2 * qk_w + (h + 1) * dv]
            qn.append(q * lax.rsqrt(jnp.sum(q * q, axis=-1, keepdims=True) + EPS) * (dk ** -0.5))
            kn.append(k * lax.rsqrt(jnp.sum(k * k, axis=-1, keepdims=True) + EPS))
            beta = gb[:, heads + h:heads + h + 1]
            kb.append(kn[-1] * beta)
            vb.append(v * beta)
            diff = gc[:, h:h + 1] - gct[h:h + 1, :]
            gamma.append(jnp.where(incl, jnp.exp(jnp.where(incl, diff, 0.0)), 0.0))
            eg.append(e_g[:, h:h + 1])
            ekd.append(e_kd[:, h:h + 1])
            elast.append(e_last[:, h:h + 1])

    n = nb * heads
    kq = [_dot_nt(jnp.concatenate([kb[i], qn[i]], axis=0).astype(BF16), kn[i].astype(BF16))
          for i in range(n)]
    a = [jnp.where(strict, kq[i][:ch] * gamma[i], 0.0) for i in range(n)]
    qk = [kq[i][ch:] * gamma[i] for i in range(n)]
    y = _unit_lower_inverses_minus_eye(a, row, col)
    rhs = [jnp.concatenate([vb[i], kb[i] * eg[i]], axis=-1) for i in range(n)]
    uw = [rhs[i] + _bdot(y[i], rhs[i]) for i in range(n)]
    s = [s_ref[i // heads, i % heads] for i in range(n)]
    ws = [_bdot(jnp.concatenate([uw[i][:, dv:], qn[i] * eg[i]], axis=0), s[i]) for i in range(n)]
    v_new = [uw[i][:, :dv] - ws[i][:ch] for i in range(n)]
    o = [ws[i][ch:] + _bdot(qk[i], v_new[i]) for i in range(n)]
    s_new = [s[i] * elast[i] + _dot_tn((kn[i] * ekd[i]).astype(BF16), v_new[i].astype(BF16))
             for i in range(n)]
    for i in range(n):
        b, h = i // heads, i % heads
        s_ref[b, h] = s_new[i]
        on = _rms(o[i], ng_ref[...])
        o_ref[b, :, h * dv:(h + 1) * dv] = (
            on * go_ref[b, :, h * dv:(h + 1) * dv].astype(F32)).astype(BF16)


def _mid_kernel(h_ref, o_ref, ga_ref, ybg_ref, wdn_ref, wout_ref, g2_ref, wr_ref, br_ref,
                h2_ref, hn_ref, topi_ref, topg_ref, *, tm, n_exp, parts):
    rs = tm // parts
    sl = [pl.ds(p * rs, rs) for p in range(parts)]
    ya = [_dot(o_ref[s, :], wdn_ref[...]) for s in sl]
    y = [ga_ref[s, :].astype(F32) * a + ybg_ref[s, :].astype(F32) for s, a in zip(sl, ya)]
    h2 = [h_ref[s, :] + _dot(v.astype(BF16), wout_ref[...]) for s, v in zip(sl, y)]
    hn = [_rms(v, g2_ref[...]) for v in h2]
    for s, v, w in zip(sl, h2, hn):
        h2_ref[s, :] = v
        hn_ref[s, :] = w
    logits = [_dot(v.astype(BF16), wr_ref[...]) + br_ref[...] for v in hn]
    lane = _iota((rs, LANES), 1)
    lane_f = lane.astype(F32)
    l = [jnp.where(lane < n_exp, v, -jnp.inf) for v in logits]
    vals = [[] for _ in range(parts)]
    idxs = [[] for _ in range(parts)]
    for _ in range(TOP_K):
        m = [jnp.max(v, axis=-1, keepdims=True) for v in l]
        idx = [jnp.min(jnp.where(v == mm, lane_f, float(LANES)), axis=-1, keepdims=True)
               for v, mm in zip(l, m)]
        l = [jnp.where(lane_f == ii, -jnp.inf, v) for v, ii in zip(l, idx)]
        for p in range(parts):
            vals[p].append(m[p])
            idxs[p].append(idx[p])
    for p in range(parts):
        es = [jnp.exp(v - vals[p][0]) for v in vals[p]]
        den = es[0]
        for e in es[1:]:
            den = den + e
        topi = jnp.zeros((rs, LANES), F32)
        topg = jnp.zeros((rs, LANES), F32)
        for k in range(TOP_K):
            topi = jnp.where(lane == k, idxs[p][k], topi)
            topg = jnp.where(lane == k, es[k] / den, topg)
        topi_ref[sl[p], :] = topi.astype(I32)
        topg_ref[sl[p], :] = topg


def _choice_mask(topi, lane, n_exp):
    m = jnp.zeros(topi.shape, F32)
    for k in range(TOP_K):
        m = m + (lane == topi[:, k:k + 1]).astype(F32)
    return jnp.where(lane < n_exp, m, 0.0)


def _plan_kernel(topi_ref, pos_ref, post_ref, n8_ref, off8_ref, carry8_ref, tot8_ref, carry_ref,
                 *, tm, tiles, n_exp):
    i = pl.program_id(0)

    @pl.when(i == 0)
    def _():
        carry_ref[...] = jnp.zeros(carry_ref.shape, F32)

    lane = _iota((tm, LANES), 1)
    tri = (_iota((tm, tm), 0) >= _iota((tm, tm), 1)).astype(BF16)
    before = (_iota((LANES, LANES), 0) < _iota((LANES, LANES), 1)).astype(BF16)
    js = range(tiles)
    rows = [pl.ds(j * tm, tm) for j in js]
    topi = [topi_ref[r, :] for r in rows]
    m = [_choice_mask(t, lane, n_exp) for t in topi]
    cum = [_dot(tri, v.astype(BF16)) for v in m]
    n8 = [jnp.floor((c[tm - 1:tm, :] + (SUBLANES - 1)) * (1.0 / SUBLANES)) * SUBLANES for c in cum]
    off8 = [_dot(jnp.broadcast_to(v, (SUBLANES, LANES)).astype(BF16), before)[0:1] for v in n8]
    posf = [c - v + o for c, v, o in zip(cum, m, off8)]
    pos = [jnp.zeros((tm, LANES), F32) for _ in js]
    for k in range(TOP_K):
        pk = [jnp.sum(jnp.where(lane == t[:, k:k + 1], p, 0.0), axis=-1, keepdims=True)
              for t, p in zip(topi, posf)]
        pos = [jnp.where(lane == k, a, b) for a, b in zip(pk, pos)]
    carry = carry_ref[...]
    for j in js:
        pos_ref[rows[j], :] = pos[j].astype(I32)
        post_ref[j] = jnp.transpose(pos[j])[0:SUBLANES, :].astype(I32)
        n8_ref[j] = n8[j].astype(I32)
        off8_ref[j] = off8[j].astype(I32)
        carry8_ref[j] = carry.astype(I32)
        carry = carry + n8[j]
    carry_ref[...] = carry
    tot8_ref[...] = carry


def _pow2_floor(n):
    return 1 << (n.bit_length() - 1)


def _strip_pieces(n8, max_rows):
    pieces = []
    b = max_rows
    while b >= SUBLANES:
        pieces.append(((n8 & b) != 0, n8 & ~(2 * b - 1), b))
        b //= 2
    return pieces


LONG_PIECE = 64


def _copy_strip(n8, start_piece):
    n_long = n8 >> (LONG_PIECE.bit_length() - 1)

    def long_piece(j, carry):
        start_piece(j * LONG_PIECE, LONG_PIECE)
        return carry

    lax.fori_loop(0, n_long, long_piece, 0)
    for cond, ofs, rows in _strip_pieces(n8 & (LONG_PIECE - 1), LONG_PIECE // 2):
        @pl.when(cond)
        def _():
            start_piece(n_long * LONG_PIECE + ofs, rows)


def _wait_rows(total, make_copy, max_rows):
    b = max_rows
    while b >= SUBLANES:
        @pl.when((total & b) != 0)
        def _():
            make_copy(b).wait()
        b //= 2


def _aligned(start, rows):
    return pl.ds(pl.multiple_of(start, SUBLANES), rows)


def _dispatch_kernel(pend_ref, padded_ref, n8_ref, off8_ref, dbase_ref, hn_ref, post_ref, xs_ref,
                     zbuf_ref, sbuf_ref, tot_ref, zsem, sems, *, tm, width, n_exp, n_steps, n_blocks):
    i = pl.program_id(0)
    slot = i % 2

    def zero_block(start):
        return pltpu.make_async_copy(zbuf_ref, xs_ref.at[_aligned(start, MOE_BLOCK), :], zsem)

    @pl.when(i == 0)
    def _():
        zbuf_ref[...] = jnp.zeros(zbuf_ref.shape, F32)
        for e in range(n_exp):
            @pl.when(padded_ref[e] > 0)
            def _():
                zero_block(pend_ref[e] - MOE_BLOCK).start()
        first_unused = pend_ref[n_exp - 1] // MOE_BLOCK

        def start_tail(j, carry):
            zero_block(j * MOE_BLOCK).start()
            return carry

        lax.fori_loop(first_unused, n_blocks, start_tail, 0)
        for e in range(n_exp):
            @pl.when(padded_ref[e] > 0)
            def _():
                zero_block(0).wait()

        def wait_tail(j, carry):
            zero_block(0).wait()
            return carry

        lax.fori_loop(first_unused, n_blocks, wait_tail, 0)

    def strip_wait(s):
        _wait_rows(tot_ref[s], lambda b: pltpu.make_async_copy(
            sbuf_ref.at[s, pl.ds(0, b), :], xs_ref.at[pl.ds(0, b), :], sems.at[s]), _pow2_floor(width))

    @pl.when(i >= 2)
    def _():
        strip_wait(slot)

    post = post_ref[0]
    orow = _iota((width, tm), 0)
    sel = orow == post[0:1, :]
    for k in range(1, TOP_K):
        sel = jnp.logical_or(sel, orow == post[k:k + 1, :])
    sbuf_ref[slot] = _dot(sel.astype(BF16), hn_ref[...].astype(BF16))

    total = 0
    for e in range(n_exp):
        n8, so, do = n8_ref[e], off8_ref[e], dbase_ref[e]
        total = total + n8
        _copy_strip(n8, lambda ofs, rows, so=so, do=do, e=e: pltpu.make_async_copy(
            sbuf_ref.at[slot, _aligned(so + ofs, rows), :], xs_ref.at[_aligned(do + ofs, rows), :],
            sems.at[slot]).start(priority=e % 2))
    tot_ref[slot] = total

    @pl.when(i == n_steps - 1)
    def _():
        if n_steps >= 2:
            strip_wait(1 - slot)
        strip_wait(slot)


def _expert_kernel(be_ref, nbu_ref, nxt_ref, xs_ref, wgu_ref, bgu_ref, wd_ref, bd_ref, ys_ref,
                   xbuf_ref, ybuf_ref, sgu_ref, sd_ref, wgu16_ref, wd16_ref, zbuf_ref, xsem, ysem, wsem, zsem,
                   *, d_ff, parts, n_blocks):
    bm = xbuf_ref.shape[1]
    n_used = nbu_ref[0]

    def rows(i):
        return pl.ds(pl.multiple_of(i * bm, bm), bm)

    def x_copy(i, s):
        return pltpu.make_async_copy(xs_ref.at[rows(i), :], xbuf_ref.at[s], xsem.at[s])

    def y_copy(i, s):
        return pltpu.make_async_copy(ybuf_ref.at[s], ys_ref.at[rows(i), :], ysem.at[s])

    def w_copies(e):
        return (pltpu.make_async_copy(wgu_ref.at[e], sgu_ref, wsem.at[0]),
                pltpu.make_async_copy(wd_ref.at[e], sd_ref, wsem.at[1]))

    for c in w_copies(be_ref[0]):
        c.start()
    x_copy(0, 0).start()

    zbuf_ref[...] = jnp.zeros(zbuf_ref.shape, F32)

    def z_copy(i):
        return pltpu.make_async_copy(zbuf_ref, ys_ref.at[rows(i), :], zsem.at[0])

    def z_start(i, c):
        z_copy(i).start()
        return c

    jax.lax.fori_loop(n_used, n_blocks, z_start, 0)

    def body(i, carry):
        slot = i % 2
        e = be_ref[i]
        new_expert = jnp.logical_or(i == 0, e != be_ref[jnp.maximum(i - 1, 0)])

        @pl.when(new_expert)
        def _():
            for c in w_copies(e):
                c.wait()
            wgu16_ref[...] = sgu_ref[...].astype(BF16)
            wd16_ref[...] = sd_ref[...].astype(BF16)
            nx = nxt_ref[e]

            @pl.when(nx != e)
            def _():
                for c in w_copies(nx):
                    c.start()

        x_copy(i, slot).wait()

        @pl.when(i + 1 < n_used)
        def _():
            x_copy(i + 1, 1 - slot).start()

        @pl.when(i >= 2)
        def _():
            y_copy(0, slot).wait()

        rs = bm // parts
        sl = [pl.ds(p * rs, rs) for p in range(parts)]
        gu = [_dot(xbuf_ref[slot, s, :].astype(BF16), wgu16_ref[...]) + bgu_ref[e] for s in sl]
        act = []
        for v in gu:
            gate = jnp.minimum(v[:, :d_ff], SWIGLU_LIMIT)
            up = jnp.clip(v[:, d_ff:], -SWIGLU_LIMIT, SWIGLU_LIMIT)
            act.append(((up + 1.0) * gate * _sigmoid(SWIGLU_ALPHA * gate)).astype(BF16))
        for s, a in zip(sl, act):
            ybuf_ref[slot, s, :] = _dot(a, wd16_ref[...]) + bd_ref[e]
        y_copy(i, slot).start()
        return carry

    jax.lax.fori_loop(0, n_used, body, 0)

    @pl.when(n_used >= 2)
    def _():
        y_copy(0, n_used % 2).wait()

    y_copy(0, (n_used + 1) % 2).wait()

    def z_wait(i, c):
        z_copy(0).wait()
        return c

    jax.lax.fori_loop(n_used, n_blocks, z_wait, 0)


def _combine_kernel(n8c_ref, off8c_ref, dbc_ref, n8n_ref, off8n_ref, dbn_ref, pos_ref, h2_ref, topg_ref, fg_ref,
                    ys_ref, out_ref, gbuf_ref, tot_ref, sems, *, tm, width, n_exp, n_steps):
    i = pl.program_id(0)
    slot = i % 2

    def fetch(n8_ref, off8_ref, db_ref, s):
        total = 0
        for e in range(n_exp):
            n8, so, do = n8_ref[e], off8_ref[e], db_ref[e]
            total = total + n8
            _copy_strip(n8, lambda ofs, rows, so=so, do=do, e=e: pltpu.make_async_copy(
                ys_ref.at[_aligned(do + ofs, rows), :], gbuf_ref.at[s, _aligned(so + ofs, rows), :],
                sems.at[s]).start(priority=e % 2))
        tot_ref[s] = total

    @pl.when(i == 0)
    def _():
        gbuf_ref[...] = jnp.zeros(gbuf_ref.shape, F32)
        fetch(n8c_ref, off8c_ref, dbc_ref, 0)

    @pl.when(i + 1 < n_steps)
    def _():
        fetch(n8n_ref, off8n_ref, dbn_ref, 1 - slot)

    _wait_rows(tot_ref[slot], lambda b: pltpu.make_async_copy(
        ys_ref.at[pl.ds(0, b), :], gbuf_ref.at[slot, pl.ds(0, b), :], sems.at[slot]), _pow2_floor(width))
    pos = pos_ref[...]
    topg = topg_ref[...]
    lane = _iota((tm, width), 1)
    g = jnp.zeros((tm, width), F32)
    for k in range(TOP_K):
        g = g + jnp.where(lane == pos[:, k:k + 1], topg[:, k:k + 1], 0.0)
    moe = _dot(g.astype(BF16), gbuf_ref[slot].astype(BF16))
    out_ref[...] = _rms(h2_ref[...] + moe, fg_ref[...])


def _pad_lanes(v, fill=0.0):
    v = v.astype(F32)
    return jnp.concatenate([v, jnp.full((LANES - v.shape[0],), fill, F32)])[None, :]


def kernel(x, meta_tokens, norm_mix_g, w_in, conv_qkv_w, a_log, dt_bias, dn_norm_g, w_dn_out, b_glu,
           conv_dw_w, conv_dw_b, conv_ln_g, conv_ln_b, w_conv_out, b_conv_out, w_merge, b_merge, w_out,
           norm_ffn_g, w_router, b_router, w_gate_up, b_gate_up, w_down, b_down, final_norm_g):
    bsz, seq, d = x.shape
    depth = w_in.shape[0]
    n_meta = meta_tokens.shape[0]
    heads = a_log.shape[1]
    vw = w_dn_out.shape[1]
    qk_w = (conv_qkv_w.shape[2] - vw) // 2
    dk, dv = qk_w // heads, vw // heads
    ksz = conv_qkv_w.shape[1]
    kw, conv_ch = conv_dw_w.shape[1], conv_dw_w.shape[2]
    n_exp, d_ff = w_gate_up.shape[1], w_down.shape[2]
    meta_pad = CHUNK - n_meta
    lp = seq + n_meta + meta_pad
    assert depth == 1 and lp % CHUNK == 0 and dk == LANES and dv == LANES and d % LANES == 0
    assert 2 * heads <= LANES and n_exp <= LANES and kw - 1 <= 32 and ksz - 1 <= SUBLANES
    rows = bsz * lp
    n_head = meta_pad + n_meta
    head = jnp.concatenate([jnp.zeros((meta_pad, d), x.dtype), meta_tokens.astype(x.dtype)], axis=0)

    wi = w_in[0]
    o_a = 2 * qk_w + vw
    o_go = o_a + 2 * heads
    o_glu = o_go + vw
    w_qkv = wi[:, :o_a].astype(BF16)
    w_ab = jnp.pad(wi[:, o_a:o_go], ((0, 0), (0, LANES - 2 * heads))).astype(BF16)
    w_go = wi[:, o_go:o_glu].astype(BF16)
    w_glu = wi[:, o_glu:].astype(BF16)
    row2 = lambda v: v.astype(F32).reshape(1, -1)

    tm1 = _pick_tile(lp, min(832, seq), BF16_ROWS)
    nt1 = lp // tm1
    assert n_head % SUBLANES == 0 and n_head <= tm1 <= seq
    rspec = lambda cols, tm, nt: pl.BlockSpec((tm, cols), lambda b, j: (b * nt + j, 0))
    full = lambda a: pl.BlockSpec(a.shape, lambda *_: (0,) * a.ndim)
    full1 = lambda a: pl.BlockSpec(a.shape, lambda *_: (0,) * a.ndim, pipeline_mode=pl.Buffered(1))
    x_rows = pl.BlockSpec((pl.Element(1), pl.Element(tm1), pl.Element(d)),
                          lambda b, j: (b, pl.multiple_of(jnp.maximum(j * tm1 - n_head, 0), SUBLANES), 0))
    front_in = [head, x, row2(norm_mix_g[0]), w_qkv, w_ab, w_go, w_glu, w_merge[0].astype(BF16),
                row2(b_glu[0]), row2(b_merge[0]), _pad_lanes(a_log[0]), _pad_lanes(dt_bias[0])]
    hp, qkv_pre, gbeta, go_act, u_glu, gates = pl.pallas_call(
        functools.partial(_front_kernel, tm=tm1, meta_pad=meta_pad, heads=heads, conv_ch=conv_ch),
        grid=(bsz, nt1),
        in_specs=[full1(head), x_rows] + [full1(a) for a in front_in[2:]],
        out_specs=[rspec(d, tm1, nt1), rspec(o_a, tm1, nt1), rspec(LANES, tm1, nt1), rspec(vw, tm1, nt1),
                   rspec(conv_ch, tm1, nt1), rspec(2 * d, tm1, nt1)],
        out_shape=[jax.ShapeDtypeStruct((rows, d), F32),
                   jax.ShapeDtypeStruct((rows, o_a), BF16), jax.ShapeDtypeStruct((rows, LANES), F32),
                   jax.ShapeDtypeStruct((rows, vw), BF16), jax.ShapeDtypeStruct((rows, conv_ch), BF16),
                   jax.ShapeDtypeStruct((rows, 2 * d), BF16)],
        compiler_params=_params(("arbitrary", "arbitrary")),
        name="front",
    )(*front_in)

    tmc = _pick_tile(lp, 832, 32)
    ntc = lp // tmc
    halo, lead = 32, BF16_ROWS
    conv_w8 = jnp.repeat(conv_dw_w[0].astype(F32), SUBLANES, axis=0)
    conv_in = [u_glu, gates, conv_w8, row2(conv_dw_b[0]), row2(conv_ln_g[0]),
               row2(conv_ln_b[0]), w_conv_out[0].astype(BF16), row2(b_conv_out[0])]
    ybg = pl.pallas_call(
        functools.partial(_conv_kernel, tm=tmc, kw=kw, halo=halo, lead=lead, rb=32),
        grid=(bsz, ntc),
        in_specs=[rspec(conv_ch, tmc, ntc),
                  pl.BlockSpec((tmc, d), lambda b, j: (b * ntc + j, 1))] + [full(a) for a in conv_in[2:]],
        out_specs=rspec(d, tmc, ntc),
        out_shape=jax.ShapeDtypeStruct((rows, d), BF16),
        scratch_shapes=[pltpu.VMEM((lead + halo + tmc, conv_ch), BF16),
                        pltpu.VMEM((SUBLANES, lead + halo + tmc, conv_ch), F32),
                        pltpu.VMEM((tmc, conv_ch), F32)],
        compiler_params=_params(("arbitrary", "arbitrary")),
        name="convmod",
    )(*conv_in)

    nc = lp // CHUNK
    nbd = max(n for n in (1, 2, 4, 8) if bsz % n == 0)
    cspec = lambda cols: pl.BlockSpec((nbd, CHUNK, cols), lambda b, c: (b, c, 0))
    delta_in = [qkv_pre.reshape(bsz, lp, o_a), gbeta.reshape(bsz, lp, LANES), go_act.reshape(bsz, lp, vw),
                conv_qkv_w[0].astype(F32), row2(dn_norm_g[0])]
    o_dn = pl.pallas_call(
        functools.partial(_delta_kernel, nb=nbd, heads=heads, dk=dk, dv=dv, ch=CHUNK, ksz=ksz, halo=BF16_ROWS),
        grid=(bsz // nbd, nc),
        in_specs=[cspec(o_a), cspec(LANES), cspec(vw), full(delta_in[3]), full(delta_in[4])],
        out_specs=cspec(vw),
        out_shape=jax.ShapeDtypeStruct((bsz, lp, vw), BF16),
        scratch_shapes=[pltpu.VMEM((nbd, heads, dk, dv), F32),
                        pltpu.VMEM((nbd, CHUNK + BF16_ROWS, o_a), BF16)],
        compiler_params=_params(("arbitrary", "arbitrary")),
        name="delta",
    )(*delta_in).reshape(rows, vw)

    tm3 = _pick_tile(lp, MID_TILE, BF16_ROWS)
    nt3 = lp // tm3
    w_r = jnp.pad(w_router[0], ((0, 0), (0, LANES - n_exp))).astype(BF16)
    mid_in = [hp, o_dn, gates, ybg, w_dn_out[0].astype(BF16), w_out[0].astype(BF16),
              row2(norm_ffn_g[0]), w_r, _pad_lanes(b_router[0])]
    h2, hn2, topi, topg = pl.pallas_call(
        functools.partial(_mid_kernel, tm=tm3, n_exp=n_exp, parts=MID_PARTS if tm3 % (MID_PARTS * BF16_ROWS) == 0 else 1),
        grid=(bsz, nt3),
        in_specs=[rspec(d, tm3, nt3), rspec(vw, tm3, nt3), rspec(d, tm3, nt3), rspec(d, tm3, nt3)]
        + [full(a) for a in mid_in[4:]],
        out_specs=[rspec(d, tm3, nt3), rspec(d, tm3, nt3), rspec(LANES, tm3, nt3), rspec(LANES, tm3, nt3)],
        out_shape=[jax.ShapeDtypeStruct((rows, d), F32), jax.ShapeDtypeStruct((rows, d), F32),
                   jax.ShapeDtypeStruct((rows, LANES), I32), jax.ShapeDtypeStruct((rows, LANES), F32)],
        compiler_params=_params(("arbitrary", "arbitrary")),
        name="mid",
    )(*mid_in)

    tmd = _pick_tile(seq, TOKEN_TILE, SUBLANES)
    tpb = seq // tmd
    ntd = bsz * tpb
    tokens = bsz * seq
    width = tmd * TOP_K + n_exp * SUBLANES
    row0 = lambda i: pl.multiple_of((i // tpb) * lp + n_head + (i % tpb) * tmd, SUBLANES)
    prow = lambda r, cols, f=row0: pl.BlockSpec((pl.Element(r), pl.Element(cols)), lambda i, *_: (f(i), 0))
    tspec = pl.BlockSpec((tmd, LANES), lambda i, *_: (i, 0))
    ptiles = _pick_tile(tpb, 8, 1)
    pspec = pl.BlockSpec((ptiles * tmd, LANES), lambda i: (i, 0))
    tab = pl.BlockSpec((ptiles, 1, LANES), lambda i: (i, 0, 0))
    tab_shape = jax.ShapeDtypeStruct((ntd, 1, LANES), I32)
    pos, pos_t, n8_t, off8_t, carry8_t, tot8 = pl.pallas_call(
        functools.partial(_plan_kernel, tm=tmd, tiles=ptiles, n_exp=n_exp),
        grid=(ntd // ptiles,),
        in_specs=[prow(ptiles * tmd, LANES, lambda i: row0(i * ptiles))],
        out_specs=[pspec, pl.BlockSpec((ptiles, SUBLANES, tmd), lambda i: (i, 0, 0)), tab, tab, tab,
                   pl.BlockSpec((1, LANES), lambda i: (0, 0))],
        out_shape=[jax.ShapeDtypeStruct((tokens, LANES), I32), jax.ShapeDtypeStruct((ntd, SUBLANES, tmd), I32),
                   tab_shape, tab_shape, tab_shape, jax.ShapeDtypeStruct((1, LANES), F32)],
        scratch_shapes=[pltpu.VMEM((1, LANES), F32)],
        compiler_params=_params(("arbitrary",)),
        name="plan",
    )(topi)

    n_blocks = -(-(tokens * TOP_K + n_exp * (SUBLANES - 1) * ntd) // MOE_BLOCK) + n_exp
    cap = n_blocks * MOE_BLOCK
    counts = tot8[0, :n_exp].astype(I32)
    padded = (counts + MOE_BLOCK - 1) // MOE_BLOCK * MOE_BLOCK
    pend = jnp.cumsum(padded).astype(I32)
    pstart = pend - padded
    n_used = (pend[-1] // MOE_BLOCK).astype(I32).reshape(1)
    block_start = jnp.arange(n_blocks, dtype=I32) * MOE_BLOCK
    block_e = jnp.minimum(jnp.sum((pend[None, :] <= block_start[:, None]).astype(I32), axis=1), n_exp - 1)
    dbase = (carry8_t.reshape(ntd, LANES) + jnp.pad(pstart, (0, LANES - n_exp))[None, :]).reshape(ntd * LANES)
    n8_f = n8_t.reshape(ntd * LANES)
    off8_f = off8_t.reshape(ntd * LANES)

    anyspec = pl.BlockSpec(memory_space=pl.ANY)
    stab = lambda f: pl.BlockSpec((LANES,), lambda i, *_: (f(i),), memory_space=pltpu.SMEM)
    cur = lambda i: i
    xs = pl.pallas_call(
        functools.partial(_dispatch_kernel, tm=tmd, width=width, n_exp=n_exp, n_steps=ntd, n_blocks=n_blocks),
        grid_spec=pltpu.PrefetchScalarGridSpec(
            num_scalar_prefetch=2, grid=(ntd,),
            in_specs=[stab(cur), stab(cur), stab(cur), prow(tmd, d),
                      pl.BlockSpec((1, SUBLANES, tmd), lambda i, *_: (i, 0, 0))],
            out_specs=anyspec,
            scratch_shapes=[pltpu.VMEM((MOE_BLOCK, d), F32), pltpu.VMEM((2, width, d), F32),
                            pltpu.SMEM((2,), I32), pltpu.SemaphoreType.DMA(()), pltpu.SemaphoreType.DMA((2,))]),
        out_shape=jax.ShapeDtypeStruct((cap, d), F32),
        compiler_params=_params(("arbitrary",)),
        name="dispatch",
    )(pend, padded, n8_f, off8_f, dbase, hn2, pos_t)

    eidx = jnp.arange(n_exp, dtype=I32)
    later = jnp.logical_and(eidx[None, :] > eidx[:, None], (padded > 0)[None, :])
    nxt = jnp.min(jnp.where(later, eidx[None, :], n_exp), axis=1)
    next_e = jnp.where(nxt == n_exp, eidx, nxt).astype(I32)
    whole = lambda a: pl.BlockSpec(a.shape, lambda i, *_: (0,) * a.ndim)
    e_in = [w_gate_up[0], b_gate_up[0].astype(F32)[:, None, :], w_down[0], b_down[0].astype(F32)[:, None, :]]
    ys = pl.pallas_call(
        functools.partial(_expert_kernel, d_ff=d_ff, parts=2, n_blocks=n_blocks),
        grid_spec=pltpu.PrefetchScalarGridSpec(
            num_scalar_prefetch=3, grid=(1,),
            in_specs=[anyspec, anyspec, whole(e_in[1]), anyspec, whole(e_in[3])],
            out_specs=anyspec,
            scratch_shapes=[pltpu.VMEM((2, MOE_BLOCK, d), F32), pltpu.VMEM((2, MOE_BLOCK, d), F32),
                            pltpu.VMEM((d, 2 * d_ff), F32), pltpu.VMEM((d_ff, d), F32),
                            pltpu.VMEM((d, 2 * d_ff), BF16), pltpu.VMEM((d_ff, d), BF16),
                            pltpu.VMEM((MOE_BLOCK, d), F32),
                            pltpu.SemaphoreType.DMA((2,)), pltpu.SemaphoreType.DMA((2,)),
                            pltpu.SemaphoreType.DMA((2,)), pltpu.SemaphoreType.DMA((1,))]),
        out_shape=jax.ShapeDtypeStruct((cap, d), F32),
        compiler_params=_params(("arbitrary",)),
        name="experts",
    )(block_e, n_used, next_e, xs, *e_in)

    nxt = lambda i: jnp.minimum(i + 1, ntd - 1)
    out = pl.pallas_call(
        functools.partial(_combine_kernel, tm=tmd, width=width, n_exp=n_exp, n_steps=ntd),
        grid=(ntd,),
        in_specs=[stab(cur), stab(cur), stab(cur), stab(nxt), stab(nxt), stab(nxt), tspec,
                  prow(tmd, d), prow(tmd, LANES), pl.BlockSpec((1, d), lambda i: (0, 0)), anyspec],
        out_specs=pl.BlockSpec((tmd, d), lambda i: (i, 0)),
        out_shape=jax.ShapeDtypeStruct((tokens, d), x.dtype),
        scratch_shapes=[pltpu.VMEM((2, width, d), F32), pltpu.SMEM((2,), I32), pltpu.SemaphoreType.DMA((2,))],
        compiler_params=_params(("arbitrary",)),
        name="combine",
    )(n8_f, off8_f, dbase, n8_f, off8_f, dbase, pos, h2, topg, row2(final_norm_g), ys)
    return out.reshape(bsz, seq, d)
```
